```python
import math
import jax, jax.numpy as jnp
from jax import lax
import numpy as np

D_MODEL = 1024
BATCH = 8
SEQ = 2048
DEPTH = 4

A_HEADS = 4
A_HEAD_DIM = 128
IDX_HEADS = 8
IDX_DIM = 64
TOPK_MAX = 256
B_HEADS = 4
B_KEY_DIM = 128
B_VAL_DIM = 128
B_CHUNK = 64
C_HEADS = 4
C_Q_RANK = 384
C_KV_RANK = 256
C_NOPE = 128
C_ROPE = 64
C_V = 128
ROPE_THETA = 10000.0
REL_BUCKETS = 32
REL_MAX_DIST = 128
D_FF = -(-8 * D_MODEL // (3 * 256)) * 256
QBLOCK = 128
EPS = 1e-6
N_BRANCH = 3
NEG_BIG = -1e30
LB_FLOOR = 1e-30

A_WIDTH = A_HEADS * A_HEAD_DIM
B_WIDTH = B_HEADS * B_VAL_DIM
C_WIDTH = C_HEADS * C_V
IN_SPLITS = (A_HEADS * A_HEAD_DIM, A_HEAD_DIM, A_HEAD_DIM,
             IDX_HEADS * IDX_DIM, IDX_DIM, IDX_HEADS,
             B_HEADS * B_KEY_DIM, B_HEADS * B_KEY_DIM,
             B_HEADS * B_VAL_DIM, B_HEADS * B_VAL_DIM,
             C_Q_RANK, C_KV_RANK, C_ROPE,
             N_BRANCH * D_MODEL)
IN_COLS = sum(IN_SPLITS)

kernel_name = 'hybrid_dsa_hgrn2_mla_block'


def rms_norm(x, g):
    x32 = x.astype(jnp.float32)
    y = x32 * lax.rsqrt(jnp.mean(x32 * x32, axis=-1, keepdims=True) + EPS)
    return (y * g.astype(jnp.float32)).astype(x.dtype)


def apply_rope(x, cos, sin):
    x32 = x.astype(jnp.float32)
    x1, x2 = jnp.split(x32, 2, axis=-1)
    return jnp.concatenate([x1 * cos - x2 * sin, x2 * cos + x1 * sin], axis=-1).astype(x.dtype)


def t5_bucket(dist):
    max_exact = REL_BUCKETS // 2
    d = jnp.maximum(dist, 0)
    dl = jnp.maximum(d, max_exact).astype(jnp.float32)
    large = max_exact + (jnp.log(dl / max_exact) / math.log(REL_MAX_DIST / max_exact)
                         * (REL_BUCKETS - max_exact)).astype(jnp.int32)
    large = jnp.minimum(large, REL_BUCKETS - 1)
    return jnp.where(d < max_exact, d, large)


def dsa_attention(q, k, v, iq, ik, iw, positions, rel_bias):
    bsz, s_len = q.shape[0], q.shape[1]
    topk = min(TOPK_MAX, s_len // 4)
    n_blk = s_len // QBLOCK
    key_idx = jnp.arange(s_len)
    scale = A_HEAD_DIM ** -0.5
    idx_scale = (IDX_DIM ** -0.5) * (IDX_HEADS ** -0.5)
    ik32 = ik.astype(jnp.float32)
    gather = jax.vmap(lambda arr, ix: arr[ix])

    def block(i):
        start = i * QBLOCK
        qb = lax.dynamic_slice_in_dim(q, start, QBLOCK, axis=1)
        iqb = lax.dynamic_slice_in_dim(iq, start, QBLOCK, axis=1).astype(jnp.float32)
        iwb = lax.dynamic_slice_in_dim(iw, start, QBLOCK, axis=1).astype(jnp.float32)
        pb = lax.dynamic_slice_in_dim(positions, start, QBLOCK, axis=1)
        t_idx = start + jnp.arange(QBLOCK)
        causal = key_idx[None, :] <= t_idx[:, None]
        logits = jnp.einsum('bthd,bsd->bths', iqb, ik32)
        score = jnp.einsum('bths,bth->bts', jax.nn.relu(logits), iwb) * idx_scale
        score = jnp.where(causal[None], score, NEG_BIG)
        _, sel = lax.top_k(score, topk)
        k_sel = gather(k, sel)
        v_sel = gather(v, sel)
        p_sel = gather(positions, sel)
        bias = rel_bias[t5_bucket(pb[:, :, None] - p_sel)]
        s = (jnp.einsum('bthd,btkd->bthk', qb, k_sel).astype(jnp.float32) * scale
             + jnp.moveaxis(bias, -1, 2).astype(jnp.float32))
        valid = sel <= t_idx[None, :, None]
        s = jnp.where(valid[:, :, None, :], s, NEG_BIG)
        p = jax.nn.softmax(s, axis=-1).astype(v.dtype)
        return jnp.einsum('bthk,btkd->bthd', p, v_sel)

    out = lax.map(block, jnp.arange(n_blk))
    return jnp.moveaxis(out, 0, 1).reshape(bsz, s_len, A_WIDTH)


def hgrn2_mixer(q, f_raw, inp, g, lb, out_gain):
    bsz, s_len = q.shape[0], q.shape[1]
    f32 = jnp.float32
    n_chunk = s_len // B_CHUNK
    fr = f_raw.astype(f32)
    log_lb = jnp.log(jnp.maximum(lb, LB_FLOOR))
    log_f = jnp.logaddexp(log_lb, jnp.log1p(-lb) + jax.nn.log_sigmoid(fr))
    k_in = (1.0 - lb) * jax.nn.sigmoid(-fr)
    q_s = q.astype(f32) * (B_KEY_DIM ** -0.5)

    def chunks(t, d):
        return t.astype(f32).reshape(bsz, n_chunk, B_CHUNK, B_HEADS, d).transpose(1, 0, 3, 2, 4)

    causal = jnp.tril(jnp.ones((B_CHUNK, B_CHUNK), dtype=bool))

    def step(state, xs):
        qc, kc, vc, lfc = xs
        b = jnp.cumsum(lfc, axis=2)
        o_inter = jnp.einsum('bhck,bhkv->bhcv', qc * jnp.exp(b), state)
        diff = b[:, :, :, None, :] - b[:, :, None, :, :]
        decay = jnp.exp(jnp.where(causal[None, None, :, :, None], diff, NEG_BIG))
        attn = jnp.einsum('bhtk,bhsk,bhtsk->bhts', qc, kc, decay)
        o_intra = jnp.einsum('bhts,bhsv->bhtv', attn, vc)
        b_last = b[:, :, -1:, :]
        new_state = (jnp.exp(b_last[:, :, 0, :])[..., None] * state
                     + jnp.einsum('bhsk,bhsv->bhkv', kc * jnp.exp(b_last - b), vc))
        return new_state, o_inter + o_intra

    xs = (chunks(q_s, B_KEY_DIM), chunks(k_in, B_KEY_DIM), chunks(inp, B_VAL_DIM), chunks(log_f, B_KEY_DIM))
    state0 = jnp.zeros((bsz, B_HEADS, B_KEY_DIM, B_VAL_DIM), f32)
    _, o = lax.scan(step, state0, xs)
    o = o.transpose(1, 0, 3, 2, 4).reshape(bsz, s_len, B_HEADS, B_VAL_DIM)
    o = rms_norm(o, out_gain) * jax.nn.silu(g.astype(f32).reshape(bsz, s_len, B_HEADS, B_VAL_DIM))
    return o.reshape(bsz, s_len, B_WIDTH).astype(q.dtype)


def mla_attention(q_nope, q_pe, k_nope, k_pe, v):
    bsz, s_len = q_nope.shape[0], q_nope.shape[1]
    n_blk = s_len // QBLOCK
    scale = (C_NOPE + C_ROPE) ** -0.5
    key_idx = jnp.arange(s_len)

    def block(i):
        start = i * QBLOCK
        qn = lax.dynamic_slice_in_dim(q_nope, start, QBLOCK, axis=1)
        qp = lax.dynamic_slice_in_dim(q_pe, start, QBLOCK, axis=1)
        t_idx = start + jnp.arange(QBLOCK)
        causal = key_idx[None, :] <= t_idx[:, None]
        s = (jnp.einsum('bthd,bshd->bhts', qn, k_nope)
             + jnp.einsum('bthr,bsr->bhts', qp, k_pe)).astype(jnp.float32) * scale
        s = jnp.where(causal[None, None], s, NEG_BIG)
        p = jax.nn.softmax(s, axis=-1).astype(v.dtype)
        return jnp.einsum('bhts,bshd->bthd', p, v)

    out = lax.map(block, jnp.arange(n_blk))
    return jnp.moveaxis(out, 0, 1).reshape(bsz, s_len, C_WIDTH)


def setup_inputs(seed: int = 0) -> dict:
    key = jax.random.key(seed)
    ks = jax.random.split(key, 24)
    f32 = jnp.float32
    res_scale = (2 * DEPTH) ** -0.5

    def dense(k, shape, fan_in, scale=1.0):
        return jax.random.normal(k, shape, f32) * (scale * fan_in ** -0.5)

    def gain(k, shape):
        return 1.0 + 0.02 * jax.random.normal(k, shape, f32)

    x = jax.random.normal(ks[0], (BATCH, SEQ, D_MODEL), f32)
    offset = jax.random.randint(ks[1], (BATCH, 1), 0, 4096, dtype=jnp.int32)
    positions = offset + jnp.arange(SEQ, dtype=jnp.int32)[None, :]
    return {
        'x': x,
        'positions': positions,
        'w_in': dense(ks[2], (DEPTH, D_MODEL, IN_COLS), D_MODEL),
        'w_up_a': dense(ks[3], (DEPTH, A_WIDTH, D_MODEL), A_WIDTH),
        'w_up_b': dense(ks[4], (DEPTH, B_WIDTH, D_MODEL), B_WIDTH),
        'w_up_c': dense(ks[5], (DEPTH, C_WIDTH, D_MODEL), C_WIDTH),
        'w_out': dense(ks[6], (DEPTH, D_MODEL, D_MODEL), D_MODEL, res_scale),
        'mla_q_norm': gain(ks[7], (DEPTH, C_Q_RANK)),
        'mla_w_qb': dense(ks[8], (DEPTH, C_Q_RANK, C_HEADS * (C_NOPE + C_ROPE)), C_Q_RANK),
        'mla_kv_norm': gain(ks[9], (DEPTH, C_KV_RANK)),
        'mla_w_kvb': dense(ks[10], (DEPTH, C_KV_RANK, C_HEADS * (C_NOPE + C_V)), C_KV_RANK),
        'hgrn_lb_logits': 0.5 * jax.random.normal(ks[11], (DEPTH, B_HEADS * B_KEY_DIM), f32),
        'hgrn_out_norm': gain(ks[12], (DEPTH, B_VAL_DIM)),
        'rel_bias': 0.5 * jax.random.normal(ks[13], (REL_BUCKETS, A_HEADS), f32),
        'attn_norm': gain(ks[14], (DEPTH, D_MODEL)),
        'ffn_norm': gain(ks[15], (DEPTH, D_MODEL)),
        'w_ffn_gate': dense(ks[16], (DEPTH, D_MODEL, D_FF), D_MODEL),
        'w_ffn_up': dense(ks[17], (DEPTH, D_MODEL, D_FF), D_MODEL),
        'w_ffn_down': dense(ks[18], (DEPTH, D_FF, D_MODEL), D_FF, res_scale),
        'final_norm': gain(ks[19], (D_MODEL,)),
    }


def reference(x, positions, w_in, w_up_a, w_up_b, w_up_c, w_out, mla_q_norm, mla_w_qb, mla_kv_norm,
              mla_w_kvb, hgrn_lb_logits, hgrn_out_norm, rel_bias, attn_norm, ffn_norm, w_ffn_gate,
              w_ffn_up, w_ffn_down, final_norm):
    bsz, s_len, _ = x.shape
    f32 = jnp.float32
    split_pts = np.cumsum(IN_SPLITS)[:-1].tolist()
    p_lb = jax.nn.softmax(hgrn_lb_logits.astype(f32), axis=0)
    lower_bounds = jnp.cumsum(p_lb, axis=0) - p_lb[0:1]
    inv_freq = ROPE_THETA ** (-jnp.arange(0, C_ROPE, 2, dtype=f32) / C_ROPE)
    ang = positions.astype(f32)[..., None] * inv_freq
    cos, sin = jnp.cos(ang), jnp.sin(ang)

    for l in range(DEPTH):
        h = rms_norm(x, attn_norm[l])
        proj = h @ w_in[l]
        (aq, ak, av, iq, ik, iw, bq, bf, bi, bg, cq, ckv, ckpe, gate_logits) = jnp.split(proj, split_pts, axis=-1)
        o_a = dsa_attention(aq.reshape(bsz, s_len, A_HEADS, A_HEAD_DIM), ak, av,
                            iq.reshape(bsz, s_len, IDX_HEADS, IDX_DIM), ik, iw, positions, rel_bias)
        o_b = hgrn2_mixer(bq, bf, bi, bg, lower_bounds[l], hgrn_out_norm[l])
        q_c = (rms_norm(cq, mla_q_norm[l]) @ mla_w_qb[l]).reshape(bsz, s_len, C_HEADS, C_NOPE + C_ROPE)
        q_nope, q_pe = q_c[..., :C_NOPE], q_c[..., C_NOPE:]
        kv_c = (rms_norm(ckv, mla_kv_norm[l]) @ mla_w_kvb[l]).reshape(bsz, s_len, C_HEADS, C_NOPE + C_V)
        k_nope, v_c = kv_c[..., :C_NOPE], kv_c[..., C_NOPE:]
        q_pe = apply_rope(q_pe, cos[:, :, None, :], sin[:, :, None, :])
        k_pe = apply_rope(ckpe, cos, sin)
        o_c = mla_attention(q_nope, q_pe, k_nope, k_pe, v_c)
        gates = jax.nn.sigmoid(gate_logits.astype(f32)).astype(x.dtype)
        g_a, g_b, g_c = jnp.split(gates, N_BRANCH, axis=-1)
        mixed = g_a * (o_a @ w_up_a[l]) + g_b * (o_b @ w_up_b[l]) + g_c * (o_c @ w_up_c[l])
        x = x + mixed @ w_out[l]
        h = rms_norm(x, ffn_norm[l])
        x = x + (jax.nn.silu(h @ w_ffn_gate[l]) * (h @ w_ffn_up[l])) @ w_ffn_down[l]

    return rms_norm(x, final_norm)
```

```python
import functools
import math

import jax
import jax.numpy as jnp
import numpy as np
from jax import lax
from jax.experimental import pallas as pl
from jax.experimental.pallas import tpu as pltpu

F32 = jnp.float32
BF16 = jnp.bfloat16
I32 = jnp.int32

A_HEADS = 4
A_HEAD_DIM = 128
IDX_HEADS = 8
IDX_DIM = 64
TOPK_MAX = 256
B_HEADS = 4
B_KEY_DIM = 128
B_VAL_DIM = 128
B_CHUNK = 64
C_HEADS = 4
C_Q_RANK = 384
C_KV_RANK = 256
C_NOPE = 128
C_ROPE = 64
C_V = 128
ROPE_THETA = 10000.0
REL_BUCKETS = 32
REL_MAX_DIST = 128
N_BRANCH = 3
EPS = 1e-6
NEG_BIG = -1e30
LB_FLOOR = 1e-30

LANES = 128
VMEM_LIMIT = 56 * 1024 * 1024
INT_MIN = np.int32(-2 ** 31)

_NT = (((1,), (1,)), ((), ()))


def _params(*sem):
    return pltpu.CompilerParams(dimension_semantics=sem, vmem_limit_bytes=VMEM_LIMIT)


def _rms(x, g):
    return x * lax.rsqrt(jnp.mean(x * x, axis=-1, keepdims=True) + EPS) * g


def _const_spec(shape):
    nd = len(shape)
    return pl.BlockSpec(shape, lambda *_: (0,) * nd)


def _norm_matmul_kernel(x_ref, g_ref, w_ref, *out_refs, widths, col_chunk):
    h = _rms(x_ref[...], g_ref[...]).astype(BF16)
    c0 = 0
    for o_ref, w in zip(out_refs, widths):
        for s in range(0, w, col_chunk):
            e = min(s + col_chunk, w)
            o_ref[:, s:e] = jnp.dot(h, w_ref[:, c0 + s:c0 + e],
                                    preferred_element_type=F32).astype(o_ref.dtype)
        c0 += w


def _norm_matmul(x, g, w, widths, dtypes, tm):
    n, k = x.shape
    wtot = sum(widths)
    assert w.shape == (k, wtot) and n % tm == 0
    kern = functools.partial(_norm_matmul_kernel, widths=tuple(widths), col_chunk=512)
    return pl.pallas_call(
        kern,
        grid=(n // tm,),
        in_specs=[pl.BlockSpec((tm, k), lambda i: (i, 0)),
                  _const_spec((1, k)),
                  _const_spec((k, wtot))],
        out_specs=[pl.BlockSpec((tm, wd), lambda i: (i, 0)) for wd in widths],
        out_shape=[jax.ShapeDtypeStruct((n, wd), dt) for wd, dt in zip(widths, dtypes)],
        compiler_params=_params("parallel"),
        name="norm_proj",
    )(x, g.reshape(1, k), w)


def _dsa_kernel(qa_ref, iq_ref, iw_ref, ka_ref, va_ref, ik_ref, bias_ref, o_ref,
                keys_ref, qm_ref, wb_ref, *, topk, scale, idx_scale):
    t = LANES
    i = pl.program_id(1)
    n_tiles = i + 1
    row = lax.broadcasted_iota(I32, (t, t), 0)
    col = lax.broadcasted_iota(I32, (t, t), 1)
    lane_lo = col < IDX_DIM

    for h in range(IDX_HEADS):
        pair = iq_ref[0, :, (h // 2) * t:(h // 2 + 1) * t]
        keep = lane_lo if h % 2 == 0 else jnp.logical_not(lane_lo)
        qm_ref[h] = jnp.where(keep, pair, jnp.zeros_like(pair))
        wb_ref[h] = jnp.broadcast_to(iw_ref[0, :, h:h + 1], (t, t))

    def score_tile(j, carry):
        ks = pl.multiple_of(j * t, t)
        ik_t = ik_ref[0, pl.ds(ks, t), :]
        acc = jnp.zeros((t, t), F32)
        for h in range(IDX_HEADS):
            lg = lax.dot_general(qm_ref[h], ik_t, _NT, preferred_element_type=F32)
            acc = acc + jnp.maximum(lg, 0.0) * wb_ref[h]
        score = acc * idx_scale
        score = jnp.where(col + j * t <= row + i * t, score, NEG_BIG)
        score = jnp.where(score == 0.0, 0.0, score)
        bits = pltpu.bitcast(score, I32)
        keys_ref[j] = bits ^ ((bits >> 31) & np.int32(0x7FFFFFFF))
        return carry

    lax.fori_loop(0, n_tiles, score_tile, 0)

    def count_where(pred):
        def body(j, cnt):
            return cnt + jnp.where(pred(keys_ref[j]), 1.0, 0.0)
        cnt = lax.fori_loop(0, n_tiles, body, jnp.zeros((t, t), F32))
        return jnp.sum(cnt, axis=1, keepdims=True)

    def search_pass(b, thr):
        cand = thr + lax.shift_left(np.int32(1), 31 - b)
        tot = count_where(lambda key: key >= cand)
        return jnp.where(tot >= topk, cand, thr)

    thr = lax.fori_loop(0, 32, search_pass, jnp.full((t, t), INT_MIN, I32))
    n_ties = topk - count_where(lambda key: key > thr)

    upper = (row < col).astype(BF16)
    ones = jnp.ones((t, t), BF16)

    def attn_tile(j, carry):
        seen, state = carry
        ks = pl.multiple_of(j * t, t)
        key = keys_ref[j]
        eq = key == thr
        eq_b = jnp.where(eq, 1.0, 0.0).astype(BF16)
        before = seen + jnp.dot(eq_b, upper, preferred_element_type=F32)
        seen = seen + jnp.dot(eq_b, ones, preferred_element_type=F32)
        sel = jnp.logical_or(key > thr, jnp.logical_and(eq, before < n_ties))
        valid = jnp.logical_and(sel, col + j * t <= row + i * t)
        k_t = ka_ref[0, pl.ds(ks, t), :]
        v_t = va_ref[0, pl.ds(ks, t), :]
        which = jnp.where(j == i, 1, jnp.where(j == i - 1, 0, 2))
        new_state = []
        for h in range(A_HEADS):
            m, l, acc = state[h]
            s = lax.dot_general(qa_ref[0, :, h * t:(h + 1) * t], k_t, _NT,
                                preferred_element_type=F32) * scale + bias_ref[3 * h + which]
            s = jnp.where(valid, s, NEG_BIG)
            m_new = jnp.maximum(m, jnp.max(s, axis=1, keepdims=True))
            alpha = jnp.exp(m - m_new)
            p = jnp.where(valid, jnp.exp(s - m_new), 0.0)
            l = alpha * l + jnp.sum(p, axis=1, keepdims=True)
            acc = alpha * acc + jnp.dot(p.astype(BF16), v_t, preferred_element_type=F32)
            new_state.append((m_new, l, acc))
        return seen, tuple(new_state)

    init = tuple((jnp.full((t, 1), NEG_BIG, F32), jnp.zeros((t, 1), F32),
                  jnp.zeros((t, t), F32)) for _ in range(A_HEADS))
    _, state = lax.fori_loop(0, n_tiles, attn_tile, (jnp.zeros((t, t), F32), init))
    for h in range(A_HEADS):
        _, l, acc = state[h]
        o_ref[0, :, h * t:(h + 1) * t] = (acc / l).astype(o_ref.dtype)


def _dsa(qa, iq, iw, ka, va, ik2, bias_tab):
    b, s, _ = qa.shape
    t = LANES
    topk = min(TOPK_MAX, s // 4)
    kern = functools.partial(_dsa_kernel, topk=float(topk), scale=A_HEAD_DIM ** -0.5,
                             idx_scale=(IDX_DIM ** -0.5) * (IDX_HEADS ** -0.5))
    qspec = lambda w: pl.BlockSpec((1, t, w), lambda bi, i: (bi, i, 0))
    kspec = pl.BlockSpec((1, s, t), lambda bi, i: (bi, 0, 0))
    return pl.pallas_call(
        kern,
        grid=(b, s // t),
        in_specs=[qspec(A_HEADS * t), qspec(IDX_HEADS * IDX_DIM), qspec(t), kspec, kspec, kspec,
                  _const_spec(bias_tab.shape)],
        out_specs=qspec(A_HEADS * t),
        out_shape=jax.ShapeDtypeStruct((b, s, A_HEADS * t), BF16),
        scratch_shapes=[pltpu.VMEM((s // t, t, t), I32),
                        pltpu.VMEM((IDX_HEADS, t, t), BF16),
                        pltpu.VMEM((IDX_HEADS, t, t), F32)],
        compiler_params=_params("parallel", "parallel"),
        name="dsa_attention",
    )(qa, iq, iw, ka, va, ik2, bias_tab)


def _hgrn_kernel(q_ref, f_ref, i_ref, g_ref, lb_ref, gain_ref, o_ref, state_ref, *, n_chunks):
    c = B_CHUNK
    kd, vd = B_KEY_DIM, B_VAL_DIM

    @pl.when(pl.program_id(1) == 0)
    def _():
        state_ref[...] = jnp.zeros_like(state_ref)

    row = lax.broadcasted_iota(I32, (c, c), 0)
    col = lax.broadcasted_iota(I32, (c, c), 1)
    causal = row >= col
    tril = causal.astype(F32)
    q_scale = kd ** -0.5

    def chunk_body(ci, carry):
        r0 = pl.multiple_of(ci * c, c)
        for h in range(B_HEADS):
            ks = slice(h * kd, (h + 1) * kd)
            vs = slice(h * vd, (h + 1) * vd)
            fr = f_ref[0, pl.ds(r0, c), ks]
            lb = lb_ref[:, ks]
            z = jnp.exp(-jnp.abs(fr))
            r = 1.0 / (1.0 + z)
            sig_pos = jnp.where(fr >= 0, r, z * r)
            sig_neg = jnp.where(fr >= 0, z * r, r)
            log_f = jnp.log(jnp.maximum(lb, LB_FLOOR) + (1.0 - lb) * sig_pos)
            k_in = (1.0 - lb) * sig_neg
            bsum = jnp.dot(tril, log_f, precision=lax.Precision.HIGHEST,
                           preferred_element_type=F32)
            b_mid = bsum[c // 2 - 1:c // 2, :]
            b_last = bsum[c - 1:c, :]
            qs = q_ref[0, pl.ds(r0, c), ks] * q_scale
            v = i_ref[0, pl.ds(r0, c), vs]
            v_b = v.astype(BF16)
            q_mid = (qs * jnp.exp(bsum - b_mid)).astype(BF16)
            k_mid = (k_in * jnp.exp(b_mid - bsum)).astype(BF16)
            attn = lax.dot_general(q_mid, k_mid, _NT, preferred_element_type=F32)
            attn = jnp.where(causal, attn, 0.0).astype(BF16)
            o = jnp.dot(attn, v_b, preferred_element_type=F32)
            st = state_ref[h]
            o = o + lax.dot_general((qs * jnp.exp(bsum)).astype(BF16), st.astype(BF16), _NT,
                                    preferred_element_type=F32)
            k_end = (k_in * jnp.exp(b_last - bsum)).astype(BF16)
            state_ref[h] = jnp.exp(b_last) * st + jnp.dot(v.T.astype(BF16), k_end,
                                                         preferred_element_type=F32)
            g = g_ref[0, pl.ds(r0, c), vs]
            o = _rms(o, gain_ref[...]) * (g * (1.0 / (1.0 + jnp.exp(-g))))
            o_ref[0, pl.ds(r0, c), vs] = o.astype(o_ref.dtype)
        return carry

    lax.fori_loop(0, n_chunks, chunk_body, 0)


def _hgrn(bq, bf, bi, bg, lb, gain, sb):
    b, s, w = bq.shape
    assert s % sb == 0 and sb % B_CHUNK == 0
    kern = functools.partial(_hgrn_kernel, n_chunks=sb // B_CHUNK)
    spec = pl.BlockSpec((1, sb, w), lambda bi_, si: (bi_, si, 0))
    return pl.pallas_call(
        kern,
        grid=(b, s // sb),
        in_specs=[spec, spec, spec, spec, _const_spec((1, w)), _const_spec((1, B_VAL_DIM))],
        out_specs=spec,
        out_shape=jax.ShapeDtypeStruct((b, s, w), BF16),
        scratch_shapes=[pltpu.VMEM((B_HEADS, B_VAL_DIM, B_KEY_DIM), F32)],
        compiler_params=_params("parallel", "arbitrary"),
        name="hgrn2",
    )(bq, bf, bi, bg, lb.reshape(1, w), gain.reshape(1, B_VAL_DIM))


def _mla_prep_kernel(cq_ref, ckv_ref, kpe_raw_ref, cos_ref, sin_ref, gq_ref, gkv_ref, wq_ref, wkv_ref,
                     qn_ref, qpe_ref, kn_ref, v_ref, kpe_ref):
    nw = C_HEADS * C_NOPE
    cos, sin = cos_ref[...], sin_ref[...]
    q = jnp.dot(_rms(cq_ref[...], gq_ref[...]).astype(BF16), wq_ref[...], preferred_element_type=F32)
    qn_ref[...] = q[:, :nw].astype(qn_ref.dtype)
    x1, x2 = q[:, nw:nw + LANES], q[:, nw + LANES:nw + 2 * LANES]
    qpe_ref[:, :LANES] = (x1 * cos - x2 * sin).astype(qpe_ref.dtype)
    qpe_ref[:, LANES:] = (x2 * cos + x1 * sin).astype(qpe_ref.dtype)
    kv = jnp.dot(_rms(ckv_ref[...], gkv_ref[...]).astype(BF16), wkv_ref[...], preferred_element_type=F32)
    kn_ref[...] = kv[:, :nw].astype(kn_ref.dtype)
    v_ref[...] = kv[:, nw:].astype(v_ref.dtype)
    k1, k2 = kpe_raw_ref[:, :LANES], kpe_raw_ref[:, LANES:]
    kpe_ref[:, :LANES] = (k1 * cos - k2 * sin).astype(kpe_ref.dtype)
    kpe_ref[:, LANES:] = (k2 * cos + k1 * sin).astype(kpe_ref.dtype)


def _mla_prep(cq, ckv, kpe_raw, cos4, sin4, gq, gkv, wq, wkv, tm):
    n = cq.shape[0]
    nw = C_HEADS * C_NOPE
    row = lambda w: pl.BlockSpec((tm, w), lambda i: (i, 0))
    widths = (nw, 2 * LANES, nw, C_HEADS * C_V, 2 * LANES)
    return pl.pallas_call(
        _mla_prep_kernel,
        grid=(n // tm,),
        in_specs=[row(C_Q_RANK), row(C_KV_RANK), row(2 * LANES), row(LANES), row(LANES),
                  _const_spec((1, C_Q_RANK)), _const_spec((1, C_KV_RANK)),
                  _const_spec(wq.shape), _const_spec(wkv.shape)],
        out_specs=[row(w) for w in widths],
        out_shape=[jax.ShapeDtypeStruct((n, w), BF16) for w in widths],
        compiler_params=_params("parallel"),
        name="mla_prep",
    )(cq, ckv, kpe_raw, cos4, sin4, gq.reshape(1, -1), gkv.reshape(1, -1), wq, wkv)


def _mla_kernel(qn_ref, qpe_ref, kn_ref, kpe_ref, v_ref, o_ref, *, tq, tk, scale):
    i = pl.program_id(1)
    n_tiles = (i + 1) * (tq // tk)
    row = lax.broadcasted_iota(I32, (tq, tk), 0)
    col = lax.broadcasted_iota(I32, (tq, tk), 1)
    lane = lax.broadcasted_iota(I32, (tq, 2 * LANES), 1)
    rope_head = (lane & (LANES - 1)) >> 5
    qpe = qpe_ref[0]
    for h in range(C_HEADS):
        hs = slice(h * C_NOPE, (h + 1) * C_NOPE)
        vs = slice(h * C_V, (h + 1) * C_V)
        q_n = qn_ref[0, :, hs]
        q_p = jnp.where(rope_head == h, qpe, jnp.zeros_like(qpe))

        def body(j, carry, q_n=q_n, q_p=q_p, hs=hs, vs=vs):
            m, l, acc = carry
            ks = pl.multiple_of(j * tk, tk)
            s = lax.dot_general(q_n, kn_ref[0, pl.ds(ks, tk), hs], _NT, preferred_element_type=F32)
            s = s + lax.dot_general(q_p, kpe_ref[0, pl.ds(ks, tk), :], _NT,
                                    preferred_element_type=F32)
            valid = col + j * tk <= row + i * tq
            s = jnp.where(valid, s * scale, NEG_BIG)
            m_new = jnp.maximum(m, jnp.max(s, axis=1, keepdims=True))
            alpha = jnp.exp(m - m_new)
            p = jnp.where(valid, jnp.exp(s - m_new), 0.0)
            l = alpha * l + jnp.sum(p, axis=1, keepdims=True)
            acc = alpha * acc + jnp.dot(p.astype(BF16), v_ref[0, pl.ds(ks, tk), vs],
                                        preferred_element_type=F32)
            return m_new, l, acc

        init = (jnp.full((tq, 1), NEG_BIG, F32), jnp.zeros((tq, 1), F32), jnp.zeros((tq, C_V), F32))
        _, l, acc = lax.fori_loop(0, n_tiles, body, init)
        o_ref[0, :, vs] = (acc / l).astype(o_ref.dtype)


def _mla(qn, qpe, kn, kpe, v, tq, tk):
    b, s, _ = qn.shape
    kern = functools.partial(_mla_kernel, tq=tq, tk=tk, scale=(C_NOPE + C_ROPE) ** -0.5)
    qspec = lambda w: pl.BlockSpec((1, tq, w), lambda bi, i: (bi, i, 0))
    kspec = lambda w: pl.BlockSpec((1, s, w), lambda bi, i: (bi, 0, 0))
    return pl.pallas_call(
        kern,
        grid=(b, s // tq),
        in_specs=[qspec(qn.shape[2]), qspec(qpe.shape[2]), kspec(kn.shape[2]), kspec(kpe.shape[2]),
                  kspec(v.shape[2])],
        out_specs=qspec(C_HEADS * C_V),
        out_shape=jax.ShapeDtypeStruct((b, s, C_HEADS * C_V), BF16),
        compiler_params=_params("parallel", "parallel"),
        name="mla_attention",
    )(qn, qpe, kn, kpe, v)


def _merge_kernel(x_ref, g_ref, wg_ref, oa_ref, ob_ref, oc_ref, wa_ref, wb_ref, wc_ref, wo_ref,
                  out_ref, mixed_ref, *, col_chunk):
    x = x_ref[...]
    d = x.shape[1]
    h = _rms(x, g_ref[...]).astype(BF16)
    branches = ((oa_ref, wa_ref), (ob_ref, wb_ref), (oc_ref, wc_ref))
    for s in range(0, d, col_chunk):
        cs = slice(s, s + col_chunk)
        mixed = None
        for bidx, (o_ref, w_ref) in enumerate(branches):
            logits = jnp.dot(h, wg_ref[:, bidx * d + s:bidx * d + s + col_chunk],
                             preferred_element_type=F32)
            gate = 1.0 / (1.0 + jnp.exp(-logits))
            term = gate * jnp.dot(o_ref[...], w_ref[:, cs], preferred_element_type=F32)
            mixed = term if mixed is None else mixed + term
        mixed_ref[:, cs] = mixed.astype(BF16)
    out_ref[...] = x + jnp.dot(mixed_ref[...], wo_ref[...], preferred_element_type=F32)


def _merge(x, g, w_gate, oa, ob, oc, wa, wb, wc, wo, tm):
    n, d = x.shape
    row = lambda w: pl.BlockSpec((tm, w), lambda i: (i, 0))
    kern = functools.partial(_merge_kernel, col_chunk=256)
    return pl.pallas_call(
        kern,
        grid=(n // tm,),
        in_specs=[row(d), _const_spec((1, d)), _const_spec(w_gate.shape),
                  row(oa.shape[1]), row(ob.shape[1]), row(oc.shape[1]),
                  _const_spec(wa.shape), _const_spec(wb.shape), _const_spec(wc.shape),
                  _const_spec(wo.shape)],
        out_specs=row(d),
        out_shape=jax.ShapeDtypeStruct((n, d), F32),
        scratch_shapes=[pltpu.VMEM((tm, d), BF16)],
        compiler_params=_params("parallel"),
        name="merge_out_proj",
    )(x, g.reshape(1, d), w_gate, oa, ob, oc, wa, wb, wc, wo)


def _ffn_kernel(x_ref, g_ref, wg_ref, wu_ref, wd_ref, gf_ref, out_ref, act_ref, *, col_chunk,
                final_norm):
    x = x_ref[...]
    h = _rms(x, g_ref[...]).astype(BF16)
    dff = wg_ref.shape[1]
    for s in range(0, dff, col_chunk):
        cs = slice(s, s + col_chunk)
        gate = jnp.dot(h, wg_ref[:, cs], preferred_element_type=F32)
        up = jnp.dot(h, wu_ref[:, cs], preferred_element_type=F32)
        act_ref[:, cs] = (gate * (1.0 / (1.0 + jnp.exp(-gate))) * up).astype(BF16)
    y = x + jnp.dot(act_ref[...], wd_ref[...], preferred_element_type=F32)
    if final_norm:
        y = _rms(y, gf_ref[...])
    out_ref[...] = y


def _ffn(x, g, wg, wu, wd, gf, tm, final_norm):
    n, d = x.shape
    dff = wg.shape[1]
    row = pl.BlockSpec((tm, d), lambda i: (i, 0))
    kern = functools.partial(_ffn_kernel, col_chunk=256, final_norm=final_norm)
    return pl.pallas_call(
        kern,
        grid=(n // tm,),
        in_specs=[row, _const_spec((1, d)), _const_spec(wg.shape), _const_spec(wu.shape),
                  _const_spec(wd.shape), _const_spec((1, d))],
        out_specs=row,
        out_shape=jax.ShapeDtypeStruct((n, d), F32),
        scratch_shapes=[pltpu.VMEM((tm, dff), BF16)],
        compiler_params=_params("parallel"),
        name="swiglu_ffn",
    )(x, g.reshape(1, d), wg, wu, wd, gf.reshape(1, d))


def _t5_bucket(dist):
    max_exact = REL_BUCKETS // 2
    d = jnp.maximum(dist, 0)
    dl = jnp.maximum(d, max_exact).astype(F32)
    large = max_exact + (jnp.log(dl / max_exact) / math.log(REL_MAX_DIST / max_exact)
                         * (REL_BUCKETS - max_exact)).astype(I32)
    large = jnp.minimum(large, REL_BUCKETS - 1)
    return jnp.where(d < max_exact, d, large)


def _in_proj_columns(d_model):
    splits = (A_HEADS * A_HEAD_DIM, A_HEAD_DIM, A_HEAD_DIM, IDX_HEADS * IDX_DIM, IDX_DIM, IDX_HEADS,
              B_HEADS * B_KEY_DIM, B_HEADS * B_KEY_DIM, B_HEADS * B_VAL_DIM, B_HEADS * B_VAL_DIM,
              C_Q_RANK, C_KV_RANK, C_ROPE, N_BRANCH * d_model)
    starts = np.concatenate([[0], np.cumsum(splits)])
    rng = lambda k: np.arange(starts[k], starts[k + 1])
    half = C_ROPE // 2
    pe = rng(12)
    cols = [rng(0), rng(1), rng(2), rng(3),
            np.tile(rng(4), 2),
            rng(6), rng(7), rng(8), rng(9), rng(10), rng(11),
            np.concatenate([np.tile(pe[:half], C_HEADS), np.tile(pe[half:], C_HEADS)])]
    return cols, rng(5), rng(13)


def kernel(x, positions, w_in, w_up_a, w_up_b, w_up_c, w_out, mla_q_norm, mla_w_qb, mla_kv_norm,
           mla_w_kvb, hgrn_lb_logits, hgrn_out_norm, rel_bias, attn_norm, ffn_norm, w_ffn_gate,
           w_ffn_up, w_ffn_down, final_norm):
    bsz, s_len, d_model = x.shape
    depth = w_in.shape[0]
    n = bsz * s_len

    cols, iw_cols, gate_cols = _in_proj_columns(d_model)
    iw_w = jnp.pad(w_in[:, :, iw_cols], ((0, 0), (0, 0), (0, LANES - IDX_HEADS)))
    w_proj = jnp.concatenate([w_in[:, :, np.concatenate(cols[:5])], iw_w,
                              w_in[:, :, np.concatenate(cols[5:])]], axis=2).astype(BF16)
    widths = [len(c) for c in cols[:5]] + [LANES] + [len(c) for c in cols[5:]]
    dtypes = [BF16] * 5 + [F32] * 8
    w_gate = w_in[:, :, gate_cols].astype(BF16)

    per_head = C_NOPE + C_ROPE
    half = C_ROPE // 2
    heads = np.arange(C_HEADS)[:, None]
    q_cols = np.concatenate([(heads * per_head + np.arange(C_NOPE)).ravel(),
                             (heads * per_head + C_NOPE + np.arange(half)).ravel(),
                             (heads * per_head + C_NOPE + half + np.arange(half)).ravel()])
    kv_cols = np.concatenate([(heads * (C_NOPE + C_V) + np.arange(C_NOPE)).ravel(),
                              (heads * (C_NOPE + C_V) + C_NOPE + np.arange(C_V)).ravel()])
    w_qb = mla_w_qb[:, :, q_cols].astype(BF16)
    w_kvb = mla_w_kvb[:, :, kv_cols].astype(BF16)
    bf = lambda w: w.astype(BF16)
    w_up_a, w_up_b, w_up_c, w_out = bf(w_up_a), bf(w_up_b), bf(w_up_c), bf(w_out)
    w_ffn_gate, w_ffn_up, w_ffn_down = bf(w_ffn_gate), bf(w_ffn_up), bf(w_ffn_down)

    p_lb = jax.nn.softmax(hgrn_lb_logits.astype(F32), axis=0)
    lower_bounds = jnp.cumsum(p_lb, axis=0) - p_lb[0:1]
    inv_freq = ROPE_THETA ** (-jnp.arange(0, C_ROPE, 2, dtype=F32) / C_ROPE)
    ang = positions.astype(F32)[..., None] * inv_freq
    cos4 = jnp.tile(jnp.cos(ang), (1, 1, C_HEADS)).reshape(n, LANES)
    sin4 = jnp.tile(jnp.sin(ang), (1, 1, C_HEADS)).reshape(n, LANES)
    t_idx = jnp.arange(LANES, dtype=I32)[:, None]
    c_idx = jnp.arange(2 * LANES, dtype=I32)[None, :]
    near = jnp.moveaxis(rel_bias[_t5_bucket(t_idx + LANES - c_idx)], -1, 0).astype(F32)
    far = jnp.broadcast_to(rel_bias[REL_BUCKETS - 1].astype(F32)[:, None, None],
                           (A_HEADS, LANES, LANES))
    bias_tab = jnp.stack([near[:, :, :LANES], near[:, :, LANES:], far], axis=1)
    bias_tab = bias_tab.reshape(3 * A_HEADS, LANES, LANES)

    tm = min(512, n)
    x2 = x.reshape(n, d_model)
    r3 = lambda a: a.reshape(bsz, s_len, a.shape[-1])
    for l in range(depth):
        (qa, ka, va, iq, ik2, iw, bq, bfr, bi, bg, cq, ckv, kpe_raw) = _norm_matmul(
            x2, attn_norm[l], w_proj[l], widths, dtypes, tm)
        o_a = _dsa(r3(qa), r3(iq), r3(iw), r3(ka), r3(va), r3(ik2), bias_tab)
        o_b = _hgrn(r3(bq), r3(bfr), r3(bi), r3(bg), lower_bounds[l], hgrn_out_norm[l],
                    sb=min(512, s_len))
        qn, qpe, kn, vc, kpe = _mla_prep(cq, ckv, kpe_raw, cos4, sin4, mla_q_norm[l], mla_kv_norm[l],
                                         w_qb[l], w_kvb[l], tm)
        tq = min(256, s_len)
        o_c = _mla(r3(qn), r3(qpe), r3(kn), r3(kpe), r3(vc), tq, tq)
        x2 = _merge(x2, attn_norm[l], w_gate[l], o_a.reshape(n, -1), o_b.reshape(n, -1),
                    o_c.reshape(n, -1), w_up_a[l], w_up_b[l], w_up_c[l], w_out[l], min(256, n))
        x2 = _ffn(x2, ffn_norm[l], w_ffn_gate[l], w_ffn_up[l], w_ffn_down[l], final_norm,
                  min(256, n), final_norm=(l == depth - 1))
    return x2.reshape(bsz, s_len, d_model)
```

```python
import functools
import math

import jax
import jax.numpy as jnp
import numpy as np
from jax import lax
from jax.experimental import pallas as pl
from jax.experimental.pallas import tpu as pltpu

F32 = jnp.float32
BF16 = jnp.bfloat16
I32 = jnp.int32

A_HEADS = 4
A_HEAD_DIM = 128
IDX_HEADS = 8
IDX_DIM = 64
TOPK_MAX = 256
B_HEADS = 4
B_KEY_DIM = 128
B_VAL_DIM = 128
B_CHUNK = 64
C_HEADS = 4
C_Q_RANK = 384
C_KV_RANK = 256
C_NOPE = 128
C_ROPE = 64
C_V = 128
ROPE_THETA = 10000.0
REL_BUCKETS = 32
REL_MAX_DIST = 128
N_BRANCH = 3
EPS = 1e-6
NEG_BIG = -1e30
LB_FLOOR = 1e-30

LANES = 128
SUBLANES = 8
VMEM_LIMIT = 56 * 1024 * 1024
INT_MIN = np.int32(-2 ** 31)

_NT = (((1,), (1,)), ((), ()))


def _params(*sem):
    return pltpu.CompilerParams(dimension_semantics=sem, vmem_limit_bytes=VMEM_LIMIT)


def _rms(x, g):
    return x * lax.rsqrt(jnp.mean(x * x, axis=-1, keepdims=True) + EPS) * g


def _const_spec(shape):
    nd = len(shape)
    return pl.BlockSpec(shape, lambda *_: (0,) * nd)


def _fold8(x, op):
    r, c = x.shape
    return op(x.reshape(r // SUBLANES, SUBLANES, c), axis=0)


def _norm_matmul_kernel(x_ref, g_ref, w_ref, *out_refs, widths, col_chunk):
    h = _rms(x_ref[...], g_ref[...]).astype(BF16)
    c0 = 0
    for o_ref, w in zip(out_refs, widths):
        for s in range(0, w, col_chunk):
            e = min(s + col_chunk, w)
            o_ref[:, s:e] = jnp.dot(h, w_ref[:, c0 + s:c0 + e],
                                    preferred_element_type=F32).astype(o_ref.dtype)
        c0 += w


def _norm_matmul(x, g, w, widths, dtypes, tm):
    n, k = x.shape
    wtot = sum(widths)
    assert w.shape == (k, wtot) and n % tm == 0
    kern = functools.partial(_norm_matmul_kernel, widths=tuple(widths), col_chunk=512)
    return pl.pallas_call(
        kern,
        grid=(n // tm,),
        in_specs=[pl.BlockSpec((tm, k), lambda i: (i, 0)),
                  _const_spec((1, k)),
                  _const_spec((k, wtot))],
        out_specs=[pl.BlockSpec((tm, wd), lambda i: (i, 0)) for wd in widths],
        out_shape=[jax.ShapeDtypeStruct((n, wd), dt) for wd, dt in zip(widths, dtypes)],
        compiler_params=_params("parallel"),
        name="norm_proj",
    )(x, g.reshape(1, k), w)


def _dsa_kernel(qa_ref, iq_ref, iw_ref, ka_ref, vt_ref, ik_ref, bias_ref, o_ref,
                keys_ref, s_ref, acc_ref, qm_ref, qs_ref, wt_ref, thr_ref, ties_ref,
                *, topk, n_blocks, scale, idx_scale):
    t = LANES
    i = pl.program_id(1)
    n_tiles = i + 1
    krow = lax.broadcasted_iota(I32, (t, t), 0)
    qcol = lax.broadcasted_iota(I32, (t, t), 1)
    lane_lo = qcol < IDX_DIM

    for h in range(IDX_HEADS):
        pair = iq_ref[0, :, (h // 2) * t:(h // 2 + 1) * t]
        keep = lane_lo if h % 2 == 0 else jnp.logical_not(lane_lo)
        qm_ref[h * t:(h + 1) * t, :] = jnp.where(keep, pair, jnp.zeros_like(pair))
    for h in range(A_HEADS):
        qs_ref[h * t:(h + 1) * t, :] = qa_ref[0, :, h * t:(h + 1) * t]
    wt_ref[...] = iw_ref[0].T

    def score_tile(j, carry):
        ks = pl.multiple_of(j * t, t)
        lg = lax.dot_general(ik_ref[0, pl.ds(ks, t), :], qm_ref[...], _NT,
                             preferred_element_type=F32)
        acc = jnp.zeros((t, t), F32)
        for h in range(IDX_HEADS):
            acc = acc + jnp.maximum(lg[:, h * t:(h + 1) * t], 0.0) * wt_ref[h:h + 1, :]
        score = acc * idx_scale
        score = jnp.where(krow + j * t <= qcol + i * t, score, NEG_BIG)
        score = jnp.where(score == 0.0, 0.0, score)
        bits = pltpu.bitcast(score, I32)
        keys_ref[j] = bits ^ ((bits >> 31) & np.int32(0x7FFFFFFF))
        return carry

    lax.fori_loop(0, n_tiles, score_tile, 0)

    thr_ref[...] = jnp.full(thr_ref.shape, INT_MIN, I32)
    ties_ref[...] = jnp.zeros(ties_ref.shape, F32)

    def count(n, pred):
        cnt = jnp.zeros((SUBLANES, t), F32)
        for j in range(n):
            cnt = cnt + _fold8(jnp.where(pred(keys_ref[j]), 1.0, 0.0), jnp.sum)
        return jnp.sum(cnt, axis=0, keepdims=True)

    def search(n):
        def search_pass(b, thr):
            cand = thr + lax.shift_left(np.int32(1), 31 - b)
            return jnp.where(count(n, lambda key: key >= cand) >= topk, cand, thr)

        thr = lax.fori_loop(0, 32, search_pass, jnp.full((1, t), INT_MIN, I32))
        thr_ref[...] = jnp.broadcast_to(thr, thr_ref.shape)
        ties_ref[...] = jnp.broadcast_to(topk - count(n, lambda key: key > thr), ties_ref.shape)

    for c in range(n_blocks):
        if (c + 1) * t > topk:
            pl.when(i == c)(functools.partial(search, c + 1))

    thr = thr_ref[0:1, :]
    n_ties = ties_ref[0:1, :]

    r2 = lax.broadcasted_iota(I32, (2 * t, t), 0)
    c2 = lax.broadcasted_iota(I32, (2 * t, t), 1)
    tie_lhs = jnp.where(jnp.logical_or(r2 >= t, c2 < r2), 1.0, 0.0).astype(BF16)

    def sweep_a(j, carry):
        seen, mx = carry
        ks = pl.multiple_of(j * t, t)
        key = keys_ref[j]
        eq = key == thr
        pref = jnp.dot(tie_lhs, jnp.where(eq, 1.0, 0.0).astype(BF16), preferred_element_type=F32)
        rank = jnp.where(key > thr, -1.0, jnp.where(eq, seen + pref[:t], 3e38))
        rank = jnp.where(krow + j * t <= qcol + i * t, rank, 3e38)
        valid = rank < n_ties
        which = jnp.where(j == i, 1, jnp.where(j == i - 1, 0, 2))
        s_all = lax.dot_general(ka_ref[0, pl.ds(ks, t), :], qs_ref[...], _NT,
                                preferred_element_type=F32) * scale + bias_ref[which]
        new_mx = []
        for h in range(A_HEADS):
            s_h = jnp.where(valid, s_all[:, h * t:(h + 1) * t], NEG_BIG)
            s_ref[j, :, h * t:(h + 1) * t] = s_h
            new_mx.append(jnp.maximum(mx[h], _fold8(s_h, jnp.max)))
        return seen + pref[t:t + 1], tuple(new_mx)

    mx0 = tuple(jnp.full((SUBLANES, t), NEG_BIG, F32) for _ in range(A_HEADS))
    _, mx = lax.fori_loop(0, n_tiles, sweep_a, (jnp.zeros((1, t), F32), mx0))
    m_all = jnp.concatenate([jnp.max(m, axis=0, keepdims=True) for m in mx], axis=1)

    acc_ref[...] = jnp.zeros_like(acc_ref)

    def sweep_b(j, l8):
        p = jnp.exp(s_ref[j] - m_all)
        acc_ref[...] += jnp.dot(vt_ref[0, j], p.astype(BF16), preferred_element_type=F32)
        return l8 + _fold8(p, jnp.sum)

    l8 = lax.fori_loop(0, n_tiles, sweep_b, jnp.zeros((SUBLANES, A_HEADS * t), F32))
    out = acc_ref[...] * (1.0 / jnp.sum(l8, axis=0, keepdims=True))
    for h in range(A_HEADS):
        o_ref[0, :, h * t:(h + 1) * t] = out[:, h * t:(h + 1) * t].T.astype(o_ref.dtype)


def _dsa(qa, iq, iw, ka, vt, ik2, bias_tab):
    b, s, _ = qa.shape
    t = LANES
    nb = s // t
    topk = min(TOPK_MAX, s // 4)
    kern = functools.partial(_dsa_kernel, topk=float(topk), n_blocks=nb, scale=A_HEAD_DIM ** -0.5,
                             idx_scale=(IDX_DIM ** -0.5) * (IDX_HEADS ** -0.5))
    qspec = lambda w: pl.BlockSpec((1, t, w), lambda bi, i: (bi, i, 0))
    kspec = pl.BlockSpec((1, s, t), lambda bi, i: (bi, 0, 0))
    return pl.pallas_call(
        kern,
        grid=(b, nb),
        in_specs=[qspec(A_HEADS * t), qspec(IDX_HEADS * IDX_DIM), qspec(t), kspec,
                  pl.BlockSpec((1, nb, t, t), lambda bi, i: (bi, 0, 0, 0)), kspec,
                  _const_spec(bias_tab.shape)],
        out_specs=qspec(A_HEADS * t),
        out_shape=jax.ShapeDtypeStruct((b, s, A_HEADS * t), BF16),
        scratch_shapes=[pltpu.VMEM((nb, t, t), I32),
                        pltpu.VMEM((nb, t, A_HEADS * t), F32),
                        pltpu.VMEM((t, A_HEADS * t), F32),
                        pltpu.VMEM((IDX_HEADS * t, t), BF16),
                        pltpu.VMEM((A_HEADS * t, t), BF16),
                        pltpu.VMEM((t, t), F32),
                        pltpu.VMEM((SUBLANES, t), I32),
                        pltpu.VMEM((SUBLANES, t), F32)],
        compiler_params=_params("parallel", "parallel"),
        name="dsa_attention",
    )(qa, iq, iw, ka, vt, ik2, bias_tab)


def _hgrn_kernel(q_ref, f_ref, i_ref, g_ref, lb_ref, gain_ref, o_ref, state_ref, *, n_chunks):
    c = B_CHUNK
    kd, vd = B_KEY_DIM, B_VAL_DIM

    @pl.when(pl.program_id(1) == 0)
    def _():
        state_ref[...] = jnp.zeros_like(state_ref)

    row = lax.broadcasted_iota(I32, (c, c), 0)
    col = lax.broadcasted_iota(I32, (c, c), 1)
    causal = row >= col
    tril = causal.astype(F32)
    q_scale = kd ** -0.5

    def chunk_body(ci, carry):
        r0 = pl.multiple_of(ci * c, c)
        for h in range(B_HEADS):
            ks = slice(h * kd, (h + 1) * kd)
            vs = slice(h * vd, (h + 1) * vd)
            fr = f_ref[0, pl.ds(r0, c), ks]
            lb = lb_ref[:, ks]
            z = jnp.exp(-jnp.abs(fr))
            r = 1.0 / (1.0 + z)
            sig_pos = jnp.where(fr >= 0, r, z * r)
            sig_neg = jnp.where(fr >= 0, z * r, r)
            log_f = jnp.log(jnp.maximum(lb, LB_FLOOR) + (1.0 - lb) * sig_pos)
            k_in = (1.0 - lb) * sig_neg
            bsum = jnp.dot(tril, log_f, precision=lax.Precision.HIGHEST,
                           preferred_element_type=F32)
            b_mid = bsum[c // 2 - 1:c // 2, :]
            b_last = bsum[c - 1:c, :]
            qs = q_ref[0, pl.ds(r0, c), ks] * q_scale
            v = i_ref[0, pl.ds(r0, c), vs]
            v_b = v.astype(BF16)
            q_mid = (qs * jnp.exp(bsum - b_mid)).astype(BF16)
            k_mid = (k_in * jnp.exp(b_mid - bsum)).astype(BF16)
            attn = lax.dot_general(q_mid, k_mid, _NT, preferred_element_type=F32)
            attn = jnp.where(causal, attn, 0.0).astype(BF16)
            o = jnp.dot(attn, v_b, preferred_element_type=F32)
            st = state_ref[h]
            o = o + lax.dot_general((qs * jnp.exp(bsum)).astype(BF16), st.astype(BF16), _NT,
                                    preferred_element_type=F32)
            k_end = (k_in * jnp.exp(b_last - bsum)).astype(BF16)
            state_ref[h] = jnp.exp(b_last) * st + jnp.dot(v.T.astype(BF16), k_end,
                                                         preferred_element_type=F32)
            g = g_ref[0, pl.ds(r0, c), vs]
            o = _rms(o, gain_ref[...]) * (g * (1.0 / (1.0 + jnp.exp(-g))))
            o_ref[0, pl.ds(r0, c), vs] = o.astype(o_ref.dtype)
        return carry

    lax.fori_loop(0, n_chunks, chunk_body, 0)


def _hgrn(bq, bf, bi, bg, lb, gain, sb):
    b, s, w = bq.shape
    assert s % sb == 0 and sb % B_CHUNK == 0
    kern = functools.partial(_hgrn_kernel, n_chunks=sb // B_CHUNK)
    spec = pl.BlockSpec((1, sb, w), lambda bi_, si: (bi_, si, 0))
    return pl.pallas_call(
        kern,
        grid=(b, s // sb),
        in_specs=[spec, spec, spec, spec, _const_spec((1, w)), _const_spec((1, B_VAL_DIM))],
        out_specs=spec,
        out_shape=jax.ShapeDtypeStruct((b, s, w), BF16),
        scratch_shapes=[pltpu.VMEM((B_HEADS, B_VAL_DIM, B_KEY_DIM), F32)],
        compiler_params=_params("parallel", "arbitrary"),
        name="hgrn2",
    )(bq, bf, bi, bg, lb.reshape(1, w), gain.reshape(1, B_VAL_DIM))


def _mla_prep_kernel(cq_ref, ckv_ref, kpe_raw_ref, cos_ref, sin_ref, gq_ref, gkv_ref, wq_ref, wkv_ref,
                     qn_ref, qpe_ref, kn_ref, v_ref, kpe_ref):
    nw = C_HEADS * C_NOPE
    pw = C_HEADS * C_ROPE
    cos, sin = cos_ref[...], sin_ref[...]
    cos2 = jnp.concatenate([cos] * (pw // LANES), axis=1)
    sin2 = jnp.concatenate([sin] * (pw // LANES), axis=1)
    q = jnp.dot(_rms(cq_ref[...], gq_ref[...]).astype(BF16), wq_ref[...], preferred_element_type=F32)
    qn_ref[...] = q[:, :nw].astype(qn_ref.dtype)
    qpe_ref[...] = (q[:, nw:nw + pw] * cos2 + q[:, nw + pw:] * sin2).astype(qpe_ref.dtype)
    kv = jnp.dot(_rms(ckv_ref[...], gkv_ref[...]).astype(BF16), wkv_ref[...], preferred_element_type=F32)
    kn_ref[...] = kv[:, :nw].astype(kn_ref.dtype)
    v_ref[...] = kv[:, nw:].astype(v_ref.dtype)
    kpe_ref[...] = (kpe_raw_ref[:, :LANES] * cos + kpe_raw_ref[:, LANES:] * sin).astype(kpe_ref.dtype)


def _mla_prep(cq, ckv, kpe_raw, cos_t, sin_t, gq, gkv, wq, wkv, tm):
    n = cq.shape[0]
    nw = C_HEADS * C_NOPE
    row = lambda w: pl.BlockSpec((tm, w), lambda i: (i, 0))
    widths = (nw, C_HEADS * C_ROPE, nw, C_HEADS * C_V, LANES)
    return pl.pallas_call(
        _mla_prep_kernel,
        grid=(n // tm,),
        in_specs=[row(C_Q_RANK), row(C_KV_RANK), row(2 * LANES), row(LANES), row(LANES),
                  _const_spec((1, C_Q_RANK)), _const_spec((1, C_KV_RANK)),
                  _const_spec(wq.shape), _const_spec(wkv.shape)],
        out_specs=[row(w) for w in widths],
        out_shape=[jax.ShapeDtypeStruct((n, w), BF16) for w in widths],
        compiler_params=_params("parallel"),
        name="mla_prep",
    )(cq, ckv, kpe_raw, cos_t, sin_t, gq.reshape(1, -1), gkv.reshape(1, -1), wq, wkv)


def _mla_kernel(qn_ref, qpe_ref, kn_ref, kpe_ref, vt_ref, o_ref, q_ref, acc_ref, *, t, scale):
    i = pl.program_id(1)
    krow = lax.broadcasted_iota(I32, (t, t), 0)
    qcol = lax.broadcasted_iota(I32, (t, t), 1)
    lane_lo = lax.broadcasted_iota(I32, (t, LANES), 1) < C_ROPE
    for h in range(C_HEADS):
        pair = qpe_ref[0, :, (h // 2) * LANES:(h // 2 + 1) * LANES]
        keep = lane_lo if h % 2 == 0 else jnp.logical_not(lane_lo)
        q_ref[h, :, :C_NOPE] = qn_ref[0, :, h * C_NOPE:(h + 1) * C_NOPE]
        q_ref[h, :, C_NOPE:] = jnp.where(keep, pair, jnp.zeros_like(pair))
    acc_ref[...] = jnp.zeros_like(acc_ref)

    def step(j, carry, masked):
        ms, ls = carry
        ks = pl.multiple_of(j * t, t)
        kpe_t = kpe_ref[0, pl.ds(ks, t), :]
        new_m, new_l = [], []
        for h in range(C_HEADS):
            k_h = jnp.concatenate([kn_ref[0, pl.ds(ks, t), h * C_NOPE:(h + 1) * C_NOPE], kpe_t], axis=1)
            s = lax.dot_general(k_h, q_ref[h], _NT, preferred_element_type=F32) * scale
            if masked:
                s = jnp.where(krow <= qcol, s, NEG_BIG)
            m_new = jnp.maximum(ms[h], jnp.max(s, axis=0, keepdims=True))
            alpha = jnp.exp(ms[h] - m_new)
            p = jnp.exp(s - m_new)
            new_l.append(alpha * ls[h] + jnp.sum(p, axis=0, keepdims=True))
            acc_ref[h] = alpha * acc_ref[h] + jnp.dot(vt_ref[0, j, h * C_V:(h + 1) * C_V, :],
                                                      p.astype(BF16), preferred_element_type=F32)
            new_m.append(m_new)
        return tuple(new_m), tuple(new_l)

    init = (tuple(jnp.full((1, t), NEG_BIG, F32) for _ in range(C_HEADS)),
            tuple(jnp.zeros((1, t), F32) for _ in range(C_HEADS)))
    carry = lax.fori_loop(0, i, functools.partial(step, masked=False), init)
    _, ls = step(i, carry, masked=True)
    for h in range(C_HEADS):
        o_ref[0, :, h * C_V:(h + 1) * C_V] = (acc_ref[h] * (1.0 / ls[h])).T.astype(o_ref.dtype)


def _mla(qn, qpe, kn, kpe, vt, t):
    b, s, _ = qn.shape
    nt = s // t
    kern = functools.partial(_mla_kernel, t=t, scale=(C_NOPE + C_ROPE) ** -0.5)
    qspec = lambda w: pl.BlockSpec((1, t, w), lambda bi, i: (bi, i, 0))
    kspec = lambda w: pl.BlockSpec((1, s, w), lambda bi, i: (bi, 0, 0))
    return pl.pallas_call(
        kern,
        grid=(b, nt),
        in_specs=[qspec(qn.shape[2]), qspec(qpe.shape[2]), kspec(kn.shape[2]), kspec(kpe.shape[2]),
                  pl.BlockSpec((1, nt, C_HEADS * C_V, t), lambda bi, i: (bi, 0, 0, 0))],
        out_specs=qspec(C_HEADS * C_V),
        out_shape=jax.ShapeDtypeStruct((b, s, C_HEADS * C_V), BF16),
        scratch_shapes=[pltpu.VMEM((C_HEADS, t, C_NOPE + LANES), BF16),
                        pltpu.VMEM((C_HEADS, C_V, t), F32)],
        compiler_params=_params("parallel", "parallel"),
        name="mla_attention",
    )(qn, qpe, kn, kpe, vt)


def _merge_kernel(x_ref, g_ref, wg_ref, oa_ref, ob_ref, oc_ref, wa_ref, wb_ref, wc_ref, wo_ref,
                  out_ref, mixed_ref, *, col_chunk):
    x = x_ref[...]
    d = x.shape[1]
    h = _rms(x, g_ref[...]).astype(BF16)
    branches = ((oa_ref, wa_ref), (ob_ref, wb_ref), (oc_ref, wc_ref))
    for s in range(0, d, col_chunk):
        cs = slice(s, s + col_chunk)
        mixed = None
        for bidx, (o_ref, w_ref) in enumerate(branches):
            logits = jnp.dot(h, wg_ref[:, bidx * d + s:bidx * d + s + col_chunk],
                             preferred_element_type=F32)
            gate = 1.0 / (1.0 + jnp.exp(-logits))
            term = gate * jnp.dot(o_ref[...], w_ref[:, cs], preferred_element_type=F32)
            mixed = term if mixed is None else mixed + term
        mixed_ref[:, cs] = mixed.astype(BF16)
    out_ref[...] = x + jnp.dot(mixed_ref[...], wo_ref[...], preferred_element_type=F32)


def _merge(x, g, w_gate, oa, ob, oc, wa, wb, wc, wo, tm):
    n, d = x.shape
    row = lambda w: pl.BlockSpec((tm, w), lambda i: (i, 0))
    kern = functools.partial(_merge_kernel, col_chunk=256)
    return pl.pallas_call(
        kern,
        grid=(n // tm,),
        in_specs=[row(d), _const_spec((1, d)), _const_spec(w_gate.shape),
                  row(oa.shape[1]), row(ob.shape[1]), row(oc.shape[1]),
                  _const_spec(wa.shape), _const_spec(wb.shape), _const_spec(wc.shape),
                  _const_spec(wo.shape)],
        out_specs=row(d),
        out_shape=jax.ShapeDtypeStruct((n, d), F32),
        scratch_shapes=[pltpu.VMEM((tm, d), BF16)],
        compiler_params=_params("parallel"),
        name="merge_out_proj",
    )(x, g.reshape(1, d), w_gate, oa, ob, oc, wa, wb, wc, wo)


def _ffn_kernel(x_ref, g_ref, wg_ref, wu_ref, wd_ref, gf_ref, out_ref, act_ref, *, col_chunk,
                final_norm):
    x = x_ref[...]
    h = _rms(x, g_ref[...]).astype(BF16)
    dff = wg_ref.shape[1]
    for s in range(0, dff, col_chunk):
        cs = slice(s, s + col_chunk)
        gate = jnp.dot(h, wg_ref[:, cs], preferred_element_type=F32)
        up = jnp.dot(h, wu_ref[:, cs], preferred_element_type=F32)
        act_ref[:, cs] = (gate * (1.0 / (1.0 + jnp.exp(-gate))) * up).astype(BF16)
    y = x + jnp.dot(act_ref[...], wd_ref[...], preferred_element_type=F32)
    if final_norm:
        y = _rms(y, gf_ref[...])
    out_ref[...] = y


def _ffn(x, g, wg, wu, wd, gf, tm, final_norm):
    n, d = x.shape
    dff = wg.shape[1]
    row = pl.BlockSpec((tm, d), lambda i: (i, 0))
    kern = functools.partial(_ffn_kernel, col_chunk=256, final_norm=final_norm)
    return pl.pallas_call(
        kern,
        grid=(n // tm,),
        in_specs=[row, _const_spec((1, d)), _const_spec(wg.shape), _const_spec(wu.shape),
                  _const_spec(wd.shape), _const_spec((1, d))],
        out_specs=row,
        out_shape=jax.ShapeDtypeStruct((n, d), F32),
        scratch_shapes=[pltpu.VMEM((tm, dff), BF16)],
        compiler_params=_params("parallel"),
        name="swiglu_ffn",
    )(x, g.reshape(1, d), wg, wu, wd, gf.reshape(1, d))


def _t5_bucket(dist):
    max_exact = REL_BUCKETS // 2
    d = jnp.maximum(dist, 0)
    dl = jnp.maximum(d, max_exact).astype(F32)
    large = max_exact + (jnp.log(dl / max_exact) / math.log(REL_MAX_DIST / max_exact)
                         * (REL_BUCKETS - max_exact)).astype(I32)
    large = jnp.minimum(large, REL_BUCKETS - 1)
    return jnp.where(d < max_exact, d, large)


def _swap_halves(cols, width):
    g = np.asarray(cols).reshape(-1, 2, width // 2)
    return g[:, ::-1, :].reshape(-1)


def _in_proj_columns(d_model):
    splits = (A_HEADS * A_HEAD_DIM, A_HEAD_DIM, A_HEAD_DIM, IDX_HEADS * IDX_DIM, IDX_DIM, IDX_HEADS,
              B_HEADS * B_KEY_DIM, B_HEADS * B_KEY_DIM, B_HEADS * B_VAL_DIM, B_HEADS * B_VAL_DIM,
              C_Q_RANK, C_KV_RANK, C_ROPE, N_BRANCH * d_model)
    starts = np.concatenate([[0], np.cumsum(splits)])
    rng = lambda k: np.arange(starts[k], starts[k + 1])
    pe2 = np.tile(rng(12), LANES // C_ROPE)
    cols = [rng(0), rng(1), rng(2), rng(3),
            np.tile(rng(4), LANES // IDX_DIM),
            rng(6), rng(7), rng(8), rng(9), rng(10), rng(11),
            np.concatenate([pe2, _swap_halves(pe2, C_ROPE)])]
    return cols, rng(5), rng(13)


def kernel(x, positions, w_in, w_up_a, w_up_b, w_up_c, w_out, mla_q_norm, mla_w_qb, mla_kv_norm,
           mla_w_kvb, hgrn_lb_logits, hgrn_out_norm, rel_bias, attn_norm, ffn_norm, w_ffn_gate,
           w_ffn_up, w_ffn_down, final_norm):
    bsz, s_len, d_model = x.shape
    depth = w_in.shape[0]
    n = bsz * s_len
    t = LANES

    cols, iw_cols, gate_cols = _in_proj_columns(d_model)
    iw_w = jnp.pad(w_in[:, :, iw_cols], ((0, 0), (0, 0), (0, LANES - IDX_HEADS)))
    w_proj = jnp.concatenate([w_in[:, :, np.concatenate(cols[:5])], iw_w,
                              w_in[:, :, np.concatenate(cols[5:])]], axis=2).astype(BF16)
    widths = [len(c) for c in cols[:5]] + [LANES] + [len(c) for c in cols[5:]]
    dtypes = [BF16] * 5 + [F32] * 8
    w_gate = w_in[:, :, gate_cols].astype(BF16)

    per_head = C_NOPE + C_ROPE
    heads = np.arange(C_HEADS)[:, None]
    q_pe = (heads * per_head + C_NOPE + np.arange(C_ROPE)).ravel()
    q_cols = np.concatenate([(heads * per_head + np.arange(C_NOPE)).ravel(), q_pe,
                             _swap_halves(q_pe, C_ROPE)])
    kv_cols = np.concatenate([(heads * (C_NOPE + C_V) + np.arange(C_NOPE)).ravel(),
                              (heads * (C_NOPE + C_V) + C_NOPE + np.arange(C_V)).ravel()])
    w_qb = mla_w_qb[:, :, q_cols].astype(BF16)
    w_kvb = mla_w_kvb[:, :, kv_cols].astype(BF16)
    bf = lambda w: w.astype(BF16)
    w_up_a, w_up_b, w_up_c, w_out = bf(w_up_a), bf(w_up_b), bf(w_up_c), bf(w_out)
    w_ffn_gate, w_ffn_up, w_ffn_down = bf(w_ffn_gate), bf(w_ffn_up), bf(w_ffn_down)

    p_lb = jax.nn.softmax(hgrn_lb_logits.astype(F32), axis=0)
    lower_bounds = jnp.cumsum(p_lb, axis=0) - p_lb[0:1]
    inv_freq = ROPE_THETA ** (-jnp.arange(0, C_ROPE, 2, dtype=F32) / C_ROPE)
    ang = positions.astype(F32)[..., None] * inv_freq
    cos, sin = jnp.cos(ang), jnp.sin(ang)
    reps = LANES // C_ROPE
    cos_t = jnp.tile(jnp.concatenate([cos, cos], axis=-1), (1, 1, reps)).reshape(n, LANES)
    sin_t = jnp.tile(jnp.concatenate([-sin, sin], axis=-1), (1, 1, reps)).reshape(n, LANES)
    q_idx = jnp.arange(t, dtype=I32)[None, :]
    k_idx = jnp.arange(t, dtype=I32)[:, None]
    tab = lambda dist: rel_bias[_t5_bucket(dist)].astype(F32).transpose(0, 2, 1).reshape(t, A_HEADS * t)
    far = jnp.broadcast_to(rel_bias[REL_BUCKETS - 1].astype(F32)[None, :, None],
                           (t, A_HEADS, t)).reshape(t, A_HEADS * t)
    bias_tab = jnp.stack([tab(q_idx + t - k_idx), tab(q_idx - k_idx), far])

    tm = min(512, n)
    t_mla = min(256, s_len)
    x2 = x.reshape(n, d_model)
    r3 = lambda a: a.reshape(bsz, s_len, a.shape[-1])
    key_major = lambda a, tile: a.reshape(bsz, s_len // tile, tile, a.shape[-1]).swapaxes(2, 3)
    for l in range(depth):
        (qa, ka, va, iq, ik2, iw, bq, bfr, bi, bg, cq, ckv, kpe_raw) = _norm_matmul(
            x2, attn_norm[l], w_proj[l], widths, dtypes, tm)
        o_a = _dsa(r3(qa), r3(iq), r3(iw), r3(ka), key_major(va, t), r3(ik2), bias_tab)
        o_b = _hgrn(r3(bq), r3(bfr), r3(bi), r3(bg), lower_bounds[l], hgrn_out_norm[l],
                    sb=min(512, s_len))
        qn, qpe, kn, vc, kpe = _mla_prep(cq, ckv, kpe_raw, cos_t, sin_t, mla_q_norm[l], mla_kv_norm[l],
                                         w_qb[l], w_kvb[l], tm)
        o_c = _mla(r3(qn), r3(qpe), r3(kn), r3(kpe), key_major(vc, t_mla), t_mla)
        x2 = _merge(x2, attn_norm[l], w_gate[l], o_a.reshape(n, -1), o_b.reshape(n, -1),
                    o_c.reshape(n, -1), w_up_a[l], w_up_b[l], w_up_c[l], w_out[l], min(256, n))
        x2 = _ffn(x2, ffn_norm[l], w_ffn_gate[l], w_ffn_up[l], w_ffn_down[l], final_norm,
                  min(256, n), final_norm=(l == depth - 1))
    return x2.reshape(bsz, s_len, d_model)
```

```python
import functools
import math

import jax
import jax.numpy as jnp
import numpy as np
from jax import lax
from jax.experimental import pallas as pl
from jax.experimental.pallas import tpu as pltpu

F32 = jnp.float32
BF16 = jnp.bfloat16
I32 = jnp.int32

A_HEADS = 4
A_HEAD_DIM = 128
IDX_HEADS = 8
IDX_DIM = 64
TOPK_MAX = 256
B_HEADS = 4
B_KEY_DIM = 128
B_VAL_DIM = 128
B_CHUNK = 64
C_HEADS = 4
C_Q_RANK = 384
C_KV_RANK = 256
C_NOPE = 128
C_ROPE = 64
C_V = 128
ROPE_THETA = 10000.0
REL_BUCKETS = 32
REL_MAX_DIST = 128
N_BRANCH = 3
EPS = 1e-6
NEG_BIG = -1e30
LB_FLOOR = 1e-30

LANES = 128
SUBLANES = 8
VMEM_LIMIT = 56 * 1024 * 1024
INT_MIN = np.int32(-2 ** 31)

_NT = (((1,), (1,)), ((), ()))


def _params(*sem):
    return pltpu.CompilerParams(dimension_semantics=sem, vmem_limit_bytes=VMEM_LIMIT)


def _rms(x, g):
    return x * lax.rsqrt(jnp.mean(x * x, axis=-1, keepdims=True) + EPS) * g


def _const_spec(shape):
    nd = len(shape)
    return pl.BlockSpec(shape, lambda *_: (0,) * nd)


def _fold8(x, op):
    r, c = x.shape
    return op(x.reshape(r // SUBLANES, SUBLANES, c), axis=0)


def _norm_matmul_kernel(x_ref, g_ref, w_ref, *out_refs, widths, col_chunk):
    h = _rms(x_ref[...], g_ref[...]).astype(BF16)
    c0 = 0
    for o_ref, w in zip(out_refs, widths):
        for s in range(0, w, col_chunk):
            e = min(s + col_chunk, w)
            o_ref[:, s:e] = jnp.dot(h, w_ref[:, c0 + s:c0 + e],
                                    preferred_element_type=F32).astype(o_ref.dtype)
        c0 += w


def _norm_matmul(x, g, w, widths, dtypes, tm):
    n, k = x.shape
    wtot = sum(widths)
    assert w.shape == (k, wtot) and n % tm == 0
    kern = functools.partial(_norm_matmul_kernel, widths=tuple(widths), col_chunk=512)
    return pl.pallas_call(
        kern,
        grid=(n // tm,),
        in_specs=[pl.BlockSpec((tm, k), lambda i: (i, 0)),
                  _const_spec((1, k)),
                  _const_spec((k, wtot))],
        out_specs=[pl.BlockSpec((tm, wd), lambda i: (i, 0)) for wd in widths],
        out_shape=[jax.ShapeDtypeStruct((n, wd), dt) for wd, dt in zip(widths, dtypes)],
        compiler_params=_params("parallel"),
        name="norm_proj",
    )(x, g.reshape(1, k), w)


def _dsa_kernel(qa_ref, iq_ref, iw_ref, ka_ref, vt_ref, ik_ref, bias_ref, o_ref,
                keys_ref, s_ref, acc_ref, qm_ref, qs_ref, wt_ref, thr_ref, ties_ref,
                *, topk, n_blocks, group, scale, idx_scale):
    t = LANES
    gt = group * t
    i = pl.program_id(1)
    n_groups = (i + group) // group
    krow = lax.broadcasted_iota(I32, (t, t), 0)
    qcol = lax.broadcasted_iota(I32, (t, t), 1)
    lane_lo = qcol < IDX_DIM

    for h in range(IDX_HEADS):
        pair = iq_ref[0, :, (h // 2) * t:(h // 2 + 1) * t]
        keep = lane_lo if h % 2 == 0 else jnp.logical_not(lane_lo)
        qm_ref[h * t:(h + 1) * t, :] = jnp.where(keep, pair, jnp.zeros_like(pair))
    for h in range(A_HEADS):
        qs_ref[h * t:(h + 1) * t, :] = qa_ref[0, :, h * t:(h + 1) * t]
    wt_ref[...] = iw_ref[0].T

    def score_group(g, carry):
        ks = pl.multiple_of(g * gt, gt)
        lg = lax.dot_general(ik_ref[0, pl.ds(ks, gt), :], qm_ref[...], _NT,
                             preferred_element_type=F32)
        for u in range(group):
            j = g * group + u
            acc = jnp.zeros((t, t), F32)
            for h in range(IDX_HEADS):
                acc = acc + jnp.maximum(lg[u * t:(u + 1) * t, h * t:(h + 1) * t], 0.0) * wt_ref[h:h + 1, :]
            score = acc * idx_scale
            score = jnp.where(krow + j * t <= qcol + i * t, score, NEG_BIG)
            score = jnp.where(score == 0.0, 0.0, score)
            bits = pltpu.bitcast(score, I32)
            keys_ref[j] = bits ^ ((bits >> 31) & np.int32(0x7FFFFFFF))
        return carry

    lax.fori_loop(0, n_groups, score_group, 0)

    thr_ref[...] = jnp.full(thr_ref.shape, INT_MIN, I32)
    ties_ref[...] = jnp.zeros(ties_ref.shape, F32)

    def count(n, pred):
        cnt = jnp.zeros((SUBLANES, t), F32)
        for j in range(n):
            cnt = cnt + _fold8(jnp.where(pred(keys_ref[j]), 1.0, 0.0), jnp.sum)
        return jnp.sum(cnt, axis=0, keepdims=True)

    def search(n):
        def search_pass(b, thr):
            cand = thr + lax.shift_left(np.int32(1), 31 - b)
            return jnp.where(count(n, lambda key: key >= cand) >= topk, cand, thr)

        thr = lax.fori_loop(0, 32, search_pass, jnp.full((1, t), INT_MIN, I32))
        thr_ref[...] = jnp.broadcast_to(thr, thr_ref.shape)
        ties_ref[...] = jnp.broadcast_to(topk - count(n, lambda key: key > thr), ties_ref.shape)

    for c in range(n_blocks):
        if (c + 1) * t > topk:
            pl.when(i == c)(functools.partial(search, c + 1))

    thr = thr_ref[0:1, :]
    n_ties = ties_ref[0:1, :]

    r2 = lax.broadcasted_iota(I32, (2 * t, t), 0)
    c2 = lax.broadcasted_iota(I32, (2 * t, t), 1)
    tie_lhs = jnp.where(jnp.logical_or(r2 >= t, c2 < r2), 1.0, 0.0).astype(BF16)

    def sweep_a(g, carry):
        seen, mx = carry
        mx = list(mx)
        ks = pl.multiple_of(g * gt, gt)
        keys = [keys_ref[g * group + u] for u in range(group)]
        eqs = [key == thr for key in keys]
        eq_all = jnp.concatenate([jnp.where(eq, 1.0, 0.0).astype(BF16) for eq in eqs], axis=1)
        pref = jnp.dot(tie_lhs, eq_all, preferred_element_type=F32)
        s_grp = lax.dot_general(ka_ref[0, pl.ds(ks, gt), :], qs_ref[...], _NT,
                                preferred_element_type=F32)
        for u in range(group):
            j = g * group + u
            us = slice(u * t, (u + 1) * t)
            rank = jnp.where(keys[u] > thr, -1.0, jnp.where(eqs[u], seen + pref[:t, us], 3e38))
            rank = jnp.where(krow + j * t <= qcol + i * t, rank, 3e38)
            valid = rank < n_ties
            which = jnp.where(j == i, 1, jnp.where(j == i - 1, 0, 2))
            s_all = s_grp[us, :] * scale + bias_ref[which]
            for h in range(A_HEADS):
                s_h = jnp.where(valid, s_all[:, h * t:(h + 1) * t], NEG_BIG)
                s_ref[j, :, h * t:(h + 1) * t] = s_h
                mx[h] = jnp.maximum(mx[h], _fold8(s_h, jnp.max))
            seen = seen + pref[t:t + 1, us]
        return seen, tuple(mx)

    mx0 = tuple(jnp.full((SUBLANES, t), NEG_BIG, F32) for _ in range(A_HEADS))
    _, mx = lax.fori_loop(0, n_groups, sweep_a, (jnp.zeros((1, t), F32), mx0))
    m_all = jnp.concatenate([jnp.max(m, axis=0, keepdims=True) for m in mx], axis=1)

    acc_ref[...] = jnp.zeros_like(acc_ref)

    def sweep_b(g, l8):
        ps = []
        for u in range(group):
            p = jnp.exp(s_ref[g * group + u] - m_all)
            l8 = l8 + _fold8(p, jnp.sum)
            ps.append(p.astype(BF16))
        vt_grp = jnp.concatenate([vt_ref[0, g * group + u] for u in range(group)], axis=1)
        acc_ref[...] += jnp.dot(vt_grp, jnp.concatenate(ps, axis=0), preferred_element_type=F32)
        return l8

    l8 = lax.fori_loop(0, n_groups, sweep_b, jnp.zeros((SUBLANES, A_HEADS * t), F32))
    out = acc_ref[...] * (1.0 / jnp.sum(l8, axis=0, keepdims=True))
    for h in range(A_HEADS):
        o_ref[0, :, h * t:(h + 1) * t] = out[:, h * t:(h + 1) * t].T.astype(o_ref.dtype)


def _dsa(qa, iq, iw, ka, vt, ik2, bias_tab):
    b, s, _ = qa.shape
    t = LANES
    nb = s // t
    group = math.gcd(nb, 4)
    topk = min(TOPK_MAX, s // 4)
    kern = functools.partial(_dsa_kernel, topk=float(topk), n_blocks=nb, group=group,
                             scale=A_HEAD_DIM ** -0.5,
                             idx_scale=(IDX_DIM ** -0.5) * (IDX_HEADS ** -0.5))
    qspec = lambda w: pl.BlockSpec((1, t, w), lambda bi, i: (bi, i, 0))
    kspec = pl.BlockSpec((1, s, t), lambda bi, i: (bi, 0, 0))
    return pl.pallas_call(
        kern,
        grid=(b, nb),
        in_specs=[qspec(A_HEADS * t), qspec(IDX_HEADS * IDX_DIM), qspec(t), kspec,
                  pl.BlockSpec((1, nb, t, t), lambda bi, i: (bi, 0, 0, 0)), kspec,
                  _const_spec(bias_tab.shape)],
        out_specs=qspec(A_HEADS * t),
        out_shape=jax.ShapeDtypeStruct((b, s, A_HEADS * t), BF16),
        scratch_shapes=[pltpu.VMEM((nb, t, t), I32),
                        pltpu.VMEM((nb, t, A_HEADS * t), F32),
                        pltpu.VMEM((t, A_HEADS * t), F32),
                        pltpu.VMEM((IDX_HEADS * t, t), BF16),
                        pltpu.VMEM((A_HEADS * t, t), BF16),
                        pltpu.VMEM((t, t), F32),
                        pltpu.VMEM((SUBLANES, t), I32),
                        pltpu.VMEM((SUBLANES, t), F32)],
        compiler_params=_params("parallel", "parallel"),
        name="dsa_attention",
    )(qa, iq, iw, ka, vt, ik2, bias_tab)


def _hgrn_kernel(q_ref, f_ref, i_ref, g_ref, lb_ref, gain_ref, o_ref, state_ref, *, n_chunks):
    c = B_CHUNK
    kd, vd = B_KEY_DIM, B_VAL_DIM
    w = B_HEADS * kd

    @pl.when(pl.program_id(1) == 0)
    def _():
        state_ref[...] = jnp.zeros_like(state_ref)

    row = lax.broadcasted_iota(I32, (c, c), 0)
    col = lax.broadcasted_iota(I32, (c, c), 1)
    causal = row >= col
    tril = jnp.where(causal, 1.0, 0.0).astype(BF16)
    q_scale = kd ** -0.5
    lb = lb_ref[...]
    lb_floor = jnp.maximum(lb, LB_FLOOR)
    one_m_lb = 1.0 - lb
    heads = [slice(h * kd, (h + 1) * kd) for h in range(B_HEADS)]

    def chunk_body(ci, carry):
        rows = pl.ds(pl.multiple_of(ci * c, c), c)
        fr = f_ref[0, rows, :]
        z = jnp.exp(-jnp.abs(fr))
        r = 1.0 / (1.0 + z)
        sig_pos = jnp.where(fr >= 0, r, z * r)
        sig_neg = jnp.where(fr >= 0, z * r, r)
        log_f = jnp.log(lb_floor + one_m_lb * sig_pos)
        k_in = one_m_lb * sig_neg
        hi = log_f.astype(BF16)
        rest = log_f - hi.astype(F32)
        mid = rest.astype(BF16)
        lo = (rest - mid.astype(F32)).astype(BF16)
        cs = jnp.dot(tril, jnp.concatenate([hi, mid, lo], axis=1), preferred_element_type=F32)
        bsum = cs[:, :w] + cs[:, w:2 * w] + cs[:, 2 * w:]
        b_mid = bsum[c // 2 - 1:c // 2, :]
        b_last = bsum[c - 1:c, :]
        qs = q_ref[0, rows, :] * q_scale
        v = i_ref[0, rows, :]
        v_b = v.astype(BF16)
        v_t = v.T.astype(BF16)
        q_mid = (qs * jnp.exp(bsum - b_mid)).astype(BF16)
        k_mid = (k_in * jnp.exp(b_mid - bsum)).astype(BF16)
        q_dec = (qs * jnp.exp(bsum)).astype(BF16)
        k_end = (k_in * jnp.exp(b_last - bsum)).astype(BF16)
        d_last = jnp.exp(b_last)
        st = [state_ref[h] for h in range(B_HEADS)]
        attn = [lax.dot_general(q_mid[:, hs], k_mid[:, hs], _NT, preferred_element_type=F32)
                for hs in heads]
        inter = [lax.dot_general(q_dec[:, hs], st[h].astype(BF16), _NT, preferred_element_type=F32)
                 for h, hs in enumerate(heads)]
        upd = [jnp.dot(v_t[hs, :], k_end[:, hs], preferred_element_type=F32) for hs in heads]
        attn = [jnp.where(causal, a, 0.0).astype(BF16) for a in attn]
        outs = [jnp.dot(attn[h], v_b[:, hs], preferred_element_type=F32) + inter[h]
                for h, hs in enumerate(heads)]
        for h, hs in enumerate(heads):
            state_ref[h] = d_last[:, hs] * st[h] + upd[h]
        o = jnp.concatenate([_rms(o_h, gain_ref[...]) for o_h in outs], axis=1)
        g = g_ref[0, rows, :]
        o_ref[0, rows, :] = (o * (g * (1.0 / (1.0 + jnp.exp(-g))))).astype(o_ref.dtype)
        return carry

    lax.fori_loop(0, n_chunks, chunk_body, 0)


def _hgrn(bq, bf, bi, bg, lb, gain, sb):
    b, s, w = bq.shape
    assert s % sb == 0 and sb % B_CHUNK == 0
    kern = functools.partial(_hgrn_kernel, n_chunks=sb // B_CHUNK)
    spec = pl.BlockSpec((1, sb, w), lambda bi_, si: (bi_, si, 0))
    return pl.pallas_call(
        kern,
        grid=(b, s // sb),
        in_specs=[spec, spec, spec, spec, _const_spec((1, w)), _const_spec((1, B_VAL_DIM))],
        out_specs=spec,
        out_shape=jax.ShapeDtypeStruct((b, s, w), BF16),
        scratch_shapes=[pltpu.VMEM((B_HEADS, B_VAL_DIM, B_KEY_DIM), F32)],
        compiler_params=_params("parallel", "arbitrary"),
        name="hgrn2",
    )(bq, bf, bi, bg, lb.reshape(1, w), gain.reshape(1, B_VAL_DIM))


def _mla_prep_kernel(cq_ref, ckv_ref, kpe_raw_ref, cos_ref, sin_ref, gq_ref, gkv_ref, wq_ref, wkv_ref,
                     qn_ref, qpe_ref, kn_ref, v_ref, kpe_ref):
    nw = C_HEADS * C_NOPE
    pw = C_HEADS * C_ROPE
    cos, sin = cos_ref[...], sin_ref[...]
    cos2 = jnp.concatenate([cos] * (pw // LANES), axis=1)
    sin2 = jnp.concatenate([sin] * (pw // LANES), axis=1)
    q = jnp.dot(_rms(cq_ref[...], gq_ref[...]).astype(BF16), wq_ref[...], preferred_element_type=F32)
    qn_ref[...] = q[:, :nw].astype(qn_ref.dtype)
    qpe_ref[...] = (q[:, nw:nw + pw] * cos2 + q[:, nw + pw:] * sin2).astype(qpe_ref.dtype)
    kv = jnp.dot(_rms(ckv_ref[...], gkv_ref[...]).astype(BF16), wkv_ref[...], preferred_element_type=F32)
    kn_ref[...] = kv[:, :nw].astype(kn_ref.dtype)
    v_ref[...] = kv[:, nw:].astype(v_ref.dtype)
    kpe_ref[...] = (kpe_raw_ref[:, :LANES] * cos + kpe_raw_ref[:, LANES:] * sin).astype(kpe_ref.dtype)


def _mla_prep(cq, ckv, kpe_raw, cos_t, sin_t, gq, gkv, wq, wkv, tm):
    n = cq.shape[0]
    nw = C_HEADS * C_NOPE
    row = lambda w: pl.BlockSpec((tm, w), lambda i: (i, 0))
    widths = (nw, C_HEADS * C_ROPE, nw, C_HEADS * C_V, LANES)
    return pl.pallas_call(
        _mla_prep_kernel,
        grid=(n // tm,),
        in_specs=[row(C_Q_RANK), row(C_KV_RANK), row(2 * LANES), row(LANES), row(LANES),
                  _const_spec((1, C_Q_RANK)), _const_spec((1, C_KV_RANK)),
                  _const_spec(wq.shape), _const_spec(wkv.shape)],
        out_specs=[row(w) for w in widths],
        out_shape=[jax.ShapeDtypeStruct((n, w), BF16) for w in widths],
        compiler_params=_params("parallel"),
        name="mla_prep",
    )(cq, ckv, kpe_raw, cos_t, sin_t, gq.reshape(1, -1), gkv.reshape(1, -1), wq, wkv)


def _mla_kernel(qn_ref, qpe_ref, kn_ref, kpe_ref, vt_ref, o_ref, q_ref, acc_ref, *, t, scale):
    i = pl.program_id(1)
    krow = lax.broadcasted_iota(I32, (t, t), 0)
    qcol = lax.broadcasted_iota(I32, (t, t), 1)
    lane_lo = lax.broadcasted_iota(I32, (t, LANES), 1) < C_ROPE
    for h in range(C_HEADS):
        pair = qpe_ref[0, :, (h // 2) * LANES:(h // 2 + 1) * LANES]
        keep = lane_lo if h % 2 == 0 else jnp.logical_not(lane_lo)
        q_ref[h, :, :C_NOPE] = qn_ref[0, :, h * C_NOPE:(h + 1) * C_NOPE]
        q_ref[h, :, C_NOPE:] = jnp.where(keep, pair, jnp.zeros_like(pair))
    acc_ref[...] = jnp.zeros_like(acc_ref)

    def step(j, carry, masked):
        ms, ls = carry
        ks = pl.multiple_of(j * t, t)
        kpe_t = kpe_ref[0, pl.ds(ks, t), :]
        logits = []
        for h in range(C_HEADS):
            k_h = jnp.concatenate([kn_ref[0, pl.ds(ks, t), h * C_NOPE:(h + 1) * C_NOPE], kpe_t], axis=1)
            logits.append(lax.dot_general(k_h, q_ref[h], _NT, preferred_element_type=F32))
        new_m, new_l, alphas, probs = [], [], [], []
        for h in range(C_HEADS):
            s = logits[h] * scale
            if masked:
                s = jnp.where(krow <= qcol, s, NEG_BIG)
            m_new = jnp.maximum(ms[h], jnp.max(s, axis=0, keepdims=True))
            alpha = jnp.exp(ms[h] - m_new)
            p = jnp.exp(s - m_new)
            new_l.append(alpha * ls[h] + jnp.sum(p, axis=0, keepdims=True))
            new_m.append(m_new)
            alphas.append(alpha)
            probs.append(p.astype(BF16))
        for h in range(C_HEADS):
            acc_ref[h] = alphas[h] * acc_ref[h] + jnp.dot(vt_ref[0, j, h * C_V:(h + 1) * C_V, :],
                                                          probs[h], preferred_element_type=F32)
        return tuple(new_m), tuple(new_l)

    init = (tuple(jnp.full((1, t), NEG_BIG, F32) for _ in range(C_HEADS)),
            tuple(jnp.zeros((1, t), F32) for _ in range(C_HEADS)))
    carry = lax.fori_loop(0, i, functools.partial(step, masked=False), init)
    _, ls = step(i, carry, masked=True)
    for h in range(C_HEADS):
        o_ref[0, :, h * C_V:(h + 1) * C_V] = (acc_ref[h] * (1.0 / ls[h])).T.astype(o_ref.dtype)


def _mla(qn, qpe, kn, kpe, vt, t):
    b, s, _ = qn.shape
    nt = s // t
    kern = functools.partial(_mla_kernel, t=t, scale=(C_NOPE + C_ROPE) ** -0.5)
    qspec = lambda w: pl.BlockSpec((1, t, w), lambda bi, i: (bi, i, 0))
    kspec = lambda w: pl.BlockSpec((1, s, w), lambda bi, i: (bi, 0, 0))
    return pl.pallas_call(
        kern,
        grid=(b, nt),
        in_specs=[qspec(qn.shape[2]), qspec(qpe.shape[2]), kspec(kn.shape[2]), kspec(kpe.shape[2]),
                  pl.BlockSpec((1, nt, C_HEADS * C_V, t), lambda bi, i: (bi, 0, 0, 0))],
        out_specs=qspec(C_HEADS * C_V),
        out_shape=jax.ShapeDtypeStruct((b, s, C_HEADS * C_V), BF16),
        scratch_shapes=[pltpu.VMEM((C_HEADS, t, C_NOPE + LANES), BF16),
                        pltpu.VMEM((C_HEADS, C_V, t), F32)],
        compiler_params=_params("parallel", "parallel"),
        name="mla_attention",
    )(qn, qpe, kn, kpe, vt)


def _merge_kernel(x_ref, g_ref, wg_ref, oa_ref, ob_ref, oc_ref, wa_ref, wb_ref, wc_ref, wo_ref,
                  out_ref, mixed_ref, *, col_chunk):
    x = x_ref[...]
    d = x.shape[1]
    h = _rms(x, g_ref[...]).astype(BF16)
    branches = ((oa_ref, wa_ref), (ob_ref, wb_ref), (oc_ref, wc_ref))
    for s in range(0, d, col_chunk):
        cs = slice(s, s + col_chunk)
        mixed = None
        for bidx, (o_ref, w_ref) in enumerate(branches):
            logits = jnp.dot(h, wg_ref[:, bidx * d + s:bidx * d + s + col_chunk],
                             preferred_element_type=F32)
            gate = 1.0 / (1.0 + jnp.exp(-logits))
            term = gate * jnp.dot(o_ref[...], w_ref[:, cs], preferred_element_type=F32)
            mixed = term if mixed is None else mixed + term
        mixed_ref[:, cs] = mixed.astype(BF16)
    out_ref[...] = x + jnp.dot(mixed_ref[...], wo_ref[...], preferred_element_type=F32)


def _merge(x, g, w_gate, oa, ob, oc, wa, wb, wc, wo, tm):
    n, d = x.shape
    row = lambda w: pl.BlockSpec((tm, w), lambda i: (i, 0))
    kern = functools.partial(_merge_kernel, col_chunk=256)
    return pl.pallas_call(
        kern,
        grid=(n // tm,),
        in_specs=[row(d), _const_spec((1, d)), _const_spec(w_gate.shape),
                  row(oa.shape[1]), row(ob.shape[1]), row(oc.shape[1]),
                  _const_spec(wa.shape), _const_spec(wb.shape), _const_spec(wc.shape),
                  _const_spec(wo.shape)],
        out_specs=row(d),
        out_shape=jax.ShapeDtypeStruct((n, d), F32),
        scratch_shapes=[pltpu.VMEM((tm, d), BF16)],
        compiler_params=_params("parallel"),
        name="merge_out_proj",
    )(x, g.reshape(1, d), w_gate, oa, ob, oc, wa, wb, wc, wo)


def _ffn_kernel(x_ref, g_ref, wg_ref, wu_ref, wd_ref, gf_ref, out_ref, act_ref, *, col_chunk,
                final_norm):
    x = x_ref[...]
    h = _rms(x, g_ref[...]).astype(BF16)
    dff = wg_ref.shape[1]
    for s in range(0, dff, col_chunk):
        cs = slice(s, s + col_chunk)
        gate = jnp.dot(h, wg_ref[:, cs], preferred_element_type=F32)
        up = jnp.dot(h, wu_ref[:, cs], preferred_element_type=F32)
        act_ref[:, cs] = (gate * (1.0 / (1.0 + jnp.exp(-gate))) * up).astype(BF16)
    y = x + jnp.dot(act_ref[...], wd_ref[...], preferred_element_type=F32)
    if final_norm:
        y = _rms(y, gf_ref[...])
    out_ref[...] = y


def _ffn(x, g, wg, wu, wd, gf, tm, final_norm):
    n, d = x.shape
    dff = wg.shape[1]
    row = pl.BlockSpec((tm, d), lambda i: (i, 0))
    kern = functools.partial(_ffn_kernel, col_chunk=256, final_norm=final_norm)
    return pl.pallas_call(
        kern,
        grid=(n // tm,),
        in_specs=[row, _const_spec((1, d)), _const_spec(wg.shape), _const_spec(wu.shape),
                  _const_spec(wd.shape), _const_spec((1, d))],
        out_specs=row,
        out_shape=jax.ShapeDtypeStruct((n, d), F32),
        scratch_shapes=[pltpu.VMEM((tm, dff), BF16)],
        compiler_params=_params("parallel"),
        name="swiglu_ffn",
    )(x, g.reshape(1, d), wg, wu, wd, gf.reshape(1, d))


def _t5_bucket(dist):
    max_exact = REL_BUCKETS // 2
    d = jnp.maximum(dist, 0)
    dl = jnp.maximum(d, max_exact).astype(F32)
    large = max_exact + (jnp.log(dl / max_exact) / math.log(REL_MAX_DIST / max_exact)
                         * (REL_BUCKETS - max_exact)).astype(I32)
    large = jnp.minimum(large, REL_BUCKETS - 1)
    return jnp.where(d < max_exact, d, large)


def _swap_halves(w):
    half = w.shape[-1] // 2
    return jnp.concatenate([w[..., half:], w[..., :half]], axis=-1)


def _in_proj_weights(w_in, d_model):
    splits = (A_HEADS * A_HEAD_DIM, A_HEAD_DIM, A_HEAD_DIM, IDX_HEADS * IDX_DIM, IDX_DIM, IDX_HEADS,
              B_HEADS * B_KEY_DIM, B_HEADS * B_KEY_DIM, B_HEADS * B_VAL_DIM, B_HEADS * B_VAL_DIM,
              C_Q_RANK, C_KV_RANK, C_ROPE, N_BRANCH * d_model)
    st = np.concatenate([[0], np.cumsum(splits)])
    col = lambda a, b: w_in[:, :, st[a]:st[b]]
    ik, pe = col(4, 5), col(12, 13)
    iw = jnp.pad(col(5, 6), ((0, 0), (0, 0), (0, LANES - IDX_HEADS)))
    pe_sw = _swap_halves(pe)
    pieces = [col(0, 4), ik, ik, iw, col(6, 12)] + [pe] * (LANES // C_ROPE) + [pe_sw] * (LANES // C_ROPE)
    w_proj = jnp.concatenate(pieces, axis=2).astype(BF16)
    widths = list(splits[:4]) + [LANES, LANES] + list(splits[6:12]) + [2 * LANES]
    dtypes = [BF16] * 5 + [F32] * 8
    return w_proj, widths, dtypes, col(13, 14).astype(BF16)


def kernel(x, positions, w_in, w_up_a, w_up_b, w_up_c, w_out, mla_q_norm, mla_w_qb, mla_kv_norm,
           mla_w_kvb, hgrn_lb_logits, hgrn_out_norm, rel_bias, attn_norm, ffn_norm, w_ffn_gate,
           w_ffn_up, w_ffn_down, final_norm):
    bsz, s_len, d_model = x.shape
    depth = w_in.shape[0]
    n = bsz * s_len
    t = LANES

    w_proj, widths, dtypes, w_gate = _in_proj_weights(w_in, d_model)
    wq = mla_w_qb.reshape(depth, C_Q_RANK, C_HEADS, C_NOPE + C_ROPE)
    q_pe = wq[..., C_NOPE:]
    w_qb = jnp.concatenate([wq[..., :C_NOPE].reshape(depth, C_Q_RANK, -1),
                            q_pe.reshape(depth, C_Q_RANK, -1),
                            _swap_halves(q_pe).reshape(depth, C_Q_RANK, -1)], axis=2).astype(BF16)
    wkv = mla_w_kvb.reshape(depth, C_KV_RANK, C_HEADS, C_NOPE + C_V)
    w_kvb = jnp.concatenate([wkv[..., :C_NOPE].reshape(depth, C_KV_RANK, -1),
                             wkv[..., C_NOPE:].reshape(depth, C_KV_RANK, -1)], axis=2).astype(BF16)
    bf = lambda w: w.astype(BF16)
    w_up_a, w_up_b, w_up_c, w_out = bf(w_up_a), bf(w_up_b), bf(w_up_c), bf(w_out)
    w_ffn_gate, w_ffn_up, w_ffn_down = bf(w_ffn_gate), bf(w_ffn_up), bf(w_ffn_down)

    p_lb = jax.nn.softmax(hgrn_lb_logits.astype(F32), axis=0)
    lower_bounds = jnp.cumsum(p_lb, axis=0) - p_lb[0:1]
    inv_freq = ROPE_THETA ** (-jnp.arange(0, C_ROPE, 2, dtype=F32) / C_ROPE)
    ang = positions.astype(F32)[..., None] * inv_freq
    cos, sin = jnp.cos(ang), jnp.sin(ang)
    reps = LANES // C_ROPE
    cos_t = jnp.tile(jnp.concatenate([cos, cos], axis=-1), (1, 1, reps)).reshape(n, LANES)
    sin_t = jnp.tile(jnp.concatenate([-sin, sin], axis=-1), (1, 1, reps)).reshape(n, LANES)
    q_idx = jnp.arange(t, dtype=I32)[None, :]
    k_idx = jnp.arange(t, dtype=I32)[:, None]
    tab = lambda dist: rel_bias[_t5_bucket(dist)].astype(F32).transpose(0, 2, 1).reshape(t, A_HEADS * t)
    far = jnp.broadcast_to(rel_bias[REL_BUCKETS - 1].astype(F32)[None, :, None],
                           (t, A_HEADS, t)).reshape(t, A_HEADS * t)
    bias_tab = jnp.stack([tab(q_idx + t - k_idx), tab(q_idx - k_idx), far])

    tm = min(512, n)
    t_mla = min(256, s_len)
    x2 = x.reshape(n, d_model)
    r3 = lambda a: a.reshape(bsz, s_len, a.shape[-1])
    key_major = lambda a, tile: a.reshape(bsz, s_len // tile, tile, a.shape[-1]).swapaxes(2, 3)
    for l in range(depth):
        (qa, ka, va, iq, ik2, iw, bq, bfr, bi, bg, cq, ckv, kpe_raw) = _norm_matmul(
            x2, attn_norm[l], w_proj[l], widths, dtypes, tm)
        o_a = _dsa(r3(qa), r3(iq), r3(iw), r3(ka), key_major(va, t), r3(ik2), bias_tab)
        o_b = _hgrn(r3(bq), r3(bfr), r3(bi), r3(bg), lower_bounds[l], hgrn_out_norm[l],
                    sb=min(512, s_len))
        qn, qpe, kn, vc, kpe = _mla_prep(cq, ckv, kpe_raw, cos_t, sin_t, mla_q_norm[l], mla_kv_norm[l],
                                         w_qb[l], w_kvb[l], tm)
        o_c = _mla(r3(qn), r3(qpe), r3(kn), r3(kpe), key_major(vc, t_mla), t_mla)
        x2 = _merge(x2, attn_norm[l], w_gate[l], o_a.reshape(n, -1), o_b.reshape(n, -1),
                    o_c.reshape(n, -1), w_up_a[l], w_up_b[l], w_up_c[l], w_out[l], min(256, n))
        x2 = _ffn(x2, ffn_norm[l], w_ffn_gate[l], w_ffn_up[l], w_ffn_down[l], final_norm,
                  min(256, n), final_norm=(l == depth - 1))
    return x2.reshape(bsz, s_len, d_model)
```

```python
import functools
import math

import jax
import jax.numpy as jnp
import numpy as np
from jax import lax
from jax.experimental import pallas as pl
from jax.experimental.pallas import tpu as pltpu

F32 = jnp.float32
BF16 = jnp.bfloat16
I32 = jnp.int32

A_HEADS = 4
A_HEAD_DIM = 128
IDX_HEADS = 8
IDX_DIM = 64
TOPK_MAX = 256
B_HEADS = 4
B_KEY_DIM = 128
B_VAL_DIM = 128
B_CHUNK = 64
C_HEADS = 4
C_Q_RANK = 384
C_KV_RANK = 256
C_NOPE = 128
C_ROPE = 64
C_V = 128
ROPE_THETA = 10000.0
REL_BUCKETS = 32
REL_MAX_DIST = 128
N_BRANCH = 3
EPS = 1e-6
NEG_BIG = -1e30
LB_FLOOR = 1e-30

LANES = 128
SUBLANES = 8
VMEM_LIMIT = 56 * 1024 * 1024
INT_MIN = np.int32(-2 ** 31)

_NT = (((1,), (1,)), ((), ()))


def _params(*sem):
    return pltpu.CompilerParams(dimension_semantics=sem, vmem_limit_bytes=VMEM_LIMIT)


def _rms(x, g):
    return x * lax.rsqrt(jnp.mean(x * x, axis=-1, keepdims=True) + EPS) * g


def _const_spec(shape):
    nd = len(shape)
    return pl.BlockSpec(shape, lambda *_: (0,) * nd)


def _fold8(x, op):
    r, c = x.shape
    return op(x.reshape(r // SUBLANES, SUBLANES, c), axis=0)


def _norm_matmul_kernel(x_ref, g_ref, w_ref, *out_refs, widths, col_chunk):
    h = _rms(x_ref[...], g_ref[...]).astype(BF16)
    c0 = 0
    for o_ref, w in zip(out_refs, widths):
        for s in range(0, w, col_chunk):
            e = min(s + col_chunk, w)
            o_ref[:, s:e] = jnp.dot(h, w_ref[:, c0 + s:c0 + e],
                                    preferred_element_type=F32).astype(o_ref.dtype)
        c0 += w


def _norm_matmul(x, g, w, widths, dtypes, tm):
    n, k = x.shape
    wtot = sum(widths)
    assert w.shape == (k, wtot) and n % tm == 0
    kern = functools.partial(_norm_matmul_kernel, widths=tuple(widths), col_chunk=512)
    return pl.pallas_call(
        kern,
        grid=(n // tm,),
        in_specs=[pl.BlockSpec((tm, k), lambda i: (i, 0)),
                  _const_spec((1, k)),
                  _const_spec((k, wtot))],
        out_specs=[pl.BlockSpec((tm, wd), lambda i: (i, 0)) for wd in widths],
        out_shape=[jax.ShapeDtypeStruct((n, wd), dt) for wd, dt in zip(widths, dtypes)],
        compiler_params=_params("parallel"),
        name="norm_proj",
    )(x, g.reshape(1, k), w)


def _dsa_kernel(qa_ref, iq_ref, iw_ref, ka_ref, vt_ref, ik_ref, bias_ref, o_ref,
                keys_ref, s_ref, acc_ref, qm_ref, qs_ref, wt_ref, thr_ref, ties_ref,
                *, topk, n_blocks, group, scale, idx_scale):
    t = LANES
    gt = group * t
    i = pl.program_id(1)
    n_groups = (i + group) // group
    krow = lax.broadcasted_iota(I32, (t, t), 0)
    qcol = lax.broadcasted_iota(I32, (t, t), 1)
    lane_lo = qcol < IDX_DIM

    for h in range(IDX_HEADS):
        pair = iq_ref[0, :, (h // 2) * t:(h // 2 + 1) * t]
        keep = lane_lo if h % 2 == 0 else jnp.logical_not(lane_lo)
        qm_ref[h * t:(h + 1) * t, :] = jnp.where(keep, pair, jnp.zeros_like(pair))
    for h in range(A_HEADS):
        qs_ref[h * t:(h + 1) * t, :] = qa_ref[0, :, h * t:(h + 1) * t]
    wt_ref[...] = iw_ref[0].T

    def score_group(g, carry):
        ks = pl.multiple_of(g * gt, gt)
        lg = lax.dot_general(ik_ref[0, pl.ds(ks, gt), :], qm_ref[...], _NT,
                             preferred_element_type=F32)
        for u in range(group):
            j = g * group + u
            acc = jnp.zeros((t, t), F32)
            for h in range(IDX_HEADS):
                acc = acc + jnp.maximum(lg[u * t:(u + 1) * t, h * t:(h + 1) * t], 0.0) * wt_ref[h:h + 1, :]
            score = acc * idx_scale
            score = jnp.where(krow + j * t <= qcol + i * t, score, NEG_BIG)
            score = jnp.where(score == 0.0, 0.0, score)
            bits = pltpu.bitcast(score, I32)
            keys_ref[j] = bits ^ ((bits >> 31) & np.int32(0x7FFFFFFF))
        return carry

    lax.fori_loop(0, n_groups, score_group, 0)

    thr_ref[...] = jnp.full(thr_ref.shape, INT_MIN, I32)
    ties_ref[...] = jnp.zeros(ties_ref.shape, F32)

    def count(n, pred):
        cnt = jnp.zeros((SUBLANES, t), F32)
        for j in range(n):
            cnt = cnt + _fold8(jnp.where(pred(keys_ref[j]), 1.0, 0.0), jnp.sum)
        return jnp.sum(cnt, axis=0, keepdims=True)

    def search(n):
        def search_pass(b, thr):
            cand = thr + lax.shift_left(np.int32(1), 31 - b)
            return jnp.where(count(n, lambda key: key >= cand) >= topk, cand, thr)

        thr = lax.fori_loop(0, 32, search_pass, jnp.full((1, t), INT_MIN, I32))
        thr_ref[...] = jnp.broadcast_to(thr, thr_ref.shape)
        ties_ref[...] = jnp.broadcast_to(topk - count(n, lambda key: key > thr), ties_ref.shape)

    for c in range(n_blocks):
        if (c + 1) * t > topk:
            pl.when(i == c)(functools.partial(search, c + 1))

    thr = thr_ref[0:1, :]
    n_ties = ties_ref[0:1, :]

    r2 = lax.broadcasted_iota(I32, (2 * t, t), 0)
    c2 = lax.broadcasted_iota(I32, (2 * t, t), 1)
    tie_lhs = jnp.where(jnp.logical_or(r2 >= t, c2 < r2), 1.0, 0.0).astype(BF16)

    def sweep_a(g, carry):
        seen, mx = carry
        mx = list(mx)
        ks = pl.multiple_of(g * gt, gt)
        keys = [keys_ref[g * group + u] for u in range(group)]
        eqs = [key == thr for key in keys]
        eq_all = jnp.concatenate([jnp.where(eq, 1.0, 0.0).astype(BF16) for eq in eqs], axis=1)
        pref = jnp.dot(tie_lhs, eq_all, preferred_element_type=F32)
        s_grp = lax.dot_general(ka_ref[0, pl.ds(ks, gt), :], qs_ref[...], _NT,
                                preferred_element_type=F32)
        for u in range(group):
            j = g * group + u
            us = slice(u * t, (u + 1) * t)
            rank = jnp.where(keys[u] > thr, -1.0, jnp.where(eqs[u], seen + pref[:t, us], 3e38))
            rank = jnp.where(krow + j * t <= qcol + i * t, rank, 3e38)
            valid = rank < n_ties
            which = jnp.where(j == i, 1, jnp.where(j == i - 1, 0, 2))
            s_all = s_grp[us, :] * scale + bias_ref[which]
            for h in range(A_HEADS):
                s_h = jnp.where(valid, s_all[:, h * t:(h + 1) * t], NEG_BIG)
                s_ref[j, :, h * t:(h + 1) * t] = s_h
                mx[h] = jnp.maximum(mx[h], _fold8(s_h, jnp.max))
            seen = seen + pref[t:t + 1, us]
        return seen, tuple(mx)

    mx0 = tuple(jnp.full((SUBLANES, t), NEG_BIG, F32) for _ in range(A_HEADS))
    _, mx = lax.fori_loop(0, n_groups, sweep_a, (jnp.zeros((1, t), F32), mx0))
    m_all = jnp.concatenate([jnp.max(m, axis=0, keepdims=True) for m in mx], axis=1)

    acc_ref[...] = jnp.zeros_like(acc_ref)

    def sweep_b(g, l8):
        ps = []
        for u in range(group):
            p = jnp.exp(s_ref[g * group + u] - m_all)
            l8 = l8 + _fold8(p, jnp.sum)
            ps.append(p.astype(BF16))
        vt_grp = jnp.concatenate([vt_ref[0, g * group + u] for u in range(group)], axis=1)
        acc_ref[...] += jnp.dot(vt_grp, jnp.concatenate(ps, axis=0), preferred_element_type=F32)
        return l8

    l8 = lax.fori_loop(0, n_groups, sweep_b, jnp.zeros((SUBLANES, A_HEADS * t), F32))
    out = acc_ref[...] * (1.0 / jnp.sum(l8, axis=0, keepdims=True))
    for h in range(A_HEADS):
        o_ref[0, :, h * t:(h + 1) * t] = out[:, h * t:(h + 1) * t].T.astype(o_ref.dtype)


def _dsa(qa, iq, iw, ka, vt, ik2, bias_tab):
    b, s, _ = qa.shape
    t = LANES
    nb = s // t
    group = math.gcd(nb, 4)
    topk = min(TOPK_MAX, s // 4)
    kern = functools.partial(_dsa_kernel, topk=float(topk), n_blocks=nb, group=group,
                             scale=A_HEAD_DIM ** -0.5,
                             idx_scale=(IDX_DIM ** -0.5) * (IDX_HEADS ** -0.5))
    qspec = lambda w: pl.BlockSpec((1, t, w), lambda bi, i: (bi, i, 0))
    kspec = pl.BlockSpec((1, s, t), lambda bi, i: (bi, 0, 0))
    return pl.pallas_call(
        kern,
        grid=(b, nb),
        in_specs=[qspec(A_HEADS * t), qspec(IDX_HEADS * IDX_DIM), qspec(t), kspec,
                  pl.BlockSpec((1, nb, t, t), lambda bi, i: (bi, 0, 0, 0)), kspec,
                  _const_spec(bias_tab.shape)],
        out_specs=qspec(A_HEADS * t),
        out_shape=jax.ShapeDtypeStruct((b, s, A_HEADS * t), BF16),
        scratch_shapes=[pltpu.VMEM((nb, t, t), I32),
                        pltpu.VMEM((nb, t, A_HEADS * t), F32),
                        pltpu.VMEM((t, A_HEADS * t), F32),
                        pltpu.VMEM((IDX_HEADS * t, t), BF16),
                        pltpu.VMEM((A_HEADS * t, t), BF16),
                        pltpu.VMEM((t, t), F32),
                        pltpu.VMEM((SUBLANES, t), I32),
                        pltpu.VMEM((SUBLANES, t), F32)],
        compiler_params=_params("parallel", "parallel"),
        name="dsa_attention",
    )(qa, iq, iw, ka, vt, ik2, bias_tab)


def _hgrn_kernel(q_ref, f_ref, i_ref, g_ref, lb_ref, gain_ref, o_ref, state_ref, *, n_chunks, n_seq):
    c = B_CHUNK
    rc = n_seq * c
    kd, vd = B_KEY_DIM, B_VAL_DIM
    w = B_HEADS * kd

    @pl.when(pl.program_id(1) == 0)
    def _():
        state_ref[...] = jnp.zeros_like(state_ref)

    row = lax.broadcasted_iota(I32, (rc, rc), 0)
    col = lax.broadcasted_iota(I32, (rc, rc), 1)
    same_seq = (row >= col) if n_seq == 1 else jnp.logical_and(
        row >= col, lax.shift_right_logical(row, 6) == lax.shift_right_logical(col, 6))
    causal = same_seq
    assert c == 64
    tril = jnp.where(causal, 1.0, 0.0).astype(BF16)
    seqs = [slice(b * c, (b + 1) * c) for b in range(n_seq)]
    seq_of_row = lax.shift_right_logical(lax.broadcasted_iota(I32, (rc, w), 0), 6)

    def per_seq_row(x, r):
        return jnp.concatenate([jnp.broadcast_to(x[b * c + r:b * c + r + 1, :], (c, x.shape[1]))
                                for b in range(n_seq)], axis=0)

    def stacked(ref, rows):
        return jnp.concatenate([ref[b, rows, :] for b in range(n_seq)], axis=0)
    q_scale = kd ** -0.5
    lb = lb_ref[...]
    lb_floor = jnp.maximum(lb, LB_FLOOR)
    one_m_lb = 1.0 - lb
    heads = [slice(h * kd, (h + 1) * kd) for h in range(B_HEADS)]

    def chunk_body(ci, carry):
        rows = pl.ds(pl.multiple_of(ci * c, c), c)
        fr = stacked(f_ref, rows)
        z = jnp.exp(-jnp.abs(fr))
        r = 1.0 / (1.0 + z)
        sig_pos = jnp.where(fr >= 0, r, z * r)
        sig_neg = jnp.where(fr >= 0, z * r, r)
        log_f = jnp.log(lb_floor + one_m_lb * sig_pos)
        k_in = one_m_lb * sig_neg
        hi = log_f.astype(BF16)
        rest = log_f - hi.astype(F32)
        mid = rest.astype(BF16)
        lo = (rest - mid.astype(F32)).astype(BF16)
        cs = jnp.dot(tril, jnp.concatenate([hi, mid, lo], axis=1), preferred_element_type=F32)
        bsum = cs[:, :w] + cs[:, w:2 * w] + cs[:, 2 * w:]
        b_mid = per_seq_row(bsum, c // 2 - 1)
        b_last = per_seq_row(bsum, c - 1)
        qs = stacked(q_ref, rows) * q_scale
        v = stacked(i_ref, rows)
        v_b = v.astype(BF16)
        v_t = v.T.astype(BF16)
        q_mid = (qs * jnp.exp(bsum - b_mid)).astype(BF16)
        k_mid = (k_in * jnp.exp(b_mid - bsum)).astype(BF16)
        q_dec = (qs * jnp.exp(bsum)).astype(BF16)
        k_end = (k_in * jnp.exp(b_last - bsum)).astype(BF16)
        d_last = jnp.exp(b_last)
        st = [[state_ref[b, h] for h in range(B_HEADS)] for b in range(n_seq)]
        attn = [lax.dot_general(q_mid[:, hs], k_mid[:, hs], _NT, preferred_element_type=F32)
                for hs in heads]
        inter = [jnp.concatenate(
            [lax.dot_general(q_dec[sq, hs], st[b][h].astype(BF16), _NT, preferred_element_type=F32)
             for b, sq in enumerate(seqs)], axis=0) for h, hs in enumerate(heads)]
        k_seq = [k_end if n_seq == 1 else jnp.where(seq_of_row == b, k_end, jnp.zeros_like(k_end))
                 for b in range(n_seq)]
        upd = [[jnp.dot(v_t[hs, :], k_seq[b][:, hs], preferred_element_type=F32) for hs in heads]
               for b in range(n_seq)]
        attn = [jnp.where(causal, a, 0.0).astype(BF16) for a in attn]
        outs = [jnp.dot(attn[h], v_b[:, hs], preferred_element_type=F32) + inter[h]
                for h, hs in enumerate(heads)]
        for b, sq in enumerate(seqs):
            for h, hs in enumerate(heads):
                state_ref[b, h] = d_last[b * c:b * c + 1, hs] * st[b][h] + upd[b][h]
        o = jnp.concatenate([_rms(o_h, gain_ref[...]) for o_h in outs], axis=1)
        g = stacked(g_ref, rows)
        o = (o * (g * (1.0 / (1.0 + jnp.exp(-g))))).astype(o_ref.dtype)
        for b, sq in enumerate(seqs):
            o_ref[b, rows, :] = o[sq, :]
        return carry

    lax.fori_loop(0, n_chunks, chunk_body, 0)


def _hgrn(bq, bf, bi, bg, lb, gain, sb):
    b, s, w = bq.shape
    assert s % sb == 0 and sb % B_CHUNK == 0
    n_seq = 2 if b % 2 == 0 else 1
    kern = functools.partial(_hgrn_kernel, n_chunks=sb // B_CHUNK, n_seq=n_seq)
    spec = pl.BlockSpec((n_seq, sb, w), lambda bi_, si: (bi_, si, 0))
    return pl.pallas_call(
        kern,
        grid=(b // n_seq, s // sb),
        in_specs=[spec, spec, spec, spec, _const_spec((1, w)), _const_spec((1, B_VAL_DIM))],
        out_specs=spec,
        out_shape=jax.ShapeDtypeStruct((b, s, w), BF16),
        scratch_shapes=[pltpu.VMEM((n_seq, B_HEADS, B_VAL_DIM, B_KEY_DIM), F32)],
        compiler_params=_params("parallel", "arbitrary"),
        name="hgrn2",
    )(bq, bf, bi, bg, lb.reshape(1, w), gain.reshape(1, B_VAL_DIM))


def _mla_prep_kernel(cq_ref, ckv_ref, kpe_raw_ref, cos_ref, sin_ref, gq_ref, gkv_ref, wq_ref, wkv_ref,
                     qn_ref, qpe_ref, kn_ref, v_ref, kpe_ref):
    nw = C_HEADS * C_NOPE
    pw = C_HEADS * C_ROPE
    cos, sin = cos_ref[...], sin_ref[...]
    cos2 = jnp.concatenate([cos] * (pw // LANES), axis=1)
    sin2 = jnp.concatenate([sin] * (pw // LANES), axis=1)
    q = jnp.dot(_rms(cq_ref[...], gq_ref[...]).astype(BF16), wq_ref[...], preferred_element_type=F32)
    qn_ref[...] = q[:, :nw].astype(qn_ref.dtype)
    qpe_ref[...] = (q[:, nw:nw + pw] * cos2 + q[:, nw + pw:] * sin2).astype(qpe_ref.dtype)
    kv = jnp.dot(_rms(ckv_ref[...], gkv_ref[...]).astype(BF16), wkv_ref[...], preferred_element_type=F32)
    kn_ref[...] = kv[:, :nw].astype(kn_ref.dtype)
    v_ref[...] = kv[:, nw:].astype(v_ref.dtype)
    kpe_ref[...] = (kpe_raw_ref[:, :LANES] * cos + kpe_raw_ref[:, LANES:] * sin).astype(kpe_ref.dtype)


def _mla_prep(cq, ckv, kpe_raw, cos_t, sin_t, gq, gkv, wq, wkv, tm):
    n = cq.shape[0]
    nw = C_HEADS * C_NOPE
    row = lambda w: pl.BlockSpec((tm, w), lambda i: (i, 0))
    widths = (nw, C_HEADS * C_ROPE, nw, C_HEADS * C_V, LANES)
    return pl.pallas_call(
        _mla_prep_kernel,
        grid=(n // tm,),
        in_specs=[row(C_Q_RANK), row(C_KV_RANK), row(2 * LANES), row(LANES), row(LANES),
                  _const_spec((1, C_Q_RANK)), _const_spec((1, C_KV_RANK)),
                  _const_spec(wq.shape), _const_spec(wkv.shape)],
        out_specs=[row(w) for w in widths],
        out_shape=[jax.ShapeDtypeStruct((n, w), BF16) for w in widths],
        compiler_params=_params("parallel"),
        name="mla_prep",
    )(cq, ckv, kpe_raw, cos_t, sin_t, gq.reshape(1, -1), gkv.reshape(1, -1), wq, wkv)


def _mla_kernel(qn_ref, qpe_ref, kn_ref, kpe_ref, vt_ref, o_ref, q_ref, acc_ref, *, t, scale):
    i = pl.program_id(1)
    t2 = 2 * t
    krow = lax.broadcasted_iota(I32, (t2, t), 0)
    qcol = lax.broadcasted_iota(I32, (t2, t), 1)
    lane_lo = lax.broadcasted_iota(I32, (t, LANES), 1) < C_ROPE
    for h in range(C_HEADS):
        pair = qpe_ref[0, :, (h // 2) * LANES:(h // 2 + 1) * LANES]
        keep = lane_lo if h % 2 == 0 else jnp.logical_not(lane_lo)
        q_ref[h, :, :C_NOPE] = qn_ref[0, :, h * C_NOPE:(h + 1) * C_NOPE]
        q_ref[h, :, C_NOPE:] = jnp.where(keep, pair, jnp.zeros_like(pair))
    acc_ref[...] = jnp.zeros_like(acc_ref)

    def step(g, carry, masked):
        ms, ls = carry
        ks = pl.multiple_of(g * t2, t2)
        kpe_t = kpe_ref[0, pl.ds(ks, t2), :]
        logits = []
        for h in range(C_HEADS):
            k_h = jnp.concatenate([kn_ref[0, pl.ds(ks, t2), h * C_NOPE:(h + 1) * C_NOPE], kpe_t], axis=1)
            logits.append(lax.dot_general(k_h, q_ref[h], _NT, preferred_element_type=F32))
        new_m, new_l, alphas, probs = [], [], [], []
        for h in range(C_HEADS):
            s = logits[h] * scale
            if masked:
                s = jnp.where(krow + g * t2 <= qcol + i * t, s, NEG_BIG)
            m_new = jnp.maximum(ms[h], jnp.max(s, axis=0, keepdims=True))
            alpha = jnp.exp(ms[h] - m_new)
            p = jnp.exp(s - m_new)
            new_l.append(alpha * ls[h] + jnp.sum(p, axis=0, keepdims=True))
            new_m.append(m_new)
            alphas.append(alpha)
            probs.append(p.astype(BF16))
        for h in range(C_HEADS):
            hs = slice(h * C_V, (h + 1) * C_V)
            v_t = jnp.concatenate([vt_ref[0, 2 * g, hs, :], vt_ref[0, 2 * g + 1, hs, :]], axis=1)
            acc_ref[h] = alphas[h] * acc_ref[h] + jnp.dot(v_t, probs[h], preferred_element_type=F32)
        return tuple(new_m), tuple(new_l)

    init = (tuple(jnp.full((1, t), NEG_BIG, F32) for _ in range(C_HEADS)),
            tuple(jnp.zeros((1, t), F32) for _ in range(C_HEADS)))
    n_full = i // 2
    carry = lax.fori_loop(0, n_full, functools.partial(step, masked=False), init)
    _, ls = step(n_full, carry, masked=True)
    for h in range(C_HEADS):
        o_ref[0, :, h * C_V:(h + 1) * C_V] = (acc_ref[h] * (1.0 / ls[h])).T.astype(o_ref.dtype)


def _mla(qn, qpe, kn, kpe, vt, t):
    b, s, _ = qn.shape
    nt = s // t
    assert nt % 2 == 0
    kern = functools.partial(_mla_kernel, t=t, scale=(C_NOPE + C_ROPE) ** -0.5)
    qspec = lambda w: pl.BlockSpec((1, t, w), lambda bi, i: (bi, i, 0))
    kspec = lambda w: pl.BlockSpec((1, s, w), lambda bi, i: (bi, 0, 0))
    return pl.pallas_call(
        kern,
        grid=(b, nt),
        in_specs=[qspec(qn.shape[2]), qspec(qpe.shape[2]), kspec(kn.shape[2]), kspec(kpe.shape[2]),
                  pl.BlockSpec((1, nt, C_HEADS * C_V, t), lambda bi, i: (bi, 0, 0, 0))],
        out_specs=qspec(C_HEADS * C_V),
        out_shape=jax.ShapeDtypeStruct((b, s, C_HEADS * C_V), BF16),
        scratch_shapes=[pltpu.VMEM((C_HEADS, t, C_NOPE + LANES), BF16),
                        pltpu.VMEM((C_HEADS, C_V, t), F32)],
        compiler_params=_params("parallel", "parallel"),
        name="mla_attention",
    )(qn, qpe, kn, kpe, vt)


def _merge_kernel(x_ref, g_ref, wg_ref, oa_ref, ob_ref, oc_ref, wa_ref, wb_ref, wc_ref, wo_ref,
                  out_ref, mixed_ref, *, col_chunk):
    x = x_ref[...]
    d = x.shape[1]
    h = _rms(x, g_ref[...]).astype(BF16)
    branches = ((oa_ref, wa_ref), (ob_ref, wb_ref), (oc_ref, wc_ref))
    for s in range(0, d, col_chunk):
        cs = slice(s, s + col_chunk)
        mixed = None
        for bidx, (o_ref, w_ref) in enumerate(branches):
            logits = jnp.dot(h, wg_ref[:, bidx * d + s:bidx * d + s + col_chunk],
                             preferred_element_type=F32)
            gate = 1.0 / (1.0 + jnp.exp(-logits))
            term = gate * jnp.dot(o_ref[...], w_ref[:, cs], preferred_element_type=F32)
            mixed = term if mixed is None else mixed + term
        mixed_ref[:, cs] = mixed.astype(BF16)
    out_ref[...] = x + jnp.dot(mixed_ref[...], wo_ref[...], preferred_element_type=F32)


def _merge(x, g, w_gate, oa, ob, oc, wa, wb, wc, wo, tm):
    n, d = x.shape
    row = lambda w: pl.BlockSpec((tm, w), lambda i: (i, 0))
    kern = functools.partial(_merge_kernel, col_chunk=256)
    return pl.pallas_call(
        kern,
        grid=(n // tm,),
        in_specs=[row(d), _const_spec((1, d)), _const_spec(w_gate.shape),
                  row(oa.shape[1]), row(ob.shape[1]), row(oc.shape[1]),
                  _const_spec(wa.shape), _const_spec(wb.shape), _const_spec(wc.shape),
                  _const_spec(wo.shape)],
        out_specs=row(d),
        out_shape=jax.ShapeDtypeStruct((n, d), F32),
        scratch_shapes=[pltpu.VMEM((tm, d), BF16)],
        compiler_params=_params("parallel"),
        name="merge_out_proj",
    )(x, g.reshape(1, d), w_gate, oa, ob, oc, wa, wb, wc, wo)


def _ffn_kernel(x_ref, g_ref, wg_ref, wu_ref, wd_ref, gf_ref, out_ref, act_ref, *, col_chunk,
                final_norm):
    x = x_ref[...]
    h = _rms(x, g_ref[...]).astype(BF16)
    dff = wg_ref.shape[1]
    for s in range(0, dff, col_chunk):
        cs = slice(s, s + col_chunk)
        gate = jnp.dot(h, wg_ref[:, cs], preferred_element_type=F32)
        up = jnp.dot(h, wu_ref[:, cs], preferred_element_type=F32)
        act_ref[:, cs] = (gate * (1.0 / (1.0 + jnp.exp(-gate))) * up).astype(BF16)
    y = x + jnp.dot(act_ref[...], wd_ref[...], preferred_element_type=F32)
    if final_norm:
        y = _rms(y, gf_ref[...])
    out_ref[...] = y


def _ffn(x, g, wg, wu, wd, gf, tm, final_norm):
    n, d = x.shape
    dff = wg.shape[1]
    row = pl.BlockSpec((tm, d), lambda i: (i, 0))
    kern = functools.partial(_ffn_kernel, col_chunk=256, final_norm=final_norm)
    return pl.pallas_call(
        kern,
        grid=(n // tm,),
        in_specs=[row, _const_spec((1, d)), _const_spec(wg.shape), _const_spec(wu.shape),
                  _const_spec(wd.shape), _const_spec((1, d))],
        out_specs=row,
        out_shape=jax.ShapeDtypeStruct((n, d), F32),
        scratch_shapes=[pltpu.VMEM((tm, dff), BF16)],
        compiler_params=_params("parallel"),
        name="swiglu_ffn",
    )(x, g.reshape(1, d), wg, wu, wd, gf.reshape(1, d))


def _t5_bucket(dist):
    max_exact = REL_BUCKETS // 2
    d = jnp.maximum(dist, 0)
    dl = jnp.maximum(d, max_exact).astype(F32)
    large = max_exact + (jnp.log(dl / max_exact) / math.log(REL_MAX_DIST / max_exact)
                         * (REL_BUCKETS - max_exact)).astype(I32)
    large = jnp.minimum(large, REL_BUCKETS - 1)
    return jnp.where(d < max_exact, d, large)


def _swap_halves(w):
    half = w.shape[-1] // 2
    return jnp.concatenate([w[..., half:], w[..., :half]], axis=-1)


def _in_proj_weights(w_in, d_model):
    splits = (A_HEADS * A_HEAD_DIM, A_HEAD_DIM, A_HEAD_DIM, IDX_HEADS * IDX_DIM, IDX_DIM, IDX_HEADS,
              B_HEADS * B_KEY_DIM, B_HEADS * B_KEY_DIM, B_HEADS * B_VAL_DIM, B_HEADS * B_VAL_DIM,
              C_Q_RANK, C_KV_RANK, C_ROPE, N_BRANCH * d_model)
    st = np.concatenate([[0], np.cumsum(splits)])
    col = lambda a, b: w_in[:, :, st[a]:st[b]]
    ik, pe = col(4, 5), col(12, 13)
    iw = jnp.pad(col(5, 6), ((0, 0), (0, 0), (0, LANES - IDX_HEADS)))
    pe_sw = _swap_halves(pe)
    pieces = [col(0, 4), ik, ik, iw, col(6, 12)] + [pe] * (LANES // C_ROPE) + [pe_sw] * (LANES // C_ROPE)
    w_proj = jnp.concatenate(pieces, axis=2).astype(BF16)
    widths = list(splits[:4]) + [LANES, LANES] + list(splits[6:12]) + [2 * LANES]
    dtypes = [BF16] * 5 + [F32] * 8
    return w_proj, widths, dtypes, col(13, 14).astype(BF16)


def kernel(x, positions, w_in, w_up_a, w_up_b, w_up_c, w_out, mla_q_norm, mla_w_qb, mla_kv_norm,
           mla_w_kvb, hgrn_lb_logits, hgrn_out_norm, rel_bias, attn_norm, ffn_norm, w_ffn_gate,
           w_ffn_up, w_ffn_down, final_norm):
    bsz, s_len, d_model = x.shape
    depth = w_in.shape[0]
    n = bsz * s_len
    t = LANES

    w_proj, widths, dtypes, w_gate = _in_proj_weights(w_in, d_model)
    wq = mla_w_qb.reshape(depth, C_Q_RANK, C_HEADS, C_NOPE + C_ROPE)
    q_pe = wq[..., C_NOPE:]
    w_qb = jnp.concatenate([wq[..., :C_NOPE].reshape(depth, C_Q_RANK, -1),
                            q_pe.reshape(depth, C_Q_RANK, -1),
                            _swap_halves(q_pe).reshape(depth, C_Q_RANK, -1)], axis=2).astype(BF16)
    wkv = mla_w_kvb.reshape(depth, C_KV_RANK, C_HEADS, C_NOPE + C_V)
    w_kvb = jnp.concatenate([wkv[..., :C_NOPE].reshape(depth, C_KV_RANK, -1),
                             wkv[..., C_NOPE:].reshape(depth, C_KV_RANK, -1)], axis=2).astype(BF16)
    bf = lambda w: w.astype(BF16)
    w_up_a, w_up_b, w_up_c, w_out = bf(w_up_a), bf(w_up_b), bf(w_up_c), bf(w_out)
    w_ffn_gate, w_ffn_up, w_ffn_down = bf(w_ffn_gate), bf(w_ffn_up), bf(w_ffn_down)

    p_lb = jax.nn.softmax(hgrn_lb_logits.astype(F32), axis=0)
    lower_bounds = jnp.cumsum(p_lb, axis=0) - p_lb[0:1]
    inv_freq = ROPE_THETA ** (-jnp.arange(0, C_ROPE, 2, dtype=F32) / C_ROPE)
    ang = positions.astype(F32)[..., None] * inv_freq
    cos, sin = jnp.cos(ang), jnp.sin(ang)
    reps = LANES // C_ROPE
    cos_t = jnp.tile(jnp.concatenate([cos, cos], axis=-1), (1, 1, reps)).reshape(n, LANES)
    sin_t = jnp.tile(jnp.concatenate([-sin, sin], axis=-1), (1, 1, reps)).reshape(n, LANES)
    q_idx = jnp.arange(t, dtype=I32)[None, :]
    k_idx = jnp.arange(t, dtype=I32)[:, None]
    tab = lambda dist: rel_bias[_t5_bucket(dist)].astype(F32).transpose(0, 2, 1).reshape(t, A_HEADS * t)
    far = jnp.broadcast_to(rel_bias[REL_BUCKETS - 1].astype(F32)[None, :, None],
                           (t, A_HEADS, t)).reshape(t, A_HEADS * t)
    bias_tab = jnp.stack([tab(q_idx + t - k_idx), tab(q_idx - k_idx), far])

    tm = min(512, n)
    t_mla = min(256, s_len)
    x2 = x.reshape(n, d_model)
    r3 = lambda a: a.reshape(bsz, s_len, a.shape[-1])
    key_major = lambda a, tile: a.reshape(bsz, s_len // tile, tile, a.shape[-1]).swapaxes(2, 3)
    for l in range(depth):
        (qa, ka, va, iq, ik2, iw, bq, bfr, bi, bg, cq, ckv, kpe_raw) = _norm_matmul(
            x2, attn_norm[l], w_proj[l], widths, dtypes, tm)
        o_a = _dsa(r3(qa), r3(iq), r3(iw), r3(ka), key_major(va, t), r3(ik2), bias_tab)
        o_b = _hgrn(r3(bq), r3(bfr), r3(bi), r3(bg), lower_bounds[l], hgrn_out_norm[l],
                    sb=min(512, s_len))
        qn, qpe, kn, vc, kpe = _mla_prep(cq, ckv, kpe_raw, cos_t, sin_t, mla_q_norm[l], mla_kv_norm[l],
                                         w_qb[l], w_kvb[l], tm)
        o_c = _mla(r3(qn), r3(qpe), r3(kn), r3(kpe), key_major(vc, t_mla), t_mla)
        x2 = _merge(x2, attn_norm[l], w_gate[l], o_a.reshape(n, -1), o_b.reshape(n, -1),
                    o_c.reshape(n, -1), w_up_a[l], w_up_b[l], w_up_c[l], w_out[l], min(256, n))
        x2 = _ffn(x2, ffn_norm[l], w_ffn_gate[l], w_ffn_up[l], w_ffn_down[l], final_norm,
                  min(256, n), final_norm=(l == depth - 1))
    return x2.reshape(bsz, s_len, d_model)
```

```python
import functools
import math

import jax
import jax.numpy as jnp
import numpy as np
from jax import lax
from jax.experimental import pallas as pl
from jax.experimental.pallas import tpu as pltpu

F32 = jnp.float32
BF16 = jnp.bfloat16
I32 = jnp.int32

A_HEADS = 4
A_HEAD_DIM = 128
IDX_HEADS = 8
IDX_DIM = 64
TOPK_MAX = 256
B_HEADS = 4
B_KEY_DIM = 128
B_VAL_DIM = 128
B_CHUNK = 64
C_HEADS = 4
C_Q_RANK = 384
C_KV_RANK = 256
C_NOPE = 128
C_ROPE = 64
C_V = 128
ROPE_THETA = 10000.0
REL_BUCKETS = 32
REL_MAX_DIST = 128
N_BRANCH = 3
EPS = 1e-6
NEG_BIG = -1e30
LB_FLOOR = 1e-30

LANES = 128
SUBLANES = 8
VMEM_LIMIT = 56 * 1024 * 1024
INT_MIN = np.int32(-2 ** 31)

_NT = (((1,), (1,)), ((), ()))


def _params(*sem):
    return pltpu.CompilerParams(dimension_semantics=sem, vmem_limit_bytes=VMEM_LIMIT)


def _rms(x, g):
    return x * lax.rsqrt(jnp.mean(x * x, axis=-1, keepdims=True) + EPS) * g


def _const_spec(shape):
    nd = len(shape)
    return pl.BlockSpec(shape, lambda *_: (0,) * nd)


def _fold8(x, op):
    r, c = x.shape
    return op(x.reshape(r // SUBLANES, SUBLANES, c), axis=0)


def _norm_matmul_kernel(x_ref, g_ref, w_ref, *out_refs, widths, col_chunk):
    h = _rms(x_ref[...], g_ref[...]).astype(BF16)
    c0 = 0
    for o_ref, w in zip(out_refs, widths):
        for s in range(0, w, col_chunk):
            e = min(s + col_chunk, w)
            o_ref[:, s:e] = jnp.dot(h, w_ref[:, c0 + s:c0 + e],
                                    preferred_element_type=F32).astype(o_ref.dtype)
        c0 += w


def _norm_matmul(x, g, w, widths, dtypes, tm):
    n, k = x.shape
    wtot = sum(widths)
    assert w.shape == (k, wtot) and n % tm == 0
    kern = functools.partial(_norm_matmul_kernel, widths=tuple(widths), col_chunk=512)
    return pl.pallas_call(
        kern,
        grid=(n // tm,),
        in_specs=[pl.BlockSpec((tm, k), lambda i: (i, 0)),
                  _const_spec((1, k)),
                  _const_spec((k, wtot))],
        out_specs=[pl.BlockSpec((tm, wd), lambda i: (i, 0)) for wd in widths],
        out_shape=[jax.ShapeDtypeStruct((n, wd), dt) for wd, dt in zip(widths, dtypes)],
        compiler_params=_params("parallel"),
        name="norm_proj",
    )(x, g.reshape(1, k), w)


def _dsa_kernel(qa_ref, iq_ref, iw_ref, ka_ref, vt_ref, ik_ref, bias_ref, o_ref,
                keys_ref, s_ref, qm_ref, qs_ref, wt_ref, thr_ref, ties_ref,
                *, topk, n_blocks, group, scale, idx_scale):
    t = LANES
    gt = group * t
    i = pl.program_id(1)
    n_groups = (i + group) // group
    max_groups = n_blocks // group
    krow = lax.broadcasted_iota(I32, (t, t), 0)
    qcol = lax.broadcasted_iota(I32, (t, t), 1)
    lane_lo = qcol < IDX_DIM

    for h in range(IDX_HEADS):
        pair = iq_ref[0, :, (h // 2) * t:(h // 2 + 1) * t]
        keep = lane_lo if h % 2 == 0 else jnp.logical_not(lane_lo)
        qm_ref[h * t:(h + 1) * t, :] = jnp.where(keep, pair, jnp.zeros_like(pair))
    for h in range(A_HEADS):
        qs_ref[h * t:(h + 1) * t, :] = qa_ref[0, :, h * t:(h + 1) * t]
    wt_ref[...] = iw_ref[0].T

    def index_logits(g):
        return lax.dot_general(ik_ref[0, g * gt:(g + 1) * gt, :], qm_ref[...], _NT,
                               preferred_element_type=F32)

    def score_group(g, lg):
        for u in range(group):
            j = g * group + u
            acc = jnp.zeros((t, t), F32)
            for h in range(IDX_HEADS):
                acc = acc + jnp.maximum(lg[u * t:(u + 1) * t, h * t:(h + 1) * t], 0.0) * wt_ref[h:h + 1, :]
            score = acc * idx_scale
            score = jnp.where(krow + j * t <= qcol + i * t, score, NEG_BIG)
            score = jnp.where(score == 0.0, 0.0, score)
            bits = pltpu.bitcast(score, I32)
            keys_ref[j] = bits ^ ((bits >> 31) & np.int32(0x7FFFFFFF))

    def score_all(ng):
        lg = index_logits(0)
        for g in range(ng):
            nxt = index_logits(g + 1) if g + 1 < ng else None
            score_group(g, lg)
            lg = nxt

    for v in range(1, max_groups + 1):
        pl.when(n_groups == v)(functools.partial(score_all, v))

    thr_ref[...] = jnp.full(thr_ref.shape, INT_MIN, I32)
    ties_ref[...] = jnp.zeros(ties_ref.shape, F32)

    def count(n, pred):
        cnt = jnp.zeros((SUBLANES, t), F32)
        for j in range(n):
            cnt = cnt + _fold8(jnp.where(pred(keys_ref[j]), 1.0, 0.0), jnp.sum)
        return jnp.sum(cnt, axis=0, keepdims=True)

    def search(n):
        def search_pass(b, thr):
            cand = thr + lax.shift_left(np.int32(1), 31 - b)
            return jnp.where(count(n, lambda key: key >= cand) >= topk, cand, thr)

        thr = lax.fori_loop(0, 32, search_pass, jnp.full((1, t), INT_MIN, I32))
        thr_ref[...] = jnp.broadcast_to(thr, thr_ref.shape)
        ties_ref[...] = jnp.broadcast_to(topk - count(n, lambda key: key > thr), ties_ref.shape)

    for c in range(n_blocks):
        if (c + 1) * t > topk:
            pl.when(i == c)(functools.partial(search, c + 1))

    thr = thr_ref[0:1, :]
    n_ties = ties_ref[0:1, :]

    r2 = lax.broadcasted_iota(I32, (2 * t, t), 0)
    c2 = lax.broadcasted_iota(I32, (2 * t, t), 1)
    tie_lhs = jnp.where(jnp.logical_or(r2 >= t, c2 < r2), 1.0, 0.0).astype(BF16)

    def sweep_a_mxu(g):
        keys = [keys_ref[g * group + u] for u in range(group)]
        eqs = [key == thr for key in keys]
        eq_all = jnp.concatenate([jnp.where(eq, 1.0, 0.0).astype(BF16) for eq in eqs], axis=1)
        pref = jnp.dot(tie_lhs, eq_all, preferred_element_type=F32)
        s_grp = lax.dot_general(ka_ref[0, g * gt:(g + 1) * gt, :], qs_ref[...], _NT,
                                preferred_element_type=F32)
        return keys, eqs, pref, s_grp

    def sweep_a_vpu(g, operands, seen, mx):
        keys, eqs, pref, s_grp = operands
        for u in range(group):
            j = g * group + u
            us = slice(u * t, (u + 1) * t)
            rank = jnp.where(keys[u] > thr, -1.0, jnp.where(eqs[u], seen + pref[:t, us], 3e38))
            rank = jnp.where(krow + j * t <= qcol + i * t, rank, 3e38)
            valid = rank < n_ties
            which = jnp.where(j == i, 1, jnp.where(j == i - 1, 0, 2))
            s_all = s_grp[us, :] * (scale * math.log2(math.e)) + bias_ref[which]
            for h in range(A_HEADS):
                s_h = jnp.where(valid, s_all[:, h * t:(h + 1) * t], NEG_BIG)
                s_ref[j, :, h * t:(h + 1) * t] = s_h
                mx[h] = jnp.maximum(mx[h], _fold8(s_h, jnp.max))
            seen = seen + pref[t:t + 1, us]
        return seen, mx

    def sweep_b(g, m_all, l8):
        ps = []
        for u in range(group):
            p = jnp.exp2(s_ref[g * group + u] - m_all)
            l8 = l8 + _fold8(p, jnp.sum)
            ps.append(p.astype(BF16))
        vt_grp = jnp.concatenate([vt_ref[0, g * group + u] for u in range(group)], axis=1)
        return l8, jnp.dot(vt_grp, jnp.concatenate(ps, axis=0), preferred_element_type=F32)

    def attend(ng):
        seen = jnp.zeros((1, t), F32)
        mx = [jnp.full((SUBLANES, t), NEG_BIG, F32) for _ in range(A_HEADS)]
        operands = sweep_a_mxu(0)
        for g in range(ng):
            nxt = sweep_a_mxu(g + 1) if g + 1 < ng else None
            seen, mx = sweep_a_vpu(g, operands, seen, mx)
            operands = nxt
        m_all = jnp.concatenate([jnp.max(m, axis=0, keepdims=True) for m in mx], axis=1)
        l8 = jnp.zeros((SUBLANES, A_HEADS * t), F32)
        acc = None
        for g in range(ng):
            l8, pv = sweep_b(g, m_all, l8)
            acc = pv if acc is None else acc + pv
        out = acc * (1.0 / jnp.sum(l8, axis=0, keepdims=True))
        for h in range(A_HEADS):
            o_ref[0, :, h * t:(h + 1) * t] = out[:, h * t:(h + 1) * t].T.astype(o_ref.dtype)

    for v in range(1, max_groups + 1):
        pl.when(n_groups == v)(functools.partial(attend, v))


def _dsa(qa, iq, iw, ka, vt, ik2, bias_tab):
    b, s, _ = qa.shape
    t = LANES
    nb = s // t
    group = math.gcd(nb, 4)
    topk = min(TOPK_MAX, s // 4)
    kern = functools.partial(_dsa_kernel, topk=float(topk), n_blocks=nb, group=group,
                             scale=A_HEAD_DIM ** -0.5,
                             idx_scale=(IDX_DIM ** -0.5) * (IDX_HEADS ** -0.5))
    qspec = lambda w: pl.BlockSpec((1, t, w), lambda bi, i: (bi, i, 0))
    kspec = pl.BlockSpec((1, s, t), lambda bi, i: (bi, 0, 0))
    return pl.pallas_call(
        kern,
        grid=(b, nb),
        in_specs=[qspec(A_HEADS * t), qspec(IDX_HEADS * IDX_DIM), qspec(t), kspec,
                  pl.BlockSpec((1, nb, t, t), lambda bi, i: (bi, 0, 0, 0)), kspec,
                  _const_spec(bias_tab.shape)],
        out_specs=qspec(A_HEADS * t),
        out_shape=jax.ShapeDtypeStruct((b, s, A_HEADS * t), BF16),
        scratch_shapes=[pltpu.VMEM((nb, t, t), I32),
                        pltpu.VMEM((nb, t, A_HEADS * t), F32),
                        pltpu.VMEM((IDX_HEADS * t, t), BF16),
                        pltpu.VMEM((A_HEADS * t, t), BF16),
                        pltpu.VMEM((t, t), F32),
                        pltpu.VMEM((SUBLANES, t), I32),
                        pltpu.VMEM((SUBLANES, t), F32)],
        compiler_params=_params("parallel", "parallel"),
        name="dsa_attention",
    )(qa, iq, iw, ka, vt, ik2, bias_tab)


def _hgrn_kernel(q_ref, f_ref, i_ref, g_ref, lb_ref, gain_ref, o_ref, state_ref, *, n_chunks, n_seq):
    c = B_CHUNK
    rc = n_seq * c
    kd, vd = B_KEY_DIM, B_VAL_DIM
    w = B_HEADS * kd

    @pl.when(pl.program_id(1) == 0)
    def _():
        state_ref[...] = jnp.zeros_like(state_ref)

    row = lax.broadcasted_iota(I32, (rc, rc), 0)
    col = lax.broadcasted_iota(I32, (rc, rc), 1)
    same_seq = (row >= col) if n_seq == 1 else jnp.logical_and(
        row >= col, lax.shift_right_logical(row, 6) == lax.shift_right_logical(col, 6))
    causal = same_seq
    assert c == 64
    tril = jnp.where(causal, 1.0, 0.0).astype(BF16)
    seqs = [slice(b * c, (b + 1) * c) for b in range(n_seq)]
    seq_of_row = lax.shift_right_logical(lax.broadcasted_iota(I32, (rc, w), 0), 6)

    def per_seq_row(x, r):
        return jnp.concatenate([jnp.broadcast_to(x[b * c + r:b * c + r + 1, :], (c, x.shape[1]))
                                for b in range(n_seq)], axis=0)

    def stacked(ref, rows):
        return jnp.concatenate([ref[b, rows, :] for b in range(n_seq)], axis=0)
    q_scale = kd ** -0.5
    lb = lb_ref[...]
    lb_floor = jnp.maximum(lb, LB_FLOOR)
    one_m_lb = 1.0 - lb
    heads = [slice(h * kd, (h + 1) * kd) for h in range(B_HEADS)]

    def chunk_body(ci, carry):
        rows = pl.ds(pl.multiple_of(ci * c, c), c)
        fr = stacked(f_ref, rows)
        z = jnp.exp(-jnp.abs(fr))
        r = 1.0 / (1.0 + z)
        sig_pos = jnp.where(fr >= 0, r, z * r)
        sig_neg = jnp.where(fr >= 0, z * r, r)
        log_f = jnp.log(lb_floor + one_m_lb * sig_pos)
        k_in = one_m_lb * sig_neg
        hi = log_f.astype(BF16)
        rest = log_f - hi.astype(F32)
        mid = rest.astype(BF16)
        lo = (rest - mid.astype(F32)).astype(BF16)
        cs = jnp.dot(tril, jnp.concatenate([hi, mid, lo], axis=1), preferred_element_type=F32)
        bsum = cs[:, :w] + cs[:, w:2 * w] + cs[:, 2 * w:]
        b_mid = per_seq_row(bsum, c // 2 - 1)
        b_last = per_seq_row(bsum, c - 1)
        qs = stacked(q_ref, rows) * q_scale
        v = stacked(i_ref, rows)
        v_b = v.astype(BF16)
        v_t = v.T.astype(BF16)
        q_mid = (qs * jnp.exp(bsum - b_mid)).astype(BF16)
        k_mid = (k_in * jnp.exp(b_mid - bsum)).astype(BF16)
        q_dec = (qs * jnp.exp(bsum)).astype(BF16)
        k_end = (k_in * jnp.exp(b_last - bsum)).astype(BF16)
        d_last = jnp.exp(b_last)
        st = [[state_ref[b, h] for h in range(B_HEADS)] for b in range(n_seq)]
        attn = [lax.dot_general(q_mid[:, hs], k_mid[:, hs], _NT, preferred_element_type=F32)
                for hs in heads]
        inter = [jnp.concatenate(
            [lax.dot_general(q_dec[sq, hs], st[b][h].astype(BF16), _NT, preferred_element_type=F32)
             for b, sq in enumerate(seqs)], axis=0) for h, hs in enumerate(heads)]
        k_seq = [k_end if n_seq == 1 else jnp.where(seq_of_row == b, k_end, jnp.zeros_like(k_end))
                 for b in range(n_seq)]
        upd = [[jnp.dot(v_t[hs, :], k_seq[b][:, hs], preferred_element_type=F32) for hs in heads]
               for b in range(n_seq)]
        attn = [jnp.where(causal, a, 0.0).astype(BF16) for a in attn]
        outs = [jnp.dot(attn[h], v_b[:, hs], preferred_element_type=F32) + inter[h]
                for h, hs in enumerate(heads)]
        for b, sq in enumerate(seqs):
            for h, hs in enumerate(heads):
                state_ref[b, h] = d_last[b * c:b * c + 1, hs] * st[b][h] + upd[b][h]
        o = jnp.concatenate([_rms(o_h, gain_ref[...]) for o_h in outs], axis=1)
        g = stacked(g_ref, rows)
        o = (o * (g * (1.0 / (1.0 + jnp.exp(-g))))).astype(o_ref.dtype)
        for b, sq in enumerate(seqs):
            o_ref[b, rows, :] = o[sq, :]
        return carry

    lax.fori_loop(0, n_chunks, chunk_body, 0)


def _hgrn(bq, bf, bi, bg, lb, gain, sb):
    b, s, w = bq.shape
    assert s % sb == 0 and sb % B_CHUNK == 0
    n_seq = 2 if b % 2 == 0 else 1
    kern = functools.partial(_hgrn_kernel, n_chunks=sb // B_CHUNK, n_seq=n_seq)
    spec = pl.BlockSpec((n_seq, sb, w), lambda bi_, si: (bi_, si, 0))
    return pl.pallas_call(
        kern,
        grid=(b // n_seq, s // sb),
        in_specs=[spec, spec, spec, spec, _const_spec((1, w)), _const_spec((1, B_VAL_DIM))],
        out_specs=spec,
        out_shape=jax.ShapeDtypeStruct((b, s, w), BF16),
        scratch_shapes=[pltpu.VMEM((n_seq, B_HEADS, B_VAL_DIM, B_KEY_DIM), F32)],
        compiler_params=_params("parallel", "arbitrary"),
        name="hgrn2",
    )(bq, bf, bi, bg, lb.reshape(1, w), gain.reshape(1, B_VAL_DIM))


def _mla_prep_kernel(cq_ref, ckv_ref, kpe_raw_ref, cos_ref, sin_ref, gq_ref, gkv_ref, wq_ref, wkv_ref,
                     qn_ref, qpe_ref, kn_ref, v_ref, kpe_ref):
    nw = C_HEADS * C_NOPE
    pw = C_HEADS * C_ROPE
    cos, sin = cos_ref[...], sin_ref[...]
    cos2 = jnp.concatenate([cos] * (pw // LANES), axis=1)
    sin2 = jnp.concatenate([sin] * (pw // LANES), axis=1)
    q = jnp.dot(_rms(cq_ref[...], gq_ref[...]).astype(BF16), wq_ref[...], preferred_element_type=F32)
    qn_ref[...] = q[:, :nw].astype(qn_ref.dtype)
    qpe_ref[...] = (q[:, nw:nw + pw] * cos2 + q[:, nw + pw:] * sin2).astype(qpe_ref.dtype)
    kv = jnp.dot(_rms(ckv_ref[...], gkv_ref[...]).astype(BF16), wkv_ref[...], preferred_element_type=F32)
    kn_ref[...] = kv[:, :nw].astype(kn_ref.dtype)
    v_ref[...] = kv[:, nw:].astype(v_ref.dtype)
    kpe_ref[...] = (kpe_raw_ref[:, :LANES] * cos + kpe_raw_ref[:, LANES:] * sin).astype(kpe_ref.dtype)


def _mla_prep(cq, ckv, kpe_raw, cos_t, sin_t, gq, gkv, wq, wkv, tm):
    n = cq.shape[0]
    nw = C_HEADS * C_NOPE
    row = lambda w: pl.BlockSpec((tm, w), lambda i: (i, 0))
    widths = (nw, C_HEADS * C_ROPE, nw, C_HEADS * C_V, LANES)
    return pl.pallas_call(
        _mla_prep_kernel,
        grid=(n // tm,),
        in_specs=[row(C_Q_RANK), row(C_KV_RANK), row(2 * LANES), row(LANES), row(LANES),
                  _const_spec((1, C_Q_RANK)), _const_spec((1, C_KV_RANK)),
                  _const_spec(wq.shape), _const_spec(wkv.shape)],
        out_specs=[row(w) for w in widths],
        out_shape=[jax.ShapeDtypeStruct((n, w), BF16) for w in widths],
        compiler_params=_params("parallel"),
        name="mla_prep",
    )(cq, ckv, kpe_raw, cos_t, sin_t, gq.reshape(1, -1), gkv.reshape(1, -1), wq, wkv)


def _mla_kernel(qn_ref, qpe_ref, kn_ref, kpe_ref, vt_ref, o_ref, q_ref, acc_ref, *, t, n_tiles, scale):
    i = pl.program_id(1)
    t2 = 2 * t
    krow = lax.broadcasted_iota(I32, (t2, t), 0)
    qcol = lax.broadcasted_iota(I32, (t2, t), 1)
    lane_lo = lax.broadcasted_iota(I32, (t, LANES), 1) < C_ROPE
    for h in range(C_HEADS):
        pair = qpe_ref[0, :, (h // 2) * LANES:(h // 2 + 1) * LANES]
        keep = lane_lo if h % 2 == 0 else jnp.logical_not(lane_lo)
        q_ref[h, :, :C_NOPE] = qn_ref[0, :, h * C_NOPE:(h + 1) * C_NOPE]
        q_ref[h, :, C_NOPE:] = jnp.where(keep, pair, jnp.zeros_like(pair))
    acc_ref[...] = jnp.zeros_like(acc_ref)

    def logits_of(g):
        rows = slice(g * t2, (g + 1) * t2)
        kpe_t = kpe_ref[0, rows, :]
        return [lax.dot_general(
            jnp.concatenate([kn_ref[0, rows, h * C_NOPE:(h + 1) * C_NOPE], kpe_t], axis=1),
            q_ref[h], _NT, preferred_element_type=F32) for h in range(C_HEADS)]

    def softmax_pv(g, logits, ms, ls, masked):
        new_m, new_l = [], []
        for h in range(C_HEADS):
            s = logits[h] * (scale * math.log2(math.e))
            if masked:
                s = jnp.where(krow + g * t2 <= qcol + i * t, s, NEG_BIG)
            m_new = jnp.maximum(ms[h], jnp.max(s, axis=0, keepdims=True))
            alpha = jnp.exp2(ms[h] - m_new)
            p = jnp.exp2(s - m_new)
            new_l.append(alpha * ls[h] + jnp.sum(p, axis=0, keepdims=True))
            new_m.append(m_new)
            hs = slice(h * C_V, (h + 1) * C_V)
            v_t = jnp.concatenate([vt_ref[0, 2 * g, hs, :], vt_ref[0, 2 * g + 1, hs, :]], axis=1)
            acc_ref[h] = alpha * acc_ref[h] + jnp.dot(v_t, p.astype(BF16), preferred_element_type=F32)
        return new_m, new_l

    def run(n_full):
        ms = [jnp.full((1, t), NEG_BIG, F32) for _ in range(C_HEADS)]
        ls = [jnp.zeros((1, t), F32) for _ in range(C_HEADS)]
        logits = logits_of(0)
        for g in range(n_full + 1):
            nxt = logits_of(g + 1) if g < n_full else None
            ms, ls = softmax_pv(g, logits, ms, ls, masked=(g == n_full))
            logits = nxt
        for h in range(C_HEADS):
            o_ref[0, :, h * C_V:(h + 1) * C_V] = (acc_ref[h] * (1.0 / ls[h])).T.astype(o_ref.dtype)

    for v in range(n_tiles // 2):
        pl.when(i // 2 == v)(functools.partial(run, v))


def _mla(qn, qpe, kn, kpe, vt, t):
    b, s, _ = qn.shape
    nt = s // t
    assert nt % 2 == 0
    kern = functools.partial(_mla_kernel, t=t, n_tiles=nt, scale=(C_NOPE + C_ROPE) ** -0.5)
    qspec = lambda w: pl.BlockSpec((1, t, w), lambda bi, i: (bi, i, 0))
    kspec = lambda w: pl.BlockSpec((1, s, w), lambda bi, i: (bi, 0, 0))
    return pl.pallas_call(
        kern,
        grid=(b, nt),
        in_specs=[qspec(qn.shape[2]), qspec(qpe.shape[2]), kspec(kn.shape[2]), kspec(kpe.shape[2]),
                  pl.BlockSpec((1, nt, C_HEADS * C_V, t), lambda bi, i: (bi, 0, 0, 0))],
        out_specs=qspec(C_HEADS * C_V),
        out_shape=jax.ShapeDtypeStruct((b, s, C_HEADS * C_V), BF16),
        scratch_shapes=[pltpu.VMEM((C_HEADS, t, C_NOPE + LANES), BF16),
                        pltpu.VMEM((C_HEADS, C_V, t), F32)],
        compiler_params=_params("parallel", "parallel"),
        name="mla_attention",
    )(qn, qpe, kn, kpe, vt)


def _merge_kernel(x_ref, g_ref, wg_ref, oa_ref, ob_ref, oc_ref, wa_ref, wb_ref, wc_ref, wo_ref,
                  out_ref, mixed_ref, *, col_chunk):
    x = x_ref[...]
    d = x.shape[1]
    h = _rms(x, g_ref[...]).astype(BF16)
    branches = ((oa_ref, wa_ref), (ob_ref, wb_ref), (oc_ref, wc_ref))
    for s in range(0, d, col_chunk):
        cs = slice(s, s + col_chunk)
        mixed = None
        for bidx, (o_ref, w_ref) in enumerate(branches):
            logits = jnp.dot(h, wg_ref[:, bidx * d + s:bidx * d + s + col_chunk],
                             preferred_element_type=F32)
            gate = 1.0 / (1.0 + jnp.exp(-logits))
            term = gate * jnp.dot(o_ref[...], w_ref[:, cs], preferred_element_type=F32)
            mixed = term if mixed is None else mixed + term
        mixed_ref[:, cs] = mixed.astype(BF16)
    out_ref[...] = x + jnp.dot(mixed_ref[...], wo_ref[...], preferred_element_type=F32)


def _merge(x, g, w_gate, oa, ob, oc, wa, wb, wc, wo, tm):
    n, d = x.shape
    row = lambda w: pl.BlockSpec((tm, w), lambda i: (i, 0))
    kern = functools.partial(_merge_kernel, col_chunk=256)
    return pl.pallas_call(
        kern,
        grid=(n // tm,),
        in_specs=[row(d), _const_spec((1, d)), _const_spec(w_gate.shape),
                  row(oa.shape[1]), row(ob.shape[1]), row(oc.shape[1]),
                  _const_spec(wa.shape), _const_spec(wb.shape), _const_spec(wc.shape),
                  _const_spec(wo.shape)],
        out_specs=row(d),
        out_shape=jax.ShapeDtypeStruct((n, d), F32),
        scratch_shapes=[pltpu.VMEM((tm, d), BF16)],
        compiler_params=_params("parallel"),
        name="merge_out_proj",
    )(x, g.reshape(1, d), w_gate, oa, ob, oc, wa, wb, wc, wo)


def _ffn_kernel(x_ref, g_ref, wg_ref, wu_ref, wd_ref, gf_ref, out_ref, act_ref, *, col_chunk,
                final_norm):
    x = x_ref[...]
    h = _rms(x, g_ref[...]).astype(BF16)
    dff = wg_ref.shape[1]
    for s in range(0, dff, col_chunk):
        cs = slice(s, s + col_chunk)
        gate = jnp.dot(h, wg_ref[:, cs], preferred_element_type=F32)
        up = jnp.dot(h, wu_ref[:, cs], preferred_element_type=F32)
        act_ref[:, cs] = (gate * (1.0 / (1.0 + jnp.exp(-gate))) * up).astype(BF16)
    y = x + jnp.dot(act_ref[...], wd_ref[...], preferred_element_type=F32)
    if final_norm:
        y = _rms(y, gf_ref[...])
    out_ref[...] = y


def _ffn(x, g, wg, wu, wd, gf, tm, final_norm):
    n, d = x.shape
    dff = wg.shape[1]
    row = pl.BlockSpec((tm, d), lambda i: (i, 0))
    kern = functools.partial(_ffn_kernel, col_chunk=256, final_norm=final_norm)
    return pl.pallas_call(
        kern,
        grid=(n // tm,),
        in_specs=[row, _const_spec((1, d)), _const_spec(wg.shape), _const_spec(wu.shape),
                  _const_spec(wd.shape), _const_spec((1, d))],
        out_specs=row,
        out_shape=jax.ShapeDtypeStruct((n, d), F32),
        scratch_shapes=[pltpu.VMEM((tm, dff), BF16)],
        compiler_params=_params("parallel"),
        name="swiglu_ffn",
    )(x, g.reshape(1, d), wg, wu, wd, gf.reshape(1, d))


def _t5_bucket(dist):
    max_exact = REL_BUCKETS // 2
    d = jnp.maximum(dist, 0)
    dl = jnp.maximum(d, max_exact).astype(F32)
    large = max_exact + (jnp.log(dl / max_exact) / math.log(REL_MAX_DIST / max_exact)
                         * (REL_BUCKETS - max_exact)).astype(I32)
    large = jnp.minimum(large, REL_BUCKETS - 1)
    return jnp.where(d < max_exact, d, large)


def _swap_halves(w):
    half = w.shape[-1] // 2
    return jnp.concatenate([w[..., half:], w[..., :half]], axis=-1)


def _in_proj_weights(w_in, d_model):
    splits = (A_HEADS * A_HEAD_DIM, A_HEAD_DIM, A_HEAD_DIM, IDX_HEADS * IDX_DIM, IDX_DIM, IDX_HEADS,
              B_HEADS * B_KEY_DIM, B_HEADS * B_KEY_DIM, B_HEADS * B_VAL_DIM, B_HEADS * B_VAL_DIM,
              C_Q_RANK, C_KV_RANK, C_ROPE, N_BRANCH * d_model)
    st = np.concatenate([[0], np.cumsum(splits)])
    col = lambda a, b: w_in[:, :, st[a]:st[b]]
    ik, pe = col(4, 5), col(12, 13)
    iw = jnp.pad(col(5, 6), ((0, 0), (0, 0), (0, LANES - IDX_HEADS)))
    pe_sw = _swap_halves(pe)
    pieces = [col(0, 4), ik, ik, iw, col(6, 12)] + [pe] * (LANES // C_ROPE) + [pe_sw] * (LANES // C_ROPE)
    w_proj = jnp.concatenate(pieces, axis=2).astype(BF16)
    widths = list(splits[:4]) + [LANES, LANES] + list(splits[6:12]) + [2 * LANES]
    dtypes = [BF16] * 5 + [F32] * 8
    return w_proj, widths, dtypes, col(13, 14).astype(BF16)


def kernel(x, positions, w_in, w_up_a, w_up_b, w_up_c, w_out, mla_q_norm, mla_w_qb, mla_kv_norm,
           mla_w_kvb, hgrn_lb_logits, hgrn_out_norm, rel_bias, attn_norm, ffn_norm, w_ffn_gate,
           w_ffn_up, w_ffn_down, final_norm):
    bsz, s_len, d_model = x.shape
    depth = w_in.shape[0]
    n = bsz * s_len
    t = LANES

    w_proj, widths, dtypes, w_gate = _in_proj_weights(w_in, d_model)
    wq = mla_w_qb.reshape(depth, C_Q_RANK, C_HEADS, C_NOPE + C_ROPE)
    q_pe = wq[..., C_NOPE:]
    w_qb = jnp.concatenate([wq[..., :C_NOPE].reshape(depth, C_Q_RANK, -1),
                            q_pe.reshape(depth, C_Q_RANK, -1),
                            _swap_halves(q_pe).reshape(depth, C_Q_RANK, -1)], axis=2).astype(BF16)
    wkv = mla_w_kvb.reshape(depth, C_KV_RANK, C_HEADS, C_NOPE + C_V)
    w_kvb = jnp.concatenate([wkv[..., :C_NOPE].reshape(depth, C_KV_RANK, -1),
                             wkv[..., C_NOPE:].reshape(depth, C_KV_RANK, -1)], axis=2).astype(BF16)
    bf = lambda w: w.astype(BF16)
    w_up_a, w_up_b, w_up_c, w_out = bf(w_up_a), bf(w_up_b), bf(w_up_c), bf(w_out)
    w_ffn_gate, w_ffn_up, w_ffn_down = bf(w_ffn_gate), bf(w_ffn_up), bf(w_ffn_down)

    p_lb = jax.nn.softmax(hgrn_lb_logits.astype(F32), axis=0)
    lower_bounds = jnp.cumsum(p_lb, axis=0) - p_lb[0:1]
    inv_freq = ROPE_THETA ** (-jnp.arange(0, C_ROPE, 2, dtype=F32) / C_ROPE)
    ang = positions.astype(F32)[..., None] * inv_freq
    cos, sin = jnp.cos(ang), jnp.sin(ang)
    reps = LANES // C_ROPE
    cos_t = jnp.tile(jnp.concatenate([cos, cos], axis=-1), (1, 1, reps)).reshape(n, LANES)
    sin_t = jnp.tile(jnp.concatenate([-sin, sin], axis=-1), (1, 1, reps)).reshape(n, LANES)
    q_idx = jnp.arange(t, dtype=I32)[None, :]
    k_idx = jnp.arange(t, dtype=I32)[:, None]
    def tab(dist):
        bucket = _t5_bucket(dist)[:, None, :]
        out = jnp.zeros((t, A_HEADS, t), F32)
        for b in range(REL_BUCKETS):
            out = jnp.where(bucket == b, rel_bias[b].astype(F32)[None, :, None], out)
        return out.reshape(t, A_HEADS * t)
    far = jnp.broadcast_to(rel_bias[REL_BUCKETS - 1].astype(F32)[None, :, None],
                           (t, A_HEADS, t)).reshape(t, A_HEADS * t)
    bias_tab = jnp.stack([tab(q_idx + t - k_idx), tab(q_idx - k_idx), far]) * math.log2(math.e)

    tm = min(512, n)
    t_mla = min(256, s_len)
    x2 = x.reshape(n, d_model)
    r3 = lambda a: a.reshape(bsz, s_len, a.shape[-1])
    key_major = lambda a, tile: a.reshape(bsz, s_len // tile, tile, a.shape[-1]).swapaxes(2, 3)
    for l in range(depth):
        (qa, ka, va, iq, ik2, iw, bq, bfr, bi, bg, cq, ckv, kpe_raw) = _norm_matmul(
            x2, attn_norm[l], w_proj[l], widths, dtypes, tm)
        o_a = _dsa(r3(qa), r3(iq), r3(iw), r3(ka), key_major(va, t), r3(ik2), bias_tab)
        o_b = _hgrn(r3(bq), r3(bfr), r3(bi), r3(bg), lower_bounds[l], hgrn_out_norm[l],
                    sb=min(512, s_len))
        qn, qpe, kn, vc, kpe = _mla_prep(cq, ckv, kpe_raw, cos_t, sin_t, mla_q_norm[l], mla_kv_norm[l],
                                         w_qb[l], w_kvb[l], tm)
        o_c = _mla(r3(qn), r3(qpe), r3(kn), r3(kpe), key_major(vc, t_mla), t_mla)
        x2 = _merge(x2, attn_norm[l], w_gate[l], o_a.reshape(n, -1), o_b.reshape(n, -1),
                    o_c.reshape(n, -1), w_up_a[l], w_up_b[l], w_up_c[l], w_out[l], min(256, n))
        x2 = _ffn(x2, ffn_norm[l], w_ffn_gate[l], w_ffn_up[l], w_ffn_down[l], final_norm,
                  min(256, n), final_norm=(l == depth - 1))
    return x2.reshape(bsz, s_len, d_model)
```

```python
import functools
import math

import jax
import jax.numpy as jnp
import numpy as np
from jax import lax
from jax.experimental import pallas as pl
from jax.experimental.pallas import tpu as pltpu

F32 = jnp.float32
BF16 = jnp.bfloat16
I32 = jnp.int32

A_HEADS = 4
A_HEAD_DIM = 128
IDX_HEADS = 8
IDX_DIM = 64
TOPK_MAX = 256
B_HEADS = 4
B_KEY_DIM = 128
B_VAL_DIM = 128
B_CHUNK = 64
C_HEADS = 4
C_Q_RANK = 384
C_KV_RANK = 256
C_NOPE = 128
C_ROPE = 64
C_V = 128
ROPE_THETA = 10000.0
REL_BUCKETS = 32
REL_MAX_DIST = 128
N_BRANCH = 3
EPS = 1e-6
NEG_BIG = -1e30
LB_FLOOR = 1e-30

LANES = 128
SUBLANES = 8
VMEM_LIMIT = 56 * 1024 * 1024
INT_MIN = np.int32(-2 ** 31)

_NT = (((1,), (1,)), ((), ()))


def _params(*sem):
    return pltpu.CompilerParams(dimension_semantics=sem, vmem_limit_bytes=VMEM_LIMIT)


def _rms(x, g):
    return x * lax.rsqrt(jnp.mean(x * x, axis=-1, keepdims=True) + EPS) * g


def _const_spec(shape):
    nd = len(shape)
    return pl.BlockSpec(shape, lambda *_: (0,) * nd)


def _fold8(x, op):
    r, c = x.shape
    return op(x.reshape(r // SUBLANES, SUBLANES, c), axis=0)


def _store_key_major(o_ref, rows, tile):
    for u in range(rows.shape[0] // tile):
        o_ref[u] = rows[u * tile:(u + 1) * tile, :].T.astype(o_ref.dtype)


def _norm_matmul_kernel(x_ref, g_ref, *refs, widths, key_tiles, col_chunk):
    w_refs, out_refs = refs[:len(widths)], refs[len(widths):]
    h = _rms(x_ref[...], g_ref[...]).astype(BF16)
    k = 0
    for w_ref, w_widths in zip(w_refs, widths):
        c0 = 0
        for w in w_widths:
            o_ref = out_refs[k]
            if key_tiles[k]:
                _store_key_major(o_ref, jnp.dot(h, w_ref[:, c0:c0 + w], preferred_element_type=F32),
                                 key_tiles[k])
            else:
                for s in range(0, w, col_chunk):
                    e = min(s + col_chunk, w)
                    o_ref[:, s:e] = jnp.dot(h, w_ref[:, c0 + s:c0 + e],
                                            preferred_element_type=F32).astype(o_ref.dtype)
            c0 += w
            k += 1


def _norm_matmul(x, g, ws, widths, dtypes, key_tiles, tm):
    n, k = x.shape
    flat = [wd for w_widths in widths for wd in w_widths]
    assert n % tm == 0 and all(w.shape == (k, sum(ww)) for w, ww in zip(ws, widths))
    kern = functools.partial(_norm_matmul_kernel, widths=tuple(tuple(ww) for ww in widths),
                             key_tiles=tuple(key_tiles), col_chunk=512)
    specs, shapes = [], []
    for wd, dt, kt in zip(flat, dtypes, key_tiles):
        if kt:
            specs.append(pl.BlockSpec((tm // kt, wd, kt), lambda i: (i, 0, 0)))
            shapes.append(jax.ShapeDtypeStruct((n // kt, wd, kt), dt))
        else:
            specs.append(pl.BlockSpec((tm, wd), lambda i: (i, 0)))
            shapes.append(jax.ShapeDtypeStruct((n, wd), dt))
    return pl.pallas_call(
        kern,
        grid=(n // tm,),
        in_specs=[pl.BlockSpec((tm, k), lambda i: (i, 0)), _const_spec((1, k))]
                 + [_const_spec(w.shape) for w in ws],
        out_specs=specs,
        out_shape=shapes,
        compiler_params=_params("parallel"),
        name="norm_proj",
    )(x, g.reshape(1, k), *ws)


def _dsa_kernel(qa_ref, iq_ref, iw_ref, ka_ref, vt_ref, ik_ref, bias_ref, o_ref,
                keys_ref, s_ref, qm_ref, qs_ref, wt_ref, thr_ref, ties_ref,
                *, topk, n_blocks, group, scale, idx_scale):
    t = LANES
    gt = group * t
    i = pl.program_id(1)
    n_groups = (i + group) // group
    max_groups = n_blocks // group
    krow = lax.broadcasted_iota(I32, (t, t), 0)
    qcol = lax.broadcasted_iota(I32, (t, t), 1)
    lane_lo = qcol < IDX_DIM

    for h in range(IDX_HEADS):
        pair = iq_ref[0, :, (h // 2) * t:(h // 2 + 1) * t]
        keep = lane_lo if h % 2 == 0 else jnp.logical_not(lane_lo)
        qm_ref[h * t:(h + 1) * t, :] = jnp.where(keep, pair, jnp.zeros_like(pair))
    for h in range(A_HEADS):
        qs_ref[h * t:(h + 1) * t, :] = qa_ref[0, :, h * t:(h + 1) * t]
    wt_ref[...] = iw_ref[0].T

    def index_logits(g):
        return lax.dot_general(ik_ref[0, g * gt:(g + 1) * gt, :], qm_ref[...], _NT,
                               preferred_element_type=F32)

    def score_group(g, lg):
        for u in range(group):
            j = g * group + u
            acc = jnp.zeros((t, t), F32)
            for h in range(IDX_HEADS):
                acc = acc + jnp.maximum(lg[u * t:(u + 1) * t, h * t:(h + 1) * t], 0.0) * wt_ref[h:h + 1, :]
            score = acc * idx_scale
            score = jnp.where(krow + j * t <= qcol + i * t, score, NEG_BIG)
            score = jnp.where(score == 0.0, 0.0, score)
            bits = pltpu.bitcast(score, I32)
            keys_ref[j] = bits ^ ((bits >> 31) & np.int32(0x7FFFFFFF))

    def score_all(ng):
        lg = index_logits(0)
        for g in range(ng):
            nxt = index_logits(g + 1) if g + 1 < ng else None
            score_group(g, lg)
            lg = nxt

    for v in range(1, max_groups + 1):
        pl.when(n_groups == v)(functools.partial(score_all, v))

    thr_ref[...] = jnp.full(thr_ref.shape, INT_MIN, I32)
    ties_ref[...] = jnp.zeros(ties_ref.shape, F32)

    def count(n, pred):
        cnt = jnp.zeros((SUBLANES, t), F32)
        for j in range(n):
            cnt = cnt + _fold8(jnp.where(pred(keys_ref[j]), 1.0, 0.0), jnp.sum)
        return jnp.sum(cnt, axis=0, keepdims=True)

    def search(n):
        def search_pass(b, thr):
            cand = thr + lax.shift_left(np.int32(1), 31 - b)
            return jnp.where(count(n, lambda key: key >= cand) >= topk, cand, thr)

        thr = lax.fori_loop(0, 32, search_pass, jnp.full((1, t), INT_MIN, I32))
        thr_ref[...] = jnp.broadcast_to(thr, thr_ref.shape)
        ties_ref[...] = jnp.broadcast_to(topk - count(n, lambda key: key > thr), ties_ref.shape)

    for c in range(n_blocks):
        if (c + 1) * t > topk:
            pl.when(i == c)(functools.partial(search, c + 1))

    thr = thr_ref[0:1, :]
    n_ties = ties_ref[0:1, :]

    r2 = lax.broadcasted_iota(I32, (2 * t, t), 0)
    c2 = lax.broadcasted_iota(I32, (2 * t, t), 1)
    tie_lhs = jnp.where(jnp.logical_or(r2 >= t, c2 < r2), 1.0, 0.0).astype(BF16)

    def sweep_a_mxu(g):
        keys = [keys_ref[g * group + u] for u in range(group)]
        eqs = [key == thr for key in keys]
        eq_all = jnp.concatenate([jnp.where(eq, 1.0, 0.0).astype(BF16) for eq in eqs], axis=1)
        pref = jnp.dot(tie_lhs, eq_all, preferred_element_type=F32)
        s_grp = lax.dot_general(ka_ref[0, g * gt:(g + 1) * gt, :], qs_ref[...], _NT,
                                preferred_element_type=F32)
        return keys, eqs, pref, s_grp

    def sweep_a_vpu(g, operands, seen, mx):
        keys, eqs, pref, s_grp = operands
        for u in range(group):
            j = g * group + u
            us = slice(u * t, (u + 1) * t)
            rank = jnp.where(keys[u] > thr, -1.0, jnp.where(eqs[u], seen + pref[:t, us], 3e38))
            rank = jnp.where(krow + j * t <= qcol + i * t, rank, 3e38)
            valid = rank < n_ties
            which = jnp.where(j == i, 1, jnp.where(j == i - 1, 0, 2))
            s_all = s_grp[us, :] * (scale * math.log2(math.e)) + bias_ref[which]
            for h in range(A_HEADS):
                s_h = jnp.where(valid, s_all[:, h * t:(h + 1) * t], NEG_BIG)
                s_ref[j, :, h * t:(h + 1) * t] = s_h
                mx[h] = jnp.maximum(mx[h], _fold8(s_h, jnp.max))
            seen = seen + pref[t:t + 1, us]
        return seen, mx

    def sweep_b(g, m_all, l8):
        ps = []
        for u in range(group):
            p = jnp.exp2(s_ref[g * group + u] - m_all)
            l8 = l8 + _fold8(p, jnp.sum)
            ps.append(p.astype(BF16))
        vt_grp = jnp.concatenate([vt_ref[0, g * group + u] for u in range(group)], axis=1)
        return l8, jnp.dot(vt_grp, jnp.concatenate(ps, axis=0), preferred_element_type=F32)

    def attend(ng):
        seen = jnp.zeros((1, t), F32)
        mx = [jnp.full((SUBLANES, t), NEG_BIG, F32) for _ in range(A_HEADS)]
        operands = sweep_a_mxu(0)
        for g in range(ng):
            nxt = sweep_a_mxu(g + 1) if g + 1 < ng else None
            seen, mx = sweep_a_vpu(g, operands, seen, mx)
            operands = nxt
        m_all = jnp.concatenate([jnp.max(m, axis=0, keepdims=True) for m in mx], axis=1)
        l8 = jnp.zeros((SUBLANES, A_HEADS * t), F32)
        acc = None
        for g in range(ng):
            l8, pv = sweep_b(g, m_all, l8)
            acc = pv if acc is None else acc + pv
        out = acc * (1.0 / jnp.sum(l8, axis=0, keepdims=True))
        for h in range(A_HEADS):
            o_ref[0, :, h * t:(h + 1) * t] = out[:, h * t:(h + 1) * t].T.astype(o_ref.dtype)

    for v in range(1, max_groups + 1):
        pl.when(n_groups == v)(functools.partial(attend, v))


def _dsa(qa, iq, iw, ka, vt, ik2, bias_tab):
    b, s, _ = qa.shape
    t = LANES
    nb = s // t
    group = math.gcd(nb, 4)
    topk = min(TOPK_MAX, s // 4)
    kern = functools.partial(_dsa_kernel, topk=float(topk), n_blocks=nb, group=group,
                             scale=A_HEAD_DIM ** -0.5,
                             idx_scale=(IDX_DIM ** -0.5) * (IDX_HEADS ** -0.5))
    qspec = lambda w: pl.BlockSpec((1, t, w), lambda bi, i: (bi, i, 0))
    kspec = pl.BlockSpec((1, s, t), lambda bi, i: (bi, 0, 0))
    return pl.pallas_call(
        kern,
        grid=(b, nb),
        in_specs=[qspec(A_HEADS * t), qspec(IDX_HEADS * IDX_DIM), qspec(t), kspec,
                  pl.BlockSpec((1, nb, t, t), lambda bi, i: (bi, 0, 0, 0)), kspec,
                  _const_spec(bias_tab.shape)],
        out_specs=qspec(A_HEADS * t),
        out_shape=jax.ShapeDtypeStruct((b, s, A_HEADS * t), BF16),
        scratch_shapes=[pltpu.VMEM((nb, t, t), I32),
                        pltpu.VMEM((nb, t, A_HEADS * t), F32),
                        pltpu.VMEM((IDX_HEADS * t, t), BF16),
                        pltpu.VMEM((A_HEADS * t, t), BF16),
                        pltpu.VMEM((t, t), F32),
                        pltpu.VMEM((SUBLANES, t), I32),
                        pltpu.VMEM((SUBLANES, t), F32)],
        compiler_params=_params("parallel", "parallel"),
        name="dsa_attention",
    )(qa, iq, iw, ka, vt, ik2, bias_tab)


def _hgrn_kernel(q_ref, f_ref, i_ref, g_ref, lb_ref, gain_ref, o_ref, state_ref, *, n_chunks, n_seq):
    c = B_CHUNK
    rc = n_seq * c
    kd, vd = B_KEY_DIM, B_VAL_DIM
    w = B_HEADS * kd

    @pl.when(pl.program_id(1) == 0)
    def _():
        state_ref[...] = jnp.zeros_like(state_ref)

    row = lax.broadcasted_iota(I32, (rc, rc), 0)
    col = lax.broadcasted_iota(I32, (rc, rc), 1)
    same_seq = (row >= col) if n_seq == 1 else jnp.logical_and(
        row >= col, lax.shift_right_logical(row, 6) == lax.shift_right_logical(col, 6))
    causal = same_seq
    assert c == 64
    tril = jnp.where(causal, 1.0, 0.0).astype(BF16)
    seqs = [slice(b * c, (b + 1) * c) for b in range(n_seq)]
    seq_of_row = lax.shift_right_logical(lax.broadcasted_iota(I32, (rc, w), 0), 6)

    def per_seq_row(x, r):
        return jnp.concatenate([jnp.broadcast_to(x[b * c + r:b * c + r + 1, :], (c, x.shape[1]))
                                for b in range(n_seq)], axis=0)

    def stacked(ref, rows):
        return jnp.concatenate([ref[b, rows, :] for b in range(n_seq)], axis=0)
    q_scale = kd ** -0.5
    lb = lb_ref[...]
    lb_floor = jnp.maximum(lb, LB_FLOOR)
    one_m_lb = 1.0 - lb
    heads = [slice(h * kd, (h + 1) * kd) for h in range(B_HEADS)]

    def stage_decay(ci):
        rows = slice(ci * c, (ci + 1) * c)
        fr = stacked(f_ref, rows)
        z = jnp.exp(-jnp.abs(fr))
        r = 1.0 / (1.0 + z)
        sig_pos = jnp.where(fr >= 0, r, z * r)
        sig_neg = jnp.where(fr >= 0, z * r, r)
        log_f = jnp.log(lb_floor + one_m_lb * sig_pos)
        k_in = one_m_lb * sig_neg
        hi = log_f.astype(BF16)
        rest = log_f - hi.astype(F32)
        mid = rest.astype(BF16)
        lo = (rest - mid.astype(F32)).astype(BF16)
        cs = jnp.dot(tril, jnp.concatenate([hi, mid, lo], axis=1), preferred_element_type=F32)
        return k_in, cs

    def apply_update(pending):
        if pending is not None:
            d_last, st, upd = pending
            for b in range(n_seq):
                for h, hs in enumerate(heads):
                    state_ref[b, h] = d_last[b * c:b * c + 1, hs] * st[b][h] + upd[b][h]

    def stage_scores(ci, decay, pending):
        rows = slice(ci * c, (ci + 1) * c)
        k_in, cs = decay
        bsum = cs[:, :w] + cs[:, w:2 * w] + cs[:, 2 * w:]
        b_mid = per_seq_row(bsum, c // 2 - 1)
        b_last = per_seq_row(bsum, c - 1)
        qs = stacked(q_ref, rows) * q_scale
        v = stacked(i_ref, rows)
        v_b = v.astype(BF16)
        v_t = v.T.astype(BF16)
        q_mid = (qs * jnp.exp(bsum - b_mid)).astype(BF16)
        k_mid = (k_in * jnp.exp(b_mid - bsum)).astype(BF16)
        q_dec = (qs * jnp.exp(bsum)).astype(BF16)
        k_end = (k_in * jnp.exp(b_last - bsum)).astype(BF16)
        d_last = jnp.exp(b_last)
        apply_update(pending)
        st = [[state_ref[b, h] for h in range(B_HEADS)] for b in range(n_seq)]
        attn = [lax.dot_general(q_mid[:, hs], k_mid[:, hs], _NT, preferred_element_type=F32)
                for hs in heads]
        inter = [jnp.concatenate(
            [lax.dot_general(q_dec[sq, hs], st[b][h].astype(BF16), _NT, preferred_element_type=F32)
             for b, sq in enumerate(seqs)], axis=0) for h, hs in enumerate(heads)]
        k_seq = [k_end if n_seq == 1 else jnp.where(seq_of_row == b, k_end, jnp.zeros_like(k_end))
                 for b in range(n_seq)]
        upd = [[jnp.dot(v_t[hs, :], k_seq[b][:, hs], preferred_element_type=F32) for hs in heads]
               for b in range(n_seq)]
        return (attn, inter, v_b), (d_last, st, upd)

    def stage_output(ci, scores):
        rows = slice(ci * c, (ci + 1) * c)
        attn, inter, v_b = scores
        attn = [jnp.where(causal, a, 0.0).astype(BF16) for a in attn]
        outs = [jnp.dot(attn[h], v_b[:, hs], preferred_element_type=F32) + inter[h]
                for h, hs in enumerate(heads)]
        o = jnp.concatenate([_rms(o_h, gain_ref[...]) for o_h in outs], axis=1)
        g = stacked(g_ref, rows)
        o = (o * (g * (1.0 / (1.0 + jnp.exp(-g))))).astype(o_ref.dtype)
        for b, sq in enumerate(seqs):
            o_ref[b, rows, :] = o[sq, :]

    decay, scores, pending = {}, {}, None
    for step in range(n_chunks + 2):
        if step < n_chunks:
            decay[step] = stage_decay(step)
        if 0 <= step - 1 < n_chunks:
            scores[step - 1], pending = stage_scores(step - 1, decay.pop(step - 1), pending)
        if 0 <= step - 2 < n_chunks:
            stage_output(step - 2, scores.pop(step - 2))
    apply_update(pending)


def _hgrn(bq, bf, bi, bg, lb, gain, sb):
    b, s, w = bq.shape
    assert s % sb == 0 and sb % B_CHUNK == 0
    n_seq = 2 if b % 2 == 0 else 1
    kern = functools.partial(_hgrn_kernel, n_chunks=sb // B_CHUNK, n_seq=n_seq)
    spec = pl.BlockSpec((n_seq, sb, w), lambda bi_, si: (bi_, si, 0))
    return pl.pallas_call(
        kern,
        grid=(b // n_seq, s // sb),
        in_specs=[spec, spec, spec, spec, _const_spec((1, w)), _const_spec((1, B_VAL_DIM))],
        out_specs=spec,
        out_shape=jax.ShapeDtypeStruct((b, s, w), BF16),
        scratch_shapes=[pltpu.VMEM((n_seq, B_HEADS, B_VAL_DIM, B_KEY_DIM), F32)],
        compiler_params=_params("parallel", "arbitrary"),
        name="hgrn2",
    )(bq, bf, bi, bg, lb.reshape(1, w), gain.reshape(1, B_VAL_DIM))


def _mla_prep_kernel(cq_ref, ckv_ref, kpe_raw_ref, cos_ref, sin_ref, gq_ref, gkv_ref, wq_ref, wkv_ref,
                     qn_ref, qpe_ref, kn_ref, vt_ref, kpe_ref, *, key_tile):
    nw = C_HEADS * C_NOPE
    pw = C_HEADS * C_ROPE
    cos, sin = cos_ref[...], sin_ref[...]
    cos2 = jnp.concatenate([cos] * (pw // LANES), axis=1)
    sin2 = jnp.concatenate([sin] * (pw // LANES), axis=1)
    q = jnp.dot(_rms(cq_ref[...], gq_ref[...]).astype(BF16), wq_ref[...], preferred_element_type=F32)
    qn_ref[...] = q[:, :nw].astype(qn_ref.dtype)
    qpe_ref[...] = (q[:, nw:nw + pw] * cos2 + q[:, nw + pw:] * sin2).astype(qpe_ref.dtype)
    kv = jnp.dot(_rms(ckv_ref[...], gkv_ref[...]).astype(BF16), wkv_ref[...], preferred_element_type=F32)
    kn_ref[...] = kv[:, :nw].astype(kn_ref.dtype)
    _store_key_major(vt_ref, kv[:, nw:], key_tile)
    kpe_ref[...] = (kpe_raw_ref[:, :LANES] * cos + kpe_raw_ref[:, LANES:] * sin).astype(kpe_ref.dtype)


def _mla_prep(cq, ckv, kpe_raw, cos_t, sin_t, gq, gkv, wq, wkv, tm, key_tile):
    n = cq.shape[0]
    nw = C_HEADS * C_NOPE
    vw = C_HEADS * C_V
    row = lambda w: pl.BlockSpec((tm, w), lambda i: (i, 0))
    widths = (nw, C_HEADS * C_ROPE, nw, None, LANES)
    vt_spec = pl.BlockSpec((tm // key_tile, vw, key_tile), lambda i: (i, 0, 0))
    vt_shape = jax.ShapeDtypeStruct((n // key_tile, vw, key_tile), BF16)
    return pl.pallas_call(
        functools.partial(_mla_prep_kernel, key_tile=key_tile),
        grid=(n // tm,),
        in_specs=[row(C_Q_RANK), row(C_KV_RANK), row(2 * LANES), row(LANES), row(LANES),
                  _const_spec((1, C_Q_RANK)), _const_spec((1, C_KV_RANK)),
                  _const_spec(wq.shape), _const_spec(wkv.shape)],
        out_specs=[vt_spec if w is None else row(w) for w in widths],
        out_shape=[vt_shape if w is None else jax.ShapeDtypeStruct((n, w), BF16) for w in widths],
        compiler_params=_params("parallel"),
        name="mla_prep",
    )(cq, ckv, kpe_raw, cos_t, sin_t, gq.reshape(1, -1), gkv.reshape(1, -1), wq, wkv)


def _mla_kernel(qn_ref, qpe_ref, kn_ref, kpe_ref, vt_ref, o_ref, q_ref, acc_ref, *, t, n_tiles, scale):
    i = pl.program_id(1)
    t2 = 2 * t
    krow = lax.broadcasted_iota(I32, (t2, t), 0)
    qcol = lax.broadcasted_iota(I32, (t2, t), 1)
    lane_lo = lax.broadcasted_iota(I32, (t, LANES), 1) < C_ROPE
    for h in range(C_HEADS):
        pair = qpe_ref[0, :, (h // 2) * LANES:(h // 2 + 1) * LANES]
        keep = lane_lo if h % 2 == 0 else jnp.logical_not(lane_lo)
        q_ref[h, :, :C_NOPE] = qn_ref[0, :, h * C_NOPE:(h + 1) * C_NOPE]
        q_ref[h, :, C_NOPE:] = jnp.where(keep, pair, jnp.zeros_like(pair))
    acc_ref[...] = jnp.zeros_like(acc_ref)

    def logits_of(g):
        rows = slice(g * t2, (g + 1) * t2)
        kpe_t = kpe_ref[0, rows, :]
        return [lax.dot_general(
            jnp.concatenate([kn_ref[0, rows, h * C_NOPE:(h + 1) * C_NOPE], kpe_t], axis=1),
            q_ref[h], _NT, preferred_element_type=F32) for h in range(C_HEADS)]

    def softmax_pv(g, logits, ms, ls, masked):
        new_m, new_l = [], []
        for h in range(C_HEADS):
            s = logits[h] * (scale * math.log2(math.e))
            if masked:
                s = jnp.where(krow + g * t2 <= qcol + i * t, s, NEG_BIG)
            m_new = jnp.maximum(ms[h], jnp.max(s, axis=0, keepdims=True))
            alpha = jnp.exp2(ms[h] - m_new)
            p = jnp.exp2(s - m_new)
            new_l.append(alpha * ls[h] + jnp.sum(p, axis=0, keepdims=True))
            new_m.append(m_new)
            hs = slice(h * C_V, (h + 1) * C_V)
            v_t = jnp.concatenate([vt_ref[0, 2 * g, hs, :], vt_ref[0, 2 * g + 1, hs, :]], axis=1)
            acc_ref[h] = alpha * acc_ref[h] + jnp.dot(v_t, p.astype(BF16), preferred_element_type=F32)
        return new_m, new_l

    def run(n_full):
        ms = [jnp.full((1, t), NEG_BIG, F32) for _ in range(C_HEADS)]
        ls = [jnp.zeros((1, t), F32) for _ in range(C_HEADS)]
        logits = logits_of(0)
        for g in range(n_full + 1):
            nxt = logits_of(g + 1) if g < n_full else None
            ms, ls = softmax_pv(g, logits, ms, ls, masked=(g == n_full))
            logits = nxt
        for h in range(C_HEADS):
            o_ref[0, :, h * C_V:(h + 1) * C_V] = (acc_ref[h] * (1.0 / ls[h])).T.astype(o_ref.dtype)

    for v in range(n_tiles // 2):
        pl.when(i // 2 == v)(functools.partial(run, v))


def _mla(qn, qpe, kn, kpe, vt, t):
    b, s, _ = qn.shape
    nt = s // t
    assert nt % 2 == 0
    kern = functools.partial(_mla_kernel, t=t, n_tiles=nt, scale=(C_NOPE + C_ROPE) ** -0.5)
    qspec = lambda w: pl.BlockSpec((1, t, w), lambda bi, i: (bi, i, 0))
    kspec = lambda w: pl.BlockSpec((1, s, w), lambda bi, i: (bi, 0, 0))
    return pl.pallas_call(
        kern,
        grid=(b, nt),
        in_specs=[qspec(qn.shape[2]), qspec(qpe.shape[2]), kspec(kn.shape[2]), kspec(kpe.shape[2]),
                  pl.BlockSpec((1, nt, C_HEADS * C_V, t), lambda bi, i: (bi, 0, 0, 0))],
        out_specs=qspec(C_HEADS * C_V),
        out_shape=jax.ShapeDtypeStruct((b, s, C_HEADS * C_V), BF16),
        scratch_shapes=[pltpu.VMEM((C_HEADS, t, C_NOPE + LANES), BF16),
                        pltpu.VMEM((C_HEADS, C_V, t), F32)],
        compiler_params=_params("parallel", "parallel"),
        name="mla_attention",
    )(qn, qpe, kn, kpe, vt)


def _merge_kernel(x_ref, g_ref, wg_ref, oa_ref, ob_ref, oc_ref, wa_ref, wb_ref, wc_ref, wo_ref,
                  out_ref, mixed_ref, *, col_chunk):
    x = x_ref[...]
    d = x.shape[1]
    h = _rms(x, g_ref[...]).astype(BF16)
    branches = ((oa_ref, wa_ref), (ob_ref, wb_ref), (oc_ref, wc_ref))
    for s in range(0, d, col_chunk):
        cs = slice(s, s + col_chunk)
        mixed = None
        for bidx, (o_ref, w_ref) in enumerate(branches):
            logits = jnp.dot(h, wg_ref[:, bidx * d + s:bidx * d + s + col_chunk],
                             preferred_element_type=F32)
            gate = 1.0 / (1.0 + jnp.exp(-logits))
            term = gate * jnp.dot(o_ref[...], w_ref[:, cs], preferred_element_type=F32)
            mixed = term if mixed is None else mixed + term
        mixed_ref[:, cs] = mixed.astype(BF16)
    out_ref[...] = x + jnp.dot(mixed_ref[...], wo_ref[...], preferred_element_type=F32)


def _merge(x, g, w_gate, oa, ob, oc, wa, wb, wc, wo, tm):
    n, d = x.shape
    row = lambda w: pl.BlockSpec((tm, w), lambda i: (i, 0))
    kern = functools.partial(_merge_kernel, col_chunk=256)
    return pl.pallas_call(
        kern,
        grid=(n // tm,),
        in_specs=[row(d), _const_spec((1, d)), _const_spec(w_gate.shape),
                  row(oa.shape[1]), row(ob.shape[1]), row(oc.shape[1]),
                  _const_spec(wa.shape), _const_spec(wb.shape), _const_spec(wc.shape),
                  _const_spec(wo.shape)],
        out_specs=row(d),
        out_shape=jax.ShapeDtypeStruct((n, d), F32),
        scratch_shapes=[pltpu.VMEM((tm, d), BF16)],
        compiler_params=_params("parallel"),
        name="merge_out_proj",
    )(x, g.reshape(1, d), w_gate, oa, ob, oc, wa, wb, wc, wo)


def _ffn_kernel(x_ref, g_ref, wg_ref, wu_ref, wd_ref, gf_ref, out_ref, act_ref, *, col_chunk,
                final_norm):
    x = x_ref[...]
    h = _rms(x, g_ref[...]).astype(BF16)
    dff = wg_ref.shape[1]
    for s in range(0, dff, col_chunk):
        cs = slice(s, s + col_chunk)
        gate = jnp.dot(h, wg_ref[:, cs], preferred_element_type=F32)
        up = jnp.dot(h, wu_ref[:, cs], preferred_element_type=F32)
        act_ref[:, cs] = (gate * (1.0 / (1.0 + jnp.exp(-gate))) * up).astype(BF16)
    y = x + jnp.dot(act_ref[...], wd_ref[...], preferred_element_type=F32)
    if final_norm:
        y = _rms(y, gf_ref[...])
    out_ref[...] = y


def _ffn(x, g, wg, wu, wd, gf, tm, final_norm):
    n, d = x.shape
    dff = wg.shape[1]
    row = pl.BlockSpec((tm, d), lambda i: (i, 0))
    kern = functools.partial(_ffn_kernel, col_chunk=256, final_norm=final_norm)
    return pl.pallas_call(
        kern,
        grid=(n // tm,),
        in_specs=[row, _const_spec((1, d)), _const_spec(wg.shape), _const_spec(wu.shape),
                  _const_spec(wd.shape), _const_spec((1, d))],
        out_specs=row,
        out_shape=jax.ShapeDtypeStruct((n, d), F32),
        scratch_shapes=[pltpu.VMEM((tm, dff), BF16)],
        compiler_params=_params("parallel"),
        name="swiglu_ffn",
    )(x, g.reshape(1, d), wg, wu, wd, gf.reshape(1, d))


def _t5_bucket(dist):
    max_exact = REL_BUCKETS // 2
    d = jnp.maximum(dist, 0)
    dl = jnp.maximum(d, max_exact).astype(F32)
    large = max_exact + (jnp.log(dl / max_exact) / math.log(REL_MAX_DIST / max_exact)
                         * (REL_BUCKETS - max_exact)).astype(I32)
    large = jnp.minimum(large, REL_BUCKETS - 1)
    return jnp.where(d < max_exact, d, large)


def _swap_halves(w):
    half = w.shape[-1] // 2
    return jnp.concatenate([w[..., half:], w[..., :half]], axis=-1)


def _in_proj_weights(w_in, d_model):
    splits = (A_HEADS * A_HEAD_DIM, A_HEAD_DIM, A_HEAD_DIM, IDX_HEADS * IDX_DIM, IDX_DIM, IDX_HEADS,
              B_HEADS * B_KEY_DIM, B_HEADS * B_KEY_DIM, B_HEADS * B_VAL_DIM, B_HEADS * B_VAL_DIM,
              C_Q_RANK, C_KV_RANK, C_ROPE, N_BRANCH * d_model)
    st = np.concatenate([[0], np.cumsum(splits)])
    col = lambda a, b: w_in[:, :, st[a]:st[b]].astype(BF16)
    ik, pe = col(4, 5), col(12, 13)
    iw = jnp.pad(col(5, 6), ((0, 0), (0, 0), (0, LANES - IDX_HEADS)))
    pe_sw = _swap_halves(pe)
    reps = LANES // C_ROPE
    ws = [col(0, 4),
          jnp.concatenate([ik, ik, iw], axis=2),
          col(6, 12),
          jnp.concatenate([pe] * reps + [pe_sw] * reps, axis=2)]
    widths = [splits[:4], (LANES, LANES), splits[6:12], (2 * LANES,)]
    dtypes = [BF16] * 5 + [F32] * 8
    return ws, widths, dtypes, col(13, 14)


def kernel(x, positions, w_in, w_up_a, w_up_b, w_up_c, w_out, mla_q_norm, mla_w_qb, mla_kv_norm,
           mla_w_kvb, hgrn_lb_logits, hgrn_out_norm, rel_bias, attn_norm, ffn_norm, w_ffn_gate,
           w_ffn_up, w_ffn_down, final_norm):
    bsz, s_len, d_model = x.shape
    depth = w_in.shape[0]
    n = bsz * s_len
    t = LANES

    w_proj, widths, dtypes, w_gate = _in_proj_weights(w_in, d_model)
    wq = mla_w_qb.reshape(depth, C_Q_RANK, C_HEADS, C_NOPE + C_ROPE)
    q_pe = wq[..., C_NOPE:]
    w_qb = jnp.concatenate([wq[..., :C_NOPE].reshape(depth, C_Q_RANK, -1),
                            q_pe.reshape(depth, C_Q_RANK, -1),
                            _swap_halves(q_pe).reshape(depth, C_Q_RANK, -1)], axis=2).astype(BF16)
    wkv = mla_w_kvb.reshape(depth, C_KV_RANK, C_HEADS, C_NOPE + C_V)
    w_kvb = jnp.concatenate([wkv[..., :C_NOPE].reshape(depth, C_KV_RANK, -1),
                             wkv[..., C_NOPE:].reshape(depth, C_KV_RANK, -1)], axis=2).astype(BF16)
    bf = lambda w: w.astype(BF16)
    w_up_a, w_up_b, w_up_c, w_out = bf(w_up_a), bf(w_up_b), bf(w_up_c), bf(w_out)
    w_ffn_gate, w_ffn_up, w_ffn_down = bf(w_ffn_gate), bf(w_ffn_up), bf(w_ffn_down)

    p_lb = jax.nn.softmax(hgrn_lb_logits.astype(F32), axis=0)
    lower_bounds = jnp.cumsum(p_lb, axis=0) - p_lb[0:1]
    inv_freq = ROPE_THETA ** (-jnp.arange(0, C_ROPE, 2, dtype=F32) / C_ROPE)
    ang = positions.astype(F32)[..., None] * inv_freq
    cos, sin = jnp.cos(ang), jnp.sin(ang)
    reps = LANES // C_ROPE
    cos_t = jnp.tile(jnp.concatenate([cos, cos], axis=-1), (1, 1, reps)).reshape(n, LANES)
    sin_t = jnp.tile(jnp.concatenate([-sin, sin], axis=-1), (1, 1, reps)).reshape(n, LANES)
    q_idx = jnp.arange(t, dtype=I32)[None, :]
    k_idx = jnp.arange(t, dtype=I32)[:, None]
    def tab(dist):
        onehot = jax.nn.one_hot(_t5_bucket(dist), REL_BUCKETS, dtype=F32)
        return jnp.einsum("kqb,bh->khq", onehot, rel_bias.astype(F32),
                          precision=lax.Precision.HIGHEST).reshape(t, A_HEADS * t)
    far = jnp.broadcast_to(rel_bias[REL_BUCKETS - 1].astype(F32)[None, :, None],
                           (t, A_HEADS, t)).reshape(t, A_HEADS * t)
    bias_tab = jnp.stack([tab(q_idx + t - k_idx), tab(q_idx - k_idx), far]) * math.log2(math.e)

    tm = min(512, n)
    t_mla = min(256, s_len)
    x2 = x.reshape(n, d_model)
    r3 = lambda a: a.reshape(bsz, s_len, a.shape[-1])
    per_seq = lambda a: a.reshape(bsz, a.shape[0] // bsz, a.shape[1], a.shape[2])
    key_tiles = [0, 0, t] + [0] * 10
    for l in range(depth):
        (qa, ka, vta, iq, ik2, iw, bq, bfr, bi, bg, cq, ckv, kpe_raw) = _norm_matmul(
            x2, attn_norm[l], [w[l] for w in w_proj], widths, dtypes, key_tiles, tm)
        o_a = _dsa(r3(qa), r3(iq), r3(iw), r3(ka), per_seq(vta), r3(ik2), bias_tab)
        o_b = _hgrn(r3(bq), r3(bfr), r3(bi), r3(bg), lower_bounds[l], hgrn_out_norm[l],
                    sb=min(512, s_len))
        qn, qpe, kn, vtc, kpe = _mla_prep(cq, ckv, kpe_raw, cos_t, sin_t, mla_q_norm[l], mla_kv_norm[l],
                                          w_qb[l], w_kvb[l], tm, t_mla)
        o_c = _mla(r3(qn), r3(qpe), r3(kn), r3(kpe), per_seq(vtc), t_mla)
        x2 = _merge(x2, attn_norm[l], w_gate[l], o_a.reshape(n, -1), o_b.reshape(n, -1),
                    o_c.reshape(n, -1), w_up_a[l], w_up_b[l], w_up_c[l], w_out[l], min(256, n))
        x2 = _ffn(x2, ffn_norm[l], w_ffn_gate[l], w_ffn_up[l], w_ffn_down[l], final_norm,
                  min(256, n), final_norm=(l == depth - 1))
    return x2.reshape(bsz, s_len, d_model)
```

```python
import functools
import math

import jax
import jax.numpy as jnp
import numpy as np
from jax import lax
from jax.experimental import pallas as pl
from jax.experimental.pallas import tpu as pltpu

F32 = jnp.float32
BF16 = jnp.bfloat16
I32 = jnp.int32

A_HEADS = 4
A_HEAD_DIM = 128
IDX_HEADS = 8
IDX_DIM = 64
TOPK_MAX = 256
B_HEADS = 4
B_KEY_DIM = 128
B_VAL_DIM = 128
B_CHUNK = 64
C_HEADS = 4
C_Q_RANK = 384
C_KV_RANK = 256
C_NOPE = 128
C_ROPE = 64
C_V = 128
ROPE_THETA = 10000.0
REL_BUCKETS = 32
REL_MAX_DIST = 128
N_BRANCH = 3
EPS = 1e-6
NEG_BIG = -1e30
LB_FLOOR = 1e-30

LANES = 128
SUBLANES = 8
VMEM_LIMIT = 56 * 1024 * 1024
INT_MIN = np.int32(-2 ** 31)
SORT_ROWS = LANES // SUBLANES


def _oddeven_merge_sort(n):
    pairs = []
    p = 1
    while p < n:
        k = p
        while k >= 1:
            for j in range(k % p, n - k, 2 * k):
                for i in range(min(k, n - j - k)):
                    if (i + j) // (2 * p) == (i + j + k) // (2 * p):
                        pairs.append((i + j, i + j + k))
            k //= 2
        p *= 2
    return tuple(pairs)


_SORT16 = _oddeven_merge_sort(SORT_ROWS)

_NT = (((1,), (1,)), ((), ()))


def _params(*sem):
    return pltpu.CompilerParams(dimension_semantics=sem, vmem_limit_bytes=VMEM_LIMIT)


def _rms(x, g):
    return x * lax.rsqrt(jnp.mean(x * x, axis=-1, keepdims=True) + EPS) * g


def _const_spec(shape):
    nd = len(shape)
    return pl.BlockSpec(shape, lambda *_: (0,) * nd)


def _fold8(x, op):
    r, c = x.shape
    return op(x.reshape(r // SUBLANES, SUBLANES, c), axis=0)


def _store_key_major(o_ref, rows, tile):
    for u in range(rows.shape[0] // tile):
        o_ref[u] = rows[u * tile:(u + 1) * tile, :].T.astype(o_ref.dtype)


def _norm_matmul_kernel(x_ref, g_ref, *refs, widths, key_tiles, col_chunk):
    w_refs, out_refs = refs[:len(widths)], refs[len(widths):]
    h = _rms(x_ref[...], g_ref[...]).astype(BF16)
    k = 0
    for w_ref, w_widths in zip(w_refs, widths):
        c0 = 0
        for w in w_widths:
            o_ref = out_refs[k]
            if key_tiles[k]:
                _store_key_major(o_ref, jnp.dot(h, w_ref[:, c0:c0 + w], preferred_element_type=F32),
                                 key_tiles[k])
            else:
                for s in range(0, w, col_chunk):
                    e = min(s + col_chunk, w)
                    o_ref[:, s:e] = jnp.dot(h, w_ref[:, c0 + s:c0 + e],
                                            preferred_element_type=F32).astype(o_ref.dtype)
            c0 += w
            k += 1


def _norm_matmul(x, g, ws, widths, dtypes, key_tiles, tm):
    n, k = x.shape
    flat = [wd for w_widths in widths for wd in w_widths]
    assert n % tm == 0 and all(w.shape == (k, sum(ww)) for w, ww in zip(ws, widths))
    kern = functools.partial(_norm_matmul_kernel, widths=tuple(tuple(ww) for ww in widths),
                             key_tiles=tuple(key_tiles), col_chunk=512)
    specs, shapes = [], []
    for wd, dt, kt in zip(flat, dtypes, key_tiles):
        if kt:
            specs.append(pl.BlockSpec((tm // kt, wd, kt), lambda i: (i, 0, 0)))
            shapes.append(jax.ShapeDtypeStruct((n // kt, wd, kt), dt))
        else:
            specs.append(pl.BlockSpec((tm, wd), lambda i: (i, 0)))
            shapes.append(jax.ShapeDtypeStruct((n, wd), dt))
    return pl.pallas_call(
        kern,
        grid=(n // tm,),
        in_specs=[pl.BlockSpec((tm, k), lambda i: (i, 0)), _const_spec((1, k))]
                 + [_const_spec(w.shape) for w in ws],
        out_specs=specs,
        out_shape=shapes,
        compiler_params=_params("parallel"),
        name="norm_proj",
    )(x, g.reshape(1, k), *ws)


def _dsa_kernel(qa_ref, iq_ref, iw_ref, ka_ref, vt_ref, ik_ref, bias_ref, o_ref,
                keys_ref, sorted_ref, s_ref, qm_ref, qs_ref, wt_ref, thr_ref, ties_ref,
                *, topk, n_blocks, group, scale, idx_scale):
    t = LANES
    gt = group * t
    i = pl.program_id(1)
    n_groups = (i + group) // group
    max_groups = n_blocks // group
    krow = lax.broadcasted_iota(I32, (t, t), 0)
    qcol = lax.broadcasted_iota(I32, (t, t), 1)
    lane_lo = qcol < IDX_DIM

    for h in range(IDX_HEADS):
        pair = iq_ref[0, :, (h // 2) * t:(h // 2 + 1) * t]
        keep = lane_lo if h % 2 == 0 else jnp.logical_not(lane_lo)
        qm_ref[h * t:(h + 1) * t, :] = jnp.where(keep, pair, jnp.zeros_like(pair))
    for h in range(A_HEADS):
        qs_ref[h * t:(h + 1) * t, :] = qa_ref[0, :, h * t:(h + 1) * t]
    wt_ref[...] = iw_ref[0].T

    def index_logits(g):
        return lax.dot_general(ik_ref[0, g * gt:(g + 1) * gt, :], qm_ref[...], _NT,
                               preferred_element_type=F32)

    def score_group(g, lg):
        for u in range(group):
            j = g * group + u
            acc = jnp.zeros((t, t), F32)
            for h in range(IDX_HEADS):
                acc = acc + jnp.maximum(lg[u * t:(u + 1) * t, h * t:(h + 1) * t], 0.0) * wt_ref[h:h + 1, :]
            score = acc * idx_scale
            score = jnp.where(krow + j * t <= qcol + i * t, score, NEG_BIG)
            score = jnp.where(score == 0.0, 0.0, score)
            bits = pltpu.bitcast(score, I32)
            key = bits ^ ((bits >> 31) & np.int32(0x7FFFFFFF))
            keys_ref[j] = key
            rows = [key[r * SUBLANES:(r + 1) * SUBLANES, :] for r in range(SORT_ROWS)]
            for a, b in _SORT16:
                rows[a], rows[b] = jnp.maximum(rows[a], rows[b]), jnp.minimum(rows[a], rows[b])
            for r in range(SORT_ROWS):
                sorted_ref[j, r] = rows[r]

    def score_all(ng):
        lg = index_logits(0)
        for g in range(ng):
            nxt = index_logits(g + 1) if g + 1 < ng else None
            score_group(g, lg)
            lg = nxt

    for v in range(1, max_groups + 1):
        pl.when(n_groups == v)(functools.partial(score_all, v))

    thr_ref[...] = jnp.full(thr_ref.shape, INT_MIN, I32)
    ties_ref[...] = jnp.zeros(ties_ref.shape, F32)

    def count(n, pred):
        acc = [jnp.zeros((SUBLANES, t), F32) for _ in range(5)]
        for j in range(n):
            v = [sorted_ref[j, r] for r in range(SORT_ROWS)]
            m1 = pred(v[7])
            m2 = pred(jnp.where(m1, v[11], v[3]))
            m3 = pred(jnp.where(m1, jnp.where(m2, v[13], v[9]), jnp.where(m2, v[5], v[1])))
            m4 = pred(jnp.where(m1, jnp.where(m2, jnp.where(m3, v[14], v[12]), jnp.where(m3, v[10], v[8])),
                                jnp.where(m2, jnp.where(m3, v[6], v[4]), jnp.where(m3, v[2], v[0]))))
            m5 = pred(v[15])
            acc = [a + jnp.where(m, 1.0, 0.0) for a, m in zip(acc, (m1, m2, m3, m4, m5))]
        cnt = 8.0 * acc[0] + 4.0 * acc[1] + 2.0 * acc[2] + acc[3] + acc[4]
        return jnp.sum(cnt, axis=0, keepdims=True)

    def search(n):
        def search_pass(b, thr):
            cand = thr + lax.shift_left(np.int32(1), 31 - b)
            return jnp.where(count(n, lambda key: key >= cand) >= topk, cand, thr)

        thr = lax.fori_loop(0, 32, search_pass, jnp.full((1, t), INT_MIN, I32))
        thr_ref[...] = jnp.broadcast_to(thr, thr_ref.shape)
        ties_ref[...] = jnp.broadcast_to(topk - count(n, lambda key: key > thr), ties_ref.shape)

    for c in range(n_blocks):
        if (c + 1) * t > topk:
            pl.when(i == c)(functools.partial(search, c + 1))

    thr = thr_ref[0:1, :]
    n_ties = ties_ref[0:1, :]

    r2 = lax.broadcasted_iota(I32, (2 * t, t), 0)
    c2 = lax.broadcasted_iota(I32, (2 * t, t), 1)
    tie_lhs = jnp.where(jnp.logical_or(r2 >= t, c2 < r2), 1.0, 0.0).astype(BF16)

    def sweep_a_mxu(g):
        keys = [keys_ref[g * group + u] for u in range(group)]
        eqs = [key == thr for key in keys]
        eq_all = jnp.concatenate([jnp.where(eq, 1.0, 0.0).astype(BF16) for eq in eqs], axis=1)
        pref = jnp.dot(tie_lhs, eq_all, preferred_element_type=F32)
        s_grp = lax.dot_general(ka_ref[0, g * gt:(g + 1) * gt, :], qs_ref[...], _NT,
                                preferred_element_type=F32)
        return keys, eqs, pref, s_grp

    def sweep_a_vpu(g, operands, seen, mx):
        keys, eqs, pref, s_grp = operands
        for u in range(group):
            j = g * group + u
            us = slice(u * t, (u + 1) * t)
            rank = jnp.where(keys[u] > thr, -1.0, jnp.where(eqs[u], seen + pref[:t, us], 3e38))
            rank = jnp.where(krow + j * t <= qcol + i * t, rank, 3e38)
            valid = rank < n_ties
            which = jnp.where(j == i, 1, jnp.where(j == i - 1, 0, 2))
            s_all = s_grp[us, :] * (scale * math.log2(math.e)) + bias_ref[which]
            for h in range(A_HEADS):
                s_h = jnp.where(valid, s_all[:, h * t:(h + 1) * t], NEG_BIG)
                s_ref[j, :, h * t:(h + 1) * t] = s_h
                mx[h] = jnp.maximum(mx[h], _fold8(s_h, jnp.max))
            seen = seen + pref[t:t + 1, us]
        return seen, mx

    def sweep_b(g, m_all, l8):
        ps = []
        for u in range(group):
            p = jnp.exp2(s_ref[g * group + u] - m_all)
            l8 = l8 + _fold8(p, jnp.sum)
            ps.append(p.astype(BF16))
        vt_grp = jnp.concatenate([vt_ref[0, g * group + u] for u in range(group)], axis=1)
        return l8, jnp.dot(vt_grp, jnp.concatenate(ps, axis=0), preferred_element_type=F32)

    def attend(ng):
        seen = jnp.zeros((1, t), F32)
        mx = [jnp.full((SUBLANES, t), NEG_BIG, F32) for _ in range(A_HEADS)]
        operands = sweep_a_mxu(0)
        for g in range(ng):
            nxt = sweep_a_mxu(g + 1) if g + 1 < ng else None
            seen, mx = sweep_a_vpu(g, operands, seen, mx)
            operands = nxt
        m_all = jnp.concatenate([jnp.max(m, axis=0, keepdims=True) for m in mx], axis=1)
        l8 = jnp.zeros((SUBLANES, A_HEADS * t), F32)
        acc = None
        for g in range(ng):
            l8, pv = sweep_b(g, m_all, l8)
            acc = pv if acc is None else acc + pv
        out = acc * (1.0 / jnp.sum(l8, axis=0, keepdims=True))
        for h in range(A_HEADS):
            o_ref[0, :, h * t:(h + 1) * t] = out[:, h * t:(h + 1) * t].T.astype(o_ref.dtype)

    for v in range(1, max_groups + 1):
        pl.when(n_groups == v)(functools.partial(attend, v))


def _dsa(qa, iq, iw, ka, vt, ik2, bias_tab):
    b, s, _ = qa.shape
    t = LANES
    nb = s // t
    group = math.gcd(nb, 4)
    topk = min(TOPK_MAX, s // 4)
    kern = functools.partial(_dsa_kernel, topk=float(topk), n_blocks=nb, group=group,
                             scale=A_HEAD_DIM ** -0.5,
                             idx_scale=(IDX_DIM ** -0.5) * (IDX_HEADS ** -0.5))
    qspec = lambda w: pl.BlockSpec((1, t, w), lambda bi, i: (bi, i, 0))
    kspec = pl.BlockSpec((1, s, t), lambda bi, i: (bi, 0, 0))
    return pl.pallas_call(
        kern,
        grid=(b, nb),
        in_specs=[qspec(A_HEADS * t), qspec(IDX_HEADS * IDX_DIM), qspec(t), kspec,
                  pl.BlockSpec((1, nb, t, t), lambda bi, i: (bi, 0, 0, 0)), kspec,
                  _const_spec(bias_tab.shape)],
        out_specs=qspec(A_HEADS * t),
        out_shape=jax.ShapeDtypeStruct((b, s, A_HEADS * t), BF16),
        scratch_shapes=[pltpu.VMEM((nb, t, t), I32),
                        pltpu.VMEM((nb, SORT_ROWS, SUBLANES, t), I32),
                        pltpu.VMEM((nb, t, A_HEADS * t), F32),
                        pltpu.VMEM((IDX_HEADS * t, t), BF16),
                        pltpu.VMEM((A_HEADS * t, t), BF16),
                        pltpu.VMEM((t, t), F32),
                        pltpu.VMEM((SUBLANES, t), I32),
                        pltpu.VMEM((SUBLANES, t), F32)],
        compiler_params=_params("parallel", "parallel"),
        name="dsa_attention",
    )(qa, iq, iw, ka, vt, ik2, bias_tab)


def _hgrn_kernel(q_ref, f_ref, i_ref, g_ref, lb_ref, gain_ref, o_ref, state_ref, *, n_chunks, n_seq):
    c = B_CHUNK
    rc = n_seq * c
    kd, vd = B_KEY_DIM, B_VAL_DIM
    w = B_HEADS * kd

    @pl.when(pl.program_id(1) == 0)
    def _():
        state_ref[...] = jnp.zeros_like(state_ref)

    row = lax.broadcasted_iota(I32, (rc, rc), 0)
    col = lax.broadcasted_iota(I32, (rc, rc), 1)
    same_seq = (row >= col) if n_seq == 1 else jnp.logical_and(
        row >= col, lax.shift_right_logical(row, 6) == lax.shift_right_logical(col, 6))
    causal = same_seq
    assert c == 64
    tril = jnp.where(causal, 1.0, 0.0).astype(BF16)
    seqs = [slice(b * c, (b + 1) * c) for b in range(n_seq)]
    seq_of_row = lax.shift_right_logical(lax.broadcasted_iota(I32, (rc, w), 0), 6)

    def per_seq_row(x, r):
        return jnp.concatenate([jnp.broadcast_to(x[b * c + r:b * c + r + 1, :], (c, x.shape[1]))
                                for b in range(n_seq)], axis=0)

    def stacked(ref, rows):
        return jnp.concatenate([ref[b, rows, :] for b in range(n_seq)], axis=0)
    q_scale = kd ** -0.5
    lb = lb_ref[...]
    lb_floor = jnp.maximum(lb, LB_FLOOR)
    one_m_lb = 1.0 - lb
    heads = [slice(h * kd, (h + 1) * kd) for h in range(B_HEADS)]

    def stage_decay(ci):
        rows = slice(ci * c, (ci + 1) * c)
        fr = stacked(f_ref, rows)
        z = jnp.exp(-jnp.abs(fr))
        r = 1.0 / (1.0 + z)
        sig_pos = jnp.where(fr >= 0, r, z * r)
        sig_neg = jnp.where(fr >= 0, z * r, r)
        log_f = jnp.log(lb_floor + one_m_lb * sig_pos)
        k_in = one_m_lb * sig_neg
        hi = log_f.astype(BF16)
        rest = log_f - hi.astype(F32)
        mid = rest.astype(BF16)
        lo = (rest - mid.astype(F32)).astype(BF16)
        cs = jnp.dot(tril, jnp.concatenate([hi, mid, lo], axis=1), preferred_element_type=F32)
        return k_in, cs

    def apply_update(pending):
        if pending is not None:
            d_last, st, upd = pending
            for b in range(n_seq):
                for h, hs in enumerate(heads):
                    state_ref[b, h] = d_last[b * c:b * c + 1, hs] * st[b][h] + upd[b][h]

    def stage_scores(ci, decay, pending):
        rows = slice(ci * c, (ci + 1) * c)
        k_in, cs = decay
        bsum = cs[:, :w] + cs[:, w:2 * w] + cs[:, 2 * w:]
        b_mid = per_seq_row(bsum, c // 2 - 1)
        b_last = per_seq_row(bsum, c - 1)
        qs = stacked(q_ref, rows) * q_scale
        v = stacked(i_ref, rows)
        v_b = v.astype(BF16)
        v_t = v.T.astype(BF16)
        q_mid = (qs * jnp.exp(bsum - b_mid)).astype(BF16)
        k_mid = (k_in * jnp.exp(b_mid - bsum)).astype(BF16)
        q_dec = (qs * jnp.exp(bsum)).astype(BF16)
        k_end = (k_in * jnp.exp(b_last - bsum)).astype(BF16)
        d_last = jnp.exp(b_last)
        apply_update(pending)
        st = [[state_ref[b, h] for h in range(B_HEADS)] for b in range(n_seq)]
        attn = [lax.dot_general(q_mid[:, hs], k_mid[:, hs], _NT, preferred_element_type=F32)
                for hs in heads]
        inter = [jnp.concatenate(
            [lax.dot_general(q_dec[sq, hs], st[b][h].astype(BF16), _NT, preferred_element_type=F32)
             for b, sq in enumerate(seqs)], axis=0) for h, hs in enumerate(heads)]
        k_seq = [k_end if n_seq == 1 else jnp.where(seq_of_row == b, k_end, jnp.zeros_like(k_end))
                 for b in range(n_seq)]
        upd = [[jnp.dot(v_t[hs, :], k_seq[b][:, hs], preferred_element_type=F32) for hs in heads]
               for b in range(n_seq)]
        return (attn, inter, v_b), (d_last, st, upd)

    def stage_output(ci, scores):
        rows = slice(ci * c, (ci + 1) * c)
        attn, inter, v_b = scores
        attn = [jnp.where(causal, a, 0.0).astype(BF16) for a in attn]
        outs = [jnp.dot(attn[h], v_b[:, hs], preferred_element_type=F32) + inter[h]
                for h, hs in enumerate(heads)]
        o = jnp.concatenate([_rms(o_h, gain_ref[...]) for o_h in outs], axis=1)
        g = stacked(g_ref, rows)
        o = (o * (g * (1.0 / (1.0 + jnp.exp(-g))))).astype(o_ref.dtype)
        for b, sq in enumerate(seqs):
            o_ref[b, rows, :] = o[sq, :]

    decay, scores, pending = {}, {}, None
    for step in range(n_chunks + 2):
        if step < n_chunks:
            decay[step] = stage_decay(step)
        if 0 <= step - 1 < n_chunks:
            scores[step - 1], pending = stage_scores(step - 1, decay.pop(step - 1), pending)
        if 0 <= step - 2 < n_chunks:
            stage_output(step - 2, scores.pop(step - 2))
    apply_update(pending)


def _hgrn(bq, bf, bi, bg, lb, gain, sb):
    b, s, w = bq.shape
    assert s % sb == 0 and sb % B_CHUNK == 0
    n_seq = 2 if b % 2 == 0 else 1
    kern = functools.partial(_hgrn_kernel, n_chunks=sb // B_CHUNK, n_seq=n_seq)
    spec = pl.BlockSpec((n_seq, sb, w), lambda bi_, si: (bi_, si, 0))
    return pl.pallas_call(
        kern,
        grid=(b // n_seq, s // sb),
        in_specs=[spec, spec, spec, spec, _const_spec((1, w)), _const_spec((1, B_VAL_DIM))],
        out_specs=spec,
        out_shape=jax.ShapeDtypeStruct((b, s, w), BF16),
        scratch_shapes=[pltpu.VMEM((n_seq, B_HEADS, B_VAL_DIM, B_KEY_DIM), F32)],
        compiler_params=_params("parallel", "arbitrary"),
        name="hgrn2",
    )(bq, bf, bi, bg, lb.reshape(1, w), gain.reshape(1, B_VAL_DIM))


def _mla_prep_kernel(cq_ref, ckv_ref, kpe_raw_ref, cos_ref, sin_ref, gq_ref, gkv_ref, wq_ref, wkv_ref,
                     qn_ref, qpe_ref, kn_ref, vt_ref, kpe_ref, *, key_tile):
    nw = C_HEADS * C_NOPE
    pw = C_HEADS * C_ROPE
    cos, sin = cos_ref[...], sin_ref[...]
    cos2 = jnp.concatenate([cos] * (pw // LANES), axis=1)
    sin2 = jnp.concatenate([sin] * (pw // LANES), axis=1)
    q = jnp.dot(_rms(cq_ref[...], gq_ref[...]).astype(BF16), wq_ref[...], preferred_element_type=F32)
    qn_ref[...] = q[:, :nw].astype(qn_ref.dtype)
    qpe_ref[...] = (q[:, nw:nw + pw] * cos2 + q[:, nw + pw:] * sin2).astype(qpe_ref.dtype)
    kv = jnp.dot(_rms(ckv_ref[...], gkv_ref[...]).astype(BF16), wkv_ref[...], preferred_element_type=F32)
    kn_ref[...] = kv[:, :nw].astype(kn_ref.dtype)
    _store_key_major(vt_ref, kv[:, nw:], key_tile)
    kpe_ref[...] = (kpe_raw_ref[:, :LANES] * cos + kpe_raw_ref[:, LANES:] * sin).astype(kpe_ref.dtype)


def _mla_prep(cq, ckv, kpe_raw, cos_t, sin_t, gq, gkv, wq, wkv, tm, key_tile):
    n = cq.shape[0]
    nw = C_HEADS * C_NOPE
    vw = C_HEADS * C_V
    row = lambda w: pl.BlockSpec((tm, w), lambda i: (i, 0))
    widths = (nw, C_HEADS * C_ROPE, nw, None, LANES)
    vt_spec = pl.BlockSpec((tm // key_tile, vw, key_tile), lambda i: (i, 0, 0))
    vt_shape = jax.ShapeDtypeStruct((n // key_tile, vw, key_tile), BF16)
    return pl.pallas_call(
        functools.partial(_mla_prep_kernel, key_tile=key_tile),
        grid=(n // tm,),
        in_specs=[row(C_Q_RANK), row(C_KV_RANK), row(2 * LANES), row(LANES), row(LANES),
                  _const_spec((1, C_Q_RANK)), _const_spec((1, C_KV_RANK)),
                  _const_spec(wq.shape), _const_spec(wkv.shape)],
        out_specs=[vt_spec if w is None else row(w) for w in widths],
        out_shape=[vt_shape if w is None else jax.ShapeDtypeStruct((n, w), BF16) for w in widths],
        compiler_params=_params("parallel"),
        name="mla_prep",
    )(cq, ckv, kpe_raw, cos_t, sin_t, gq.reshape(1, -1), gkv.reshape(1, -1), wq, wkv)


def _mla_kernel(qn_ref, qpe_ref, kn_ref, kpe_ref, vt_ref, o_ref, q_ref, acc_ref, *, t, n_tiles, scale):
    i = pl.program_id(1)
    t2 = 2 * t
    krow = lax.broadcasted_iota(I32, (t2, t), 0)
    qcol = lax.broadcasted_iota(I32, (t2, t), 1)
    lane_lo = lax.broadcasted_iota(I32, (t, LANES), 1) < C_ROPE
    for h in range(C_HEADS):
        pair = qpe_ref[0, :, (h // 2) * LANES:(h // 2 + 1) * LANES]
        keep = lane_lo if h % 2 == 0 else jnp.logical_not(lane_lo)
        q_ref[h, :, :C_NOPE] = qn_ref[0, :, h * C_NOPE:(h + 1) * C_NOPE]
        q_ref[h, :, C_NOPE:] = jnp.where(keep, pair, jnp.zeros_like(pair))
    acc_ref[...] = jnp.zeros_like(acc_ref)

    def logits_of(g):
        rows = slice(g * t2, (g + 1) * t2)
        kpe_t = kpe_ref[0, rows, :]
        return [lax.dot_general(
            jnp.concatenate([kn_ref[0, rows, h * C_NOPE:(h + 1) * C_NOPE], kpe_t], axis=1),
            q_ref[h], _NT, preferred_element_type=F32) for h in range(C_HEADS)]

    def softmax_pv(g, logits, ms, ls, masked):
        new_m, new_l = [], []
        for h in range(C_HEADS):
            s = logits[h] * (scale * math.log2(math.e))
            if masked:
                s = jnp.where(krow + g * t2 <= qcol + i * t, s, NEG_BIG)
            m_new = jnp.maximum(ms[h], jnp.max(s, axis=0, keepdims=True))
            alpha = jnp.exp2(ms[h] - m_new)
            p = jnp.exp2(s - m_new)
            new_l.append(alpha * ls[h] + jnp.sum(p, axis=0, keepdims=True))
            new_m.append(m_new)
            hs = slice(h * C_V, (h + 1) * C_V)
            v_t = jnp.concatenate([vt_ref[0, 2 * g, hs, :], vt_ref[0, 2 * g + 1, hs, :]], axis=1)
            acc_ref[h] = alpha * acc_ref[h] + jnp.dot(v_t, p.astype(BF16), preferred_element_type=F32)
        return new_m, new_l

    def run(n_full):
        ms = [jnp.full((1, t), NEG_BIG, F32) for _ in range(C_HEADS)]
        ls = [jnp.zeros((1, t), F32) for _ in range(C_HEADS)]
        logits = logits_of(0)
        for g in range(n_full + 1):
            nxt = logits_of(g + 1) if g < n_full else None
            ms, ls = softmax_pv(g, logits, ms, ls, masked=(g == n_full))
            logits = nxt
        for h in range(C_HEADS):
            o_ref[0, :, h * C_V:(h + 1) * C_V] = (acc_ref[h] * (1.0 / ls[h])).T.astype(o_ref.dtype)

    for v in range(n_tiles // 2):
        pl.when(i // 2 == v)(functools.partial(run, v))


def _mla(qn, qpe, kn, kpe, vt, t):
    b, s, _ = qn.shape
    nt = s // t
    assert nt % 2 == 0
    kern = functools.partial(_mla_kernel, t=t, n_tiles=nt, scale=(C_NOPE + C_ROPE) ** -0.5)
    qspec = lambda w: pl.BlockSpec((1, t, w), lambda bi, i: (bi, i, 0))
    kspec = lambda w: pl.BlockSpec((1, s, w), lambda bi, i: (bi, 0, 0))
    return pl.pallas_call(
        kern,
        grid=(b, nt),
        in_specs=[qspec(qn.shape[2]), qspec(qpe.shape[2]), kspec(kn.shape[2]), kspec(kpe.shape[2]),
                  pl.BlockSpec((1, nt, C_HEADS * C_V, t), lambda bi, i: (bi, 0, 0, 0))],
        out_specs=qspec(C_HEADS * C_V),
        out_shape=jax.ShapeDtypeStruct((b, s, C_HEADS * C_V), BF16),
        scratch_shapes=[pltpu.VMEM((C_HEADS, t, C_NOPE + LANES), BF16),
                        pltpu.VMEM((C_HEADS, C_V, t), F32)],
        compiler_params=_params("parallel", "parallel"),
        name="mla_attention",
    )(qn, qpe, kn, kpe, vt)


def _merge_kernel(x_ref, g_ref, wg_ref, oa_ref, ob_ref, oc_ref, wa_ref, wb_ref, wc_ref, wo_ref,
                  out_ref, mixed_ref, *, col_chunk):
    x = x_ref[...]
    d = x.shape[1]
    h = _rms(x, g_ref[...]).astype(BF16)
    branches = ((oa_ref, wa_ref), (ob_ref, wb_ref), (oc_ref, wc_ref))
    for s in range(0, d, col_chunk):
        cs = slice(s, s + col_chunk)
        mixed = None
        for bidx, (o_ref, w_ref) in enumerate(branches):
            logits = jnp.dot(h, wg_ref[:, bidx * d + s:bidx * d + s + col_chunk],
                             preferred_element_type=F32)
            gate = 1.0 / (1.0 + jnp.exp(-logits))
            term = gate * jnp.dot(o_ref[...], w_ref[:, cs], preferred_element_type=F32)
            mixed = term if mixed is None else mixed + term
        mixed_ref[:, cs] = mixed.astype(BF16)
    out_ref[...] = x + jnp.dot(mixed_ref[...], wo_ref[...], preferred_element_type=F32)


def _merge(x, g, w_gate, oa, ob, oc, wa, wb, wc, wo, tm):
    n, d = x.shape
    row = lambda w: pl.BlockSpec((tm, w), lambda i: (i, 0))
    kern = functools.partial(_merge_kernel, col_chunk=256)
    return pl.pallas_call(
        kern,
        grid=(n // tm,),
        in_specs=[row(d), _const_spec((1, d)), _const_spec(w_gate.shape),
                  row(oa.shape[1]), row(ob.shape[1]), row(oc.shape[1]),
                  _const_spec(wa.shape), _const_spec(wb.shape), _const_spec(wc.shape),
                  _const_spec(wo.shape)],
        out_specs=row(d),
        out_shape=jax.ShapeDtypeStruct((n, d), F32),
        scratch_shapes=[pltpu.VMEM((tm, d), BF16)],
        compiler_params=_params("parallel"),
        name="merge_out_proj",
    )(x, g.reshape(1, d), w_gate, oa, ob, oc, wa, wb, wc, wo)


def _ffn_kernel(x_ref, g_ref, wg_ref, wu_ref, wd_ref, gf_ref, out_ref, act_ref, *, col_chunk,
                final_norm):
    x = x_ref[...]
    h = _rms(x, g_ref[...]).astype(BF16)
    dff = wg_ref.shape[1]
    for s in range(0, dff, col_chunk):
        cs = slice(s, s + col_chunk)
        gate = jnp.dot(h, wg_ref[:, cs], preferred_element_type=F32)
        up = jnp.dot(h, wu_ref[:, cs], preferred_element_type=F32)
        act_ref[:, cs] = (gate * (1.0 / (1.0 + jnp.exp(-gate))) * up).astype(BF16)
    y = x + jnp.dot(act_ref[...], wd_ref[...], preferred_element_type=F32)
    if final_norm:
        y = _rms(y, gf_ref[...])
    out_ref[...] = y


def _ffn(x, g, wg, wu, wd, gf, tm, final_norm):
    n, d = x.shape
    dff = wg.shape[1]
    row = pl.BlockSpec((tm, d), lambda i: (i, 0))
    kern = functools.partial(_ffn_kernel, col_chunk=256, final_norm=final_norm)
    return pl.pallas_call(
        kern,
        grid=(n // tm,),
        in_specs=[row, _const_spec((1, d)), _const_spec(wg.shape), _const_spec(wu.shape),
                  _const_spec(wd.shape), _const_spec((1, d))],
        out_specs=row,
        out_shape=jax.ShapeDtypeStruct((n, d), F32),
        scratch_shapes=[pltpu.VMEM((tm, dff), BF16)],
        compiler_params=_params("parallel"),
        name="swiglu_ffn",
    )(x, g.reshape(1, d), wg, wu, wd, gf.reshape(1, d))


def _t5_bucket(dist):
    max_exact = REL_BUCKETS // 2
    d = jnp.maximum(dist, 0)
    dl = jnp.maximum(d, max_exact).astype(F32)
    large = max_exact + (jnp.log(dl / max_exact) / math.log(REL_MAX_DIST / max_exact)
                         * (REL_BUCKETS - max_exact)).astype(I32)
    large = jnp.minimum(large, REL_BUCKETS - 1)
    return jnp.where(d < max_exact, d, large)


def _swap_halves(w):
    half = w.shape[-1] // 2
    return jnp.concatenate([w[..., half:], w[..., :half]], axis=-1)


def _in_proj_weights(w_in, d_model):
    splits = (A_HEADS * A_HEAD_DIM, A_HEAD_DIM, A_HEAD_DIM, IDX_HEADS * IDX_DIM, IDX_DIM, IDX_HEADS,
              B_HEADS * B_KEY_DIM, B_HEADS * B_KEY_DIM, B_HEADS * B_VAL_DIM, B_HEADS * B_VAL_DIM,
              C_Q_RANK, C_KV_RANK, C_ROPE, N_BRANCH * d_model)
    st = np.concatenate([[0], np.cumsum(splits)])
    col = lambda a, b: w_in[:, :, st[a]:st[b]].astype(BF16)
    ik, pe = col(4, 5), col(12, 13)
    iw = jnp.pad(col(5, 6), ((0, 0), (0, 0), (0, LANES - IDX_HEADS)))
    pe_sw = _swap_halves(pe)
    reps = LANES // C_ROPE
    ws = [col(0, 4),
          jnp.concatenate([ik, ik, iw], axis=2),
          col(6, 12),
          jnp.concatenate([pe] * reps + [pe_sw] * reps, axis=2)]
    widths = [splits[:4], (LANES, LANES), splits[6:12], (2 * LANES,)]
    dtypes = [BF16] * 5 + [F32] * 8
    return ws, widths, dtypes, col(13, 14)


def kernel(x, positions, w_in, w_up_a, w_up_b, w_up_c, w_out, mla_q_norm, mla_w_qb, mla_kv_norm,
           mla_w_kvb, hgrn_lb_logits, hgrn_out_norm, rel_bias, attn_norm, ffn_norm, w_ffn_gate,
           w_ffn_up, w_ffn_down, final_norm):
    bsz, s_len, d_model = x.shape
    depth = w_in.shape[0]
    n = bsz * s_len
    t = LANES

    w_proj, widths, dtypes, w_gate = _in_proj_weights(w_in, d_model)
    wq = mla_w_qb.reshape(depth, C_Q_RANK, C_HEADS, C_NOPE + C_ROPE)
    q_pe = wq[..., C_NOPE:]
    w_qb = jnp.concatenate([wq[..., :C_NOPE].reshape(depth, C_Q_RANK, -1),
                            q_pe.reshape(depth, C_Q_RANK, -1),
                            _swap_halves(q_pe).reshape(depth, C_Q_RANK, -1)], axis=2).astype(BF16)
    wkv = mla_w_kvb.reshape(depth, C_KV_RANK, C_HEADS, C_NOPE + C_V)
    w_kvb = jnp.concatenate([wkv[..., :C_NOPE].reshape(depth, C_KV_RANK, -1),
                             wkv[..., C_NOPE:].reshape(depth, C_KV_RANK, -1)], axis=2).astype(BF16)
    bf = lambda w: w.astype(BF16)
    w_up_a, w_up_b, w_up_c, w_out = bf(w_up_a), bf(w_up_b), bf(w_up_c), bf(w_out)
    w_ffn_gate, w_ffn_up, w_ffn_down = bf(w_ffn_gate), bf(w_ffn_up), bf(w_ffn_down)

    p_lb = jax.nn.softmax(hgrn_lb_logits.astype(F32), axis=0)
    lower_bounds = jnp.cumsum(p_lb, axis=0) - p_lb[0:1]
    inv_freq = ROPE_THETA ** (-jnp.arange(0, C_ROPE, 2, dtype=F32) / C_ROPE)
    ang = positions.astype(F32)[..., None] * inv_freq
    cos, sin = jnp.cos(ang), jnp.sin(ang)
    reps = LANES // C_ROPE
    cos_t = jnp.tile(jnp.concatenate([cos, cos], axis=-1), (1, 1, reps)).reshape(n, LANES)
    sin_t = jnp.tile(jnp.concatenate([-sin, sin], axis=-1), (1, 1, reps)).reshape(n, LANES)
    q_idx = jnp.arange(t, dtype=I32)[None, :]
    k_idx = jnp.arange(t, dtype=I32)[:, None]
    def tab(dist):
        onehot = jax.nn.one_hot(_t5_bucket(dist), REL_BUCKETS, dtype=F32)
        return jnp.einsum("kqb,bh->khq", onehot, rel_bias.astype(F32),
                          precision=lax.Precision.HIGHEST).reshape(t, A_HEADS * t)
    far = jnp.broadcast_to(rel_bias[REL_BUCKETS - 1].astype(F32)[None, :, None],
                           (t, A_HEADS, t)).reshape(t, A_HEADS * t)
    bias_tab = jnp.stack([tab(q_idx + t - k_idx), tab(q_idx - k_idx), far]) * math.log2(math.e)

    tm = min(512, n)
    t_mla = min(256, s_len)
    x2 = x.reshape(n, d_model)
    r3 = lambda a: a.reshape(bsz, s_len, a.shape[-1])
    per_seq = lambda a: a.reshape(bsz, a.shape[0] // bsz, a.shape[1], a.shape[2])
    key_tiles = [0, 0, t] + [0] * 10
    for l in range(depth):
        (qa, ka, vta, iq, ik2, iw, bq, bfr, bi, bg, cq, ckv, kpe_raw) = _norm_matmul(
            x2, attn_norm[l], [w[l] for w in w_proj], widths, dtypes, key_tiles, tm)
        o_a = _dsa(r3(qa), r3(iq), r3(iw), r3(ka), per_seq(vta), r3(ik2), bias_tab)
        o_b = _hgrn(r3(bq), r3(bfr), r3(bi), r3(bg), lower_bounds[l], hgrn_out_norm[l],
                    sb=min(512, s_len))
        qn, qpe, kn, vtc, kpe = _mla_prep(cq, ckv, kpe_raw, cos_t, sin_t, mla_q_norm[l], mla_kv_norm[l],
                                          w_qb[l], w_kvb[l], tm, t_mla)
        o_c = _mla(r3(qn), r3(qpe), r3(kn), r3(kpe), per_seq(vtc), t_mla)
        x2 = _merge(x2, attn_norm[l], w_gate[l], o_a.reshape(n, -1), o_b.reshape(n, -1),
                    o_c.reshape(n, -1), w_up_a[l], w_up_b[l], w_up_c[l], w_out[l], min(256, n))
        x2 = _ffn(x2, ffn_norm[l], w_ffn_gate[l], w_ffn_up[l], w_ffn_down[l], final_norm,
                  min(256, n), final_norm=(l == depth - 1))
    return x2.reshape(bsz, s_len, d_model)
```

```python
import functools
import math

import jax
import jax.numpy as jnp
import numpy as np
from jax import lax
from jax.experimental import pallas as pl
from jax.experimental.pallas import tpu as pltpu

F32 = jnp.float32
BF16 = jnp.bfloat16
I32 = jnp.int32

A_HEADS = 4
A_HEAD_DIM = 128
IDX_HEADS = 8
IDX_DIM = 64
TOPK_MAX = 256
B_HEADS = 4
B_KEY_DIM = 128
B_VAL_DIM = 128
B_CHUNK = 64
C_HEADS = 4
C_Q_RANK = 384
C_KV_RANK = 256
C_NOPE = 128
C_ROPE = 64
C_V = 128
ROPE_THETA = 10000.0
REL_BUCKETS = 32
REL_MAX_DIST = 128
N_BRANCH = 3
EPS = 1e-6
NEG_BIG = -1e30
LB_FLOOR = 1e-30

LANES = 128
SUBLANES = 8
VMEM_LIMIT = 56 * 1024 * 1024
INT_MIN = np.int32(-2 ** 31)
SORT_ROWS = LANES // SUBLANES


def _oddeven_merge_sort(n):
    pairs = []
    p = 1
    while p < n:
        k = p
        while k >= 1:
            for j in range(k % p, n - k, 2 * k):
                for i in range(min(k, n - j - k)):
                    if (i + j) // (2 * p) == (i + j + k) // (2 * p):
                        pairs.append((i + j, i + j + k))
            k //= 2
        p *= 2
    return tuple(pairs)


_SORT16 = _oddeven_merge_sort(SORT_ROWS)

_NT = (((1,), (1,)), ((), ()))


def _params(*sem):
    return pltpu.CompilerParams(dimension_semantics=sem, vmem_limit_bytes=VMEM_LIMIT)


def _rms(x, g):
    return x * lax.rsqrt(jnp.mean(x * x, axis=-1, keepdims=True) + EPS) * g


def _const_spec(shape):
    nd = len(shape)
    return pl.BlockSpec(shape, lambda *_: (0,) * nd)


def _fold8(x, op):
    r, c = x.shape
    return op(x.reshape(r // SUBLANES, SUBLANES, c), axis=0)


def _store_key_major(o_ref, rows, tile):
    for u in range(rows.shape[0] // tile):
        o_ref[u] = rows[u * tile:(u + 1) * tile, :].T.astype(o_ref.dtype)


def _norm_matmul_kernel(x_ref, g_ref, *refs, widths, key_tiles, col_chunk):
    w_refs, out_refs = refs[:len(widths)], refs[len(widths):]
    h = _rms(x_ref[...], g_ref[...]).astype(BF16)
    k = 0
    for w_ref, w_widths in zip(w_refs, widths):
        c0 = 0
        for w in w_widths:
            o_ref = out_refs[k]
            if key_tiles[k]:
                _store_key_major(o_ref, jnp.dot(h, w_ref[:, c0:c0 + w], preferred_element_type=F32),
                                 key_tiles[k])
            else:
                for s in range(0, w, col_chunk):
                    e = min(s + col_chunk, w)
                    o_ref[:, s:e] = jnp.dot(h, w_ref[:, c0 + s:c0 + e],
                                            preferred_element_type=F32).astype(o_ref.dtype)
            c0 += w
            k += 1


def _norm_matmul(x, g, ws, widths, dtypes, key_tiles, tm):
    n, k = x.shape
    flat = [wd for w_widths in widths for wd in w_widths]
    assert n % tm == 0 and all(w.shape == (k, sum(ww)) for w, ww in zip(ws, widths))
    kern = functools.partial(_norm_matmul_kernel, widths=tuple(tuple(ww) for ww in widths),
                             key_tiles=tuple(key_tiles), col_chunk=512)
    specs, shapes = [], []
    for wd, dt, kt in zip(flat, dtypes, key_tiles):
        if kt:
            specs.append(pl.BlockSpec((tm // kt, wd, kt), lambda i: (i, 0, 0)))
            shapes.append(jax.ShapeDtypeStruct((n // kt, wd, kt), dt))
        else:
            specs.append(pl.BlockSpec((tm, wd), lambda i: (i, 0)))
            shapes.append(jax.ShapeDtypeStruct((n, wd), dt))
    return pl.pallas_call(
        kern,
        grid=(n // tm,),
        in_specs=[pl.BlockSpec((tm, k), lambda i: (i, 0)), _const_spec((1, k))]
                 + [_const_spec(w.shape) for w in ws],
        out_specs=specs,
        out_shape=shapes,
        compiler_params=_params("parallel"),
        name="norm_proj",
    )(x, g.reshape(1, k), *ws)


def _dsa_kernel(qa_ref, iq_ref, iw_ref, ka_ref, vt_ref, ik_ref, bias_ref, o_ref,
                keys_ref, sorted_ref, s_ref, qm_ref, qs_ref, wt_ref, thr_ref, ties_ref,
                *, topk, n_blocks, group, scale, idx_scale):
    t = LANES
    gt = group * t
    i = pl.program_id(1)
    n_groups = (i + group) // group
    max_groups = n_blocks // group
    krow = lax.broadcasted_iota(I32, (t, t), 0)
    qcol = lax.broadcasted_iota(I32, (t, t), 1)
    lane_lo = qcol < IDX_DIM

    for h in range(IDX_HEADS):
        pair = iq_ref[0, :, (h // 2) * t:(h // 2 + 1) * t]
        keep = lane_lo if h % 2 == 0 else jnp.logical_not(lane_lo)
        qm_ref[h * t:(h + 1) * t, :] = jnp.where(keep, pair, jnp.zeros_like(pair))
    for h in range(A_HEADS):
        qs_ref[h * t:(h + 1) * t, :] = qa_ref[0, :, h * t:(h + 1) * t]
    wt_ref[...] = iw_ref[0].T

    def index_logits(g):
        return lax.dot_general(ik_ref[0, g * gt:(g + 1) * gt, :], qm_ref[...], _NT,
                               preferred_element_type=F32)

    def score_group(g, lg):
        for u in range(group):
            j = g * group + u
            acc = jnp.zeros((t, t), F32)
            for h in range(IDX_HEADS):
                acc = acc + jnp.maximum(lg[u * t:(u + 1) * t, h * t:(h + 1) * t], 0.0) * wt_ref[h:h + 1, :]
            score = acc * idx_scale
            score = jnp.where(krow + j * t <= qcol + i * t, score, NEG_BIG)
            score = jnp.where(score == 0.0, 0.0, score)
            bits = pltpu.bitcast(score, I32)
            key = bits ^ ((bits >> 31) & np.int32(0x7FFFFFFF))
            keys_ref[j] = key
            rows = [key[r * SUBLANES:(r + 1) * SUBLANES, :] for r in range(SORT_ROWS)]
            for a, b in _SORT16:
                rows[a], rows[b] = jnp.maximum(rows[a], rows[b]), jnp.minimum(rows[a], rows[b])
            for r in range(SORT_ROWS):
                sorted_ref[j, r] = rows[r]

    def score_all(ng):
        lg = index_logits(0)
        for g in range(ng):
            nxt = index_logits(g + 1) if g + 1 < ng else None
            score_group(g, lg)
            lg = nxt

    for v in range(1, max_groups + 1):
        pl.when(n_groups == v)(functools.partial(score_all, v))

    thr_ref[...] = jnp.full(thr_ref.shape, INT_MIN, I32)
    ties_ref[...] = jnp.zeros(ties_ref.shape, F32)

    def count(n, pred):
        acc = [jnp.zeros((SUBLANES, t), F32) for _ in range(5)]
        for j in range(n):
            v = [sorted_ref[j, r] for r in range(SORT_ROWS)]
            m1 = pred(v[7])
            m2 = pred(jnp.where(m1, v[11], v[3]))
            m3 = pred(jnp.where(m1, jnp.where(m2, v[13], v[9]), jnp.where(m2, v[5], v[1])))
            m4 = pred(jnp.where(m1, jnp.where(m2, jnp.where(m3, v[14], v[12]), jnp.where(m3, v[10], v[8])),
                                jnp.where(m2, jnp.where(m3, v[6], v[4]), jnp.where(m3, v[2], v[0]))))
            m5 = pred(v[15])
            acc = [a + jnp.where(m, 1.0, 0.0) for a, m in zip(acc, (m1, m2, m3, m4, m5))]
        cnt = 8.0 * acc[0] + 4.0 * acc[1] + 2.0 * acc[2] + acc[3] + acc[4]
        return jnp.sum(cnt, axis=0, keepdims=True)

    def search(n):
        def search_pass(b, thr):
            cand = thr + lax.shift_left(np.int32(1), 31 - b)
            return jnp.where(count(n, lambda key: key >= cand) >= topk, cand, thr)

        thr = lax.fori_loop(0, 32, search_pass, jnp.full((1, t), INT_MIN, I32))
        thr_ref[...] = jnp.broadcast_to(thr, thr_ref.shape)
        ties_ref[...] = jnp.broadcast_to(topk - count(n, lambda key: key > thr), ties_ref.shape)

    for c in range(n_blocks):
        if (c + 1) * t > topk:
            pl.when(i == c)(functools.partial(search, c + 1))

    thr = thr_ref[0:1, :]
    n_ties = ties_ref[0:1, :]

    r2 = lax.broadcasted_iota(I32, (2 * t, t), 0)
    c2 = lax.broadcasted_iota(I32, (2 * t, t), 1)
    tie_lhs = jnp.where(jnp.logical_or(r2 >= t, c2 < r2), 1.0, 0.0).astype(BF16)

    def sweep_a_mxu(g):
        keys = [keys_ref[g * group + u] for u in range(group)]
        eqs = [key == thr for key in keys]
        eq_all = jnp.concatenate([jnp.where(eq, 1.0, 0.0).astype(BF16) for eq in eqs], axis=1)
        pref = jnp.dot(tie_lhs, eq_all, preferred_element_type=F32)
        s_grp = lax.dot_general(ka_ref[0, g * gt:(g + 1) * gt, :], qs_ref[...], _NT,
                                preferred_element_type=F32)
        return keys, eqs, pref, s_grp

    def sweep_a_vpu(g, operands, seen, mx):
        keys, eqs, pref, s_grp = operands
        for u in range(group):
            j = g * group + u
            us = slice(u * t, (u + 1) * t)
            rank = jnp.where(keys[u] > thr, -1.0, jnp.where(eqs[u], seen + pref[:t, us], 3e38))
            rank = jnp.where(krow + j * t <= qcol + i * t, rank, 3e38)
            valid = rank < n_ties
            which = jnp.where(j == i, 1, jnp.where(j == i - 1, 0, 2))
            s_all = s_grp[us, :] * (scale * math.log2(math.e)) + bias_ref[which]
            for h in range(A_HEADS):
                s_h = jnp.where(valid, s_all[:, h * t:(h + 1) * t], NEG_BIG)
                s_ref[j, :, h * t:(h + 1) * t] = s_h
                mx[h] = jnp.maximum(mx[h], _fold8(s_h, jnp.max))
            seen = seen + pref[t:t + 1, us]
        return seen, mx

    def sweep_b(g, m_all, l8):
        ps = []
        for u in range(group):
            p = jnp.exp2(s_ref[g * group + u] - m_all)
            l8 = l8 + _fold8(p, jnp.sum)
            ps.append(p.astype(BF16))
        vt_grp = jnp.concatenate([vt_ref[0, g * group + u] for u in range(group)], axis=1)
        return l8, jnp.dot(vt_grp, jnp.concatenate(ps, axis=0), preferred_element_type=F32)

    def attend(ng):
        seen = jnp.zeros((1, t), F32)
        mx = [jnp.full((SUBLANES, t), NEG_BIG, F32) for _ in range(A_HEADS)]
        operands = sweep_a_mxu(0)
        for g in range(ng):
            nxt = sweep_a_mxu(g + 1) if g + 1 < ng else None
            seen, mx = sweep_a_vpu(g, operands, seen, mx)
            operands = nxt
        m_all = jnp.concatenate([jnp.max(m, axis=0, keepdims=True) for m in mx], axis=1)
        l8 = jnp.zeros((SUBLANES, A_HEADS * t), F32)
        acc = None
        for g in range(ng):
            l8, pv = sweep_b(g, m_all, l8)
            acc = pv if acc is None else acc + pv
        out = acc * (1.0 / jnp.sum(l8, axis=0, keepdims=True))
        for h in range(A_HEADS):
            o_ref[0, :, h * t:(h + 1) * t] = out[:, h * t:(h + 1) * t].T.astype(o_ref.dtype)

    for v in range(1, max_groups + 1):
        pl.when(n_groups == v)(functools.partial(attend, v))


def _dsa(qa, iq, iw, ka, vt, ik2, bias_tab):
    b, s, _ = qa.shape
    t = LANES
    nb = s // t
    group = math.gcd(nb, 2)
    topk = min(TOPK_MAX, s // 4)
    kern = functools.partial(_dsa_kernel, topk=float(topk), n_blocks=nb, group=group,
                             scale=A_HEAD_DIM ** -0.5,
                             idx_scale=(IDX_DIM ** -0.5) * (IDX_HEADS ** -0.5))
    qspec = lambda w: pl.BlockSpec((1, t, w), lambda bi, i: (bi, i, 0))
    kspec = pl.BlockSpec((1, s, t), lambda bi, i: (bi, 0, 0))
    return pl.pallas_call(
        kern,
        grid=(b, nb),
        in_specs=[qspec(A_HEADS * t), qspec(IDX_HEADS * IDX_DIM), qspec(t), kspec,
                  pl.BlockSpec((1, nb, t, t), lambda bi, i: (bi, 0, 0, 0)), kspec,
                  _const_spec(bias_tab.shape)],
        out_specs=qspec(A_HEADS * t),
        out_shape=jax.ShapeDtypeStruct((b, s, A_HEADS * t), BF16),
        scratch_shapes=[pltpu.VMEM((nb, t, t), I32),
                        pltpu.VMEM((nb, SORT_ROWS, SUBLANES, t), I32),
                        pltpu.VMEM((nb, t, A_HEADS * t), F32),
                        pltpu.VMEM((IDX_HEADS * t, t), BF16),
                        pltpu.VMEM((A_HEADS * t, t), BF16),
                        pltpu.VMEM((t, t), F32),
                        pltpu.VMEM((SUBLANES, t), I32),
                        pltpu.VMEM((SUBLANES, t), F32)],
        compiler_params=_params("parallel", "parallel"),
        name="dsa_attention",
    )(qa, iq, iw, ka, vt, ik2, bias_tab)


def _hgrn_kernel(q_ref, f_ref, i_ref, g_ref, lb_ref, gain_ref, o_ref, state_ref, *, n_chunks, n_seq):
    c = B_CHUNK
    rc = n_seq * c
    kd, vd = B_KEY_DIM, B_VAL_DIM
    w = B_HEADS * kd

    @pl.when(pl.program_id(1) == 0)
    def _():
        state_ref[...] = jnp.zeros_like(state_ref)

    row = lax.broadcasted_iota(I32, (rc, rc), 0)
    col = lax.broadcasted_iota(I32, (rc, rc), 1)
    same_seq = (row >= col) if n_seq == 1 else jnp.logical_and(
        row >= col, lax.shift_right_logical(row, 6) == lax.shift_right_logical(col, 6))
    causal = same_seq
    assert c == 64
    tril = jnp.where(causal, 1.0, 0.0).astype(BF16)
    seqs = [slice(b * c, (b + 1) * c) for b in range(n_seq)]
    seq_of_row = lax.shift_right_logical(lax.broadcasted_iota(I32, (rc, w), 0), 6)

    def per_seq_row(x, r):
        return jnp.concatenate([jnp.broadcast_to(x[b * c + r:b * c + r + 1, :], (c, x.shape[1]))
                                for b in range(n_seq)], axis=0)

    def stacked(ref, rows):
        return jnp.concatenate([ref[b, rows, :] for b in range(n_seq)], axis=0)
    q_scale = kd ** -0.5
    lb = lb_ref[...]
    lb_floor = jnp.maximum(lb, LB_FLOOR)
    one_m_lb = 1.0 - lb
    heads = [slice(h * kd, (h + 1) * kd) for h in range(B_HEADS)]

    def stage_decay(ci):
        rows = slice(ci * c, (ci + 1) * c)
        fr = stacked(f_ref, rows)
        z = jnp.exp(-jnp.abs(fr))
        r = 1.0 / (1.0 + z)
        sig_pos = jnp.where(fr >= 0, r, z * r)
        sig_neg = jnp.where(fr >= 0, z * r, r)
        log_f = jnp.log(lb_floor + one_m_lb * sig_pos)
        k_in = one_m_lb * sig_neg
        hi = log_f.astype(BF16)
        rest = log_f - hi.astype(F32)
        mid = rest.astype(BF16)
        lo = (rest - mid.astype(F32)).astype(BF16)
        cs = jnp.dot(tril, jnp.concatenate([hi, mid, lo], axis=1), preferred_element_type=F32)
        return k_in, cs

    def apply_update(pending):
        if pending is not None:
            d_last, st, upd = pending
            for b in range(n_seq):
                for h, hs in enumerate(heads):
                    state_ref[b, h] = d_last[b * c:b * c + 1, hs] * st[b][h] + upd[b][h]

    def stage_scores(ci, decay, pending):
        rows = slice(ci * c, (ci + 1) * c)
        k_in, cs = decay
        bsum = cs[:, :w] + cs[:, w:2 * w] + cs[:, 2 * w:]
        b_mid = per_seq_row(bsum, c // 2 - 1)
        b_last = per_seq_row(bsum, c - 1)
        qs = stacked(q_ref, rows) * q_scale
        v = stacked(i_ref, rows)
        v_b = v.astype(BF16)
        v_t = v.T.astype(BF16)
        q_mid = (qs * jnp.exp(bsum - b_mid)).astype(BF16)
        k_mid = (k_in * jnp.exp(b_mid - bsum)).astype(BF16)
        q_dec = (qs * jnp.exp(bsum)).astype(BF16)
        k_end = (k_in * jnp.exp(b_last - bsum)).astype(BF16)
        d_last = jnp.exp(b_last)
        apply_update(pending)
        st = [[state_ref[b, h] for h in range(B_HEADS)] for b in range(n_seq)]
        attn = [lax.dot_general(q_mid[:, hs], k_mid[:, hs], _NT, preferred_element_type=F32)
                for hs in heads]
        inter = [jnp.concatenate(
            [lax.dot_general(q_dec[sq, hs], st[b][h].astype(BF16), _NT, preferred_element_type=F32)
             for b, sq in enumerate(seqs)], axis=0) for h, hs in enumerate(heads)]
        k_seq = [k_end if n_seq == 1 else jnp.where(seq_of_row == b, k_end, jnp.zeros_like(k_end))
                 for b in range(n_seq)]
        upd = [[jnp.dot(v_t[hs, :], k_seq[b][:, hs], preferred_element_type=F32) for hs in heads]
               for b in range(n_seq)]
        return (attn, inter, v_b), (d_last, st, upd)

    def stage_output(ci, scores):
        rows = slice(ci * c, (ci + 1) * c)
        attn, inter, v_b = scores
        attn = [jnp.where(causal, a, 0.0).astype(BF16) for a in attn]
        outs = [jnp.dot(attn[h], v_b[:, hs], preferred_element_type=F32) + inter[h]
                for h, hs in enumerate(heads)]
        o = jnp.concatenate([_rms(o_h, gain_ref[...]) for o_h in outs], axis=1)
        g = stacked(g_ref, rows)
        o = (o * (g * (1.0 / (1.0 + jnp.exp(-g))))).astype(o_ref.dtype)
        for b, sq in enumerate(seqs):
            o_ref[b, rows, :] = o[sq, :]

    decay, scores, pending = {}, {}, None
    for step in range(n_chunks + 2):
        if step < n_chunks:
            decay[step] = stage_decay(step)
        if 0 <= step - 1 < n_chunks:
            scores[step - 1], pending = stage_scores(step - 1, decay.pop(step - 1), pending)
        if 0 <= step - 2 < n_chunks:
            stage_output(step - 2, scores.pop(step - 2))
    apply_update(pending)


def _hgrn(bq, bf, bi, bg, lb, gain, sb):
    b, s, w = bq.shape
    assert s % sb == 0 and sb % B_CHUNK == 0
    n_seq = 2 if b % 2 == 0 else 1
    kern = functools.partial(_hgrn_kernel, n_chunks=sb // B_CHUNK, n_seq=n_seq)
    spec = pl.BlockSpec((n_seq, sb, w), lambda bi_, si: (bi_, si, 0))
    return pl.pallas_call(
        kern,
        grid=(b // n_seq, s // sb),
        in_specs=[spec, spec, spec, spec, _const_spec((1, w)), _const_spec((1, B_VAL_DIM))],
        out_specs=spec,
        out_shape=jax.ShapeDtypeStruct((b, s, w), BF16),
        scratch_shapes=[pltpu.VMEM((n_seq, B_HEADS, B_VAL_DIM, B_KEY_DIM), F32)],
        compiler_params=_params("parallel", "arbitrary"),
        name="hgrn2",
    )(bq, bf, bi, bg, lb.reshape(1, w), gain.reshape(1, B_VAL_DIM))


def _mla_prep_kernel(cq_ref, ckv_ref, kpe_raw_ref, cos_ref, sin_ref, gq_ref, gkv_ref, wq_ref, wkv_ref,
                     qn_ref, qpe_ref, kn_ref, vt_ref, kpe_ref, *, key_tile):
    nw = C_HEADS * C_NOPE
    pw = C_HEADS * C_ROPE
    cos, sin = cos_ref[...], sin_ref[...]
    cos2 = jnp.concatenate([cos] * (pw // LANES), axis=1)
    sin2 = jnp.concatenate([sin] * (pw // LANES), axis=1)
    q = jnp.dot(_rms(cq_ref[...], gq_ref[...]).astype(BF16), wq_ref[...], preferred_element_type=F32)
    qn_ref[...] = q[:, :nw].astype(qn_ref.dtype)
    qpe_ref[...] = (q[:, nw:nw + pw] * cos2 + q[:, nw + pw:] * sin2).astype(qpe_ref.dtype)
    kv = jnp.dot(_rms(ckv_ref[...], gkv_ref[...]).astype(BF16), wkv_ref[...], preferred_element_type=F32)
    kn_ref[...] = kv[:, :nw].astype(kn_ref.dtype)
    _store_key_major(vt_ref, kv[:, nw:], key_tile)
    kpe_ref[...] = (kpe_raw_ref[:, :LANES] * cos + kpe_raw_ref[:, LANES:] * sin).astype(kpe_ref.dtype)


def _mla_prep(cq, ckv, kpe_raw, cos_t, sin_t, gq, gkv, wq, wkv, tm, key_tile):
    n = cq.shape[0]
    nw = C_HEADS * C_NOPE
    vw = C_HEADS * C_V
    row = lambda w: pl.BlockSpec((tm, w), lambda i: (i, 0))
    widths = (nw, C_HEADS * C_ROPE, nw, None, LANES)
    vt_spec = pl.BlockSpec((tm // key_tile, vw, key_tile), lambda i: (i, 0, 0))
    vt_shape = jax.ShapeDtypeStruct((n // key_tile, vw, key_tile), BF16)
    return pl.pallas_call(
        functools.partial(_mla_prep_kernel, key_tile=key_tile),
        grid=(n // tm,),
        in_specs=[row(C_Q_RANK), row(C_KV_RANK), row(2 * LANES), row(LANES), row(LANES),
                  _const_spec((1, C_Q_RANK)), _const_spec((1, C_KV_RANK)),
                  _const_spec(wq.shape), _const_spec(wkv.shape)],
        out_specs=[vt_spec if w is None else row(w) for w in widths],
        out_shape=[vt_shape if w is None else jax.ShapeDtypeStruct((n, w), BF16) for w in widths],
        compiler_params=_params("parallel"),
        name="mla_prep",
    )(cq, ckv, kpe_raw, cos_t, sin_t, gq.reshape(1, -1), gkv.reshape(1, -1), wq, wkv)


def _mla_kernel(qn_ref, qpe_ref, kn_ref, kpe_ref, vt_ref, o_ref, q_ref, acc_ref, *, t, n_tiles, scale):
    i = pl.program_id(1)
    t2 = 2 * t
    krow = lax.broadcasted_iota(I32, (t2, t), 0)
    qcol = lax.broadcasted_iota(I32, (t2, t), 1)
    lane_lo = lax.broadcasted_iota(I32, (t, LANES), 1) < C_ROPE
    for h in range(C_HEADS):
        pair = qpe_ref[0, :, (h // 2) * LANES:(h // 2 + 1) * LANES]
        keep = lane_lo if h % 2 == 0 else jnp.logical_not(lane_lo)
        q_ref[h, :, :C_NOPE] = qn_ref[0, :, h * C_NOPE:(h + 1) * C_NOPE]
        q_ref[h, :, C_NOPE:] = jnp.where(keep, pair, jnp.zeros_like(pair))
    acc_ref[...] = jnp.zeros_like(acc_ref)

    def logits_of(g):
        rows = slice(g * t2, (g + 1) * t2)
        kpe_t = kpe_ref[0, rows, :]
        return [lax.dot_general(
            jnp.concatenate([kn_ref[0, rows, h * C_NOPE:(h + 1) * C_NOPE], kpe_t], axis=1),
            q_ref[h], _NT, preferred_element_type=F32) for h in range(C_HEADS)]

    def softmax_pv(g, logits, ms, ls, masked):
        new_m, new_l = [], []
        for h in range(C_HEADS):
            s = logits[h] * (scale * math.log2(math.e))
            if masked:
                s = jnp.where(krow + g * t2 <= qcol + i * t, s, NEG_BIG)
            m_new = jnp.maximum(ms[h], jnp.max(s, axis=0, keepdims=True))
            alpha = jnp.exp2(ms[h] - m_new)
            p = jnp.exp2(s - m_new)
            new_l.append(alpha * ls[h] + jnp.sum(p, axis=0, keepdims=True))
            new_m.append(m_new)
            hs = slice(h * C_V, (h + 1) * C_V)
            v_t = jnp.concatenate([vt_ref[0, 2 * g, hs, :], vt_ref[0, 2 * g + 1, hs, :]], axis=1)
            acc_ref[h] = alpha * acc_ref[h] + jnp.dot(v_t, p.astype(BF16), preferred_element_type=F32)
        return new_m, new_l

    def run(n_full):
        ms = [jnp.full((1, t), NEG_BIG, F32) for _ in range(C_HEADS)]
        ls = [jnp.zeros((1, t), F32) for _ in range(C_HEADS)]
        logits = logits_of(0)
        for g in range(n_full + 1):
            nxt = logits_of(g + 1) if g < n_full else None
            ms, ls = softmax_pv(g, logits, ms, ls, masked=(g == n_full))
            logits = nxt
        for h in range(C_HEADS):
            o_ref[0, :, h * C_V:(h + 1) * C_V] = (acc_ref[h] * (1.0 / ls[h])).T.astype(o_ref.dtype)

    for v in range(n_tiles // 2):
        pl.when(i // 2 == v)(functools.partial(run, v))


def _mla(qn, qpe, kn, kpe, vt, t):
    b, s, _ = qn.shape
    nt = s // t
    assert nt % 2 == 0
    kern = functools.partial(_mla_kernel, t=t, n_tiles=nt, scale=(C_NOPE + C_ROPE) ** -0.5)
    qspec = lambda w: pl.BlockSpec((1, t, w), lambda bi, i: (bi, i, 0))
    kspec = lambda w: pl.BlockSpec((1, s, w), lambda bi, i: (bi, 0, 0))
    return pl.pallas_call(
        kern,
        grid=(b, nt),
        in_specs=[qspec(qn.shape[2]), qspec(qpe.shape[2]), kspec(kn.shape[2]), kspec(kpe.shape[2]),
                  pl.BlockSpec((1, nt, C_HEADS * C_V, t), lambda bi, i: (bi, 0, 0, 0))],
        out_specs=qspec(C_HEADS * C_V),
        out_shape=jax.ShapeDtypeStruct((b, s, C_HEADS * C_V), BF16),
        scratch_shapes=[pltpu.VMEM((C_HEADS, t, C_NOPE + LANES), BF16),
                        pltpu.VMEM((C_HEADS, C_V, t), F32)],
        compiler_params=_params("parallel", "parallel"),
        name="mla_attention",
    )(qn, qpe, kn, kpe, vt)


def _merge_kernel(x_ref, g_ref, wg_ref, oa_ref, ob_ref, oc_ref, wa_ref, wb_ref, wc_ref, wo_ref,
                  out_ref, mixed_ref, *, col_chunk):
    x = x_ref[...]
    d = x.shape[1]
    h = _rms(x, g_ref[...]).astype(BF16)
    branches = ((oa_ref, wa_ref), (ob_ref, wb_ref), (oc_ref, wc_ref))
    for s in range(0, d, col_chunk):
        cs = slice(s, s + col_chunk)
        mixed = None
        for bidx, (o_ref, w_ref) in enumerate(branches):
            logits = jnp.dot(h, wg_ref[:, bidx * d + s:bidx * d + s + col_chunk],
                             preferred_element_type=F32)
            gate = 1.0 / (1.0 + jnp.exp(-logits))
            term = gate * jnp.dot(o_ref[...], w_ref[:, cs], preferred_element_type=F32)
            mixed = term if mixed is None else mixed + term
        mixed_ref[:, cs] = mixed.astype(BF16)
    out_ref[...] = x + jnp.dot(mixed_ref[...], wo_ref[...], preferred_element_type=F32)


def _merge(x, g, w_gate, oa, ob, oc, wa, wb, wc, wo, tm):
    n, d = x.shape
    row = lambda w: pl.BlockSpec((tm, w), lambda i: (i, 0))
    kern = functools.partial(_merge_kernel, col_chunk=256)
    return pl.pallas_call(
        kern,
        grid=(n // tm,),
        in_specs=[row(d), _const_spec((1, d)), _const_spec(w_gate.shape),
                  row(oa.shape[1]), row(ob.shape[1]), row(oc.shape[1]),
                  _const_spec(wa.shape), _const_spec(wb.shape), _const_spec(wc.shape),
                  _const_spec(wo.shape)],
        out_specs=row(d),
        out_shape=jax.ShapeDtypeStruct((n, d), F32),
        scratch_shapes=[pltpu.VMEM((tm, d), BF16)],
        compiler_params=_params("parallel"),
        name="merge_out_proj",
    )(x, g.reshape(1, d), w_gate, oa, ob, oc, wa, wb, wc, wo)


def _ffn_kernel(x_ref, g_ref, wg_ref, wu_ref, wd_ref, gf_ref, out_ref, act_ref, *, col_chunk,
                final_norm):
    x = x_ref[...]
    h = _rms(x, g_ref[...]).astype(BF16)
    dff = wg_ref.shape[1]
    for s in range(0, dff, col_chunk):
        cs = slice(s, s + col_chunk)
        gate = jnp.dot(h, wg_ref[:, cs], preferred_element_type=F32)
        up = jnp.dot(h, wu_ref[:, cs], preferred_element_type=F32)
        act_ref[:, cs] = (gate * (1.0 / (1.0 + jnp.exp(-gate))) * up).astype(BF16)
    y = x + jnp.dot(act_ref[...], wd_ref[...], preferred_element_type=F32)
    if final_norm:
        y = _rms(y, gf_ref[...])
    out_ref[...] = y


def _ffn(x, g, wg, wu, wd, gf, tm, final_norm):
    n, d = x.shape
    dff = wg.shape[1]
    row = pl.BlockSpec((tm, d), lambda i: (i, 0))
    kern = functools.partial(_ffn_kernel, col_chunk=256, final_norm=final_norm)
    return pl.pallas_call(
        kern,
        grid=(n // tm,),
        in_specs=[row, _const_spec((1, d)), _const_spec(wg.shape), _const_spec(wu.shape),
                  _const_spec(wd.shape), _const_spec((1, d))],
        out_specs=row,
        out_shape=jax.ShapeDtypeStruct((n, d), F32),
        scratch_shapes=[pltpu.VMEM((tm, dff), BF16)],
        compiler_params=_params("parallel"),
        name="swiglu_ffn",
    )(x, g.reshape(1, d), wg, wu, wd, gf.reshape(1, d))


def _t5_bucket(dist):
    max_exact = REL_BUCKETS // 2
    d = jnp.maximum(dist, 0)
    dl = jnp.maximum(d, max_exact).astype(F32)
    large = max_exact + (jnp.log(dl / max_exact) / math.log(REL_MAX_DIST / max_exact)
                         * (REL_BUCKETS - max_exact)).astype(I32)
    large = jnp.minimum(large, REL_BUCKETS - 1)
    return jnp.where(d < max_exact, d, large)


def _swap_halves(w):
    half = w.shape[-1] // 2
    return jnp.concatenate([w[..., half:], w[..., :half]], axis=-1)


def _in_proj_weights(w_in, d_model):
    splits = (A_HEADS * A_HEAD_DIM, A_HEAD_DIM, A_HEAD_DIM, IDX_HEADS * IDX_DIM, IDX_DIM, IDX_HEADS,
              B_HEADS * B_KEY_DIM, B_HEADS * B_KEY_DIM, B_HEADS * B_VAL_DIM, B_HEADS * B_VAL_DIM,
              C_Q_RANK, C_KV_RANK, C_ROPE, N_BRANCH * d_model)
    st = np.concatenate([[0], np.cumsum(splits)])
    w_bf = w_in.astype(BF16)
    col = lambda a, b: w_bf[:, :, st[a]:st[b]]
    ik, pe = col(4, 5), col(12, 13)
    iw = jnp.pad(col(5, 6), ((0, 0), (0, 0), (0, LANES - IDX_HEADS)))
    pe_sw = _swap_halves(pe)
    reps = LANES // C_ROPE
    ws = [col(0, 4),
          jnp.concatenate([ik, ik, iw], axis=2),
          col(6, 12),
          jnp.concatenate([pe] * reps + [pe_sw] * reps, axis=2)]
    widths = [splits[:4], (LANES, LANES), splits[6:12], (2 * LANES,)]
    dtypes = [BF16] * 5 + [F32] * 8
    return ws, widths, dtypes, col(13, 14)


def kernel(x, positions, w_in, w_up_a, w_up_b, w_up_c, w_out, mla_q_norm, mla_w_qb, mla_kv_norm,
           mla_w_kvb, hgrn_lb_logits, hgrn_out_norm, rel_bias, attn_norm, ffn_norm, w_ffn_gate,
           w_ffn_up, w_ffn_down, final_norm):
    bsz, s_len, d_model = x.shape
    depth = w_in.shape[0]
    n = bsz * s_len
    t = LANES

    w_proj, widths, dtypes, w_gate = _in_proj_weights(w_in, d_model)
    wq = mla_w_qb.reshape(depth, C_Q_RANK, C_HEADS, C_NOPE + C_ROPE)
    q_pe = wq[..., C_NOPE:]
    w_qb = jnp.concatenate([wq[..., :C_NOPE].reshape(depth, C_Q_RANK, -1),
                            q_pe.reshape(depth, C_Q_RANK, -1),
                            _swap_halves(q_pe).reshape(depth, C_Q_RANK, -1)], axis=2).astype(BF16)
    wkv = mla_w_kvb.reshape(depth, C_KV_RANK, C_HEADS, C_NOPE + C_V)
    w_kvb = jnp.concatenate([wkv[..., :C_NOPE].reshape(depth, C_KV_RANK, -1),
                             wkv[..., C_NOPE:].reshape(depth, C_KV_RANK, -1)], axis=2).astype(BF16)
    bf = lambda w: w.astype(BF16)
    w_up_a, w_up_b, w_up_c, w_out = bf(w_up_a), bf(w_up_b), bf(w_up_c), bf(w_out)
    w_ffn_gate, w_ffn_up, w_ffn_down = bf(w_ffn_gate), bf(w_ffn_up), bf(w_ffn_down)

    p_lb = jax.nn.softmax(hgrn_lb_logits.astype(F32), axis=0)
    lower_bounds = jnp.cumsum(p_lb, axis=0) - p_lb[0:1]
    inv_freq = ROPE_THETA ** (-jnp.arange(0, C_ROPE, 2, dtype=F32) / C_ROPE)
    ang = positions.astype(F32)[..., None] * inv_freq
    cos, sin = jnp.cos(ang), jnp.sin(ang)
    reps = LANES // C_ROPE
    cos_t = jnp.tile(jnp.concatenate([cos, cos], axis=-1), (1, 1, reps)).reshape(n, LANES)
    sin_t = jnp.tile(jnp.concatenate([-sin, sin], axis=-1), (1, 1, reps)).reshape(n, LANES)
    q_idx = jnp.arange(t, dtype=I32)[None, :]
    k_idx = jnp.arange(t, dtype=I32)[:, None]
    def tab(dist):
        onehot = jax.nn.one_hot(_t5_bucket(dist), REL_BUCKETS, dtype=F32)
        return jnp.einsum("kqb,bh->khq", onehot, rel_bias.astype(F32),
                          precision=lax.Precision.HIGHEST).reshape(t, A_HEADS * t)
    far = jnp.broadcast_to(rel_bias[REL_BUCKETS - 1].astype(F32)[None, :, None],
                           (t, A_HEADS, t)).reshape(t, A_HEADS * t)
    bias_tab = jnp.stack([tab(q_idx + t - k_idx), tab(q_idx - k_idx), far]) * math.log2(math.e)

    tm = min(512, n)
    t_mla = min(256, s_len)
    x2 = x.reshape(n, d_model)
    r3 = lambda a: a.reshape(bsz, s_len, a.shape[-1])
    per_seq = lambda a: a.reshape(bsz, a.shape[0] // bsz, a.shape[1], a.shape[2])
    key_tiles = [0, 0, t] + [0] * 10
    for l in range(depth):
        (qa, ka, vta, iq, ik2, iw, bq, bfr, bi, bg, cq, ckv, kpe_raw) = _norm_matmul(
            x2, attn_norm[l], [w[l] for w in w_proj], widths, dtypes, key_tiles, tm)
        o_a = _dsa(r3(qa), r3(iq), r3(iw), r3(ka), per_seq(vta), r3(ik2), bias_tab)
        o_b = _hgrn(r3(bq), r3(bfr), r3(bi), r3(bg), lower_bounds[l], hgrn_out_norm[l],
                    sb=min(512, s_len))
        qn, qpe, kn, vtc, kpe = _mla_prep(cq, ckv, kpe_raw, cos_t, sin_t, mla_q_norm[l], mla_kv_norm[l],
                                          w_qb[l], w_kvb[l], tm, t_mla)
        o_c = _mla(r3(qn), r3(qpe), r3(kn), r3(kpe), per_seq(vtc), t_mla)
        x2 = _merge(x2, attn_norm[l], w_gate[l], o_a.reshape(n, -1), o_b.reshape(n, -1),
                    o_c.reshape(n, -1), w_up_a[l], w_up_b[l], w_up_c[l], w_out[l], tm)
        x2 = _ffn(x2, ffn_norm[l], w_ffn_gate[l], w_ffn_up[l], w_ffn_down[l], final_norm,
                  tm, final_norm=(l == depth - 1))
    return x2.reshape(bsz, s_len, d_model)
```

```python
import functools
import math

import jax
import jax.numpy as jnp
import numpy as np
from jax import lax
from jax.experimental import pallas as pl
from jax.experimental.pallas import tpu as pltpu

F32 = jnp.float32
BF16 = jnp.bfloat16
I32 = jnp.int32

A_HEADS = 4
A_HEAD_DIM = 128
IDX_HEADS = 8
IDX_DIM = 64
TOPK_MAX = 256
B_HEADS = 4
B_KEY_DIM = 128
B_VAL_DIM = 128
B_CHUNK = 64
C_HEADS = 4
C_Q_RANK = 384
C_KV_RANK = 256
C_NOPE = 128
C_ROPE = 64
C_V = 128
ROPE_THETA = 10000.0
REL_BUCKETS = 32
REL_MAX_DIST = 128
N_BRANCH = 3
EPS = 1e-6
NEG_BIG = -1e30
LB_FLOOR = 1e-30

LANES = 128
SUBLANES = 8
VMEM_LIMIT = 56 * 1024 * 1024
INT_MIN = np.int32(-2 ** 31)
SORT_ROWS = LANES // SUBLANES


def _oddeven_merge_sort(n):
    pairs = []
    p = 1
    while p < n:
        k = p
        while k >= 1:
            for j in range(k % p, n - k, 2 * k):
                for i in range(min(k, n - j - k)):
                    if (i + j) // (2 * p) == (i + j + k) // (2 * p):
                        pairs.append((i + j, i + j + k))
            k //= 2
        p *= 2
    return tuple(pairs)


_SORT16 = _oddeven_merge_sort(SORT_ROWS)

_NT = (((1,), (1,)), ((), ()))


def _params(*sem):
    return pltpu.CompilerParams(dimension_semantics=sem, vmem_limit_bytes=VMEM_LIMIT)


def _rms(x, g):
    return x * lax.rsqrt(jnp.mean(x * x, axis=-1, keepdims=True) + EPS) * g


def _const_spec(shape):
    nd = len(shape)
    return pl.BlockSpec(shape, lambda *_: (0,) * nd)


def _fold8(x, op):
    r, c = x.shape
    return op(x.reshape(r // SUBLANES, SUBLANES, c), axis=0)


def _store_key_major(o_ref, rows, tile):
    for u in range(rows.shape[0] // tile):
        o_ref[u] = rows[u * tile:(u + 1) * tile, :].T.astype(o_ref.dtype)


def _norm_matmul_kernel(x_ref, g_ref, *refs, widths, key_tiles, col_chunk):
    w_refs, out_refs = refs[:len(widths)], refs[len(widths):]
    h = _rms(x_ref[...], g_ref[...]).astype(BF16)
    k = 0
    for w_ref, w_widths in zip(w_refs, widths):
        c0 = 0
        for w in w_widths:
            o_ref = out_refs[k]
            if key_tiles[k]:
                _store_key_major(o_ref, jnp.dot(h, w_ref[:, c0:c0 + w], preferred_element_type=F32),
                                 key_tiles[k])
            else:
                for s in range(0, w, col_chunk):
                    e = min(s + col_chunk, w)
                    o_ref[:, s:e] = jnp.dot(h, w_ref[:, c0 + s:c0 + e],
                                            preferred_element_type=F32).astype(o_ref.dtype)
            c0 += w
            k += 1


def _norm_matmul(x, g, ws, widths, dtypes, key_tiles, tm):
    n, k = x.shape
    flat = [wd for w_widths in widths for wd in w_widths]
    assert n % tm == 0 and all(w.shape == (k, sum(ww)) for w, ww in zip(ws, widths))
    kern = functools.partial(_norm_matmul_kernel, widths=tuple(tuple(ww) for ww in widths),
                             key_tiles=tuple(key_tiles), col_chunk=512)
    specs, shapes = [], []
    for wd, dt, kt in zip(flat, dtypes, key_tiles):
        if kt:
            specs.append(pl.BlockSpec((tm // kt, wd, kt), lambda i: (i, 0, 0)))
            shapes.append(jax.ShapeDtypeStruct((n // kt, wd, kt), dt))
        else:
            specs.append(pl.BlockSpec((tm, wd), lambda i: (i, 0)))
            shapes.append(jax.ShapeDtypeStruct((n, wd), dt))
    return pl.pallas_call(
        kern,
        grid=(n // tm,),
        in_specs=[pl.BlockSpec((tm, k), lambda i: (i, 0)), _const_spec((1, k))]
                 + [_const_spec(w.shape) for w in ws],
        out_specs=specs,
        out_shape=shapes,
        compiler_params=_params("parallel"),
        name="norm_proj",
    )(x, g.reshape(1, k), *ws)


def _dsa_kernel(qa_ref, iq_ref, iw_ref, ka_ref, vt_ref, ik_ref, bias_ref, o_ref,
                keys_ref, sorted_ref, s_ref, qm_ref, qs_ref, wt_ref, thr_ref, ties_ref,
                *, topk, n_blocks, group, scale, idx_scale):
    t = LANES
    gt = group * t
    i = pl.program_id(1)
    n_groups = (i + group) // group
    max_groups = n_blocks // group
    krow = lax.broadcasted_iota(I32, (t, t), 0)
    qcol = lax.broadcasted_iota(I32, (t, t), 1)
    lane_lo = qcol < IDX_DIM

    for h in range(IDX_HEADS):
        pair = iq_ref[0, :, (h // 2) * t:(h // 2 + 1) * t]
        keep = lane_lo if h % 2 == 0 else jnp.logical_not(lane_lo)
        qm_ref[h * t:(h + 1) * t, :] = jnp.where(keep, pair, jnp.zeros_like(pair))
    for h in range(A_HEADS):
        qs_ref[h * t:(h + 1) * t, :] = qa_ref[0, :, h * t:(h + 1) * t]
    wt_ref[...] = iw_ref[0].T

    def index_logits(g):
        return lax.dot_general(ik_ref[0, g * gt:(g + 1) * gt, :], qm_ref[...], _NT,
                               preferred_element_type=F32)

    def score_group(g, lg, ng):
        for u in range(group):
            j = g * group + u
            acc = jnp.zeros((t, t), F32)
            for h in range(IDX_HEADS):
                acc = acc + jnp.maximum(lg[u * t:(u + 1) * t, h * t:(h + 1) * t], 0.0) * wt_ref[h:h + 1, :]
            score = acc * idx_scale
            if j >= (ng - 1) * group:
                score = jnp.where(krow + j * t <= qcol + i * t, score, NEG_BIG)
            score = jnp.where(score == 0.0, 0.0, score)
            bits = pltpu.bitcast(score, I32)
            key = bits ^ ((bits >> 31) & np.int32(0x7FFFFFFF))
            keys_ref[j] = key
            rows = [key[r * SUBLANES:(r + 1) * SUBLANES, :] for r in range(SORT_ROWS)]
            for a, b in _SORT16:
                rows[a], rows[b] = jnp.maximum(rows[a], rows[b]), jnp.minimum(rows[a], rows[b])
            for r in range(SORT_ROWS):
                sorted_ref[j, r] = rows[r]

    def score_all(ng):
        lg = index_logits(0)
        for g in range(ng):
            nxt = index_logits(g + 1) if g + 1 < ng else None
            score_group(g, lg, ng)
            lg = nxt

    for v in range(1, max_groups + 1):
        pl.when(n_groups == v)(functools.partial(score_all, v))

    thr_ref[...] = jnp.full(thr_ref.shape, INT_MIN, I32)
    ties_ref[...] = jnp.zeros(ties_ref.shape, F32)

    def count(n, pred):
        acc = [jnp.zeros((SUBLANES, t), F32) for _ in range(5)]
        for j in range(n):
            v = [sorted_ref[j, r] for r in range(SORT_ROWS)]
            m1 = pred(v[7])
            m2 = pred(jnp.where(m1, v[11], v[3]))
            m3 = pred(jnp.where(m1, jnp.where(m2, v[13], v[9]), jnp.where(m2, v[5], v[1])))
            m4 = pred(jnp.where(m1, jnp.where(m2, jnp.where(m3, v[14], v[12]), jnp.where(m3, v[10], v[8])),
                                jnp.where(m2, jnp.where(m3, v[6], v[4]), jnp.where(m3, v[2], v[0]))))
            m5 = pred(v[15])
            acc = [a + jnp.where(m, 1.0, 0.0) for a, m in zip(acc, (m1, m2, m3, m4, m5))]
        cnt = 8.0 * acc[0] + 4.0 * acc[1] + 2.0 * acc[2] + acc[3] + acc[4]
        return jnp.sum(cnt, axis=0, keepdims=True)

    def search(n):
        def search_pass(b, thr):
            cand = thr + lax.shift_left(np.int32(1), 31 - b)
            return jnp.where(count(n, lambda key: key >= cand) >= topk, cand, thr)

        thr = lax.fori_loop(0, 32, search_pass, jnp.full((1, t), INT_MIN, I32))
        thr_ref[...] = jnp.broadcast_to(thr, thr_ref.shape)
        ties_ref[...] = jnp.broadcast_to(topk - count(n, lambda key: key > thr), ties_ref.shape)

    for c in range(n_blocks):
        if (c + 1) * t > topk:
            pl.when(i == c)(functools.partial(search, c + 1))

    thr = thr_ref[0:1, :]
    n_ties = ties_ref[0:1, :]

    r2 = lax.broadcasted_iota(I32, (2 * t, t), 0)
    c2 = lax.broadcasted_iota(I32, (2 * t, t), 1)
    tie_lhs = jnp.where(jnp.logical_or(r2 >= t, c2 < r2), 1.0, 0.0).astype(BF16)

    def sweep_a_mxu(g):
        keys = [keys_ref[g * group + u] for u in range(group)]
        eqs = [key == thr for key in keys]
        eq_all = jnp.concatenate([jnp.where(eq, 1.0, 0.0).astype(BF16) for eq in eqs], axis=1)
        pref = jnp.dot(tie_lhs, eq_all, preferred_element_type=F32)
        s_grp = lax.dot_general(ka_ref[0, g * gt:(g + 1) * gt, :], qs_ref[...], _NT,
                                preferred_element_type=F32)
        return keys, eqs, pref, s_grp

    def sweep_a_vpu(g, operands, seen, mx, ng):
        keys, eqs, pref, s_grp = operands
        for u in range(group):
            j = g * group + u
            us = slice(u * t, (u + 1) * t)
            rank = jnp.where(keys[u] > thr, -1.0, jnp.where(eqs[u], seen + pref[:t, us], 3e38))
            if j >= (ng - 1) * group:
                rank = jnp.where(krow + j * t <= qcol + i * t, rank, 3e38)
            valid = rank < n_ties
            which = jnp.where(j == i, 1, jnp.where(j == i - 1, 0, 2))
            s_all = s_grp[us, :] * (scale * math.log2(math.e))
            if j > (ng - 1) * group - 2:
                s_all = s_all + bias_ref[which]
            for h in range(A_HEADS):
                s_h = jnp.where(valid, s_all[:, h * t:(h + 1) * t], NEG_BIG)
                s_ref[j, :, h * t:(h + 1) * t] = s_h
                mx[h] = jnp.maximum(mx[h], _fold8(s_h, jnp.max))
            seen = seen + pref[t:t + 1, us]
        return seen, mx

    def sweep_b(g, m_all, l8):
        ps = []
        for u in range(group):
            p = jnp.exp2(s_ref[g * group + u] - m_all)
            l8 = l8 + _fold8(p, jnp.sum)
            ps.append(p.astype(BF16))
        vt_grp = jnp.concatenate([vt_ref[0, g * group + u] for u in range(group)], axis=1)
        return l8, jnp.dot(vt_grp, jnp.concatenate(ps, axis=0), preferred_element_type=F32)

    def attend(ng):
        seen = jnp.zeros((1, t), F32)
        mx = [jnp.full((SUBLANES, t), NEG_BIG, F32) for _ in range(A_HEADS)]
        operands = sweep_a_mxu(0)
        for g in range(ng):
            nxt = sweep_a_mxu(g + 1) if g + 1 < ng else None
            seen, mx = sweep_a_vpu(g, operands, seen, mx, ng)
            operands = nxt
        m_all = jnp.concatenate([jnp.max(m, axis=0, keepdims=True) for m in mx], axis=1)
        l8 = jnp.zeros((SUBLANES, A_HEADS * t), F32)
        acc = None
        for g in range(ng):
            l8, pv = sweep_b(g, m_all, l8)
            acc = pv if acc is None else acc + pv
        out = acc * (1.0 / jnp.sum(l8, axis=0, keepdims=True))
        for h in range(A_HEADS):
            o_ref[0, :, h * t:(h + 1) * t] = out[:, h * t:(h + 1) * t].T.astype(o_ref.dtype)

    for v in range(1, max_groups + 1):
        pl.when(n_groups == v)(functools.partial(attend, v))


def _dsa(qa, iq, iw, ka, vt, ik2, bias_tab):
    b, s, _ = qa.shape
    t = LANES
    nb = s // t
    group = math.gcd(nb, 2)
    topk = min(TOPK_MAX, s // 4)
    kern = functools.partial(_dsa_kernel, topk=float(topk), n_blocks=nb, group=group,
                             scale=A_HEAD_DIM ** -0.5,
                             idx_scale=(IDX_DIM ** -0.5) * (IDX_HEADS ** -0.5))
    qspec = lambda w: pl.BlockSpec((1, t, w), lambda bi, i: (bi, i, 0))
    kspec = pl.BlockSpec((1, s, t), lambda bi, i: (bi, 0, 0))
    return pl.pallas_call(
        kern,
        grid=(b, nb),
        in_specs=[qspec(A_HEADS * t), qspec(IDX_HEADS * IDX_DIM), qspec(t), kspec,
                  pl.BlockSpec((1, nb, t, t), lambda bi, i: (bi, 0, 0, 0)), kspec,
                  _const_spec(bias_tab.shape)],
        out_specs=qspec(A_HEADS * t),
        out_shape=jax.ShapeDtypeStruct((b, s, A_HEADS * t), BF16),
        scratch_shapes=[pltpu.VMEM((nb, t, t), I32),
                        pltpu.VMEM((nb, SORT_ROWS, SUBLANES, t), I32),
                        pltpu.VMEM((nb, t, A_HEADS * t), F32),
                        pltpu.VMEM((IDX_HEADS * t, t), BF16),
                        pltpu.VMEM((A_HEADS * t, t), BF16),
                        pltpu.VMEM((t, t), F32),
                        pltpu.VMEM((SUBLANES, t), I32),
                        pltpu.VMEM((SUBLANES, t), F32)],
        compiler_params=_params("parallel", "parallel"),
        name="dsa_attention",
    )(qa, iq, iw, ka, vt, ik2, bias_tab)


def _hgrn_kernel(q_ref, f_ref, i_ref, g_ref, lb_ref, gain_ref, o_ref, state_ref, *, n_chunks, n_seq):
    c = B_CHUNK
    rc = n_seq * c
    kd, vd = B_KEY_DIM, B_VAL_DIM
    w = B_HEADS * kd

    @pl.when(pl.program_id(1) == 0)
    def _():
        state_ref[...] = jnp.zeros_like(state_ref)

    row = lax.broadcasted_iota(I32, (rc, rc), 0)
    col = lax.broadcasted_iota(I32, (rc, rc), 1)
    same_seq = (row >= col) if n_seq == 1 else jnp.logical_and(
        row >= col, lax.shift_right_logical(row, 6) == lax.shift_right_logical(col, 6))
    causal = same_seq
    assert c == 64
    tril = jnp.where(causal, 1.0, 0.0).astype(BF16)
    seqs = [slice(b * c, (b + 1) * c) for b in range(n_seq)]
    seq_of_row = lax.shift_right_logical(lax.broadcasted_iota(I32, (rc, w), 0), 6)

    def per_seq_row(x, r):
        return jnp.concatenate([jnp.broadcast_to(x[b * c + r:b * c + r + 1, :], (c, x.shape[1]))
                                for b in range(n_seq)], axis=0)

    def stacked(ref, rows):
        return jnp.concatenate([ref[b, rows, :] for b in range(n_seq)], axis=0)
    q_scale = kd ** -0.5
    lb = lb_ref[...]
    lb_floor = jnp.maximum(lb, LB_FLOOR)
    one_m_lb = 1.0 - lb
    heads = [slice(h * kd, (h + 1) * kd) for h in range(B_HEADS)]

    def stage_decay(ci):
        rows = slice(ci * c, (ci + 1) * c)
        fr = stacked(f_ref, rows)
        z = jnp.exp(-jnp.abs(fr))
        r = 1.0 / (1.0 + z)
        sig_pos = jnp.where(fr >= 0, r, z * r)
        sig_neg = jnp.where(fr >= 0, z * r, r)
        log_f = jnp.log(lb_floor + one_m_lb * sig_pos)
        k_in = one_m_lb * sig_neg
        hi = log_f.astype(BF16)
        rest = log_f - hi.astype(F32)
        mid = rest.astype(BF16)
        lo = (rest - mid.astype(F32)).astype(BF16)
        cs = jnp.dot(tril, jnp.concatenate([hi, mid, lo], axis=1), preferred_element_type=F32)
        return k_in, cs

    def apply_update(pending):
        if pending is not None:
            d_last, st, upd = pending
            for b in range(n_seq):
                for h, hs in enumerate(heads):
                    state_ref[b, h] = d_last[b * c:b * c + 1, hs] * st[b][h] + upd[b][h]

    def stage_scores(ci, decay, pending):
        rows = slice(ci * c, (ci + 1) * c)
        k_in, cs = decay
        bsum = cs[:, :w] + cs[:, w:2 * w] + cs[:, 2 * w:]
        b_mid = per_seq_row(bsum, c // 2 - 1)
        b_last = per_seq_row(bsum, c - 1)
        qs = stacked(q_ref, rows) * q_scale
        v = stacked(i_ref, rows)
        v_b = v.astype(BF16)
        v_t = v.T.astype(BF16)
        q_mid = (qs * jnp.exp(bsum - b_mid)).astype(BF16)
        k_mid = (k_in * jnp.exp(b_mid - bsum)).astype(BF16)
        q_dec = (qs * jnp.exp(bsum)).astype(BF16)
        k_end = (k_in * jnp.exp(b_last - bsum)).astype(BF16)
        d_last = jnp.exp(b_last)
        apply_update(pending)
        st = [[state_ref[b, h] for h in range(B_HEADS)] for b in range(n_seq)]
        attn = [lax.dot_general(q_mid[:, hs], k_mid[:, hs], _NT, preferred_element_type=F32)
                for hs in heads]
        inter = [jnp.concatenate(
            [lax.dot_general(q_dec[sq, hs], st[b][h].astype(BF16), _NT, preferred_element_type=F32)
             for b, sq in enumerate(seqs)], axis=0) for h, hs in enumerate(heads)]
        k_seq = [k_end if n_seq == 1 else jnp.where(seq_of_row == b, k_end, jnp.zeros_like(k_end))
                 for b in range(n_seq)]
        upd = [[jnp.dot(v_t[hs, :], k_seq[b][:, hs], preferred_element_type=F32) for hs in heads]
               for b in range(n_seq)]
        return (attn, inter, v_b), (d_last, st, upd)

    def stage_output(ci, scores):
        rows = slice(ci * c, (ci + 1) * c)
        attn, inter, v_b = scores
        attn = [jnp.where(causal, a, 0.0).astype(BF16) for a in attn]
        outs = [jnp.dot(attn[h], v_b[:, hs], preferred_element_type=F32) + inter[h]
                for h, hs in enumerate(heads)]
        o = jnp.concatenate([_rms(o_h, gain_ref[...]) for o_h in outs], axis=1)
        g = stacked(g_ref, rows)
        o = (o * (g * (1.0 / (1.0 + jnp.exp(-g))))).astype(o_ref.dtype)
        for b, sq in enumerate(seqs):
            o_ref[b, rows, :] = o[sq, :]

    decay, scores, pending = {}, {}, None
    for step in range(n_chunks + 2):
        if step < n_chunks:
            decay[step] = stage_decay(step)
        if 0 <= step - 1 < n_chunks:
            scores[step - 1], pending = stage_scores(step - 1, decay.pop(step - 1), pending)
        if 0 <= step - 2 < n_chunks:
            stage_output(step - 2, scores.pop(step - 2))
    apply_update(pending)


def _hgrn(bq, bf, bi, bg, lb, gain, sb):
    b, s, w = bq.shape
    assert s % sb == 0 and sb % B_CHUNK == 0
    n_seq = 2 if b % 2 == 0 else 1
    kern = functools.partial(_hgrn_kernel, n_chunks=sb // B_CHUNK, n_seq=n_seq)
    spec = pl.BlockSpec((n_seq, sb, w), lambda bi_, si: (bi_, si, 0))
    return pl.pallas_call(
        kern,
        grid=(b // n_seq, s // sb),
        in_specs=[spec, spec, spec, spec, _const_spec((1, w)), _const_spec((1, B_VAL_DIM))],
        out_specs=spec,
        out_shape=jax.ShapeDtypeStruct((b, s, w), BF16),
        scratch_shapes=[pltpu.VMEM((n_seq, B_HEADS, B_VAL_DIM, B_KEY_DIM), F32)],
        compiler_params=_params("parallel", "arbitrary"),
        name="hgrn2",
    )(bq, bf, bi, bg, lb.reshape(1, w), gain.reshape(1, B_VAL_DIM))


def _mla_prep_kernel(cq_ref, ckv_ref, kpe_raw_ref, cos_ref, sin_ref, gq_ref, gkv_ref, wq_ref, wkv_ref,
                     qn_ref, qpe_ref, kn_ref, vt_ref, kpe_ref, *, key_tile):
    nw = C_HEADS * C_NOPE
    pw = C_HEADS * C_ROPE
    cos, sin = cos_ref[...], sin_ref[...]
    cos2 = jnp.concatenate([cos] * (pw // LANES), axis=1)
    sin2 = jnp.concatenate([sin] * (pw // LANES), axis=1)
    q = jnp.dot(_rms(cq_ref[...], gq_ref[...]).astype(BF16), wq_ref[...], preferred_element_type=F32)
    qn_ref[...] = q[:, :nw].astype(qn_ref.dtype)
    qpe_ref[...] = (q[:, nw:nw + pw] * cos2 + q[:, nw + pw:] * sin2).astype(qpe_ref.dtype)
    kv = jnp.dot(_rms(ckv_ref[...], gkv_ref[...]).astype(BF16), wkv_ref[...], preferred_element_type=F32)
    kn_ref[...] = kv[:, :nw].astype(kn_ref.dtype)
    _store_key_major(vt_ref, kv[:, nw:], key_tile)
    kpe_ref[...] = (kpe_raw_ref[:, :LANES] * cos + kpe_raw_ref[:, LANES:] * sin).astype(kpe_ref.dtype)


def _mla_prep(cq, ckv, kpe_raw, cos_t, sin_t, gq, gkv, wq, wkv, tm, key_tile):
    n = cq.shape[0]
    nw = C_HEADS * C_NOPE
    vw = C_HEADS * C_V
    row = lambda w: pl.BlockSpec((tm, w), lambda i: (i, 0))
    widths = (nw, C_HEADS * C_ROPE, nw, None, LANES)
    vt_spec = pl.BlockSpec((tm // key_tile, vw, key_tile), lambda i: (i, 0, 0))
    vt_shape = jax.ShapeDtypeStruct((n // key_tile, vw, key_tile), BF16)
    return pl.pallas_call(
        functools.partial(_mla_prep_kernel, key_tile=key_tile),
        grid=(n // tm,),
        in_specs=[row(C_Q_RANK), row(C_KV_RANK), row(2 * LANES), row(LANES), row(LANES),
                  _const_spec((1, C_Q_RANK)), _const_spec((1, C_KV_RANK)),
                  _const_spec(wq.shape), _const_spec(wkv.shape)],
        out_specs=[vt_spec if w is None else row(w) for w in widths],
        out_shape=[vt_shape if w is None else jax.ShapeDtypeStruct((n, w), BF16) for w in widths],
        compiler_params=_params("parallel"),
        name="mla_prep",
    )(cq, ckv, kpe_raw, cos_t, sin_t, gq.reshape(1, -1), gkv.reshape(1, -1), wq, wkv)


def _mla_kernel(qn_ref, qpe_ref, kn_ref, kpe_ref, vt_ref, o_ref, q_ref, acc_ref, *, t, n_tiles, scale):
    i = pl.program_id(1)
    lane_lo = lax.broadcasted_iota(I32, (t, LANES), 1) < C_ROPE
    for h in range(C_HEADS):
        pair = qpe_ref[0, :, (h // 2) * LANES:(h // 2 + 1) * LANES]
        keep = lane_lo if h % 2 == 0 else jnp.logical_not(lane_lo)
        q_ref[h, :, :C_NOPE] = qn_ref[0, :, h * C_NOPE:(h + 1) * C_NOPE]
        q_ref[h, :, C_NOPE:] = jnp.where(keep, pair, jnp.zeros_like(pair))
    acc_ref[...] = jnp.zeros_like(acc_ref)

    def logits_of(step):
        first, count = step
        rows = slice(first * t, (first + count) * t)
        kpe_t = kpe_ref[0, rows, :]
        return [lax.dot_general(
            jnp.concatenate([kn_ref[0, rows, h * C_NOPE:(h + 1) * C_NOPE], kpe_t], axis=1),
            q_ref[h], _NT, preferred_element_type=F32) for h in range(C_HEADS)]

    def softmax_pv(step, block, logits, ms, ls, masked):
        first, count = step
        new_m, new_l = [], []
        for h in range(C_HEADS):
            s = logits[h] * (scale * math.log2(math.e))
            if masked:
                key_pos = lax.broadcasted_iota(I32, (count * t, t), 0) + first * t
                query_pos = lax.broadcasted_iota(I32, (count * t, t), 1) + block * t
                s = jnp.where(key_pos <= query_pos, s, NEG_BIG)
            m_new = jnp.maximum(ms[h], jnp.max(s, axis=0, keepdims=True))
            alpha = jnp.exp2(ms[h] - m_new)
            p = jnp.exp2(s - m_new)
            new_l.append(alpha * ls[h] + jnp.sum(p, axis=0, keepdims=True))
            new_m.append(m_new)
            hs = slice(h * C_V, (h + 1) * C_V)
            tiles = [vt_ref[0, first + u, hs, :] for u in range(count)]
            v_t = tiles[0] if count == 1 else jnp.concatenate(tiles, axis=1)
            acc_ref[h] = alpha * acc_ref[h] + jnp.dot(v_t, p.astype(BF16), preferred_element_type=F32)
        return new_m, new_l

    def run(block):
        steps = [(2 * g, 2) for g in range(block // 2)] + [(block - block % 2, 1 + block % 2)]
        ms = [jnp.full((1, t), NEG_BIG, F32) for _ in range(C_HEADS)]
        ls = [jnp.zeros((1, t), F32) for _ in range(C_HEADS)]
        logits = logits_of(steps[0])
        for k, step in enumerate(steps):
            nxt = logits_of(steps[k + 1]) if k + 1 < len(steps) else None
            ms, ls = softmax_pv(step, block, logits, ms, ls, masked=(k + 1 == len(steps)))
            logits = nxt
        for h in range(C_HEADS):
            o_ref[0, :, h * C_V:(h + 1) * C_V] = (acc_ref[h] * (1.0 / ls[h])).T.astype(o_ref.dtype)

    for v in range(n_tiles):
        pl.when(i == v)(functools.partial(run, v))


def _mla(qn, qpe, kn, kpe, vt, t):
    b, s, _ = qn.shape
    nt = s // t
    assert nt % 2 == 0
    kern = functools.partial(_mla_kernel, t=t, n_tiles=nt, scale=(C_NOPE + C_ROPE) ** -0.5)
    qspec = lambda w: pl.BlockSpec((1, t, w), lambda bi, i: (bi, i, 0))
    kspec = lambda w: pl.BlockSpec((1, s, w), lambda bi, i: (bi, 0, 0))
    return pl.pallas_call(
        kern,
        grid=(b, nt),
        in_specs=[qspec(qn.shape[2]), qspec(qpe.shape[2]), kspec(kn.shape[2]), kspec(kpe.shape[2]),
                  pl.BlockSpec((1, nt, C_HEADS * C_V, t), lambda bi, i: (bi, 0, 0, 0))],
        out_specs=qspec(C_HEADS * C_V),
        out_shape=jax.ShapeDtypeStruct((b, s, C_HEADS * C_V), BF16),
        scratch_shapes=[pltpu.VMEM((C_HEADS, t, C_NOPE + LANES), BF16),
                        pltpu.VMEM((C_HEADS, C_V, t), F32)],
        compiler_params=_params("parallel", "parallel"),
        name="mla_attention",
    )(qn, qpe, kn, kpe, vt)


def _merge_kernel(x_ref, g_ref, wg_ref, oa_ref, ob_ref, oc_ref, wa_ref, wb_ref, wc_ref, wo_ref,
                  out_ref, mixed_ref, *, col_chunk):
    x = x_ref[...]
    d = x.shape[1]
    h = _rms(x, g_ref[...]).astype(BF16)
    branches = ((oa_ref, wa_ref), (ob_ref, wb_ref), (oc_ref, wc_ref))
    for s in range(0, d, col_chunk):
        cs = slice(s, s + col_chunk)
        mixed = None
        for bidx, (o_ref, w_ref) in enumerate(branches):
            logits = jnp.dot(h, wg_ref[:, bidx * d + s:bidx * d + s + col_chunk],
                             preferred_element_type=F32)
            gate = 1.0 / (1.0 + jnp.exp(-logits))
            term = gate * jnp.dot(o_ref[...], w_ref[:, cs], preferred_element_type=F32)
            mixed = term if mixed is None else mixed + term
        mixed_ref[:, cs] = mixed.astype(BF16)
    out_ref[...] = x + jnp.dot(mixed_ref[...], wo_ref[...], preferred_element_type=F32)


def _merge(x, g, w_gate, oa, ob, oc, wa, wb, wc, wo, tm):
    n, d = x.shape
    row = lambda w: pl.BlockSpec((tm, w), lambda i: (i, 0))
    kern = functools.partial(_merge_kernel, col_chunk=256)
    return pl.pallas_call(
        kern,
        grid=(n // tm,),
        in_specs=[row(d), _const_spec((1, d)), _const_spec(w_gate.shape),
                  row(oa.shape[1]), row(ob.shape[1]), row(oc.shape[1]),
                  _const_spec(wa.shape), _const_spec(wb.shape), _const_spec(wc.shape),
                  _const_spec(wo.shape)],
        out_specs=row(d),
        out_shape=jax.ShapeDtypeStruct((n, d), F32),
        scratch_shapes=[pltpu.VMEM((tm, d), BF16)],
        compiler_params=_params("parallel"),
        name="merge_out_proj",
    )(x, g.reshape(1, d), w_gate, oa, ob, oc, wa, wb, wc, wo)


def _ffn_kernel(x_ref, g_ref, wg_ref, wu_ref, wd_ref, gf_ref, out_ref, act_ref, *, col_chunk,
                final_norm):
    x = x_ref[...]
    h = _rms(x, g_ref[...]).astype(BF16)
    dff = wg_ref.shape[1]
    for s in range(0, dff, col_chunk):
        cs = slice(s, s + col_chunk)
        gate = jnp.dot(h, wg_ref[:, cs], preferred_element_type=F32)
        up = jnp.dot(h, wu_ref[:, cs], preferred_element_type=F32)
        act_ref[:, cs] = (gate * (1.0 / (1.0 + jnp.exp(-gate))) * up).astype(BF16)
    y = x + jnp.dot(act_ref[...], wd_ref[...], preferred_element_type=F32)
    if final_norm:
        y = _rms(y, gf_ref[...])
    out_ref[...] = y


def _ffn(x, g, wg, wu, wd, gf, tm, final_norm):
    n, d = x.shape
    dff = wg.shape[1]
    row = pl.BlockSpec((tm, d), lambda i: (i, 0))
    kern = functools.partial(_ffn_kernel, col_chunk=256, final_norm=final_norm)
    return pl.pallas_call(
        kern,
        grid=(n // tm,),
        in_specs=[row, _const_spec((1, d)), _const_spec(wg.shape), _const_spec(wu.shape),
                  _const_spec(wd.shape), _const_spec((1, d))],
        out_specs=row,
        out_shape=jax.ShapeDtypeStruct((n, d), F32),
        scratch_shapes=[pltpu.VMEM((tm, dff), BF16)],
        compiler_params=_params("parallel"),
        name="swiglu_ffn",
    )(x, g.reshape(1, d), wg, wu, wd, gf.reshape(1, d))


def _t5_bucket(dist):
    max_exact = REL_BUCKETS // 2
    d = jnp.maximum(dist, 0)
    dl = jnp.maximum(d, max_exact).astype(F32)
    large = max_exact + (jnp.log(dl / max_exact) / math.log(REL_MAX_DIST / max_exact)
                         * (REL_BUCKETS - max_exact)).astype(I32)
    large = jnp.minimum(large, REL_BUCKETS - 1)
    return jnp.where(d < max_exact, d, large)


def _swap_halves(w):
    half = w.shape[-1] // 2
    return jnp.concatenate([w[..., half:], w[..., :half]], axis=-1)


def _in_proj_weights(w_in, d_model):
    splits = (A_HEADS * A_HEAD_DIM, A_HEAD_DIM, A_HEAD_DIM, IDX_HEADS * IDX_DIM, IDX_DIM, IDX_HEADS,
              B_HEADS * B_KEY_DIM, B_HEADS * B_KEY_DIM, B_HEADS * B_VAL_DIM, B_HEADS * B_VAL_DIM,
              C_Q_RANK, C_KV_RANK, C_ROPE, N_BRANCH * d_model)
    st = [int(v) for v in np.concatenate([[0], np.cumsum(splits)])]
    depth, k, n_cols = w_in.shape
    reps = LANES // C_ROPE
    rows = 128

    def regroup_kernel(w_ref, a_ref, b_ref, c_ref, d_ref, g_ref):
        col = lambda a, b: w_ref[:, st[a]:st[b]]
        ik, pe = col(4, 5), col(12, 13)
        iw_pad = jnp.zeros((rows, LANES - IDX_HEADS), F32)
        a_ref[...] = col(0, 4).astype(BF16)
        b_ref[...] = jnp.concatenate([ik, ik, col(5, 6), iw_pad], axis=1).astype(BF16)
        c_ref[...] = col(6, 12).astype(BF16)
        d_ref[...] = jnp.concatenate([pe] * reps + [_swap_halves(pe)] * reps, axis=1).astype(BF16)
        g_ref[...] = col(13, 14).astype(BF16)

    out_w = (st[4], 2 * LANES, st[12] - st[6], 2 * LANES, st[14] - st[13])
    outs = pl.pallas_call(
        regroup_kernel,
        grid=(depth * k // rows,),
        in_specs=[pl.BlockSpec((rows, n_cols), lambda i: (i, 0))],
        out_specs=[pl.BlockSpec((rows, wd), lambda i: (i, 0)) for wd in out_w],
        out_shape=[jax.ShapeDtypeStruct((depth * k, wd), BF16) for wd in out_w],
        compiler_params=_params("parallel"),
        name="regroup_w_in",
    )(w_in.reshape(depth * k, n_cols))
    ws = [o.reshape(depth, k, -1) for o in outs]
    widths = [splits[:4], (LANES, LANES), splits[6:12], (2 * LANES,)]
    dtypes = [BF16] * 5 + [F32] * 8
    return ws[:4], widths, dtypes, ws[4]


def kernel(x, positions, w_in, w_up_a, w_up_b, w_up_c, w_out, mla_q_norm, mla_w_qb, mla_kv_norm,
           mla_w_kvb, hgrn_lb_logits, hgrn_out_norm, rel_bias, attn_norm, ffn_norm, w_ffn_gate,
           w_ffn_up, w_ffn_down, final_norm):
    bsz, s_len, d_model = x.shape
    depth = w_in.shape[0]
    n = bsz * s_len
    t = LANES

    w_proj, widths, dtypes, w_gate = _in_proj_weights(w_in, d_model)
    wq = mla_w_qb.reshape(depth, C_Q_RANK, C_HEADS, C_NOPE + C_ROPE)
    q_pe = wq[..., C_NOPE:]
    w_qb = jnp.concatenate([wq[..., :C_NOPE].reshape(depth, C_Q_RANK, -1),
                            q_pe.reshape(depth, C_Q_RANK, -1),
                            _swap_halves(q_pe).reshape(depth, C_Q_RANK, -1)], axis=2).astype(BF16)
    wkv = mla_w_kvb.reshape(depth, C_KV_RANK, C_HEADS, C_NOPE + C_V)
    w_kvb = jnp.concatenate([wkv[..., :C_NOPE].reshape(depth, C_KV_RANK, -1),
                             wkv[..., C_NOPE:].reshape(depth, C_KV_RANK, -1)], axis=2).astype(BF16)
    bf = lambda w: w.astype(BF16)
    w_up_a, w_up_b, w_up_c, w_out = bf(w_up_a), bf(w_up_b), bf(w_up_c), bf(w_out)
    w_ffn_gate, w_ffn_up, w_ffn_down = bf(w_ffn_gate), bf(w_ffn_up), bf(w_ffn_down)

    p_lb = jax.nn.softmax(hgrn_lb_logits.astype(F32), axis=0)
    lower_bounds = jnp.cumsum(p_lb, axis=0) - p_lb[0:1]
    inv_freq = ROPE_THETA ** (-jnp.arange(0, C_ROPE, 2, dtype=F32) / C_ROPE)
    ang = positions.astype(F32)[..., None] * inv_freq
    cos, sin = jnp.cos(ang), jnp.sin(ang)
    reps = LANES // C_ROPE
    cos_t = jnp.tile(jnp.concatenate([cos, cos], axis=-1), (1, 1, reps)).reshape(n, LANES)
    sin_t = jnp.tile(jnp.concatenate([-sin, sin], axis=-1), (1, 1, reps)).reshape(n, LANES)
    q_idx = jnp.arange(t, dtype=I32)[None, :]
    k_idx = jnp.arange(t, dtype=I32)[:, None]
    def tab(dist):
        onehot = jax.nn.one_hot(_t5_bucket(dist), REL_BUCKETS, dtype=F32)
        return jnp.einsum("kqb,bh->khq", onehot, rel_bias.astype(F32),
                          precision=lax.Precision.HIGHEST).reshape(t, A_HEADS * t)
    far = jnp.broadcast_to(rel_bias[REL_BUCKETS - 1].astype(F32)[None, :, None],
                           (t, A_HEADS, t)).reshape(t, A_HEADS * t)
    bias_tab = (jnp.stack([tab(q_idx + t - k_idx), tab(q_idx - k_idx), far]) - far) * math.log2(math.e)

    tm = min(512, n)
    t_mla = min(256, s_len)
    x2 = x.reshape(n, d_model)
    r3 = lambda a: a.reshape(bsz, s_len, a.shape[-1])
    per_seq = lambda a: a.reshape(bsz, a.shape[0] // bsz, a.shape[1], a.shape[2])
    key_tiles = [0, 0, t] + [0] * 10
    for l in range(depth):
        (qa, ka, vta, iq, ik2, iw, bq, bfr, bi, bg, cq, ckv, kpe_raw) = _norm_matmul(
            x2, attn_norm[l], [w[l] for w in w_proj], widths, dtypes, key_tiles, tm)
        o_a = _dsa(r3(qa), r3(iq), r3(iw), r3(ka), per_seq(vta), r3(ik2), bias_tab)
        o_b = _hgrn(r3(bq), r3(bfr), r3(bi), r3(bg), lower_bounds[l], hgrn_out_norm[l],
                    sb=min(512, s_len))
        qn, qpe, kn, vtc, kpe = _mla_prep(cq, ckv, kpe_raw, cos_t, sin_t, mla_q_norm[l], mla_kv_norm[l],
                                          w_qb[l], w_kvb[l], tm, t_mla)
        o_c = _mla(r3(qn), r3(qpe), r3(kn), r3(kpe), per_seq(vtc), t_mla)
        x2 = _merge(x2, attn_norm[l], w_gate[l], o_a.reshape(n, -1), o_b.reshape(n, -1),
                    o_c.reshape(n, -1), w_up_a[l], w_up_b[l], w_up_c[l], w_out[l], tm)
        x2 = _ffn(x2, ffn_norm[l], w_ffn_gate[l], w_ffn_up[l], w_ffn_down[l], final_norm,
                  tm, final_norm=(l == depth - 1))
    return x2.reshape(bsz, s_len, d_model)
```

```python
import functools
import math

import jax
import jax.numpy as jnp
import numpy as np
from jax import lax
from jax.experimental import pallas as pl
from jax.experimental.pallas import tpu as pltpu

F32 = jnp.float32
BF16 = jnp.bfloat16
I32 = jnp.int32

A_HEADS = 4
A_HEAD_DIM = 128
IDX_HEADS = 8
IDX_DIM = 64
TOPK_MAX = 256
B_HEADS = 4
B_KEY_DIM = 128
B_VAL_DIM = 128
B_CHUNK = 64
C_HEADS = 4
C_Q_RANK = 384
C_KV_RANK = 256
C_NOPE = 128
C_ROPE = 64
C_V = 128
ROPE_THETA = 10000.0
REL_BUCKETS = 32
REL_MAX_DIST = 128
N_BRANCH = 3
EPS = 1e-6
NEG_BIG = -1e30
LB_FLOOR = 1e-30

LANES = 128
SUBLANES = 8
VMEM_LIMIT = 56 * 1024 * 1024
INT_MIN = np.int32(-2 ** 31)
SORT_ROWS = LANES // SUBLANES


def _oddeven_merge_sort(n):
    pairs = []
    p = 1
    while p < n:
        k = p
        while k >= 1:
            for j in range(k % p, n - k, 2 * k):
                for i in range(min(k, n - j - k)):
                    if (i + j) // (2 * p) == (i + j + k) // (2 * p):
                        pairs.append((i + j, i + j + k))
            k //= 2
        p *= 2
    return tuple(pairs)


_SORT16 = _oddeven_merge_sort(SORT_ROWS)

_NT = (((1,), (1,)), ((), ()))


def _params(*sem):
    return pltpu.CompilerParams(dimension_semantics=sem, vmem_limit_bytes=VMEM_LIMIT)


def _rms(x, g):
    return x * lax.rsqrt(jnp.mean(x * x, axis=-1, keepdims=True) + EPS) * g


def _const_spec(shape):
    nd = len(shape)
    return pl.BlockSpec(shape, lambda *_: (0,) * nd)


def _layer_spec(w, layer):
    return pl.BlockSpec((None,) + w.shape[1:], lambda *_: (layer, 0, 0))


def _fold8(x, op):
    r, c = x.shape
    return op(x.reshape(r // SUBLANES, SUBLANES, c), axis=0)


def _store_key_major(o_ref, rows, tile):
    for u in range(rows.shape[0] // tile):
        o_ref[u] = rows[u * tile:(u + 1) * tile, :].T.astype(o_ref.dtype)


def _norm_matmul_kernel(x_ref, g_ref, *refs, widths, key_tiles, col_chunk):
    w_refs, out_refs = refs[:len(widths)], refs[len(widths):]
    h = _rms(x_ref[...], g_ref[...]).astype(BF16)
    k = 0
    for w_ref, w_widths in zip(w_refs, widths):
        c0 = 0
        for w in w_widths:
            o_ref = out_refs[k]
            if key_tiles[k]:
                _store_key_major(o_ref, jnp.dot(h, w_ref[:, c0:c0 + w], preferred_element_type=F32),
                                 key_tiles[k])
            else:
                for s in range(0, w, col_chunk):
                    e = min(s + col_chunk, w)
                    o_ref[:, s:e] = jnp.dot(h, w_ref[:, c0 + s:c0 + e],
                                            preferred_element_type=F32).astype(o_ref.dtype)
            c0 += w
            k += 1


def _norm_matmul(x, g, ws, layer, widths, dtypes, key_tiles, tm):
    n, k = x.shape
    flat = [wd for w_widths in widths for wd in w_widths]
    assert n % tm == 0 and all(w.shape[1:] == (k, sum(ww)) for w, ww in zip(ws, widths))
    kern = functools.partial(_norm_matmul_kernel, widths=tuple(tuple(ww) for ww in widths),
                             key_tiles=tuple(key_tiles), col_chunk=512)
    specs, shapes = [], []
    for wd, dt, kt in zip(flat, dtypes, key_tiles):
        if kt:
            specs.append(pl.BlockSpec((tm // kt, wd, kt), lambda i: (i, 0, 0)))
            shapes.append(jax.ShapeDtypeStruct((n // kt, wd, kt), dt))
        else:
            specs.append(pl.BlockSpec((tm, wd), lambda i: (i, 0)))
            shapes.append(jax.ShapeDtypeStruct((n, wd), dt))
    return pl.pallas_call(
        kern,
        grid=(n // tm,),
        in_specs=[pl.BlockSpec((tm, k), lambda i: (i, 0)), _const_spec((1, k))]
                 + [_layer_spec(w, layer) for w in ws],
        out_specs=specs,
        out_shape=shapes,
        compiler_params=_params("parallel"),
        name="norm_proj",
    )(x, g.reshape(1, k), *ws)


def _dsa_kernel(qa_ref, iq_ref, iw_ref, ka_ref, vt_ref, ik_ref, bias_ref, o_ref,
                keys_ref, sorted_ref, s_ref, qm_ref, qs_ref, wt_ref, thr_ref, ties_ref,
                *, topk, n_blocks, group, scale, idx_scale):
    t = LANES
    gt = group * t
    i = pl.program_id(1)
    n_groups = (i + group) // group
    max_groups = n_blocks // group
    krow = lax.broadcasted_iota(I32, (t, t), 0)
    qcol = lax.broadcasted_iota(I32, (t, t), 1)
    lane_lo = qcol < IDX_DIM

    for h in range(IDX_HEADS):
        pair = iq_ref[0, :, (h // 2) * t:(h // 2 + 1) * t]
        keep = lane_lo if h % 2 == 0 else jnp.logical_not(lane_lo)
        qm_ref[h * t:(h + 1) * t, :] = jnp.where(keep, pair, jnp.zeros_like(pair))
    for h in range(A_HEADS):
        qs_ref[h * t:(h + 1) * t, :] = qa_ref[0, :, h * t:(h + 1) * t]
    wt_ref[...] = iw_ref[0].T

    def index_logits(g):
        return lax.dot_general(ik_ref[0, g * gt:(g + 1) * gt, :], qm_ref[...], _NT,
                               preferred_element_type=F32)

    def score_group(g, lg, ng):
        for u in range(group):
            j = g * group + u
            acc = jnp.zeros((t, t), F32)
            for h in range(IDX_HEADS):
                acc = acc + jnp.maximum(lg[u * t:(u + 1) * t, h * t:(h + 1) * t], 0.0) * wt_ref[h:h + 1, :]
            score = acc * idx_scale
            if j >= (ng - 1) * group:
                score = jnp.where(krow + j * t <= qcol + i * t, score, NEG_BIG)
            score = jnp.where(score == 0.0, 0.0, score)
            bits = pltpu.bitcast(score, I32)
            key = bits ^ ((bits >> 31) & np.int32(0x7FFFFFFF))
            keys_ref[j] = key
            rows = [key[r * SUBLANES:(r + 1) * SUBLANES, :] for r in range(SORT_ROWS)]
            for a, b in _SORT16:
                rows[a], rows[b] = jnp.maximum(rows[a], rows[b]), jnp.minimum(rows[a], rows[b])
            for r in range(SORT_ROWS):
                sorted_ref[j, r] = rows[r]

    def score_all(ng):
        lg = index_logits(0)
        for g in range(ng):
            nxt = index_logits(g + 1) if g + 1 < ng else None
            score_group(g, lg, ng)
            lg = nxt

    for v in range(1, max_groups + 1):
        pl.when(n_groups == v)(functools.partial(score_all, v))

    thr_ref[...] = jnp.full(thr_ref.shape, INT_MIN, I32)
    ties_ref[...] = jnp.zeros(ties_ref.shape, F32)

    def count(n, pred):
        acc = [jnp.zeros((SUBLANES, t), F32) for _ in range(5)]
        for j in range(n):
            v = [sorted_ref[j, r] for r in range(SORT_ROWS)]
            m1 = pred(v[7])
            m2 = pred(jnp.where(m1, v[11], v[3]))
            m3 = pred(jnp.where(m1, jnp.where(m2, v[13], v[9]), jnp.where(m2, v[5], v[1])))
            m4 = pred(jnp.where(m1, jnp.where(m2, jnp.where(m3, v[14], v[12]), jnp.where(m3, v[10], v[8])),
                                jnp.where(m2, jnp.where(m3, v[6], v[4]), jnp.where(m3, v[2], v[0]))))
            m5 = pred(v[15])
            acc = [a + jnp.where(m, 1.0, 0.0) for a, m in zip(acc, (m1, m2, m3, m4, m5))]
        cnt = 8.0 * acc[0] + 4.0 * acc[1] + 2.0 * acc[2] + acc[3] + acc[4]
        return jnp.sum(cnt, axis=0, keepdims=True)

    def search(n):
        def search_pass(b, thr):
            cand = thr + lax.shift_left(np.int32(1), 31 - b)
            return jnp.where(count(n, lambda key: key >= cand) >= topk, cand, thr)

        thr = lax.fori_loop(0, 32, search_pass, jnp.full((1, t), INT_MIN, I32))
        thr_ref[...] = jnp.broadcast_to(thr, thr_ref.shape)
        ties_ref[...] = jnp.broadcast_to(topk - count(n, lambda key: key > thr), ties_ref.shape)

    for c in range(n_blocks):
        if (c + 1) * t > topk:
            pl.when(i == c)(functools.partial(search, c + 1))

    thr = thr_ref[0:1, :]
    n_ties = ties_ref[0:1, :]

    r2 = lax.broadcasted_iota(I32, (2 * t, t), 0)
    c2 = lax.broadcasted_iota(I32, (2 * t, t), 1)
    tie_lhs = jnp.where(jnp.logical_or(r2 >= t, c2 < r2), 1.0, 0.0).astype(BF16)

    def sweep_a_mxu(g):
        keys = [keys_ref[g * group + u] for u in range(group)]
        eqs = [key == thr for key in keys]
        eq_all = jnp.concatenate([jnp.where(eq, 1.0, 0.0).astype(BF16) for eq in eqs], axis=1)
        pref = jnp.dot(tie_lhs, eq_all, preferred_element_type=F32)
        s_grp = lax.dot_general(ka_ref[0, g * gt:(g + 1) * gt, :], qs_ref[...], _NT,
                                preferred_element_type=F32)
        return keys, eqs, pref, s_grp

    def sweep_a_vpu(g, operands, seen, mx, ng):
        keys, eqs, pref, s_grp = operands
        for u in range(group):
            j = g * group + u
            us = slice(u * t, (u + 1) * t)
            rank = jnp.where(keys[u] > thr, -1.0, jnp.where(eqs[u], seen + pref[:t, us], 3e38))
            if j >= (ng - 1) * group:
                rank = jnp.where(krow + j * t <= qcol + i * t, rank, 3e38)
            valid = rank < n_ties
            which = jnp.where(j == i, 1, jnp.where(j == i - 1, 0, 2))
            s_all = s_grp[us, :] * (scale * math.log2(math.e))
            if j > (ng - 1) * group - 2:
                s_all = s_all + bias_ref[which]
            for h in range(A_HEADS):
                s_h = jnp.where(valid, s_all[:, h * t:(h + 1) * t], NEG_BIG)
                s_ref[j, :, h * t:(h + 1) * t] = s_h
                mx[h] = jnp.maximum(mx[h], _fold8(s_h, jnp.max))
            seen = seen + pref[t:t + 1, us]
        return seen, mx

    def sweep_b(g, m_all, l8):
        ps = []
        for u in range(group):
            p = jnp.exp2(s_ref[g * group + u] - m_all)
            l8 = l8 + _fold8(p, jnp.sum)
            ps.append(p.astype(BF16))
        vt_grp = jnp.concatenate([vt_ref[0, g * group + u] for u in range(group)], axis=1)
        return l8, jnp.dot(vt_grp, jnp.concatenate(ps, axis=0), preferred_element_type=F32)

    def attend(ng):
        seen = jnp.zeros((1, t), F32)
        mx = [jnp.full((SUBLANES, t), NEG_BIG, F32) for _ in range(A_HEADS)]
        operands = sweep_a_mxu(0)
        for g in range(ng):
            nxt = sweep_a_mxu(g + 1) if g + 1 < ng else None
            seen, mx = sweep_a_vpu(g, operands, seen, mx, ng)
            operands = nxt
        m_all = jnp.concatenate([jnp.max(m, axis=0, keepdims=True) for m in mx], axis=1)
        l8 = jnp.zeros((SUBLANES, A_HEADS * t), F32)
        acc = None
        for g in range(ng):
            l8, pv = sweep_b(g, m_all, l8)
            acc = pv if acc is None else acc + pv
        out = acc * (1.0 / jnp.sum(l8, axis=0, keepdims=True))
        for h in range(A_HEADS):
            o_ref[0, :, h * t:(h + 1) * t] = out[:, h * t:(h + 1) * t].T.astype(o_ref.dtype)

    for v in range(1, max_groups + 1):
        pl.when(n_groups == v)(functools.partial(attend, v))


def _dsa(qa, iq, iw, ka, vt, ik2, bias_tab):
    b, s, _ = qa.shape
    t = LANES
    nb = s // t
    group = math.gcd(nb, 2)
    topk = min(TOPK_MAX, s // 4)
    kern = functools.partial(_dsa_kernel, topk=float(topk), n_blocks=nb, group=group,
                             scale=A_HEAD_DIM ** -0.5,
                             idx_scale=(IDX_DIM ** -0.5) * (IDX_HEADS ** -0.5))
    qspec = lambda w: pl.BlockSpec((1, t, w), lambda bi, i: (bi, i, 0))
    kspec = pl.BlockSpec((1, s, t), lambda bi, i: (bi, 0, 0))
    return pl.pallas_call(
        kern,
        grid=(b, nb),
        in_specs=[qspec(A_HEADS * t), qspec(IDX_HEADS * IDX_DIM), qspec(t), kspec,
                  pl.BlockSpec((1, nb, t, t), lambda bi, i: (bi, 0, 0, 0)), kspec,
                  _const_spec(bias_tab.shape)],
        out_specs=qspec(A_HEADS * t),
        out_shape=jax.ShapeDtypeStruct((b, s, A_HEADS * t), BF16),
        scratch_shapes=[pltpu.VMEM((nb, t, t), I32),
                        pltpu.VMEM((nb, SORT_ROWS, SUBLANES, t), I32),
                        pltpu.VMEM((nb, t, A_HEADS * t), F32),
                        pltpu.VMEM((IDX_HEADS * t, t), BF16),
                        pltpu.VMEM((A_HEADS * t, t), BF16),
                        pltpu.VMEM((t, t), F32),
                        pltpu.VMEM((SUBLANES, t), I32),
                        pltpu.VMEM((SUBLANES, t), F32)],
        compiler_params=_params("parallel", "parallel"),
        name="dsa_attention",
    )(qa, iq, iw, ka, vt, ik2, bias_tab)


def _hgrn_kernel(q_ref, f_ref, i_ref, g_ref, lb_ref, gain_ref, o_ref, state_ref, *, n_chunks, n_seq):
    c = B_CHUNK
    rc = n_seq * c
    kd, vd = B_KEY_DIM, B_VAL_DIM
    w = B_HEADS * kd

    @pl.when(pl.program_id(1) == 0)
    def _():
        state_ref[...] = jnp.zeros_like(state_ref)

    row = lax.broadcasted_iota(I32, (rc, rc), 0)
    col = lax.broadcasted_iota(I32, (rc, rc), 1)
    same_seq = (row >= col) if n_seq == 1 else jnp.logical_and(
        row >= col, lax.shift_right_logical(row, 6) == lax.shift_right_logical(col, 6))
    causal = same_seq
    assert c == 64
    tril = jnp.where(causal, 1.0, 0.0).astype(BF16)
    seqs = [slice(b * c, (b + 1) * c) for b in range(n_seq)]
    seq_of_row = lax.shift_right_logical(lax.broadcasted_iota(I32, (rc, w), 0), 6)

    def per_seq_row(x, r):
        return jnp.concatenate([jnp.broadcast_to(x[b * c + r:b * c + r + 1, :], (c, x.shape[1]))
                                for b in range(n_seq)], axis=0)

    def stacked(ref, rows):
        return jnp.concatenate([ref[b, rows, :] for b in range(n_seq)], axis=0)
    q_scale = kd ** -0.5
    lb = lb_ref[...]
    lb_floor = jnp.maximum(lb, LB_FLOOR)
    one_m_lb = 1.0 - lb
    heads = [slice(h * kd, (h + 1) * kd) for h in range(B_HEADS)]

    def stage_decay(ci):
        rows = slice(ci * c, (ci + 1) * c)
        fr = stacked(f_ref, rows)
        z = jnp.exp(-jnp.abs(fr))
        r = 1.0 / (1.0 + z)
        sig_pos = jnp.where(fr >= 0, r, z * r)
        sig_neg = jnp.where(fr >= 0, z * r, r)
        log_f = jnp.log(lb_floor + one_m_lb * sig_pos)
        k_in = one_m_lb * sig_neg
        hi = log_f.astype(BF16)
        rest = log_f - hi.astype(F32)
        mid = rest.astype(BF16)
        lo = (rest - mid.astype(F32)).astype(BF16)
        cs = jnp.dot(tril, jnp.concatenate([hi, mid, lo], axis=1), preferred_element_type=F32)
        return k_in, cs

    def apply_update(pending):
        if pending is not None:
            d_last, st, upd = pending
            for b in range(n_seq):
                for h, hs in enumerate(heads):
                    state_ref[b, h] = d_last[b * c:b * c + 1, hs] * st[b][h] + upd[b][h]

    def stage_scores(ci, decay, pending):
        rows = slice(ci * c, (ci + 1) * c)
        k_in, cs = decay
        bsum = cs[:, :w] + cs[:, w:2 * w] + cs[:, 2 * w:]
        b_mid = per_seq_row(bsum, c // 2 - 1)
        b_last = per_seq_row(bsum, c - 1)
        qs = stacked(q_ref, rows) * q_scale
        v = stacked(i_ref, rows)
        v_b = v.astype(BF16)
        v_t = v.T.astype(BF16)
        q_mid = (qs * jnp.exp(bsum - b_mid)).astype(BF16)
        k_mid = (k_in * jnp.exp(b_mid - bsum)).astype(BF16)
        q_dec = (qs * jnp.exp(bsum)).astype(BF16)
        k_end = (k_in * jnp.exp(b_last - bsum)).astype(BF16)
        d_last = jnp.exp(b_last)
        apply_update(pending)
        st = [[state_ref[b, h] for h in range(B_HEADS)] for b in range(n_seq)]
        attn = [lax.dot_general(q_mid[:, hs], k_mid[:, hs], _NT, preferred_element_type=F32)
                for hs in heads]
        inter = [jnp.concatenate(
            [lax.dot_general(q_dec[sq, hs], st[b][h].astype(BF16), _NT, preferred_element_type=F32)
             for b, sq in enumerate(seqs)], axis=0) for h, hs in enumerate(heads)]
        k_seq = [k_end if n_seq == 1 else jnp.where(seq_of_row == b, k_end, jnp.zeros_like(k_end))
                 for b in range(n_seq)]
        upd = [[jnp.dot(v_t[hs, :], k_seq[b][:, hs], preferred_element_type=F32) for hs in heads]
               for b in range(n_seq)]
        return (attn, inter, v_b), (d_last, st, upd)

    def stage_output(ci, scores):
        rows = slice(ci * c, (ci + 1) * c)
        attn, inter, v_b = scores
        attn = [jnp.where(causal, a, 0.0).astype(BF16) for a in attn]
        outs = [jnp.dot(attn[h], v_b[:, hs], preferred_element_type=F32) + inter[h]
                for h, hs in enumerate(heads)]
        o = jnp.concatenate([_rms(o_h, gain_ref[...]) for o_h in outs], axis=1)
        g = stacked(g_ref, rows)
        o = (o * (g * (1.0 / (1.0 + jnp.exp(-g))))).astype(o_ref.dtype)
        for b, sq in enumerate(seqs):
            o_ref[b, rows, :] = o[sq, :]

    decay, scores, pending = {}, {}, None
    for step in range(n_chunks + 2):
        if step < n_chunks:
            decay[step] = stage_decay(step)
        if 0 <= step - 1 < n_chunks:
            scores[step - 1], pending = stage_scores(step - 1, decay.pop(step - 1), pending)
        if 0 <= step - 2 < n_chunks:
            stage_output(step - 2, scores.pop(step - 2))
    apply_update(pending)


def _hgrn(bq, bf, bi, bg, lb, gain, sb):
    b, s, w = bq.shape
    assert s % sb == 0 and sb % B_CHUNK == 0
    n_seq = 2 if b % 2 == 0 else 1
    kern = functools.partial(_hgrn_kernel, n_chunks=sb // B_CHUNK, n_seq=n_seq)
    spec = pl.BlockSpec((n_seq, sb, w), lambda bi_, si: (bi_, si, 0))
    return pl.pallas_call(
        kern,
        grid=(b // n_seq, s // sb),
        in_specs=[spec, spec, spec, spec, _const_spec((1, w)), _const_spec((1, B_VAL_DIM))],
        out_specs=spec,
        out_shape=jax.ShapeDtypeStruct((b, s, w), BF16),
        scratch_shapes=[pltpu.VMEM((n_seq, B_HEADS, B_VAL_DIM, B_KEY_DIM), F32)],
        compiler_params=_params("parallel", "arbitrary"),
        name="hgrn2",
    )(bq, bf, bi, bg, lb.reshape(1, w), gain.reshape(1, B_VAL_DIM))


def _mla_prep_kernel(cq_ref, ckv_ref, kpe_raw_ref, cos_ref, sin_ref, gq_ref, gkv_ref, wq_ref, wkv_ref,
                     qn_ref, qpe_ref, kn_ref, vt_ref, kpe_ref, *, key_tile):
    nw = C_HEADS * C_NOPE
    pw = C_HEADS * C_ROPE
    cos, sin = cos_ref[...], sin_ref[...]
    cos2 = jnp.concatenate([cos] * (pw // LANES), axis=1)
    sin2 = jnp.concatenate([sin] * (pw // LANES), axis=1)
    q = jnp.dot(_rms(cq_ref[...], gq_ref[...]).astype(BF16), wq_ref[...], preferred_element_type=F32)
    qn_ref[...] = q[:, :nw].astype(qn_ref.dtype)
    qpe_ref[...] = (q[:, nw:nw + pw] * cos2 + q[:, nw + pw:] * sin2).astype(qpe_ref.dtype)
    kv = jnp.dot(_rms(ckv_ref[...], gkv_ref[...]).astype(BF16), wkv_ref[...], preferred_element_type=F32)
    kn_ref[...] = kv[:, :nw].astype(kn_ref.dtype)
    _store_key_major(vt_ref, kv[:, nw:], key_tile)
    kpe_ref[...] = (kpe_raw_ref[:, :LANES] * cos + kpe_raw_ref[:, LANES:] * sin).astype(kpe_ref.dtype)


def _mla_prep(cq, ckv, kpe_raw, cos_t, sin_t, gq, gkv, wq, wkv, layer, tm, key_tile):
    n = cq.shape[0]
    nw = C_HEADS * C_NOPE
    vw = C_HEADS * C_V
    row = lambda w: pl.BlockSpec((tm, w), lambda i: (i, 0))
    widths = (nw, C_HEADS * C_ROPE, nw, None, LANES)
    vt_spec = pl.BlockSpec((tm // key_tile, vw, key_tile), lambda i: (i, 0, 0))
    vt_shape = jax.ShapeDtypeStruct((n // key_tile, vw, key_tile), BF16)
    return pl.pallas_call(
        functools.partial(_mla_prep_kernel, key_tile=key_tile),
        grid=(n // tm,),
        in_specs=[row(C_Q_RANK), row(C_KV_RANK), row(2 * LANES), row(LANES), row(LANES),
                  _const_spec((1, C_Q_RANK)), _const_spec((1, C_KV_RANK)),
                  _layer_spec(wq, layer), _layer_spec(wkv, layer)],
        out_specs=[vt_spec if w is None else row(w) for w in widths],
        out_shape=[vt_shape if w is None else jax.ShapeDtypeStruct((n, w), BF16) for w in widths],
        compiler_params=_params("parallel"),
        name="mla_prep",
    )(cq, ckv, kpe_raw, cos_t, sin_t, gq.reshape(1, -1), gkv.reshape(1, -1), wq, wkv)


def _mla_kernel(qn_ref, qpe_ref, kn_ref, kpe_ref, vt_ref, o_ref, q_ref, acc_ref, *, t, n_tiles, scale):
    i = pl.program_id(1)
    lane_lo = lax.broadcasted_iota(I32, (t, LANES), 1) < C_ROPE
    for h in range(C_HEADS):
        pair = qpe_ref[0, :, (h // 2) * LANES:(h // 2 + 1) * LANES]
        keep = lane_lo if h % 2 == 0 else jnp.logical_not(lane_lo)
        q_ref[h, :, :C_NOPE] = qn_ref[0, :, h * C_NOPE:(h + 1) * C_NOPE]
        q_ref[h, :, C_NOPE:] = jnp.where(keep, pair, jnp.zeros_like(pair))
    acc_ref[...] = jnp.zeros_like(acc_ref)

    def logits_of(step):
        first, count = step
        rows = slice(first * t, (first + count) * t)
        kpe_t = kpe_ref[0, rows, :]
        return [lax.dot_general(
            jnp.concatenate([kn_ref[0, rows, h * C_NOPE:(h + 1) * C_NOPE], kpe_t], axis=1),
            q_ref[h], _NT, preferred_element_type=F32) for h in range(C_HEADS)]

    def softmax_pv(step, block, logits, ms, ls, masked):
        first, count = step
        new_m, new_l = [], []
        for h in range(C_HEADS):
            s = logits[h] * (scale * math.log2(math.e))
            if masked:
                key_pos = lax.broadcasted_iota(I32, (count * t, t), 0) + first * t
                query_pos = lax.broadcasted_iota(I32, (count * t, t), 1) + block * t
                s = jnp.where(key_pos <= query_pos, s, NEG_BIG)
            m_new = jnp.maximum(ms[h], jnp.max(s, axis=0, keepdims=True))
            alpha = jnp.exp2(ms[h] - m_new)
            p = jnp.exp2(s - m_new)
            new_l.append(alpha * ls[h] + jnp.sum(p, axis=0, keepdims=True))
            new_m.append(m_new)
            hs = slice(h * C_V, (h + 1) * C_V)
            tiles = [vt_ref[0, first + u, hs, :] for u in range(count)]
            v_t = tiles[0] if count == 1 else jnp.concatenate(tiles, axis=1)
            acc_ref[h] = alpha * acc_ref[h] + jnp.dot(v_t, p.astype(BF16), preferred_element_type=F32)
        return new_m, new_l

    def run(block):
        steps = [(2 * g, 2) for g in range(block // 2)] + [(block - block % 2, 1 + block % 2)]
        ms = [jnp.full((1, t), NEG_BIG, F32) for _ in range(C_HEADS)]
        ls = [jnp.zeros((1, t), F32) for _ in range(C_HEADS)]
        logits = logits_of(steps[0])
        for k, step in enumerate(steps):
            nxt = logits_of(steps[k + 1]) if k + 1 < len(steps) else None
            ms, ls = softmax_pv(step, block, logits, ms, ls, masked=(k + 1 == len(steps)))
            logits = nxt
        for h in range(C_HEADS):
            o_ref[0, :, h * C_V:(h + 1) * C_V] = (acc_ref[h] * (1.0 / ls[h])).T.astype(o_ref.dtype)

    for v in range(n_tiles):
        pl.when(i == v)(functools.partial(run, v))


def _mla(qn, qpe, kn, kpe, vt, t):
    b, s, _ = qn.shape
    nt = s // t
    assert nt % 2 == 0
    kern = functools.partial(_mla_kernel, t=t, n_tiles=nt, scale=(C_NOPE + C_ROPE) ** -0.5)
    qspec = lambda w: pl.BlockSpec((1, t, w), lambda bi, i: (bi, i, 0))
    kspec = lambda w: pl.BlockSpec((1, s, w), lambda bi, i: (bi, 0, 0))
    return pl.pallas_call(
        kern,
        grid=(b, nt),
        in_specs=[qspec(qn.shape[2]), qspec(qpe.shape[2]), kspec(kn.shape[2]), kspec(kpe.shape[2]),
                  pl.BlockSpec((1, nt, C_HEADS * C_V, t), lambda bi, i: (bi, 0, 0, 0))],
        out_specs=qspec(C_HEADS * C_V),
        out_shape=jax.ShapeDtypeStruct((b, s, C_HEADS * C_V), BF16),
        scratch_shapes=[pltpu.VMEM((C_HEADS, t, C_NOPE + LANES), BF16),
                        pltpu.VMEM((C_HEADS, C_V, t), F32)],
        compiler_params=_params("parallel", "parallel"),
        name="mla_attention",
    )(qn, qpe, kn, kpe, vt)


def _merge_kernel(x_ref, g_ref, wg_ref, oa_ref, ob_ref, oc_ref, wa_ref, wb_ref, wc_ref, wo_ref,
                  out_ref, mixed_ref, *, col_chunk):
    x = x_ref[...]
    d = x.shape[1]
    h = _rms(x, g_ref[...]).astype(BF16)
    branches = ((oa_ref, wa_ref), (ob_ref, wb_ref), (oc_ref, wc_ref))
    for s in range(0, d, col_chunk):
        cs = slice(s, s + col_chunk)
        mixed = None
        for bidx, (o_ref, w_ref) in enumerate(branches):
            logits = jnp.dot(h, wg_ref[:, bidx * d + s:bidx * d + s + col_chunk],
                             preferred_element_type=F32)
            gate = 1.0 / (1.0 + jnp.exp(-logits))
            term = gate * jnp.dot(o_ref[...], w_ref[:, cs], preferred_element_type=F32)
            mixed = term if mixed is None else mixed + term
        mixed_ref[:, cs] = mixed.astype(BF16)
    out_ref[...] = x + jnp.dot(mixed_ref[...], wo_ref[...], preferred_element_type=F32)


def _merge(x, g, w_gate, oa, ob, oc, wa, wb, wc, wo, layer, tm):
    n, d = x.shape
    row = lambda w: pl.BlockSpec((tm, w), lambda i: (i, 0))
    kern = functools.partial(_merge_kernel, col_chunk=256)
    return pl.pallas_call(
        kern,
        grid=(n // tm,),
        in_specs=[row(d), _const_spec((1, d)), _layer_spec(w_gate, layer),
                  row(oa.shape[1]), row(ob.shape[1]), row(oc.shape[1]),
                  _layer_spec(wa, layer), _layer_spec(wb, layer), _layer_spec(wc, layer),
                  _layer_spec(wo, layer)],
        out_specs=row(d),
        out_shape=jax.ShapeDtypeStruct((n, d), F32),
        scratch_shapes=[pltpu.VMEM((tm, d), BF16)],
        compiler_params=_params("parallel"),
        name="merge_out_proj",
    )(x, g.reshape(1, d), w_gate, oa, ob, oc, wa, wb, wc, wo)


def _ffn_kernel(x_ref, g_ref, wg_ref, wu_ref, wd_ref, gf_ref, out_ref, act_ref, *, col_chunk,
                final_norm):
    x = x_ref[...]
    h = _rms(x, g_ref[...]).astype(BF16)
    dff = wg_ref.shape[1]
    for s in range(0, dff, col_chunk):
        cs = slice(s, s + col_chunk)
        gate = jnp.dot(h, wg_ref[:, cs], preferred_element_type=F32)
        up = jnp.dot(h, wu_ref[:, cs], preferred_element_type=F32)
        act_ref[:, cs] = (gate * (1.0 / (1.0 + jnp.exp(-gate))) * up).astype(BF16)
    y = x + jnp.dot(act_ref[...], wd_ref[...], preferred_element_type=F32)
    if final_norm:
        y = _rms(y, gf_ref[...])
    out_ref[...] = y


def _ffn(x, g, wg, wu, wd, gf, layer, tm, final_norm):
    n, d = x.shape
    dff = wg.shape[2]
    row = pl.BlockSpec((tm, d), lambda i: (i, 0))
    kern = functools.partial(_ffn_kernel, col_chunk=256, final_norm=final_norm)
    return pl.pallas_call(
        kern,
        grid=(n // tm,),
        in_specs=[row, _const_spec((1, d)), _layer_spec(wg, layer), _layer_spec(wu, layer),
                  _layer_spec(wd, layer), _const_spec((1, d))],
        out_specs=row,
        out_shape=jax.ShapeDtypeStruct((n, d), F32),
        scratch_shapes=[pltpu.VMEM((tm, dff), BF16)],
        compiler_params=_params("parallel"),
        name="swiglu_ffn",
    )(x, g.reshape(1, d), wg, wu, wd, gf.reshape(1, d))


def _t5_bucket(dist):
    max_exact = REL_BUCKETS // 2
    d = jnp.maximum(dist, 0)
    dl = jnp.maximum(d, max_exact).astype(F32)
    large = max_exact + (jnp.log(dl / max_exact) / math.log(REL_MAX_DIST / max_exact)
                         * (REL_BUCKETS - max_exact)).astype(I32)
    large = jnp.minimum(large, REL_BUCKETS - 1)
    return jnp.where(d < max_exact, d, large)


def _swap_halves(w):
    half = w.shape[-1] // 2
    return jnp.concatenate([w[..., half:], w[..., :half]], axis=-1)


def _in_proj_weights(w_in, d_model):
    splits = (A_HEADS * A_HEAD_DIM, A_HEAD_DIM, A_HEAD_DIM, IDX_HEADS * IDX_DIM, IDX_DIM, IDX_HEADS,
              B_HEADS * B_KEY_DIM, B_HEADS * B_KEY_DIM, B_HEADS * B_VAL_DIM, B_HEADS * B_VAL_DIM,
              C_Q_RANK, C_KV_RANK, C_ROPE, N_BRANCH * d_model)
    st = [int(v) for v in np.concatenate([[0], np.cumsum(splits)])]
    depth, k, n_cols = w_in.shape
    reps = LANES // C_ROPE
    rows = 128

    def regroup_kernel(w_ref, a_ref, b_ref, c_ref, d_ref, g_ref):
        col = lambda a, b: w_ref[:, st[a]:st[b]]
        ik, pe = col(4, 5), col(12, 13)
        iw_pad = jnp.zeros((rows, LANES - IDX_HEADS), F32)
        a_ref[...] = col(0, 4).astype(BF16)
        b_ref[...] = jnp.concatenate([ik, ik, col(5, 6), iw_pad], axis=1).astype(BF16)
        c_ref[...] = col(6, 12).astype(BF16)
        d_ref[...] = jnp.concatenate([pe] * reps + [_swap_halves(pe)] * reps, axis=1).astype(BF16)
        g_ref[...] = col(13, 14).astype(BF16)

    out_w = (st[4], 2 * LANES, st[12] - st[6], 2 * LANES, st[14] - st[13])
    outs = pl.pallas_call(
        regroup_kernel,
        grid=(depth, k // rows),
        in_specs=[pl.BlockSpec((None, rows, n_cols), lambda l, i: (l, i, 0))],
        out_specs=[pl.BlockSpec((None, rows, wd), lambda l, i: (l, i, 0)) for wd in out_w],
        out_shape=[jax.ShapeDtypeStruct((depth, k, wd), BF16) for wd in out_w],
        compiler_params=_params("parallel", "parallel"),
        name="regroup_w_in",
    )(w_in)
    ws = list(outs)
    widths = [splits[:4], (LANES, LANES), splits[6:12], (2 * LANES,)]
    dtypes = [BF16] * 5 + [F32] * 8
    return ws[:4], widths, dtypes, ws[4]


def kernel(x, positions, w_in, w_up_a, w_up_b, w_up_c, w_out, mla_q_norm, mla_w_qb, mla_kv_norm,
           mla_w_kvb, hgrn_lb_logits, hgrn_out_norm, rel_bias, attn_norm, ffn_norm, w_ffn_gate,
           w_ffn_up, w_ffn_down, final_norm):
    bsz, s_len, d_model = x.shape
    depth = w_in.shape[0]
    n = bsz * s_len
    t = LANES

    w_proj, widths, dtypes, w_gate = _in_proj_weights(w_in, d_model)
    wq = mla_w_qb.reshape(depth, C_Q_RANK, C_HEADS, C_NOPE + C_ROPE)
    q_pe = wq[..., C_NOPE:]
    w_qb = jnp.concatenate([wq[..., :C_NOPE].reshape(depth, C_Q_RANK, -1),
                            q_pe.reshape(depth, C_Q_RANK, -1),
                            _swap_halves(q_pe).reshape(depth, C_Q_RANK, -1)], axis=2).astype(BF16)
    wkv = mla_w_kvb.reshape(depth, C_KV_RANK, C_HEADS, C_NOPE + C_V)
    w_kvb = jnp.concatenate([wkv[..., :C_NOPE].reshape(depth, C_KV_RANK, -1),
                             wkv[..., C_NOPE:].reshape(depth, C_KV_RANK, -1)], axis=2).astype(BF16)
    bf = lambda w: w.astype(BF16)
    w_up_a, w_up_b, w_up_c, w_out = bf(w_up_a), bf(w_up_b), bf(w_up_c), bf(w_out)
    w_ffn_gate, w_ffn_up, w_ffn_down = bf(w_ffn_gate), bf(w_ffn_up), bf(w_ffn_down)

    p_lb = jax.nn.softmax(hgrn_lb_logits.astype(F32), axis=0)
    lower_bounds = jnp.cumsum(p_lb, axis=0) - p_lb[0:1]
    inv_freq = ROPE_THETA ** (-jnp.arange(0, C_ROPE, 2, dtype=F32) / C_ROPE)
    ang = positions.astype(F32)[..., None] * inv_freq
    cos, sin = jnp.cos(ang), jnp.sin(ang)
    reps = LANES // C_ROPE
    cos_t = jnp.tile(jnp.concatenate([cos, cos], axis=-1), (1, 1, reps)).reshape(n, LANES)
    sin_t = jnp.tile(jnp.concatenate([-sin, sin], axis=-1), (1, 1, reps)).reshape(n, LANES)
    q_idx = jnp.arange(t, dtype=I32)[None, :]
    k_idx = jnp.arange(t, dtype=I32)[:, None]
    def tab(dist):
        onehot = jax.nn.one_hot(_t5_bucket(dist), REL_BUCKETS, dtype=F32)
        return jnp.einsum("kqb,bh->khq", onehot, rel_bias.astype(F32),
                          precision=lax.Precision.HIGHEST).reshape(t, A_HEADS * t)
    far = jnp.broadcast_to(rel_bias[REL_BUCKETS - 1].astype(F32)[None, :, None],
                           (t, A_HEADS, t)).reshape(t, A_HEADS * t)
    bias_tab = (jnp.stack([tab(q_idx + t - k_idx), tab(q_idx - k_idx), far]) - far) * math.log2(math.e)

    tm = min(512, n)
    t_mla = min(256, s_len)
    x2 = x.reshape(n, d_model)
    r3 = lambda a: a.reshape(bsz, s_len, a.shape[-1])
    per_seq = lambda a: a.reshape(bsz, a.shape[0] // bsz, a.shape[1], a.shape[2])
    key_tiles = [0, 0, t] + [0] * 10
    for l in range(depth):
        (qa, ka, vta, iq, ik2, iw, bq, bfr, bi, bg, cq, ckv, kpe_raw) = _norm_matmul(
            x2, attn_norm[l], w_proj, l, widths, dtypes, key_tiles, tm)
        o_a = _dsa(r3(qa), r3(iq), r3(iw), r3(ka), per_seq(vta), r3(ik2), bias_tab)
        o_b = _hgrn(r3(bq), r3(bfr), r3(bi), r3(bg), lower_bounds[l], hgrn_out_norm[l],
                    sb=min(512, s_len))
        qn, qpe, kn, vtc, kpe = _mla_prep(cq, ckv, kpe_raw, cos_t, sin_t, mla_q_norm[l], mla_kv_norm[l],
                                          w_qb, w_kvb, l, tm, t_mla)
        o_c = _mla(r3(qn), r3(qpe), r3(kn), r3(kpe), per_seq(vtc), t_mla)
        x2 = _merge(x2, attn_norm[l], w_gate, o_a.reshape(n, -1), o_b.reshape(n, -1),
                    o_c.reshape(n, -1), w_up_a, w_up_b, w_up_c, w_out, l, tm)
        x2 = _ffn(x2, ffn_norm[l], w_ffn_gate, w_ffn_up, w_ffn_down, final_norm,
                  l, tm, final_norm=(l == depth - 1))
    return x2.reshape(bsz, s_len, d_model)
```

```python
import functools
import math

import jax
import jax.numpy as jnp
import numpy as np
from jax import lax
from jax.experimental import pallas as pl
from jax.experimental.pallas import tpu as pltpu

F32 = jnp.float32
BF16 = jnp.bfloat16
I32 = jnp.int32

A_HEADS = 4
A_HEAD_DIM = 128
IDX_HEADS = 8
IDX_DIM = 64
TOPK_MAX = 256
B_HEADS = 4
B_KEY_DIM = 128
B_VAL_DIM = 128
B_CHUNK = 64
C_HEADS = 4
C_Q_RANK = 384
C_KV_RANK = 256
C_NOPE = 128
C_ROPE = 64
C_V = 128
ROPE_THETA = 10000.0
REL_BUCKETS = 32
REL_MAX_DIST = 128
N_BRANCH = 3
EPS = 1e-6
NEG_BIG = -1e30
LB_FLOOR = 1e-30

LANES = 128
SUBLANES = 8
VMEM_LIMIT = 56 * 1024 * 1024
INT_MIN = np.int32(-2 ** 31)
SORT_ROWS = LANES // SUBLANES


def _oddeven_merge_sort(n):
    pairs = []
    p = 1
    while p < n:
        k = p
        while k >= 1:
            for j in range(k % p, n - k, 2 * k):
                for i in range(min(k, n - j - k)):
                    if (i + j) // (2 * p) == (i + j + k) // (2 * p):
                        pairs.append((i + j, i + j + k))
            k //= 2
        p *= 2
    return tuple(pairs)


_SORT16 = _oddeven_merge_sort(SORT_ROWS)

_NT = (((1,), (1,)), ((), ()))


def _params(*sem):
    return pltpu.CompilerParams(dimension_semantics=sem, vmem_limit_bytes=VMEM_LIMIT)


def _rms(x, g):
    return x * lax.rsqrt(jnp.mean(x * x, axis=-1, keepdims=True) + EPS) * g


def _const_spec(shape):
    nd = len(shape)
    return pl.BlockSpec(shape, lambda *_: (0,) * nd)


def _layer_spec(w, layer):
    return pl.BlockSpec((None,) + w.shape[1:], lambda *_: (layer, 0, 0))


def _fold8(x, op):
    r, c = x.shape
    return op(x.reshape(r // SUBLANES, SUBLANES, c), axis=0)


def _store_key_major(o_ref, rows, tile):
    for u in range(rows.shape[0] // tile):
        o_ref[u] = rows[u * tile:(u + 1) * tile, :].T.astype(o_ref.dtype)


def _mla_up_project(cq, ckv, kpe_raw, cos, sin, gq, gkv, wq_ref, wkv_ref,
                    qn_ref, qpe_ref, kn_ref, vt_ref, kpe_ref, key_tile):
    nw = C_HEADS * C_NOPE
    pw = C_HEADS * C_ROPE
    cos2 = jnp.concatenate([cos] * (pw // LANES), axis=1)
    sin2 = jnp.concatenate([sin] * (pw // LANES), axis=1)
    q = jnp.dot(_rms(cq, gq).astype(BF16), wq_ref[...], preferred_element_type=F32)
    qn_ref[...] = q[:, :nw].astype(qn_ref.dtype)
    qpe_ref[...] = (q[:, nw:nw + pw] * cos2 + q[:, nw + pw:] * sin2).astype(qpe_ref.dtype)
    kv = jnp.dot(_rms(ckv, gkv).astype(BF16), wkv_ref[...], preferred_element_type=F32)
    kn_ref[...] = kv[:, :nw].astype(kn_ref.dtype)
    _store_key_major(vt_ref, kv[:, nw:], key_tile)
    kpe_ref[...] = (kpe_raw[:, :LANES] * cos + kpe_raw[:, LANES:] * sin).astype(kpe_ref.dtype)


N_LATENT = 3
N_MLA_IN = 6
N_MLA_OUT = 5


def _norm_matmul_kernel(x_ref, g_ref, *refs, widths, key_tiles, mla_key_tile, col_chunk):
    n_w = len(widths)
    w_refs, mla_in = refs[:n_w], refs[n_w:n_w + N_MLA_IN]
    out_refs, mla_out = refs[n_w + N_MLA_IN:-N_MLA_OUT], refs[-N_MLA_OUT:]
    h = _rms(x_ref[...], g_ref[...]).astype(BF16)
    k = 0
    latents = []
    for w_ref, w_widths in zip(w_refs, widths):
        c0 = 0
        for w in w_widths:
            if k >= len(out_refs):
                latents.append(jnp.dot(h, w_ref[:, c0:c0 + w], preferred_element_type=F32))
            elif key_tiles[k]:
                _store_key_major(out_refs[k], jnp.dot(h, w_ref[:, c0:c0 + w], preferred_element_type=F32),
                                 key_tiles[k])
            else:
                for s in range(0, w, col_chunk):
                    e = min(s + col_chunk, w)
                    out_refs[k][:, s:e] = jnp.dot(h, w_ref[:, c0 + s:c0 + e],
                                                  preferred_element_type=F32).astype(out_refs[k].dtype)
            c0 += w
            k += 1
    cos_ref, sin_ref, gq_ref, gkv_ref, wq_ref, wkv_ref = mla_in
    _mla_up_project(*latents, cos_ref[...], sin_ref[...], gq_ref[...], gkv_ref[...], wq_ref, wkv_ref,
                    *mla_out, mla_key_tile)


def _norm_matmul(x, g, ws, layer, widths, dtypes, key_tiles, mla, tm):
    cos_t, sin_t, gq, gkv, wq, wkv, mla_tile = mla
    n, k = x.shape
    flat = [wd for w_widths in widths for wd in w_widths]
    assert n % tm == 0 and all(w.shape[1:] == (k, sum(ww)) for w, ww in zip(ws, widths))
    kern = functools.partial(_norm_matmul_kernel, widths=tuple(tuple(ww) for ww in widths),
                             key_tiles=tuple(key_tiles), mla_key_tile=mla_tile, col_chunk=512)
    row = lambda w: pl.BlockSpec((tm, w), lambda i: (i, 0))
    tiles = lambda w, kt: pl.BlockSpec((tm // kt, w, kt), lambda i: (i, 0, 0))
    specs, shapes = [], []
    for wd, dt, kt in list(zip(flat, dtypes, key_tiles))[:-N_LATENT]:
        specs.append(tiles(wd, kt) if kt else row(wd))
        shapes.append(jax.ShapeDtypeStruct((n // kt, wd, kt) if kt else (n, wd), dt))
    nw, vw = C_HEADS * C_NOPE, C_HEADS * C_V
    for wd in (nw, C_HEADS * C_ROPE, nw, None, LANES):
        specs.append(tiles(vw, mla_tile) if wd is None else row(wd))
        shapes.append(jax.ShapeDtypeStruct((n // mla_tile, vw, mla_tile) if wd is None else (n, wd), BF16))
    return pl.pallas_call(
        kern,
        grid=(n // tm,),
        in_specs=[row(k), _const_spec((1, k))] + [_layer_spec(w, layer) for w in ws]
                 + [row(LANES), row(LANES), _const_spec((1, C_Q_RANK)), _const_spec((1, C_KV_RANK)),
                    _layer_spec(wq, layer), _layer_spec(wkv, layer)],
        out_specs=specs,
        out_shape=shapes,
        compiler_params=_params("parallel"),
        name="norm_proj",
    )(x, g.reshape(1, k), *ws, cos_t, sin_t, gq.reshape(1, -1), gkv.reshape(1, -1), wq, wkv)


def _dsa_kernel(qa_ref, iq_ref, iw_ref, ka_ref, vt_ref, ik_ref, bias_ref, o_ref,
                keys_ref, sorted_ref, s_ref, qm_ref, qs_ref, wt_ref, thr_ref, ties_ref,
                *, topk, n_blocks, group, scale, idx_scale):
    t = LANES
    gt = group * t
    i = pl.program_id(1)
    n_groups = (i + group) // group
    max_groups = n_blocks // group
    krow = lax.broadcasted_iota(I32, (t, t), 0)
    qcol = lax.broadcasted_iota(I32, (t, t), 1)
    lane_lo = qcol < IDX_DIM

    for h in range(IDX_HEADS):
        pair = iq_ref[0, :, (h // 2) * t:(h // 2 + 1) * t]
        keep = lane_lo if h % 2 == 0 else jnp.logical_not(lane_lo)
        qm_ref[h * t:(h + 1) * t, :] = jnp.where(keep, pair, jnp.zeros_like(pair))
    for h in range(A_HEADS):
        qs_ref[h * t:(h + 1) * t, :] = qa_ref[0, :, h * t:(h + 1) * t]
    wt_ref[...] = iw_ref[0].T

    def index_logits(g):
        return lax.dot_general(ik_ref[0, g * gt:(g + 1) * gt, :], qm_ref[...], _NT,
                               preferred_element_type=F32)

    def score_group(g, lg, ng):
        for u in range(group):
            j = g * group + u
            acc = jnp.zeros((t, t), F32)
            for h in range(IDX_HEADS):
                acc = acc + jnp.maximum(lg[u * t:(u + 1) * t, h * t:(h + 1) * t], 0.0) * wt_ref[h:h + 1, :]
            score = acc * idx_scale
            if j >= (ng - 1) * group:
                score = jnp.where(krow + j * t <= qcol + i * t, score, NEG_BIG)
            score = jnp.where(score == 0.0, 0.0, score)
            bits = pltpu.bitcast(score, I32)
            key = bits ^ ((bits >> 31) & np.int32(0x7FFFFFFF))
            keys_ref[j] = key
            rows = [key[r * SUBLANES:(r + 1) * SUBLANES, :] for r in range(SORT_ROWS)]
            for a, b in _SORT16:
                rows[a], rows[b] = jnp.maximum(rows[a], rows[b]), jnp.minimum(rows[a], rows[b])
            for r in range(SORT_ROWS):
                sorted_ref[j, r] = rows[r]

    def score_all(ng):
        lg = index_logits(0)
        for g in range(ng):
            nxt = index_logits(g + 1) if g + 1 < ng else None
            score_group(g, lg, ng)
            lg = nxt

    for v in range(1, max_groups + 1):
        pl.when(n_groups == v)(functools.partial(score_all, v))

    thr_ref[...] = jnp.full(thr_ref.shape, INT_MIN, I32)
    ties_ref[...] = jnp.zeros(ties_ref.shape, F32)

    def count(n, pred):
        acc = [jnp.zeros((SUBLANES, t), F32) for _ in range(5)]
        for j in range(n):
            v = [sorted_ref[j, r] for r in range(SORT_ROWS)]
            m1 = pred(v[7])
            m2 = pred(jnp.where(m1, v[11], v[3]))
            m3 = pred(jnp.where(m1, jnp.where(m2, v[13], v[9]), jnp.where(m2, v[5], v[1])))
            m4 = pred(jnp.where(m1, jnp.where(m2, jnp.where(m3, v[14], v[12]), jnp.where(m3, v[10], v[8])),
                                jnp.where(m2, jnp.where(m3, v[6], v[4]), jnp.where(m3, v[2], v[0]))))
            m5 = pred(v[15])
            acc = [a + jnp.where(m, 1.0, 0.0) for a, m in zip(acc, (m1, m2, m3, m4, m5))]
        cnt = 8.0 * acc[0] + 4.0 * acc[1] + 2.0 * acc[2] + acc[3] + acc[4]
        return jnp.sum(cnt, axis=0, keepdims=True)

    def search(n):
        def search_pass(b, thr):
            cand = thr + lax.shift_left(np.int32(1), 31 - b)
            return jnp.where(count(n, lambda key: key >= cand) >= topk, cand, thr)

        thr = lax.fori_loop(0, 32, search_pass, jnp.full((1, t), INT_MIN, I32))
        thr_ref[...] = jnp.broadcast_to(thr, thr_ref.shape)
        ties_ref[...] = jnp.broadcast_to(topk - count(n, lambda key: key > thr), ties_ref.shape)

    for c in range(n_blocks):
        if (c + 1) * t > topk:
            pl.when(i == c)(functools.partial(search, c + 1))

    thr = thr_ref[0:1, :]
    n_ties = ties_ref[0:1, :]

    r2 = lax.broadcasted_iota(I32, (2 * t, t), 0)
    c2 = lax.broadcasted_iota(I32, (2 * t, t), 1)
    tie_lhs = jnp.where(jnp.logical_or(r2 >= t, c2 < r2), 1.0, 0.0).astype(BF16)

    def sweep_a_mxu(g):
        keys = [keys_ref[g * group + u] for u in range(group)]
        eqs = [key == thr for key in keys]
        eq_all = jnp.concatenate([jnp.where(eq, 1.0, 0.0).astype(BF16) for eq in eqs], axis=1)
        pref = jnp.dot(tie_lhs, eq_all, preferred_element_type=F32)
        s_grp = lax.dot_general(ka_ref[0, g * gt:(g + 1) * gt, :], qs_ref[...], _NT,
                                preferred_element_type=F32)
        return keys, eqs, pref, s_grp

    def sweep_a_vpu(g, operands, seen, mx, ng):
        keys, eqs, pref, s_grp = operands
        for u in range(group):
            j = g * group + u
            us = slice(u * t, (u + 1) * t)
            rank = jnp.where(keys[u] > thr, -1.0, jnp.where(eqs[u], seen + pref[:t, us], 3e38))
            if j >= (ng - 1) * group:
                rank = jnp.where(krow + j * t <= qcol + i * t, rank, 3e38)
            valid = rank < n_ties
            which = jnp.where(j == i, 1, jnp.where(j == i - 1, 0, 2))
            s_all = s_grp[us, :] * (scale * math.log2(math.e))
            if j > (ng - 1) * group - 2:
                s_all = s_all + bias_ref[which]
            for h in range(A_HEADS):
                s_h = jnp.where(valid, s_all[:, h * t:(h + 1) * t], NEG_BIG)
                s_ref[j, :, h * t:(h + 1) * t] = s_h
                mx[h] = jnp.maximum(mx[h], _fold8(s_h, jnp.max))
            seen = seen + pref[t:t + 1, us]
        return seen, mx

    def sweep_b(g, m_all, l8):
        ps = []
        for u in range(group):
            p = jnp.exp2(s_ref[g * group + u] - m_all)
            l8 = l8 + _fold8(p, jnp.sum)
            ps.append(p.astype(BF16))
        vt_grp = jnp.concatenate([vt_ref[0, g * group + u] for u in range(group)], axis=1)
        return l8, jnp.dot(vt_grp, jnp.concatenate(ps, axis=0), preferred_element_type=F32)

    def attend(ng):
        seen = jnp.zeros((1, t), F32)
        mx = [jnp.full((SUBLANES, t), NEG_BIG, F32) for _ in range(A_HEADS)]
        operands = sweep_a_mxu(0)
        for g in range(ng):
            nxt = sweep_a_mxu(g + 1) if g + 1 < ng else None
            seen, mx = sweep_a_vpu(g, operands, seen, mx, ng)
            operands = nxt
        m_all = jnp.concatenate([jnp.max(m, axis=0, keepdims=True) for m in mx], axis=1)
        l8 = jnp.zeros((SUBLANES, A_HEADS * t), F32)
        acc = None
        for g in range(ng):
            l8, pv = sweep_b(g, m_all, l8)
            acc = pv if acc is None else acc + pv
        out = acc * (1.0 / jnp.sum(l8, axis=0, keepdims=True))
        for h in range(A_HEADS):
            o_ref[0, :, h * t:(h + 1) * t] = out[:, h * t:(h + 1) * t].T.astype(o_ref.dtype)

    for v in range(1, max_groups + 1):
        pl.when(n_groups == v)(functools.partial(attend, v))


def _dsa(qa, iq, iw, ka, vt, ik2, bias_tab):
    b, s, _ = qa.shape
    t = LANES
    nb = s // t
    group = math.gcd(nb, 2)
    topk = min(TOPK_MAX, s // 4)
    kern = functools.partial(_dsa_kernel, topk=float(topk), n_blocks=nb, group=group,
                             scale=A_HEAD_DIM ** -0.5,
                             idx_scale=(IDX_DIM ** -0.5) * (IDX_HEADS ** -0.5))
    qspec = lambda w: pl.BlockSpec((1, t, w), lambda bi, i: (bi, i, 0))
    kspec = pl.BlockSpec((1, s, t), lambda bi, i: (bi, 0, 0))
    return pl.pallas_call(
        kern,
        grid=(b, nb),
        in_specs=[qspec(A_HEADS * t), qspec(IDX_HEADS * IDX_DIM), qspec(t), kspec,
                  pl.BlockSpec((1, nb, t, t), lambda bi, i: (bi, 0, 0, 0)), kspec,
                  _const_spec(bias_tab.shape)],
        out_specs=qspec(A_HEADS * t),
        out_shape=jax.ShapeDtypeStruct((b, s, A_HEADS * t), BF16),
        scratch_shapes=[pltpu.VMEM((nb, t, t), I32),
                        pltpu.VMEM((nb, SORT_ROWS, SUBLANES, t), I32),
                        pltpu.VMEM((nb, t, A_HEADS * t), F32),
                        pltpu.VMEM((IDX_HEADS * t, t), BF16),
                        pltpu.VMEM((A_HEADS * t, t), BF16),
                        pltpu.VMEM((t, t), F32),
                        pltpu.VMEM((SUBLANES, t), I32),
                        pltpu.VMEM((SUBLANES, t), F32)],
        compiler_params=_params("parallel", "parallel"),
        name="dsa_attention",
    )(qa, iq, iw, ka, vt, ik2, bias_tab)


def _hgrn_kernel(q_ref, f_ref, i_ref, g_ref, lb_ref, gain_ref, o_ref, state_ref, *, n_chunks, n_seq):
    c = B_CHUNK
    rc = n_seq * c
    kd, vd = B_KEY_DIM, B_VAL_DIM
    w = B_HEADS * kd

    @pl.when(pl.program_id(1) == 0)
    def _():
        state_ref[...] = jnp.zeros_like(state_ref)

    row = lax.broadcasted_iota(I32, (rc, rc), 0)
    col = lax.broadcasted_iota(I32, (rc, rc), 1)
    same_seq = (row >= col) if n_seq == 1 else jnp.logical_and(
        row >= col, lax.shift_right_logical(row, 6) == lax.shift_right_logical(col, 6))
    causal = same_seq
    assert c == 64
    tril = jnp.where(causal, 1.0, 0.0).astype(BF16)
    seqs = [slice(b * c, (b + 1) * c) for b in range(n_seq)]
    seq_of_row = lax.shift_right_logical(lax.broadcasted_iota(I32, (rc, w), 0), 6)

    def per_seq_row(x, r):
        return jnp.concatenate([jnp.broadcast_to(x[b * c + r:b * c + r + 1, :], (c, x.shape[1]))
                                for b in range(n_seq)], axis=0)

    def stacked(ref, rows):
        return jnp.concatenate([ref[b, rows, :] for b in range(n_seq)], axis=0)
    q_scale = kd ** -0.5
    lb = lb_ref[...]
    lb_floor = jnp.maximum(lb, LB_FLOOR)
    one_m_lb = 1.0 - lb
    heads = [slice(h * kd, (h + 1) * kd) for h in range(B_HEADS)]

    def stage_decay(ci):
        rows = slice(ci * c, (ci + 1) * c)
        fr = stacked(f_ref, rows)
        z = jnp.exp(-jnp.abs(fr))
        r = 1.0 / (1.0 + z)
        sig_pos = jnp.where(fr >= 0, r, z * r)
        sig_neg = jnp.where(fr >= 0, z * r, r)
        log_f = jnp.log(lb_floor + one_m_lb * sig_pos)
        k_in = one_m_lb * sig_neg
        hi = log_f.astype(BF16)
        rest = log_f - hi.astype(F32)
        mid = rest.astype(BF16)
        lo = (rest - mid.astype(F32)).astype(BF16)
        cs = jnp.dot(tril, jnp.concatenate([hi, mid, lo], axis=1), preferred_element_type=F32)
        return k_in, cs

    def apply_update(pending):
        if pending is not None:
            d_last, st, upd = pending
            for b in range(n_seq):
                for h, hs in enumerate(heads):
                    state_ref[b, h] = d_last[b * c:b * c + 1, hs] * st[b][h] + upd[b][h]

    def stage_scores(ci, decay, pending):
        rows = slice(ci * c, (ci + 1) * c)
        k_in, cs = decay
        bsum = cs[:, :w] + cs[:, w:2 * w] + cs[:, 2 * w:]
        b_mid = per_seq_row(bsum, c // 2 - 1)
        b_last = per_seq_row(bsum, c - 1)
        qs = stacked(q_ref, rows) * q_scale
        v = stacked(i_ref, rows)
        v_b = v.astype(BF16)
        v_t = v.T.astype(BF16)
        q_mid = (qs * jnp.exp(bsum - b_mid)).astype(BF16)
        k_mid = (k_in * jnp.exp(b_mid - bsum)).astype(BF16)
        q_dec = (qs * jnp.exp(bsum)).astype(BF16)
        k_end = (k_in * jnp.exp(b_last - bsum)).astype(BF16)
        d_last = jnp.exp(b_last)
        apply_update(pending)
        st = [[state_ref[b, h] for h in range(B_HEADS)] for b in range(n_seq)]
        attn = [lax.dot_general(q_mid[:, hs], k_mid[:, hs], _NT, preferred_element_type=F32)
                for hs in heads]
        inter = [jnp.concatenate(
            [lax.dot_general(q_dec[sq, hs], st[b][h].astype(BF16), _NT, preferred_element_type=F32)
             for b, sq in enumerate(seqs)], axis=0) for h, hs in enumerate(heads)]
        k_seq = [k_end if n_seq == 1 else jnp.where(seq_of_row == b, k_end, jnp.zeros_like(k_end))
                 for b in range(n_seq)]
        upd = [[jnp.dot(v_t[hs, :], k_seq[b][:, hs], preferred_element_type=F32) for hs in heads]
               for b in range(n_seq)]
        return (attn, inter, v_b), (d_last, st, upd)

    def stage_output(ci, scores):
        rows = slice(ci * c, (ci + 1) * c)
        attn, inter, v_b = scores
        attn = [jnp.where(causal, a, 0.0).astype(BF16) for a in attn]
        outs = [jnp.dot(attn[h], v_b[:, hs], preferred_element_type=F32) + inter[h]
                for h, hs in enumerate(heads)]
        o = jnp.concatenate([_rms(o_h, gain_ref[...]) for o_h in outs], axis=1)
        g = stacked(g_ref, rows)
        o = (o * (g * (1.0 / (1.0 + jnp.exp(-g))))).astype(o_ref.dtype)
        for b, sq in enumerate(seqs):
            o_ref[b, rows, :] = o[sq, :]

    decay, scores, pending = {}, {}, None
    for step in range(n_chunks + 2):
        if step < n_chunks:
            decay[step] = stage_decay(step)
        if 0 <= step - 1 < n_chunks:
            scores[step - 1], pending = stage_scores(step - 1, decay.pop(step - 1), pending)
        if 0 <= step - 2 < n_chunks:
            stage_output(step - 2, scores.pop(step - 2))
    apply_update(pending)


def _hgrn(bq, bf, bi, bg, lb, gain, sb):
    b, s, w = bq.shape
    assert s % sb == 0 and sb % B_CHUNK == 0
    n_seq = 2 if b % 2 == 0 else 1
    kern = functools.partial(_hgrn_kernel, n_chunks=sb // B_CHUNK, n_seq=n_seq)
    spec = pl.BlockSpec((n_seq, sb, w), lambda bi_, si: (bi_, si, 0))
    return pl.pallas_call(
        kern,
        grid=(b // n_seq, s // sb),
        in_specs=[spec, spec, spec, spec, _const_spec((1, w)), _const_spec((1, B_VAL_DIM))],
        out_specs=spec,
        out_shape=jax.ShapeDtypeStruct((b, s, w), BF16),
        scratch_shapes=[pltpu.VMEM((n_seq, B_HEADS, B_VAL_DIM, B_KEY_DIM), F32)],
        compiler_params=_params("parallel", "arbitrary"),
        name="hgrn2",
    )(bq, bf, bi, bg, lb.reshape(1, w), gain.reshape(1, B_VAL_DIM))


def _mla_kernel(qn_ref, qpe_ref, kn_ref, kpe_ref, vt_ref, o_ref, q_ref, acc_ref, *, t, n_tiles, scale):
    i = pl.program_id(1)
    lane_lo = lax.broadcasted_iota(I32, (t, LANES), 1) < C_ROPE
    for h in range(C_HEADS):
        pair = qpe_ref[0, :, (h // 2) * LANES:(h // 2 + 1) * LANES]
        keep = lane_lo if h % 2 == 0 else jnp.logical_not(lane_lo)
        q_ref[h, :, :C_NOPE] = qn_ref[0, :, h * C_NOPE:(h + 1) * C_NOPE]
        q_ref[h, :, C_NOPE:] = jnp.where(keep, pair, jnp.zeros_like(pair))
    acc_ref[...] = jnp.zeros_like(acc_ref)

    def logits_of(step):
        first, count = step
        rows = slice(first * t, (first + count) * t)
        kpe_t = kpe_ref[0, rows, :]
        return [lax.dot_general(
            jnp.concatenate([kn_ref[0, rows, h * C_NOPE:(h + 1) * C_NOPE], kpe_t], axis=1),
            q_ref[h], _NT, preferred_element_type=F32) for h in range(C_HEADS)]

    def softmax_pv(step, block, logits, ms, ls, masked):
        first, count = step
        new_m, new_l = [], []
        for h in range(C_HEADS):
            s = logits[h] * (scale * math.log2(math.e))
            if masked:
                key_pos = lax.broadcasted_iota(I32, (count * t, t), 0) + first * t
                query_pos = lax.broadcasted_iota(I32, (count * t, t), 1) + block * t
                s = jnp.where(key_pos <= query_pos, s, NEG_BIG)
            m_new = jnp.maximum(ms[h], jnp.max(s, axis=0, keepdims=True))
            alpha = jnp.exp2(ms[h] - m_new)
            p = jnp.exp2(s - m_new)
            new_l.append(alpha * ls[h] + jnp.sum(p, axis=0, keepdims=True))
            new_m.append(m_new)
            hs = slice(h * C_V, (h + 1) * C_V)
            tiles = [vt_ref[0, first + u, hs, :] for u in range(count)]
            v_t = tiles[0] if count == 1 else jnp.concatenate(tiles, axis=1)
            acc_ref[h] = alpha * acc_ref[h] + jnp.dot(v_t, p.astype(BF16), preferred_element_type=F32)
        return new_m, new_l

    def run(block):
        steps = [(2 * g, 2) for g in range(block // 2)] + [(block - block % 2, 1 + block % 2)]
        ms = [jnp.full((1, t), NEG_BIG, F32) for _ in range(C_HEADS)]
        ls = [jnp.zeros((1, t), F32) for _ in range(C_HEADS)]
        logits = logits_of(steps[0])
        for k, step in enumerate(steps):
            nxt = logits_of(steps[k + 1]) if k + 1 < len(steps) else None
            ms, ls = softmax_pv(step, block, logits, ms, ls, masked=(k + 1 == len(steps)))
            logits = nxt
        for h in range(C_HEADS):
            o_ref[0, :, h * C_V:(h + 1) * C_V] = (acc_ref[h] * (1.0 / ls[h])).T.astype(o_ref.dtype)

    for v in range(n_tiles):
        pl.when(i == v)(functools.partial(run, v))


def _mla(qn, qpe, kn, kpe, vt, t):
    b, s, _ = qn.shape
    nt = s // t
    assert nt % 2 == 0
    kern = functools.partial(_mla_kernel, t=t, n_tiles=nt, scale=(C_NOPE + C_ROPE) ** -0.5)
    qspec = lambda w: pl.BlockSpec((1, t, w), lambda bi, i: (bi, i, 0))
    kspec = lambda w: pl.BlockSpec((1, s, w), lambda bi, i: (bi, 0, 0))
    return pl.pallas_call(
        kern,
        grid=(b, nt),
        in_specs=[qspec(qn.shape[2]), qspec(qpe.shape[2]), kspec(kn.shape[2]), kspec(kpe.shape[2]),
                  pl.BlockSpec((1, nt, C_HEADS * C_V, t), lambda bi, i: (bi, 0, 0, 0))],
        out_specs=qspec(C_HEADS * C_V),
        out_shape=jax.ShapeDtypeStruct((b, s, C_HEADS * C_V), BF16),
        scratch_shapes=[pltpu.VMEM((C_HEADS, t, C_NOPE + LANES), BF16),
                        pltpu.VMEM((C_HEADS, C_V, t), F32)],
        compiler_params=_params("parallel", "parallel"),
        name="mla_attention",
    )(qn, qpe, kn, kpe, vt)


def _merge_kernel(x_ref, g_ref, wg_ref, oa_ref, ob_ref, oc_ref, wa_ref, wb_ref, wc_ref, wo_ref,
                  out_ref, mixed_ref, *, col_chunk):
    x = x_ref[...]
    d = x.shape[1]
    h = _rms(x, g_ref[...]).astype(BF16)
    branches = ((oa_ref, wa_ref), (ob_ref, wb_ref), (oc_ref, wc_ref))
    for s in range(0, d, col_chunk):
        cs = slice(s, s + col_chunk)
        mixed = None
        for bidx, (o_ref, w_ref) in enumerate(branches):
            logits = jnp.dot(h, wg_ref[:, bidx * d + s:bidx * d + s + col_chunk],
                             preferred_element_type=F32)
            gate = 1.0 / (1.0 + jnp.exp(-logits))
            term = gate * jnp.dot(o_ref[...], w_ref[:, cs], preferred_element_type=F32)
            mixed = term if mixed is None else mixed + term
        mixed_ref[:, cs] = mixed.astype(BF16)
    out_ref[...] = x + jnp.dot(mixed_ref[...], wo_ref[...], preferred_element_type=F32)


def _merge(x, g, w_gate, oa, ob, oc, wa, wb, wc, wo, layer, tm):
    n, d = x.shape
    row = lambda w: pl.BlockSpec((tm, w), lambda i: (i, 0))
    kern = functools.partial(_merge_kernel, col_chunk=256)
    return pl.pallas_call(
        kern,
        grid=(n // tm,),
        in_specs=[row(d), _const_spec((1, d)), _layer_spec(w_gate, layer),
                  row(oa.shape[1]), row(ob.shape[1]), row(oc.shape[1]),
                  _layer_spec(wa, layer), _layer_spec(wb, layer), _layer_spec(wc, layer),
                  _layer_spec(wo, layer)],
        out_specs=row(d),
        out_shape=jax.ShapeDtypeStruct((n, d), F32),
        scratch_shapes=[pltpu.VMEM((tm, d), BF16)],
        compiler_params=_params("parallel"),
        name="merge_out_proj",
    )(x, g.reshape(1, d), w_gate, oa, ob, oc, wa, wb, wc, wo)


def _ffn_kernel(x_ref, g_ref, wg_ref, wu_ref, wd_ref, gf_ref, out_ref, act_ref, *, col_chunk,
                final_norm):
    x = x_ref[...]
    h = _rms(x, g_ref[...]).astype(BF16)
    dff = wg_ref.shape[1]
    for s in range(0, dff, col_chunk):
        cs = slice(s, s + col_chunk)
        gate = jnp.dot(h, wg_ref[:, cs], preferred_element_type=F32)
        up = jnp.dot(h, wu_ref[:, cs], preferred_element_type=F32)
        act_ref[:, cs] = (gate * (1.0 / (1.0 + jnp.exp(-gate))) * up).astype(BF16)
    y = x + jnp.dot(act_ref[...], wd_ref[...], preferred_element_type=F32)
    if final_norm:
        y = _rms(y, gf_ref[...])
    out_ref[...] = y


def _ffn(x, g, wg, wu, wd, gf, layer, tm, final_norm):
    n, d = x.shape
    dff = wg.shape[2]
    row = pl.BlockSpec((tm, d), lambda i: (i, 0))
    kern = functools.partial(_ffn_kernel, col_chunk=256, final_norm=final_norm)
    return pl.pallas_call(
        kern,
        grid=(n // tm,),
        in_specs=[row, _const_spec((1, d)), _layer_spec(wg, layer), _layer_spec(wu, layer),
                  _layer_spec(wd, layer), _const_spec((1, d))],
        out_specs=row,
        out_shape=jax.ShapeDtypeStruct((n, d), F32),
        scratch_shapes=[pltpu.VMEM((tm, dff), BF16)],
        compiler_params=_params("parallel"),
        name="swiglu_ffn",
    )(x, g.reshape(1, d), wg, wu, wd, gf.reshape(1, d))


def _t5_bucket(dist):
    max_exact = REL_BUCKETS // 2
    d = jnp.maximum(dist, 0)
    dl = jnp.maximum(d, max_exact).astype(F32)
    large = max_exact + (jnp.log(dl / max_exact) / math.log(REL_MAX_DIST / max_exact)
                         * (REL_BUCKETS - max_exact)).astype(I32)
    large = jnp.minimum(large, REL_BUCKETS - 1)
    return jnp.where(d < max_exact, d, large)


def _swap_halves(w):
    half = w.shape[-1] // 2
    return jnp.concatenate([w[..., half:], w[..., :half]], axis=-1)


def _in_proj_weights(w_in, d_model):
    splits = (A_HEADS * A_HEAD_DIM, A_HEAD_DIM, A_HEAD_DIM, IDX_HEADS * IDX_DIM, IDX_DIM, IDX_HEADS,
              B_HEADS * B_KEY_DIM, B_HEADS * B_KEY_DIM, B_HEADS * B_VAL_DIM, B_HEADS * B_VAL_DIM,
              C_Q_RANK, C_KV_RANK, C_ROPE, N_BRANCH * d_model)
    st = [int(v) for v in np.concatenate([[0], np.cumsum(splits)])]
    depth, k, n_cols = w_in.shape
    reps = LANES // C_ROPE
    rows = 128

    def regroup_kernel(w_ref, a_ref, b_ref, c_ref, d_ref, g_ref):
        col = lambda a, b: w_ref[:, st[a]:st[b]]
        ik, pe = col(4, 5), col(12, 13)
        iw_pad = jnp.zeros((rows, LANES - IDX_HEADS), w_ref.dtype)
        a_ref[...] = col(0, 4).astype(BF16)
        b_ref[...] = jnp.concatenate([ik, ik, col(5, 6), iw_pad], axis=1).astype(BF16)
        c_ref[...] = col(6, 12).astype(BF16)
        d_ref[...] = jnp.concatenate([pe] * reps + [_swap_halves(pe)] * reps, axis=1).astype(BF16)
        g_ref[...] = col(13, 14).astype(BF16)

    out_w = (st[4], 2 * LANES, st[12] - st[6], 2 * LANES, st[14] - st[13])
    outs = pl.pallas_call(
        regroup_kernel,
        grid=(depth, k // rows),
        in_specs=[pl.BlockSpec((None, rows, n_cols), lambda l, i: (l, i, 0))],
        out_specs=[pl.BlockSpec((None, rows, wd), lambda l, i: (l, i, 0)) for wd in out_w],
        out_shape=[jax.ShapeDtypeStruct((depth, k, wd), BF16) for wd in out_w],
        compiler_params=_params("parallel", "parallel"),
        name="regroup_w_in",
    )(w_in.astype(BF16))
    ws = list(outs)
    widths = [splits[:4], (LANES, LANES), splits[6:12], (2 * LANES,)]
    dtypes = [BF16] * 5 + [F32] * 8
    return ws[:4], widths, dtypes, ws[4]


def kernel(x, positions, w_in, w_up_a, w_up_b, w_up_c, w_out, mla_q_norm, mla_w_qb, mla_kv_norm,
           mla_w_kvb, hgrn_lb_logits, hgrn_out_norm, rel_bias, attn_norm, ffn_norm, w_ffn_gate,
           w_ffn_up, w_ffn_down, final_norm):
    bsz, s_len, d_model = x.shape
    depth = w_in.shape[0]
    n = bsz * s_len
    t = LANES

    w_proj, widths, dtypes, w_gate = _in_proj_weights(w_in, d_model)
    wq = mla_w_qb.reshape(depth, C_Q_RANK, C_HEADS, C_NOPE + C_ROPE)
    q_pe = wq[..., C_NOPE:]
    w_qb = jnp.concatenate([wq[..., :C_NOPE].reshape(depth, C_Q_RANK, -1),
                            q_pe.reshape(depth, C_Q_RANK, -1),
                            _swap_halves(q_pe).reshape(depth, C_Q_RANK, -1)], axis=2).astype(BF16)
    wkv = mla_w_kvb.reshape(depth, C_KV_RANK, C_HEADS, C_NOPE + C_V)
    w_kvb = jnp.concatenate([wkv[..., :C_NOPE].reshape(depth, C_KV_RANK, -1),
                             wkv[..., C_NOPE:].reshape(depth, C_KV_RANK, -1)], axis=2).astype(BF16)
    bf = lambda w: w.astype(BF16)
    w_up_a, w_up_b, w_up_c, w_out = bf(w_up_a), bf(w_up_b), bf(w_up_c), bf(w_out)
    w_ffn_gate, w_ffn_up, w_ffn_down = bf(w_ffn_gate), bf(w_ffn_up), bf(w_ffn_down)

    p_lb = jax.nn.softmax(hgrn_lb_logits.astype(F32), axis=0)
    lower_bounds = jnp.cumsum(p_lb, axis=0) - p_lb[0:1]
    inv_freq = ROPE_THETA ** (-jnp.arange(0, C_ROPE, 2, dtype=F32) / C_ROPE)
    ang = positions.astype(F32)[..., None] * inv_freq
    cos, sin = jnp.cos(ang), jnp.sin(ang)
    reps = LANES // C_ROPE
    cos_t = jnp.tile(jnp.concatenate([cos, cos], axis=-1), (1, 1, reps)).reshape(n, LANES)
    sin_t = jnp.tile(jnp.concatenate([-sin, sin], axis=-1), (1, 1, reps)).reshape(n, LANES)
    q_idx = jnp.arange(t, dtype=I32)[None, :]
    k_idx = jnp.arange(t, dtype=I32)[:, None]
    def tab(dist):
        onehot = jax.nn.one_hot(_t5_bucket(dist), REL_BUCKETS, dtype=F32)
        return jnp.einsum("kqb,bh->khq", onehot, rel_bias.astype(F32),
                          precision=lax.Precision.HIGHEST).reshape(t, A_HEADS * t)
    far = jnp.broadcast_to(rel_bias[REL_BUCKETS - 1].astype(F32)[None, :, None],
                           (t, A_HEADS, t)).reshape(t, A_HEADS * t)
    bias_tab = (jnp.stack([tab(q_idx + t - k_idx), tab(q_idx - k_idx), far]) - far) * math.log2(math.e)

    tm = min(512, n)
    t_mla = min(256, s_len)
    x2 = x.reshape(n, d_model)
    r3 = lambda a: a.reshape(bsz, s_len, a.shape[-1])
    per_seq = lambda a: a.reshape(bsz, a.shape[0] // bsz, a.shape[1], a.shape[2])
    key_tiles = [0, 0, t] + [0] * 10
    for l in range(depth):
        mla = (cos_t, sin_t, mla_q_norm[l], mla_kv_norm[l], w_qb, w_kvb, t_mla)
        (qa, ka, vta, iq, ik2, iw, bq, bfr, bi, bg, qn, qpe, kn, vtc, kpe) = _norm_matmul(
            x2, attn_norm[l], w_proj, l, widths, dtypes, key_tiles, mla, tm)
        o_a = _dsa(r3(qa), r3(iq), r3(iw), r3(ka), per_seq(vta), r3(ik2), bias_tab)
        o_b = _hgrn(r3(bq), r3(bfr), r3(bi), r3(bg), lower_bounds[l], hgrn_out_norm[l],
                    sb=min(512, s_len))
        o_c = _mla(r3(qn), r3(qpe), r3(kn), r3(kpe), per_seq(vtc), t_mla)
        x2 = _merge(x2, attn_norm[l], w_gate, o_a.reshape(n, -1), o_b.reshape(n, -1),
                    o_c.reshape(n, -1), w_up_a, w_up_b, w_up_c, w_out, l, tm)
        x2 = _ffn(x2, ffn_norm[l], w_ffn_gate, w_ffn_up, w_ffn_down, final_norm,
                  l, tm, final_norm=(l == depth - 1))
    return x2.reshape(bsz, s_len, d_model)
```

```python
import functools
import math

import jax
import jax.numpy as jnp
import numpy as np
from jax import lax
from jax.experimental import pallas as pl
from jax.experimental.pallas import tpu as pltpu

F32 = jnp.float32
BF16 = jnp.bfloat16
I32 = jnp.int32

A_HEADS = 4
A_HEAD_DIM = 128
IDX_HEADS = 8
IDX_DIM = 64
TOPK_MAX = 256
B_HEADS = 4
B_KEY_DIM = 128
B_VAL_DIM = 128
B_CHUNK = 64
C_HEADS = 4
C_Q_RANK = 384
C_KV_RANK = 256
C_NOPE = 128
C_ROPE = 64
C_V = 128
ROPE_THETA = 10000.0
REL_BUCKETS = 32
REL_MAX_DIST = 128
N_BRANCH = 3
EPS = 1e-6
NEG_BIG = -1e30
LB_FLOOR = 1e-30

LANES = 128
SUBLANES = 8
VMEM_LIMIT = 56 * 1024 * 1024
INT_MIN = np.int32(-2 ** 31)
SORT_ROWS = LANES // SUBLANES


def _oddeven_merge_sort(n):
    pairs = []
    p = 1
    while p < n:
        k = p
        while k >= 1:
            for j in range(k % p, n - k, 2 * k):
                for i in range(min(k, n - j - k)):
                    if (i + j) // (2 * p) == (i + j + k) // (2 * p):
                        pairs.append((i + j, i + j + k))
            k //= 2
        p *= 2
    return tuple(pairs)


_SORT16 = _oddeven_merge_sort(SORT_ROWS)

_NT = (((1,), (1,)), ((), ()))


def _params(*sem):
    return pltpu.CompilerParams(dimension_semantics=sem, vmem_limit_bytes=VMEM_LIMIT)


def _rms(x, g):
    return x * lax.rsqrt(jnp.mean(x * x, axis=-1, keepdims=True) + EPS) * g


def _const_spec(shape):
    nd = len(shape)
    return pl.BlockSpec(shape, lambda *_: (0,) * nd)


def _layer_spec(w, layer):
    return pl.BlockSpec((None,) + w.shape[1:], lambda *_: (layer, 0, 0))


def _fold8(x, op):
    r, c = x.shape
    return op(x.reshape(r // SUBLANES, SUBLANES, c), axis=0)


def _store_key_major(o_ref, rows, tile):
    for u in range(rows.shape[0] // tile):
        o_ref[u] = rows[u * tile:(u + 1) * tile, :].T.astype(o_ref.dtype)


def _mla_up_project(cq, ckv, kpe_raw, cos, sin, gq, gkv, wq_ref, wkv_ref,
                    qn_ref, qpe_ref, kn_ref, vt_ref, kpe_ref, key_tile):
    nw = C_HEADS * C_NOPE
    pw = C_HEADS * C_ROPE
    cos2 = jnp.concatenate([cos] * (pw // LANES), axis=1)
    sin2 = jnp.concatenate([sin] * (pw // LANES), axis=1)
    q = jnp.dot(_rms(cq, gq).astype(BF16), wq_ref[...], preferred_element_type=F32)
    qn_ref[...] = q[:, :nw].astype(qn_ref.dtype)
    qpe_ref[...] = (q[:, nw:nw + pw] * cos2 + q[:, nw + pw:] * sin2).astype(qpe_ref.dtype)
    kv = jnp.dot(_rms(ckv, gkv).astype(BF16), wkv_ref[...], preferred_element_type=F32)
    kn_ref[...] = kv[:, :nw].astype(kn_ref.dtype)
    _store_key_major(vt_ref, kv[:, nw:], key_tile)
    kpe_ref[...] = (kpe_raw[:, :LANES] * cos + kpe_raw[:, LANES:] * sin).astype(kpe_ref.dtype)


N_LATENT = 3
N_MLA_IN = 6
N_MLA_OUT = 5


def _norm_matmul_kernel(x_ref, g_ref, *refs, widths, key_tiles, mla_key_tile, col_chunk):
    n_w = len(widths)
    w_refs, mla_in = refs[:n_w], refs[n_w:n_w + N_MLA_IN]
    out_refs, mla_out = refs[n_w + N_MLA_IN:-N_MLA_OUT], refs[-N_MLA_OUT:]
    h = _rms(x_ref[...], g_ref[...]).astype(BF16)
    k = 0
    latents = []
    for w_ref, w_widths in zip(w_refs, widths):
        c0 = 0
        for w in w_widths:
            if k >= len(out_refs):
                latents.append(jnp.dot(h, w_ref[:, c0:c0 + w], preferred_element_type=F32))
            elif key_tiles[k]:
                _store_key_major(out_refs[k], jnp.dot(h, w_ref[:, c0:c0 + w], preferred_element_type=F32),
                                 key_tiles[k])
            else:
                for s in range(0, w, col_chunk):
                    e = min(s + col_chunk, w)
                    out_refs[k][:, s:e] = jnp.dot(h, w_ref[:, c0 + s:c0 + e],
                                                  preferred_element_type=F32).astype(out_refs[k].dtype)
            c0 += w
            k += 1
    cos_ref, sin_ref, gq_ref, gkv_ref, wq_ref, wkv_ref = mla_in
    _mla_up_project(*latents, cos_ref[...], sin_ref[...], gq_ref[...], gkv_ref[...], wq_ref, wkv_ref,
                    *mla_out, mla_key_tile)


def _norm_matmul(x, g, ws, layer, widths, dtypes, key_tiles, mla, tm):
    cos_t, sin_t, gq, gkv, wq, wkv, mla_tile = mla
    n, k = x.shape
    flat = [wd for w_widths in widths for wd in w_widths]
    assert n % tm == 0 and all(w.shape[1:] == (k, sum(ww)) for w, ww in zip(ws, widths))
    kern = functools.partial(_norm_matmul_kernel, widths=tuple(tuple(ww) for ww in widths),
                             key_tiles=tuple(key_tiles), mla_key_tile=mla_tile, col_chunk=512)
    row = lambda w: pl.BlockSpec((tm, w), lambda i: (i, 0))
    tiles = lambda w, kt: pl.BlockSpec((tm // kt, w, kt), lambda i: (i, 0, 0))
    specs, shapes = [], []
    for wd, dt, kt in list(zip(flat, dtypes, key_tiles))[:-N_LATENT]:
        specs.append(tiles(wd, kt) if kt else row(wd))
        shapes.append(jax.ShapeDtypeStruct((n // kt, wd, kt) if kt else (n, wd), dt))
    nw, vw = C_HEADS * C_NOPE, C_HEADS * C_V
    for wd in (nw, C_HEADS * C_ROPE, nw, None, LANES):
        specs.append(tiles(vw, mla_tile) if wd is None else row(wd))
        shapes.append(jax.ShapeDtypeStruct((n // mla_tile, vw, mla_tile) if wd is None else (n, wd), BF16))
    return pl.pallas_call(
        kern,
        grid=(n // tm,),
        in_specs=[row(k), _const_spec((1, k))] + [_layer_spec(w, layer) for w in ws]
                 + [row(LANES), row(LANES), _const_spec((1, C_Q_RANK)), _const_spec((1, C_KV_RANK)),
                    _layer_spec(wq, layer), _layer_spec(wkv, layer)],
        out_specs=specs,
        out_shape=shapes,
        compiler_params=_params("parallel"),
        name="norm_proj",
    )(x, g.reshape(1, k), *ws, cos_t, sin_t, gq.reshape(1, -1), gkv.reshape(1, -1), wq, wkv)


def _dsa_kernel(qa_ref, iq_ref, iw_ref, ka_ref, vt_ref, ik_ref, bias_ref, o_ref,
                keys_ref, sorted_ref, s_ref, qm_ref, qs_ref, wt_ref, thr_ref, ties_ref,
                *, topk, n_blocks, group, scale, idx_scale):
    t = LANES
    gt = group * t
    i = pl.program_id(1)
    n_groups = (i + group) // group
    max_groups = n_blocks // group
    krow = lax.broadcasted_iota(I32, (t, t), 0)
    qcol = lax.broadcasted_iota(I32, (t, t), 1)
    lane_lo = qcol < IDX_DIM

    for h in range(IDX_HEADS):
        pair = iq_ref[0, :, (h // 2) * t:(h // 2 + 1) * t]
        keep = lane_lo if h % 2 == 0 else jnp.logical_not(lane_lo)
        qm_ref[h * t:(h + 1) * t, :] = jnp.where(keep, pair, jnp.zeros_like(pair))
    for h in range(A_HEADS):
        qs_ref[h * t:(h + 1) * t, :] = qa_ref[0, :, h * t:(h + 1) * t]
    wt_ref[...] = iw_ref[0].T

    def index_logits(g):
        return lax.dot_general(ik_ref[0, g * gt:(g + 1) * gt, :], qm_ref[...], _NT,
                               preferred_element_type=F32)

    def score_group(g, lg, ng):
        for u in range(group):
            j = g * group + u
            acc = jnp.zeros((t, t), F32)
            for h in range(IDX_HEADS):
                acc = acc + jnp.maximum(lg[u * t:(u + 1) * t, h * t:(h + 1) * t], 0.0) * wt_ref[h:h + 1, :]
            score = acc * idx_scale
            if j >= (ng - 1) * group:
                score = jnp.where(krow + j * t <= qcol + i * t, score, NEG_BIG)
            score = jnp.where(score == 0.0, 0.0, score)
            bits = pltpu.bitcast(score, I32)
            key = bits ^ ((bits >> 31) & np.int32(0x7FFFFFFF))
            keys_ref[j] = key
            rows = [key[r * SUBLANES:(r + 1) * SUBLANES, :] for r in range(SORT_ROWS)]
            for a, b in _SORT16:
                rows[a], rows[b] = jnp.maximum(rows[a], rows[b]), jnp.minimum(rows[a], rows[b])
            for r in range(SORT_ROWS):
                sorted_ref[j, r] = rows[r]

    def score_all(ng):
        lg = index_logits(0)
        for g in range(ng):
            nxt = index_logits(g + 1) if g + 1 < ng else None
            score_group(g, lg, ng)
            lg = nxt

    for v in range(1, max_groups + 1):
        pl.when(n_groups == v)(functools.partial(score_all, v))

    thr_ref[...] = jnp.full(thr_ref.shape, INT_MIN, I32)
    ties_ref[...] = jnp.zeros(ties_ref.shape, F32)

    def count(n, pred):
        acc = [jnp.zeros((SUBLANES, t), F32) for _ in range(5)]
        for j in range(n):
            v = [sorted_ref[j, r] for r in range(SORT_ROWS)]
            m1 = pred(v[7])
            m2 = pred(jnp.where(m1, v[11], v[3]))
            m3 = pred(jnp.where(m1, jnp.where(m2, v[13], v[9]), jnp.where(m2, v[5], v[1])))
            m4 = pred(jnp.where(m1, jnp.where(m2, jnp.where(m3, v[14], v[12]), jnp.where(m3, v[10], v[8])),
                                jnp.where(m2, jnp.where(m3, v[6], v[4]), jnp.where(m3, v[2], v[0]))))
            m5 = pred(v[15])
            acc = [a + jnp.where(m, 1.0, 0.0) for a, m in zip(acc, (m1, m2, m3, m4, m5))]
        cnt = 8.0 * acc[0] + 4.0 * acc[1] + 2.0 * acc[2] + acc[3] + acc[4]
        return jnp.sum(cnt, axis=0, keepdims=True)

    def search(n):
        def search_pass(b, thr):
            cand = thr + lax.shift_left(np.int32(1), 31 - b)
            return jnp.where(count(n, lambda key: key >= cand) >= topk, cand, thr)

        thr = lax.fori_loop(0, 32, search_pass, jnp.full((1, t), INT_MIN, I32))
        thr_ref[...] = jnp.broadcast_to(thr, thr_ref.shape)
        ties_ref[...] = jnp.broadcast_to(topk - count(n, lambda key: key > thr), ties_ref.shape)

    for c in range(n_blocks):
        if (c + 1) * t > topk:
            pl.when(i == c)(functools.partial(search, c + 1))

    thr = thr_ref[0:1, :]
    n_ties = ties_ref[0:1, :]

    r2 = lax.broadcasted_iota(I32, (2 * t, t), 0)
    c2 = lax.broadcasted_iota(I32, (2 * t, t), 1)
    tie_lhs = jnp.where(jnp.logical_or(r2 >= t, c2 < r2), 1.0, 0.0).astype(BF16)

    def sweep_a_mxu(g):
        keys = [keys_ref[g * group + u] for u in range(group)]
        eqs = [key == thr for key in keys]
        eq_all = jnp.concatenate([jnp.where(eq, 1.0, 0.0).astype(BF16) for eq in eqs], axis=1)
        pref = jnp.dot(tie_lhs, eq_all, preferred_element_type=F32)
        s_grp = lax.dot_general(ka_ref[0, g * gt:(g + 1) * gt, :], qs_ref[...], _NT,
                                preferred_element_type=F32)
        return keys, eqs, pref, s_grp

    def sweep_a_vpu(g, operands, seen, mx, ng):
        keys, eqs, pref, s_grp = operands
        for u in range(group):
            j = g * group + u
            us = slice(u * t, (u + 1) * t)
            rank = jnp.where(keys[u] > thr, -1.0, jnp.where(eqs[u], seen + pref[:t, us], 3e38))
            if j >= (ng - 1) * group:
                rank = jnp.where(krow + j * t <= qcol + i * t, rank, 3e38)
            valid = rank < n_ties
            which = jnp.where(j == i, 1, jnp.where(j == i - 1, 0, 2))
            s_all = s_grp[us, :] * (scale * math.log2(math.e))
            if j > (ng - 1) * group - 2:
                s_all = s_all + bias_ref[which]
            for h in range(A_HEADS):
                s_h = jnp.where(valid, s_all[:, h * t:(h + 1) * t], NEG_BIG)
                s_ref[j, :, h * t:(h + 1) * t] = s_h
                mx[h] = jnp.maximum(mx[h], _fold8(s_h, jnp.max))
            seen = seen + pref[t:t + 1, us]
        return seen, mx

    def sweep_b(g, m_all, l8):
        ps = []
        for u in range(group):
            p = jnp.exp2(s_ref[g * group + u] - m_all)
            l8 = l8 + _fold8(p, jnp.sum)
            ps.append(p.astype(BF16))
        vt_grp = jnp.concatenate([vt_ref[0, g * group + u] for u in range(group)], axis=1)
        return l8, jnp.dot(vt_grp, jnp.concatenate(ps, axis=0), preferred_element_type=F32)

    def attend(ng):
        seen = jnp.zeros((1, t), F32)
        mx = [jnp.full((SUBLANES, t), NEG_BIG, F32) for _ in range(A_HEADS)]
        operands = sweep_a_mxu(0)
        for g in range(ng):
            nxt = sweep_a_mxu(g + 1) if g + 1 < ng else None
            seen, mx = sweep_a_vpu(g, operands, seen, mx, ng)
            operands = nxt
        m_all = jnp.concatenate([jnp.max(m, axis=0, keepdims=True) for m in mx], axis=1)
        l8 = jnp.zeros((SUBLANES, A_HEADS * t), F32)
        acc = None
        for g in range(ng):
            l8, pv = sweep_b(g, m_all, l8)
            acc = pv if acc is None else acc + pv
        out = acc * (1.0 / jnp.sum(l8, axis=0, keepdims=True))
        for h in range(A_HEADS):
            o_ref[0, :, h * t:(h + 1) * t] = out[:, h * t:(h + 1) * t].T.astype(o_ref.dtype)

    for v in range(1, max_groups + 1):
        pl.when(n_groups == v)(functools.partial(attend, v))


def _dsa(qa, iq, iw, ka, vt, ik2, bias_tab):
    b, s, _ = qa.shape
    t = LANES
    nb = s // t
    group = math.gcd(nb, 2)
    topk = min(TOPK_MAX, s // 4)
    kern = functools.partial(_dsa_kernel, topk=float(topk), n_blocks=nb, group=group,
                             scale=A_HEAD_DIM ** -0.5,
                             idx_scale=(IDX_DIM ** -0.5) * (IDX_HEADS ** -0.5))
    qspec = lambda w: pl.BlockSpec((1, t, w), lambda bi, i: (bi, i, 0))
    kspec = pl.BlockSpec((1, s, t), lambda bi, i: (bi, 0, 0))
    return pl.pallas_call(
        kern,
        grid=(b, nb),
        in_specs=[qspec(A_HEADS * t), qspec(IDX_HEADS * IDX_DIM), qspec(t), kspec,
                  pl.BlockSpec((1, nb, t, t), lambda bi, i: (bi, 0, 0, 0)), kspec,
                  _const_spec(bias_tab.shape)],
        out_specs=qspec(A_HEADS * t),
        out_shape=jax.ShapeDtypeStruct((b, s, A_HEADS * t), BF16),
        scratch_shapes=[pltpu.VMEM((nb, t, t), I32),
                        pltpu.VMEM((nb, SORT_ROWS, SUBLANES, t), I32),
                        pltpu.VMEM((nb, t, A_HEADS * t), F32),
                        pltpu.VMEM((IDX_HEADS * t, t), BF16),
                        pltpu.VMEM((A_HEADS * t, t), BF16),
                        pltpu.VMEM((t, t), F32),
                        pltpu.VMEM((SUBLANES, t), I32),
                        pltpu.VMEM((SUBLANES, t), F32)],
        compiler_params=_params("parallel", "parallel"),
        name="dsa_attention",
    )(qa, iq, iw, ka, vt, ik2, bias_tab)


def _hgrn_kernel(q_ref, f_ref, i_ref, g_ref, lb_ref, gain_ref, o_ref, state_ref, *, n_chunks, n_seq):
    c = B_CHUNK
    rc = n_seq * c
    kd, vd = B_KEY_DIM, B_VAL_DIM
    w = B_HEADS * kd

    @pl.when(pl.program_id(1) == 0)
    def _():
        state_ref[...] = jnp.zeros_like(state_ref)

    row = lax.broadcasted_iota(I32, (rc, rc), 0)
    col = lax.broadcasted_iota(I32, (rc, rc), 1)
    same_seq = (row >= col) if n_seq == 1 else jnp.logical_and(
        row >= col, lax.shift_right_logical(row, 6) == lax.shift_right_logical(col, 6))
    causal = same_seq
    assert c == 64
    tril = jnp.where(causal, 1.0, 0.0).astype(BF16)
    seqs = [slice(b * c, (b + 1) * c) for b in range(n_seq)]
    seq_of_row = lax.shift_right_logical(lax.broadcasted_iota(I32, (rc, w), 0), 6)

    def per_seq_row(x, r):
        return jnp.concatenate([jnp.broadcast_to(x[b * c + r:b * c + r + 1, :], (c, x.shape[1]))
                                for b in range(n_seq)], axis=0)

    def stacked(ref, rows):
        return jnp.concatenate([ref[b, rows, :] for b in range(n_seq)], axis=0)
    q_scale = kd ** -0.5
    lb = lb_ref[...]
    lb_floor = jnp.maximum(lb, LB_FLOOR)
    one_m_lb = 1.0 - lb
    heads = [slice(h * kd, (h + 1) * kd) for h in range(B_HEADS)]

    def stage_decay(ci):
        rows = slice(ci * c, (ci + 1) * c)
        fr = stacked(f_ref, rows)
        z = jnp.exp(-jnp.abs(fr))
        r = 1.0 / (1.0 + z)
        sig_pos = jnp.where(fr >= 0, r, z * r)
        sig_neg = jnp.where(fr >= 0, z * r, r)
        log_f = jnp.log(lb_floor + one_m_lb * sig_pos)
        k_in = one_m_lb * sig_neg
        hi = log_f.astype(BF16)
        rest = log_f - hi.astype(F32)
        mid = rest.astype(BF16)
        lo = (rest - mid.astype(F32)).astype(BF16)
        cs = jnp.dot(tril, jnp.concatenate([hi, mid, lo], axis=1), preferred_element_type=F32)
        return k_in, cs

    def apply_update(pending):
        if pending is not None:
            d_last, st, upd = pending
            for b in range(n_seq):
                for h, hs in enumerate(heads):
                    state_ref[b, h] = d_last[b * c:b * c + 1, hs] * st[b][h] + upd[b][h]

    def stage_scores(ci, decay, pending):
        rows = slice(ci * c, (ci + 1) * c)
        k_in, cs = decay
        bsum = cs[:, :w] + cs[:, w:2 * w] + cs[:, 2 * w:]
        b_mid = per_seq_row(bsum, c // 2 - 1)
        b_last = per_seq_row(bsum, c - 1)
        qs = stacked(q_ref, rows) * q_scale
        v = stacked(i_ref, rows)
        v_b = v.astype(BF16)
        v_t = v.T.astype(BF16)
        q_mid = (qs * jnp.exp(bsum - b_mid)).astype(BF16)
        k_mid = (k_in * jnp.exp(b_mid - bsum)).astype(BF16)
        q_dec = (qs * jnp.exp(bsum)).astype(BF16)
        k_end = (k_in * jnp.exp(b_last - bsum)).astype(BF16)
        d_last = jnp.exp(b_last)
        apply_update(pending)
        st = [[state_ref[b, h] for h in range(B_HEADS)] for b in range(n_seq)]
        attn = [lax.dot_general(q_mid[:, hs], k_mid[:, hs], _NT, preferred_element_type=F32)
                for hs in heads]
        inter = [jnp.concatenate(
            [lax.dot_general(q_dec[sq, hs], st[b][h].astype(BF16), _NT, preferred_element_type=F32)
             for b, sq in enumerate(seqs)], axis=0) for h, hs in enumerate(heads)]
        k_seq = [k_end if n_seq == 1 else jnp.where(seq_of_row == b, k_end, jnp.zeros_like(k_end))
                 for b in range(n_seq)]
        upd = [[jnp.dot(v_t[hs, :], k_seq[b][:, hs], preferred_element_type=F32) for hs in heads]
               for b in range(n_seq)]
        return (attn, inter, v_b), (d_last, st, upd)

    def stage_output(ci, scores):
        rows = slice(ci * c, (ci + 1) * c)
        attn, inter, v_b = scores
        attn = [jnp.where(causal, a, 0.0).astype(BF16) for a in attn]
        outs = [jnp.dot(attn[h], v_b[:, hs], preferred_element_type=F32) + inter[h]
                for h, hs in enumerate(heads)]
        o = jnp.concatenate([_rms(o_h, gain_ref[...]) for o_h in outs], axis=1)
        g = stacked(g_ref, rows)
        o = (o * (g * (1.0 / (1.0 + jnp.exp(-g))))).astype(o_ref.dtype)
        for b, sq in enumerate(seqs):
            o_ref[b, rows, :] = o[sq, :]

    decay, scores, pending = {}, {}, None
    for step in range(n_chunks + 2):
        if step < n_chunks:
            decay[step] = stage_decay(step)
        if 0 <= step - 1 < n_chunks:
            scores[step - 1], pending = stage_scores(step - 1, decay.pop(step - 1), pending)
        if 0 <= step - 2 < n_chunks:
            stage_output(step - 2, scores.pop(step - 2))
    apply_update(pending)


def _hgrn(bq, bf, bi, bg, lb, gain, sb):
    b, s, w = bq.shape
    assert s % sb == 0 and sb % B_CHUNK == 0
    n_seq = 2 if b % 2 == 0 else 1
    kern = functools.partial(_hgrn_kernel, n_chunks=sb // B_CHUNK, n_seq=n_seq)
    spec = pl.BlockSpec((n_seq, sb, w), lambda bi_, si: (bi_, si, 0))
    return pl.pallas_call(
        kern,
        grid=(b // n_seq, s // sb),
        in_specs=[spec, spec, spec, spec, _const_spec((1, w)), _const_spec((1, B_VAL_DIM))],
        out_specs=spec,
        out_shape=jax.ShapeDtypeStruct((b, s, w), BF16),
        scratch_shapes=[pltpu.VMEM((n_seq, B_HEADS, B_VAL_DIM, B_KEY_DIM), F32)],
        compiler_params=_params("parallel", "arbitrary"),
        name="hgrn2",
    )(bq, bf, bi, bg, lb.reshape(1, w), gain.reshape(1, B_VAL_DIM))


def _mla_kernel(qn_ref, qpe_ref, kn_ref, kpe_ref, vt_ref, o_ref, q_ref, acc_ref, *, t, n_tiles, scale):
    i = pl.program_id(1)
    lane_lo = lax.broadcasted_iota(I32, (t, LANES), 1) < C_ROPE
    for h in range(C_HEADS):
        pair = qpe_ref[0, :, (h // 2) * LANES:(h // 2 + 1) * LANES]
        keep = lane_lo if h % 2 == 0 else jnp.logical_not(lane_lo)
        q_ref[h, :, :C_NOPE] = qn_ref[0, :, h * C_NOPE:(h + 1) * C_NOPE]
        q_ref[h, :, C_NOPE:] = jnp.where(keep, pair, jnp.zeros_like(pair))
    acc_ref[...] = jnp.zeros_like(acc_ref)

    def logits_of(step):
        first, count = step
        rows = slice(first * t, (first + count) * t)
        kpe_t = kpe_ref[0, rows, :]
        return [lax.dot_general(
            jnp.concatenate([kn_ref[0, rows, h * C_NOPE:(h + 1) * C_NOPE], kpe_t], axis=1),
            q_ref[h], _NT, preferred_element_type=F32) for h in range(C_HEADS)]

    def softmax_pv(step, block, logits, ms, ls, masked):
        first, count = step
        new_m, new_l = [], []
        for h in range(C_HEADS):
            s = logits[h] * (scale * math.log2(math.e))
            if masked:
                key_pos = lax.broadcasted_iota(I32, (count * t, t), 0) + first * t
                query_pos = lax.broadcasted_iota(I32, (count * t, t), 1) + block * t
                s = jnp.where(key_pos <= query_pos, s, NEG_BIG)
            m_new = jnp.maximum(ms[h], jnp.max(s, axis=0, keepdims=True))
            alpha = jnp.exp2(ms[h] - m_new)
            p = jnp.exp2(s - m_new)
            new_l.append(alpha * ls[h] + jnp.sum(p, axis=0, keepdims=True))
            new_m.append(m_new)
            hs = slice(h * C_V, (h + 1) * C_V)
            tiles = [vt_ref[0, first + u, hs, :] for u in range(count)]
            v_t = tiles[0] if count == 1 else jnp.concatenate(tiles, axis=1)
            acc_ref[h] = alpha * acc_ref[h] + jnp.dot(v_t, p.astype(BF16), preferred_element_type=F32)
        return new_m, new_l

    def run(block):
        steps = [(2 * g, 2) for g in range(block // 2)] + [(block - block % 2, 1 + block % 2)]
        ms = [jnp.full((1, t), NEG_BIG, F32) for _ in range(C_HEADS)]
        ls = [jnp.zeros((1, t), F32) for _ in range(C_HEADS)]
        logits = logits_of(steps[0])
        for k, step in enumerate(steps):
            nxt = logits_of(steps[k + 1]) if k + 1 < len(steps) else None
            ms, ls = softmax_pv(step, block, logits, ms, ls, masked=(k + 1 == len(steps)))
            logits = nxt
        for h in range(C_HEADS):
            o_ref[0, :, h * C_V:(h + 1) * C_V] = (acc_ref[h] * (1.0 / ls[h])).T.astype(o_ref.dtype)

    for v in range(n_tiles):
        pl.when(i == v)(functools.partial(run, v))


def _mla(qn, qpe, kn, kpe, vt, t):
    b, s, _ = qn.shape
    nt = s // t
    assert nt % 2 == 0
    kern = functools.partial(_mla_kernel, t=t, n_tiles=nt, scale=(C_NOPE + C_ROPE) ** -0.5)
    qspec = lambda w: pl.BlockSpec((1, t, w), lambda bi, i: (bi, i, 0))
    kspec = lambda w: pl.BlockSpec((1, s, w), lambda bi, i: (bi, 0, 0))
    return pl.pallas_call(
        kern,
        grid=(b, nt),
        in_specs=[qspec(qn.shape[2]), qspec(qpe.shape[2]), kspec(kn.shape[2]), kspec(kpe.shape[2]),
                  pl.BlockSpec((1, nt, C_HEADS * C_V, t), lambda bi, i: (bi, 0, 0, 0))],
        out_specs=qspec(C_HEADS * C_V),
        out_shape=jax.ShapeDtypeStruct((b, s, C_HEADS * C_V), BF16),
        scratch_shapes=[pltpu.VMEM((C_HEADS, t, C_NOPE + LANES), BF16),
                        pltpu.VMEM((C_HEADS, C_V, t), F32)],
        compiler_params=_params("parallel", "parallel"),
        name="mla_attention",
    )(qn, qpe, kn, kpe, vt)


def _merge_kernel(x_ref, g_ref, wg_ref, oa_ref, ob_ref, oc_ref, wa_ref, wb_ref, wc_ref, wo_ref,
                  out_ref, mixed_ref, *, col_chunk):
    x = x_ref[...]
    d = x.shape[1]
    h = _rms(x, g_ref[...]).astype(BF16)
    branches = ((oa_ref, wa_ref), (ob_ref, wb_ref), (oc_ref, wc_ref))
    for s in range(0, d, col_chunk):
        cs = slice(s, s + col_chunk)
        mixed = None
        for bidx, (o_ref, w_ref) in enumerate(branches):
            logits = jnp.dot(h, wg_ref[:, bidx * d + s:bidx * d + s + col_chunk],
                             preferred_element_type=F32)
            gate = 1.0 / (1.0 + jnp.exp(-logits))
            term = gate * jnp.dot(o_ref[...], w_ref[:, cs], preferred_element_type=F32)
            mixed = term if mixed is None else mixed + term
        mixed_ref[:, cs] = mixed.astype(BF16)
    out_ref[...] = x + jnp.dot(mixed_ref[...], wo_ref[...], preferred_element_type=F32)


def _merge(x, g, w_gate, oa, ob, oc, wa, wb, wc, wo, layer, tm):
    n, d = x.shape
    row = lambda w: pl.BlockSpec((tm, w), lambda i: (i, 0))
    kern = functools.partial(_merge_kernel, col_chunk=256)
    return pl.pallas_call(
        kern,
        grid=(n // tm,),
        in_specs=[row(d), _const_spec((1, d)), _layer_spec(w_gate, layer),
                  row(oa.shape[1]), row(ob.shape[1]), row(oc.shape[1]),
                  _layer_spec(wa, layer), _layer_spec(wb, layer), _layer_spec(wc, layer),
                  _layer_spec(wo, layer)],
        out_specs=row(d),
        out_shape=jax.ShapeDtypeStruct((n, d), F32),
        scratch_shapes=[pltpu.VMEM((tm, d), BF16)],
        compiler_params=_params("parallel"),
        name="merge_out_proj",
    )(x, g.reshape(1, d), w_gate, oa, ob, oc, wa, wb, wc, wo)


def _ffn_kernel(x_ref, g_ref, wg_ref, wu_ref, wd_ref, gf_ref, out_ref, act_ref, *, col_chunk,
                final_norm):
    x = x_ref[...]
    h = _rms(x, g_ref[...]).astype(BF16)
    dff = wg_ref.shape[1]
    for s in range(0, dff, col_chunk):
        cs = slice(s, s + col_chunk)
        gate = jnp.dot(h, wg_ref[:, cs], preferred_element_type=F32)
        up = jnp.dot(h, wu_ref[:, cs], preferred_element_type=F32)
        act_ref[:, cs] = (gate * (1.0 / (1.0 + jnp.exp(-gate))) * up).astype(BF16)
    y = x + jnp.dot(act_ref[...], wd_ref[...], preferred_element_type=F32)
    if final_norm:
        y = _rms(y, gf_ref[...])
    out_ref[...] = y


def _ffn(x, g, wg, wu, wd, gf, layer, tm, final_norm):
    n, d = x.shape
    dff = wg.shape[2]
    row = pl.BlockSpec((tm, d), lambda i: (i, 0))
    kern = functools.partial(_ffn_kernel, col_chunk=256, final_norm=final_norm)
    return pl.pallas_call(
        kern,
        grid=(n // tm,),
        in_specs=[row, _const_spec((1, d)), _layer_spec(wg, layer), _layer_spec(wu, layer),
                  _layer_spec(wd, layer), _const_spec((1, d))],
        out_specs=row,
        out_shape=jax.ShapeDtypeStruct((n, d), F32),
        scratch_shapes=[pltpu.VMEM((tm, dff), BF16)],
        compiler_params=_params("parallel"),
        name="swiglu_ffn",
    )(x, g.reshape(1, d), wg, wu, wd, gf.reshape(1, d))


def _t5_bucket(dist):
    max_exact = REL_BUCKETS // 2
    d = jnp.maximum(dist, 0)
    dl = jnp.maximum(d, max_exact).astype(F32)
    large = max_exact + (jnp.log(dl / max_exact) / math.log(REL_MAX_DIST / max_exact)
                         * (REL_BUCKETS - max_exact)).astype(I32)
    large = jnp.minimum(large, REL_BUCKETS - 1)
    return jnp.where(d < max_exact, d, large)


def _swap_halves(w):
    half = w.shape[-1] // 2
    return jnp.concatenate([w[..., half:], w[..., :half]], axis=-1)


def _in_proj_weights(w_in, d_model):
    splits = (A_HEADS * A_HEAD_DIM, A_HEAD_DIM, A_HEAD_DIM, IDX_HEADS * IDX_DIM, IDX_DIM, IDX_HEADS,
              B_HEADS * B_KEY_DIM, B_HEADS * B_KEY_DIM, B_HEADS * B_VAL_DIM, B_HEADS * B_VAL_DIM,
              C_Q_RANK, C_KV_RANK, C_ROPE, N_BRANCH * d_model)
    st = [int(v) for v in np.concatenate([[0], np.cumsum(splits)])]
    depth, k, n_cols = w_in.shape
    w_pad = jnp.pad(w_in.astype(BF16), ((0, 0), (0, 0), (0, -n_cols % LANES)))
    n_pad = w_pad.shape[2]
    reps = LANES // C_ROPE
    rows = 128

    def regroup_kernel(w_ref, a_ref, b_ref, c_ref, d_ref, g_ref):
        col = lambda a, b: w_ref[:, st[a]:st[b]]
        ik, pe = col(4, 5), col(12, 13)
        iw_pad = jnp.zeros((rows, LANES - IDX_HEADS), w_ref.dtype)
        a_ref[...] = col(0, 4).astype(BF16)
        b_ref[...] = jnp.concatenate([ik, ik, col(5, 6), iw_pad], axis=1).astype(BF16)
        c_ref[...] = col(6, 12).astype(BF16)
        d_ref[...] = jnp.concatenate([pe] * reps + [_swap_halves(pe)] * reps, axis=1).astype(BF16)
        g_ref[...] = col(13, 14).astype(BF16)

    out_w = (st[4], 2 * LANES, st[12] - st[6], 2 * LANES, st[14] - st[13])
    outs = pl.pallas_call(
        regroup_kernel,
        grid=(depth, k // rows),
        in_specs=[pl.BlockSpec((None, rows, n_pad), lambda l, i: (l, i, 0))],
        out_specs=[pl.BlockSpec((None, rows, wd), lambda l, i: (l, i, 0)) for wd in out_w],
        out_shape=[jax.ShapeDtypeStruct((depth, k, wd), BF16) for wd in out_w],
        compiler_params=_params("parallel", "parallel"),
        name="regroup_w_in",
    )(w_pad)
    ws = list(outs)
    widths = [splits[:4], (LANES, LANES), splits[6:12], (2 * LANES,)]
    dtypes = [BF16] * 5 + [F32] * 8
    return ws[:4], widths, dtypes, ws[4]


def kernel(x, positions, w_in, w_up_a, w_up_b, w_up_c, w_out, mla_q_norm, mla_w_qb, mla_kv_norm,
           mla_w_kvb, hgrn_lb_logits, hgrn_out_norm, rel_bias, attn_norm, ffn_norm, w_ffn_gate,
           w_ffn_up, w_ffn_down, final_norm):
    bsz, s_len, d_model = x.shape
    depth = w_in.shape[0]
    n = bsz * s_len
    t = LANES

    w_proj, widths, dtypes, w_gate = _in_proj_weights(w_in, d_model)
    wq = mla_w_qb.reshape(depth, C_Q_RANK, C_HEADS, C_NOPE + C_ROPE)
    q_pe = wq[..., C_NOPE:]
    w_qb = jnp.concatenate([wq[..., :C_NOPE].reshape(depth, C_Q_RANK, -1),
                            q_pe.reshape(depth, C_Q_RANK, -1),
                            _swap_halves(q_pe).reshape(depth, C_Q_RANK, -1)], axis=2).astype(BF16)
    wkv = mla_w_kvb.reshape(depth, C_KV_RANK, C_HEADS, C_NOPE + C_V)
    w_kvb = jnp.concatenate([wkv[..., :C_NOPE].reshape(depth, C_KV_RANK, -1),
                             wkv[..., C_NOPE:].reshape(depth, C_KV_RANK, -1)], axis=2).astype(BF16)
    bf = lambda w: w.astype(BF16)
    w_up_a, w_up_b, w_up_c, w_out = bf(w_up_a), bf(w_up_b), bf(w_up_c), bf(w_out)
    w_ffn_gate, w_ffn_up, w_ffn_down = bf(w_ffn_gate), bf(w_ffn_up), bf(w_ffn_down)

    p_lb = jax.nn.softmax(hgrn_lb_logits.astype(F32), axis=0)
    lower_bounds = jnp.cumsum(p_lb, axis=0) - p_lb[0:1]
    inv_freq = ROPE_THETA ** (-jnp.arange(0, C_ROPE, 2, dtype=F32) / C_ROPE)
    ang = positions.astype(F32)[..., None] * inv_freq
    cos, sin = jnp.cos(ang), jnp.sin(ang)
    reps = LANES // C_ROPE
    cos_t = jnp.tile(jnp.concatenate([cos, cos], axis=-1), (1, 1, reps)).reshape(n, LANES)
    sin_t = jnp.tile(jnp.concatenate([-sin, sin], axis=-1), (1, 1, reps)).reshape(n, LANES)
    q_idx = jnp.arange(t, dtype=I32)[None, :]
    k_idx = jnp.arange(t, dtype=I32)[:, None]
    def tab(dist):
        onehot = jax.nn.one_hot(_t5_bucket(dist), REL_BUCKETS, dtype=F32)
        return jnp.einsum("kqb,bh->khq", onehot, rel_bias.astype(F32),
                          precision=lax.Precision.HIGHEST).reshape(t, A_HEADS * t)
    far = jnp.broadcast_to(rel_bias[REL_BUCKETS - 1].astype(F32)[None, :, None],
                           (t, A_HEADS, t)).reshape(t, A_HEADS * t)
    bias_tab = (jnp.stack([tab(q_idx + t - k_idx), tab(q_idx - k_idx), far]) - far) * math.log2(math.e)

    tm = min(512, n)
    t_mla = min(256, s_len)
    x2 = x.reshape(n, d_model)
    r3 = lambda a: a.reshape(bsz, s_len, a.shape[-1])
    per_seq = lambda a: a.reshape(bsz, a.shape[0] // bsz, a.shape[1], a.shape[2])
    key_tiles = [0, 0, t] + [0] * 10
    for l in range(depth):
        mla = (cos_t, sin_t, mla_q_norm[l], mla_kv_norm[l], w_qb, w_kvb, t_mla)
        (qa, ka, vta, iq, ik2, iw, bq, bfr, bi, bg, qn, qpe, kn, vtc, kpe) = _norm_matmul(
            x2, attn_norm[l], w_proj, l, widths, dtypes, key_tiles, mla, tm)
        o_a = _dsa(r3(qa), r3(iq), r3(iw), r3(ka), per_seq(vta), r3(ik2), bias_tab)
        o_b = _hgrn(r3(bq), r3(bfr), r3(bi), r3(bg), lower_bounds[l], hgrn_out_norm[l],
                    sb=min(512, s_len))
        o_c = _mla(r3(qn), r3(qpe), r3(kn), r3(kpe), per_seq(vtc), t_mla)
        x2 = _merge(x2, attn_norm[l], w_gate, o_a.reshape(n, -1), o_b.reshape(n, -1),
                    o_c.reshape(n, -1), w_up_a, w_up_b, w_up_c, w_out, l, tm)
        x2 = _ffn(x2, ffn_norm[l], w_ffn_gate, w_ffn_up, w_ffn_down, final_norm,
                  l, tm, final_norm=(l == depth - 1))
    return x2.reshape(bsz, s_len, d_model)
```

```python
import functools
import math

import jax
import jax.numpy as jnp
import numpy as np
from jax import lax
from jax.experimental import pallas as pl
from jax.experimental.pallas import tpu as pltpu

F32 = jnp.float32
BF16 = jnp.bfloat16
I32 = jnp.int32

A_HEADS = 4
A_HEAD_DIM = 128
IDX_HEADS = 8
IDX_DIM = 64
TOPK_MAX = 256
B_HEADS = 4
B_KEY_DIM = 128
B_VAL_DIM = 128
B_CHUNK = 64
C_HEADS = 4
C_Q_RANK = 384
C_KV_RANK = 256
C_NOPE = 128
C_ROPE = 64
C_V = 128
ROPE_THETA = 10000.0
REL_BUCKETS = 32
REL_MAX_DIST = 128
N_BRANCH = 3
EPS = 1e-6
NEG_BIG = -1e30
LB_FLOOR = 1e-30

LANES = 128
SUBLANES = 8
VMEM_LIMIT = 56 * 1024 * 1024
INT_MIN = np.int32(-2 ** 31)
SORT_ROWS = LANES // SUBLANES
RANK_NEVER = 3e38
CHUNK_SHIFT = B_CHUNK.bit_length() - 1
assert 1 << CHUNK_SHIFT == B_CHUNK


def _oddeven_merge_sort(n):
    pairs = []
    p = 1
    while p < n:
        k = p
        while k >= 1:
            for j in range(k % p, n - k, 2 * k):
                for i in range(min(k, n - j - k)):
                    if (i + j) // (2 * p) == (i + j + k) // (2 * p):
                        pairs.append((i + j, i + j + k))
            k //= 2
        p *= 2
    return tuple(pairs)


_SORT16 = _oddeven_merge_sort(SORT_ROWS)

_NT = (((1,), (1,)), ((), ()))


def _params(*sem):
    return pltpu.CompilerParams(dimension_semantics=sem, vmem_limit_bytes=VMEM_LIMIT)


def _rms(x, g):
    return x * lax.rsqrt(jnp.mean(x * x, axis=-1, keepdims=True) + EPS) * g


def _const_spec(shape):
    nd = len(shape)
    return pl.BlockSpec(shape, lambda *_: (0,) * nd)


def _layer_spec(w, layer):
    return pl.BlockSpec((None,) + w.shape[1:], lambda *_: (layer, 0, 0))


def _fold8(x, op):
    r, c = x.shape
    return op(x.reshape(r // SUBLANES, SUBLANES, c), axis=0)


def _store_key_major(o_ref, rows, tile):
    for u in range(rows.shape[0] // tile):
        o_ref[u] = rows[u * tile:(u + 1) * tile, :].T.astype(o_ref.dtype)


def _mla_up_project(cq, ckv, kpe_raw, cos, sin, gq, gkv, wq_ref, wkv_ref,
                    qn_ref, qpe_ref, kn_ref, vt_ref, kpe_ref, key_tile):
    nw = C_HEADS * C_NOPE
    pw = C_HEADS * C_ROPE
    cos2 = jnp.concatenate([cos] * (pw // LANES), axis=1)
    sin2 = jnp.concatenate([sin] * (pw // LANES), axis=1)
    q = jnp.dot(_rms(cq, gq).astype(BF16), wq_ref[...], preferred_element_type=F32)
    qn_ref[...] = q[:, :nw].astype(qn_ref.dtype)
    qpe_ref[...] = (q[:, nw:nw + pw] * cos2 + q[:, nw + pw:] * sin2).astype(qpe_ref.dtype)
    kv = jnp.dot(_rms(ckv, gkv).astype(BF16), wkv_ref[...], preferred_element_type=F32)
    kn_ref[...] = kv[:, :nw].astype(kn_ref.dtype)
    _store_key_major(vt_ref, kv[:, nw:], key_tile)
    kpe_ref[...] = (kpe_raw[:, :LANES] * cos + kpe_raw[:, LANES:] * sin).astype(kpe_ref.dtype)


N_LATENT = 3
N_MLA_IN = 6
N_MLA_OUT = 5


def _norm_matmul_kernel(x_ref, g_ref, *refs, widths, key_tiles, mla_key_tile, col_chunk):
    n_w = len(widths)
    w_refs, mla_in = refs[:n_w], refs[n_w:n_w + N_MLA_IN]
    out_refs, mla_out = refs[n_w + N_MLA_IN:-N_MLA_OUT], refs[-N_MLA_OUT:]
    h = _rms(x_ref[...], g_ref[...]).astype(BF16)
    k = 0
    latents = []
    for w_ref, w_widths in zip(w_refs, widths):
        c0 = 0
        for w in w_widths:
            if k >= len(out_refs):
                latents.append(jnp.dot(h, w_ref[:, c0:c0 + w], preferred_element_type=F32))
            elif key_tiles[k]:
                _store_key_major(out_refs[k], jnp.dot(h, w_ref[:, c0:c0 + w], preferred_element_type=F32),
                                 key_tiles[k])
            else:
                for s in range(0, w, col_chunk):
                    e = min(s + col_chunk, w)
                    out_refs[k][:, s:e] = jnp.dot(h, w_ref[:, c0 + s:c0 + e],
                                                  preferred_element_type=F32).astype(out_refs[k].dtype)
            c0 += w
            k += 1
    cos_ref, sin_ref, gq_ref, gkv_ref, wq_ref, wkv_ref = mla_in
    _mla_up_project(*latents, cos_ref[...], sin_ref[...], gq_ref[...], gkv_ref[...], wq_ref, wkv_ref,
                    *mla_out, mla_key_tile)


def _norm_matmul(x, g, ws, layer, widths, dtypes, key_tiles, mla, tm):
    cos_t, sin_t, gq, gkv, wq, wkv, mla_tile = mla
    n, k = x.shape
    flat = [wd for w_widths in widths for wd in w_widths]
    assert n % tm == 0 and all(w.shape[1:] == (k, sum(ww)) for w, ww in zip(ws, widths))
    kern = functools.partial(_norm_matmul_kernel, widths=tuple(tuple(ww) for ww in widths),
                             key_tiles=tuple(key_tiles), mla_key_tile=mla_tile, col_chunk=512)
    row = lambda w: pl.BlockSpec((tm, w), lambda i: (i, 0))
    tiles = lambda w, kt: pl.BlockSpec((tm // kt, w, kt), lambda i: (i, 0, 0))
    specs, shapes = [], []
    for wd, dt, kt in list(zip(flat, dtypes, key_tiles))[:-N_LATENT]:
        specs.append(tiles(wd, kt) if kt else row(wd))
        shapes.append(jax.ShapeDtypeStruct((n // kt, wd, kt) if kt else (n, wd), dt))
    nw, vw = C_HEADS * C_NOPE, C_HEADS * C_V
    for wd in (nw, C_HEADS * C_ROPE, nw, None, LANES):
        specs.append(tiles(vw, mla_tile) if wd is None else row(wd))
        shapes.append(jax.ShapeDtypeStruct((n // mla_tile, vw, mla_tile) if wd is None else (n, wd), BF16))
    return pl.pallas_call(
        kern,
        grid=(n // tm,),
        in_specs=[row(k), _const_spec((1, k))] + [_layer_spec(w, layer) for w in ws]
                 + [row(LANES), row(LANES), _const_spec((1, C_Q_RANK)), _const_spec((1, C_KV_RANK)),
                    _layer_spec(wq, layer), _layer_spec(wkv, layer)],
        out_specs=specs,
        out_shape=shapes,
        compiler_params=_params("parallel"),
        name="norm_proj",
    )(x, g.reshape(1, k), *ws, cos_t, sin_t, gq.reshape(1, -1), gkv.reshape(1, -1), wq, wkv)


def _dsa_kernel(qa_ref, iq_ref, iw_ref, ka_ref, vt_ref, ik_ref, bias_ref, o_ref,
                keys_ref, sorted_ref, s_ref, qm_ref, qs_ref, wt_ref, thr_ref, ties_ref,
                *, topk, n_blocks, group, scale, idx_scale):
    t = LANES
    gt = group * t
    i = pl.program_id(1)
    n_groups = (i + group) // group
    max_groups = n_blocks // group
    krow = lax.broadcasted_iota(I32, (t, t), 0)
    qcol = lax.broadcasted_iota(I32, (t, t), 1)
    lane_lo = qcol < IDX_DIM

    for h in range(IDX_HEADS):
        pair = iq_ref[0, :, (h // 2) * t:(h // 2 + 1) * t]
        keep = lane_lo if h % 2 == 0 else jnp.logical_not(lane_lo)
        qm_ref[h * t:(h + 1) * t, :] = jnp.where(keep, pair, jnp.zeros_like(pair))
    for h in range(A_HEADS):
        qs_ref[h * t:(h + 1) * t, :] = qa_ref[0, :, h * t:(h + 1) * t]
    wt_ref[...] = iw_ref[0].T

    def index_logits(g):
        return lax.dot_general(ik_ref[0, g * gt:(g + 1) * gt, :], qm_ref[...], _NT,
                               preferred_element_type=F32)

    def score_group(g, lg, ng):
        for u in range(group):
            j = g * group + u
            acc = jnp.zeros((t, t), F32)
            for h in range(IDX_HEADS):
                acc = acc + jnp.maximum(lg[u * t:(u + 1) * t, h * t:(h + 1) * t], 0.0) * wt_ref[h:h + 1, :]
            score = acc * idx_scale
            if j >= (ng - 1) * group:
                score = jnp.where(krow + j * t <= qcol + i * t, score, NEG_BIG)
            score = jnp.where(score == 0.0, 0.0, score)
            bits = pltpu.bitcast(score, I32)
            key = bits ^ ((bits >> 31) & np.int32(0x7FFFFFFF))
            keys_ref[j] = key
            rows = [key[r * SUBLANES:(r + 1) * SUBLANES, :] for r in range(SORT_ROWS)]
            for a, b in _SORT16:
                rows[a], rows[b] = jnp.maximum(rows[a], rows[b]), jnp.minimum(rows[a], rows[b])
            for r in range(SORT_ROWS):
                sorted_ref[j, r] = rows[r]

    def score_all(ng):
        lg = index_logits(0)
        for g in range(ng):
            nxt = index_logits(g + 1) if g + 1 < ng else None
            score_group(g, lg, ng)
            lg = nxt

    for v in range(1, max_groups + 1):
        pl.when(n_groups == v)(functools.partial(score_all, v))

    thr_ref[...] = jnp.full(thr_ref.shape, INT_MIN, I32)
    ties_ref[...] = jnp.zeros(ties_ref.shape, F32)

    def count(n, pred):
        acc = [jnp.zeros((SUBLANES, t), F32) for _ in range(5)]
        for j in range(n):
            v = [sorted_ref[j, r] for r in range(SORT_ROWS)]
            m1 = pred(v[7])
            m2 = pred(jnp.where(m1, v[11], v[3]))
            m3 = pred(jnp.where(m1, jnp.where(m2, v[13], v[9]), jnp.where(m2, v[5], v[1])))
            m4 = pred(jnp.where(m1, jnp.where(m2, jnp.where(m3, v[14], v[12]), jnp.where(m3, v[10], v[8])),
                                jnp.where(m2, jnp.where(m3, v[6], v[4]), jnp.where(m3, v[2], v[0]))))
            m5 = pred(v[15])
            acc = [a + jnp.where(m, 1.0, 0.0) for a, m in zip(acc, (m1, m2, m3, m4, m5))]
        cnt = 8.0 * acc[0] + 4.0 * acc[1] + 2.0 * acc[2] + acc[3] + acc[4]
        return jnp.sum(cnt, axis=0, keepdims=True)

    def search(n):
        def search_pass(b, thr):
            cand = thr + lax.shift_left(np.int32(1), 31 - b)
            return jnp.where(count(n, lambda key: key >= cand) >= topk, cand, thr)

        thr = lax.fori_loop(0, 32, search_pass, jnp.full((1, t), INT_MIN, I32))
        thr_ref[...] = jnp.broadcast_to(thr, thr_ref.shape)
        ties_ref[...] = jnp.broadcast_to(topk - count(n, lambda key: key > thr), ties_ref.shape)

    for c in range(n_blocks):
        if (c + 1) * t > topk:
            pl.when(i == c)(functools.partial(search, c + 1))

    thr = thr_ref[0:1, :]
    n_ties = ties_ref[0:1, :]

    r2 = lax.broadcasted_iota(I32, (2 * t, t), 0)
    c2 = lax.broadcasted_iota(I32, (2 * t, t), 1)
    tie_lhs = jnp.where(jnp.logical_or(r2 >= t, c2 < r2), 1.0, 0.0).astype(BF16)

    def sweep_a_mxu(g):
        keys = [keys_ref[g * group + u] for u in range(group)]
        eqs = [key == thr for key in keys]
        eq_all = jnp.concatenate([jnp.where(eq, 1.0, 0.0).astype(BF16) for eq in eqs], axis=1)
        pref = jnp.dot(tie_lhs, eq_all, preferred_element_type=F32)
        s_grp = lax.dot_general(ka_ref[0, g * gt:(g + 1) * gt, :], qs_ref[...], _NT,
                                preferred_element_type=F32)
        return keys, eqs, pref, s_grp

    def sweep_a_vpu(g, operands, seen, mx, ng):
        keys, eqs, pref, s_grp = operands
        for u in range(group):
            j = g * group + u
            us = slice(u * t, (u + 1) * t)
            rank = jnp.where(keys[u] > thr, -1.0, jnp.where(eqs[u], seen + pref[:t, us], RANK_NEVER))
            if j >= (ng - 1) * group:
                rank = jnp.where(krow + j * t <= qcol + i * t, rank, RANK_NEVER)
            valid = rank < n_ties
            which = jnp.where(j == i, 1, jnp.where(j == i - 1, 0, 2))
            s_all = s_grp[us, :] * (scale * math.log2(math.e))
            if j > (ng - 1) * group - 2:
                s_all = s_all + bias_ref[which]
            for h in range(A_HEADS):
                s_h = jnp.where(valid, s_all[:, h * t:(h + 1) * t], NEG_BIG)
                s_ref[j, :, h * t:(h + 1) * t] = s_h
                mx[h] = jnp.maximum(mx[h], _fold8(s_h, jnp.max))
            seen = seen + pref[t:t + 1, us]
        return seen, mx

    def sweep_b(g, m_all, l8):
        ps = []
        for u in range(group):
            p = jnp.exp2(s_ref[g * group + u] - m_all)
            l8 = l8 + _fold8(p, jnp.sum)
            ps.append(p.astype(BF16))
        vt_grp = jnp.concatenate([vt_ref[0, g * group + u] for u in range(group)], axis=1)
        return l8, jnp.dot(vt_grp, jnp.concatenate(ps, axis=0), preferred_element_type=F32)

    def attend(ng):
        seen = jnp.zeros((1, t), F32)
        mx = [jnp.full((SUBLANES, t), NEG_BIG, F32) for _ in range(A_HEADS)]
        operands = sweep_a_mxu(0)
        for g in range(ng):
            nxt = sweep_a_mxu(g + 1) if g + 1 < ng else None
            seen, mx = sweep_a_vpu(g, operands, seen, mx, ng)
            operands = nxt
        m_all = jnp.concatenate([jnp.max(m, axis=0, keepdims=True) for m in mx], axis=1)
        l8 = jnp.zeros((SUBLANES, A_HEADS * t), F32)
        acc = None
        for g in range(ng):
            l8, pv = sweep_b(g, m_all, l8)
            acc = pv if acc is None else acc + pv
        out = acc * (1.0 / jnp.sum(l8, axis=0, keepdims=True))
        for h in range(A_HEADS):
            o_ref[0, :, h * t:(h + 1) * t] = out[:, h * t:(h + 1) * t].T.astype(o_ref.dtype)

    for v in range(1, max_groups + 1):
        pl.when(n_groups == v)(functools.partial(attend, v))


def _dsa(qa, iq, iw, ka, vt, ik2, bias_tab):
    b, s, _ = qa.shape
    t = LANES
    nb = s // t
    group = math.gcd(nb, 2)
    topk = min(TOPK_MAX, s // 4)
    kern = functools.partial(_dsa_kernel, topk=float(topk), n_blocks=nb, group=group,
                             scale=A_HEAD_DIM ** -0.5,
                             idx_scale=(IDX_DIM ** -0.5) * (IDX_HEADS ** -0.5))
    qspec = lambda w: pl.BlockSpec((1, t, w), lambda bi, i: (bi, i, 0))
    kspec = pl.BlockSpec((1, s, t), lambda bi, i: (bi, 0, 0))
    return pl.pallas_call(
        kern,
        grid=(b, nb),
        in_specs=[qspec(A_HEADS * t), qspec(IDX_HEADS * IDX_DIM), qspec(t), kspec,
                  pl.BlockSpec((1, nb, t, t), lambda bi, i: (bi, 0, 0, 0)), kspec,
                  _const_spec(bias_tab.shape)],
        out_specs=qspec(A_HEADS * t),
        out_shape=jax.ShapeDtypeStruct((b, s, A_HEADS * t), BF16),
        scratch_shapes=[pltpu.VMEM((nb, t, t), I32),
                        pltpu.VMEM((nb, SORT_ROWS, SUBLANES, t), I32),
                        pltpu.VMEM((nb, t, A_HEADS * t), F32),
                        pltpu.VMEM((IDX_HEADS * t, t), BF16),
                        pltpu.VMEM((A_HEADS * t, t), BF16),
                        pltpu.VMEM((t, t), F32),
                        pltpu.VMEM((SUBLANES, t), I32),
                        pltpu.VMEM((SUBLANES, t), F32)],
        compiler_params=_params("parallel", "parallel"),
        name="dsa_attention",
    )(qa, iq, iw, ka, vt, ik2, bias_tab)


def _hgrn_kernel(q_ref, f_ref, i_ref, g_ref, lb_ref, gain_ref, o_ref, state_ref, *, n_chunks, n_seq):
    c = B_CHUNK
    rc = n_seq * c
    kd, vd = B_KEY_DIM, B_VAL_DIM
    w = B_HEADS * kd

    @pl.when(pl.program_id(1) == 0)
    def _():
        state_ref[...] = jnp.zeros_like(state_ref)

    row = lax.broadcasted_iota(I32, (rc, rc), 0)
    col = lax.broadcasted_iota(I32, (rc, rc), 1)
    seq_of = lambda r: lax.shift_right_logical(r, CHUNK_SHIFT)
    causal = (row >= col) if n_seq == 1 else jnp.logical_and(row >= col, seq_of(row) == seq_of(col))
    tril = jnp.where(causal, 1.0, 0.0).astype(BF16)
    seqs = [slice(b * c, (b + 1) * c) for b in range(n_seq)]
    seq_of_row = seq_of(lax.broadcasted_iota(I32, (rc, w), 0))

    def per_seq_row(x, r):
        return jnp.concatenate([jnp.broadcast_to(x[b * c + r:b * c + r + 1, :], (c, x.shape[1]))
                                for b in range(n_seq)], axis=0)

    def stacked(ref, rows):
        return jnp.concatenate([ref[b, rows, :] for b in range(n_seq)], axis=0)
    q_scale = kd ** -0.5
    lb = lb_ref[...]
    lb_floor = jnp.maximum(lb, LB_FLOOR)
    one_m_lb = 1.0 - lb
    heads = [slice(h * kd, (h + 1) * kd) for h in range(B_HEADS)]

    def stage_decay(ci):
        rows = slice(ci * c, (ci + 1) * c)
        fr = stacked(f_ref, rows)
        z = jnp.exp(-jnp.abs(fr))
        r = 1.0 / (1.0 + z)
        sig_pos = jnp.where(fr >= 0, r, z * r)
        sig_neg = jnp.where(fr >= 0, z * r, r)
        log_f = jnp.log(lb_floor + one_m_lb * sig_pos)
        k_in = one_m_lb * sig_neg
        hi = log_f.astype(BF16)
        rest = log_f - hi.astype(F32)
        mid = rest.astype(BF16)
        lo = (rest - mid.astype(F32)).astype(BF16)
        cs = jnp.dot(tril, jnp.concatenate([hi, mid, lo], axis=1), preferred_element_type=F32)
        return k_in, cs

    def apply_update(pending):
        if pending is not None:
            d_last, st, upd = pending
            for b in range(n_seq):
                for h, hs in enumerate(heads):
                    state_ref[b, h] = d_last[b * c:b * c + 1, hs] * st[b][h] + upd[b][h]

    def stage_scores(ci, decay, pending):
        rows = slice(ci * c, (ci + 1) * c)
        k_in, cs = decay
        bsum = cs[:, :w] + cs[:, w:2 * w] + cs[:, 2 * w:]
        b_mid = per_seq_row(bsum, c // 2 - 1)
        b_last = per_seq_row(bsum, c - 1)
        qs = stacked(q_ref, rows) * q_scale
        v = stacked(i_ref, rows)
        v_b = v.astype(BF16)
        v_t = v.T.astype(BF16)
        q_mid = (qs * jnp.exp(bsum - b_mid)).astype(BF16)
        k_mid = (k_in * jnp.exp(b_mid - bsum)).astype(BF16)
        q_dec = (qs * jnp.exp(bsum)).astype(BF16)
        k_end = (k_in * jnp.exp(b_last - bsum)).astype(BF16)
        d_last = jnp.exp(b_last)
        apply_update(pending)
        st = [[state_ref[b, h] for h in range(B_HEADS)] for b in range(n_seq)]
        attn = [lax.dot_general(q_mid[:, hs], k_mid[:, hs], _NT, preferred_element_type=F32)
                for hs in heads]
        inter = [jnp.concatenate(
            [lax.dot_general(q_dec[sq, hs], st[b][h].astype(BF16), _NT, preferred_element_type=F32)
             for b, sq in enumerate(seqs)], axis=0) for h, hs in enumerate(heads)]
        k_seq = [k_end if n_seq == 1 else jnp.where(seq_of_row == b, k_end, jnp.zeros_like(k_end))
                 for b in range(n_seq)]
        upd = [[jnp.dot(v_t[hs, :], k_seq[b][:, hs], preferred_element_type=F32) for hs in heads]
               for b in range(n_seq)]
        return (attn, inter, v_b), (d_last, st, upd)

    def stage_output(ci, scores):
        rows = slice(ci * c, (ci + 1) * c)
        attn, inter, v_b = scores
        attn = [jnp.where(causal, a, 0.0).astype(BF16) for a in attn]
        outs = [jnp.dot(attn[h], v_b[:, hs], preferred_element_type=F32) + inter[h]
                for h, hs in enumerate(heads)]
        o = jnp.concatenate([_rms(o_h, gain_ref[...]) for o_h in outs], axis=1)
        g = stacked(g_ref, rows)
        o = (o * (g * (1.0 / (1.0 + jnp.exp(-g))))).astype(o_ref.dtype)
        for b, sq in enumerate(seqs):
            o_ref[b, rows, :] = o[sq, :]

    decay, scores, pending = {}, {}, None
    for step in range(n_chunks + 2):
        if step < n_chunks:
            decay[step] = stage_decay(step)
        if 0 <= step - 1 < n_chunks:
            scores[step - 1], pending = stage_scores(step - 1, decay.pop(step - 1), pending)
        if 0 <= step - 2 < n_chunks:
            stage_output(step - 2, scores.pop(step - 2))
    apply_update(pending)


def _hgrn(bq, bf, bi, bg, lb, gain, sb):
    b, s, w = bq.shape
    assert s % sb == 0 and sb % B_CHUNK == 0
    n_seq = 2 if b % 2 == 0 else 1
    kern = functools.partial(_hgrn_kernel, n_chunks=sb // B_CHUNK, n_seq=n_seq)
    spec = pl.BlockSpec((n_seq, sb, w), lambda bi_, si: (bi_, si, 0))
    return pl.pallas_call(
        kern,
        grid=(b // n_seq, s // sb),
        in_specs=[spec, spec, spec, spec, _const_spec((1, w)), _const_spec((1, B_VAL_DIM))],
        out_specs=spec,
        out_shape=jax.ShapeDtypeStruct((b, s, w), BF16),
        scratch_shapes=[pltpu.VMEM((n_seq, B_HEADS, B_VAL_DIM, B_KEY_DIM), F32)],
        compiler_params=_params("parallel", "arbitrary"),
        name="hgrn2",
    )(bq, bf, bi, bg, lb.reshape(1, w), gain.reshape(1, B_VAL_DIM))


def _mla_kernel(qn_ref, qpe_ref, kn_ref, kpe_ref, vt_ref, o_ref, q_ref, acc_ref, *, t, n_tiles, scale):
    i = pl.program_id(1)
    lane_lo = lax.broadcasted_iota(I32, (t, LANES), 1) < C_ROPE
    for h in range(C_HEADS):
        pair = qpe_ref[0, :, (h // 2) * LANES:(h // 2 + 1) * LANES]
        keep = lane_lo if h % 2 == 0 else jnp.logical_not(lane_lo)
        q_ref[h, :, :C_NOPE] = qn_ref[0, :, h * C_NOPE:(h + 1) * C_NOPE]
        q_ref[h, :, C_NOPE:] = jnp.where(keep, pair, jnp.zeros_like(pair))
    acc_ref[...] = jnp.zeros_like(acc_ref)

    def logits_of(step):
        first, count = step
        rows = slice(first * t, (first + count) * t)
        kpe_t = kpe_ref[0, rows, :]
        return [lax.dot_general(
            jnp.concatenate([kn_ref[0, rows, h * C_NOPE:(h + 1) * C_NOPE], kpe_t], axis=1),
            q_ref[h], _NT, preferred_element_type=F32) for h in range(C_HEADS)]

    def softmax_pv(step, block, logits, ms, ls, masked):
        first, count = step
        new_m, new_l = [], []
        for h in range(C_HEADS):
            s = logits[h] * (scale * math.log2(math.e))
            if masked:
                key_pos = lax.broadcasted_iota(I32, (count * t, t), 0) + first * t
                query_pos = lax.broadcasted_iota(I32, (count * t, t), 1) + block * t
                s = jnp.where(key_pos <= query_pos, s, NEG_BIG)
            m_new = jnp.maximum(ms[h], jnp.max(s, axis=0, keepdims=True))
            alpha = jnp.exp2(ms[h] - m_new)
            p = jnp.exp2(s - m_new)
            new_l.append(alpha * ls[h] + jnp.sum(p, axis=0, keepdims=True))
            new_m.append(m_new)
            hs = slice(h * C_V, (h + 1) * C_V)
            tiles = [vt_ref[0, first + u, hs, :] for u in range(count)]
            v_t = tiles[0] if count == 1 else jnp.concatenate(tiles, axis=1)
            acc_ref[h] = alpha * acc_ref[h] + jnp.dot(v_t, p.astype(BF16), preferred_element_type=F32)
        return new_m, new_l

    def run(block):
        steps = [(2 * g, 2) for g in range(block // 2)] + [(block - block % 2, 1 + block % 2)]
        ms = [jnp.full((1, t), NEG_BIG, F32) for _ in range(C_HEADS)]
        ls = [jnp.zeros((1, t), F32) for _ in range(C_HEADS)]
        logits = logits_of(steps[0])
        for k, step in enumerate(steps):
            nxt = logits_of(steps[k + 1]) if k + 1 < len(steps) else None
            ms, ls = softmax_pv(step, block, logits, ms, ls, masked=(k + 1 == len(steps)))
            logits = nxt
        for h in range(C_HEADS):
            o_ref[0, :, h * C_V:(h + 1) * C_V] = (acc_ref[h] * (1.0 / ls[h])).T.astype(o_ref.dtype)

    for v in range(n_tiles):
        pl.when(i == v)(functools.partial(run, v))


def _mla(qn, qpe, kn, kpe, vt, t):
    b, s, _ = qn.shape
    nt = s // t
    kern = functools.partial(_mla_kernel, t=t, n_tiles=nt, scale=(C_NOPE + C_ROPE) ** -0.5)
    qspec = lambda w: pl.BlockSpec((1, t, w), lambda bi, i: (bi, i, 0))
    kspec = lambda w: pl.BlockSpec((1, s, w), lambda bi, i: (bi, 0, 0))
    return pl.pallas_call(
        kern,
        grid=(b, nt),
        in_specs=[qspec(qn.shape[2]), qspec(qpe.shape[2]), kspec(kn.shape[2]), kspec(kpe.shape[2]),
                  pl.BlockSpec((1, nt, C_HEADS * C_V, t), lambda bi, i: (bi, 0, 0, 0))],
        out_specs=qspec(C_HEADS * C_V),
        out_shape=jax.ShapeDtypeStruct((b, s, C_HEADS * C_V), BF16),
        scratch_shapes=[pltpu.VMEM((C_HEADS, t, C_NOPE + LANES), BF16),
                        pltpu.VMEM((C_HEADS, C_V, t), F32)],
        compiler_params=_params("parallel", "parallel"),
        name="mla_attention",
    )(qn, qpe, kn, kpe, vt)


def _merge_kernel(x_ref, g_ref, wg_ref, oa_ref, ob_ref, oc_ref, wa_ref, wb_ref, wc_ref, wo_ref,
                  out_ref, mixed_ref, *, col_chunk):
    x = x_ref[...]
    d = x.shape[1]
    h = _rms(x, g_ref[...]).astype(BF16)
    branches = ((oa_ref, wa_ref), (ob_ref, wb_ref), (oc_ref, wc_ref))
    for s in range(0, d, col_chunk):
        cs = slice(s, s + col_chunk)
        mixed = None
        for bidx, (o_ref, w_ref) in enumerate(branches):
            logits = jnp.dot(h, wg_ref[:, bidx * d + s:bidx * d + s + col_chunk],
                             preferred_element_type=F32)
            gate = 1.0 / (1.0 + jnp.exp(-logits))
            term = gate * jnp.dot(o_ref[...], w_ref[:, cs], preferred_element_type=F32)
            mixed = term if mixed is None else mixed + term
        mixed_ref[:, cs] = mixed.astype(BF16)
    out_ref[...] = x + jnp.dot(mixed_ref[...], wo_ref[...], preferred_element_type=F32)


def _merge(x, g, w_gate, oa, ob, oc, wa, wb, wc, wo, layer, tm):
    n, d = x.shape
    row = lambda w: pl.BlockSpec((tm, w), lambda i: (i, 0))
    kern = functools.partial(_merge_kernel, col_chunk=256)
    return pl.pallas_call(
        kern,
        grid=(n // tm,),
        in_specs=[row(d), _const_spec((1, d)), _layer_spec(w_gate, layer),
                  row(oa.shape[1]), row(ob.shape[1]), row(oc.shape[1]),
                  _layer_spec(wa, layer), _layer_spec(wb, layer), _layer_spec(wc, layer),
                  _layer_spec(wo, layer)],
        out_specs=row(d),
        out_shape=jax.ShapeDtypeStruct((n, d), F32),
        scratch_shapes=[pltpu.VMEM((tm, d), BF16)],
        compiler_params=_params("parallel"),
        name="merge_out_proj",
    )(x, g.reshape(1, d), w_gate, oa, ob, oc, wa, wb, wc, wo)


def _ffn_kernel(x_ref, g_ref, wg_ref, wu_ref, wd_ref, gf_ref, out_ref, act_ref, *, col_chunk,
                final_norm):
    x = x_ref[...]
    h = _rms(x, g_ref[...]).astype(BF16)
    dff = wg_ref.shape[1]
    for s in range(0, dff, col_chunk):
        cs = slice(s, s + col_chunk)
        gate = jnp.dot(h, wg_ref[:, cs], preferred_element_type=F32)
        up = jnp.dot(h, wu_ref[:, cs], preferred_element_type=F32)
        act_ref[:, cs] = (gate * (1.0 / (1.0 + jnp.exp(-gate))) * up).astype(BF16)
    y = x + jnp.dot(act_ref[...], wd_ref[...], preferred_element_type=F32)
    if final_norm:
        y = _rms(y, gf_ref[...])
    out_ref[...] = y


def _ffn(x, g, wg, wu, wd, gf, layer, tm, final_norm):
    n, d = x.shape
    dff = wg.shape[2]
    row = pl.BlockSpec((tm, d), lambda i: (i, 0))
    kern = functools.partial(_ffn_kernel, col_chunk=256, final_norm=final_norm)
    return pl.pallas_call(
        kern,
        grid=(n // tm,),
        in_specs=[row, _const_spec((1, d)), _layer_spec(wg, layer), _layer_spec(wu, layer),
                  _layer_spec(wd, layer), _const_spec((1, d))],
        out_specs=row,
        out_shape=jax.ShapeDtypeStruct((n, d), F32),
        scratch_shapes=[pltpu.VMEM((tm, dff), BF16)],
        compiler_params=_params("parallel"),
        name="swiglu_ffn",
    )(x, g.reshape(1, d), wg, wu, wd, gf.reshape(1, d))


def _t5_bucket(dist):
    max_exact = REL_BUCKETS // 2
    d = jnp.maximum(dist, 0)
    dl = jnp.maximum(d, max_exact).astype(F32)
    large = max_exact + (jnp.log(dl / max_exact) / math.log(REL_MAX_DIST / max_exact)
                         * (REL_BUCKETS - max_exact)).astype(I32)
    large = jnp.minimum(large, REL_BUCKETS - 1)
    return jnp.where(d < max_exact, d, large)


def _swap_halves(w):
    half = w.shape[-1] // 2
    return jnp.concatenate([w[..., half:], w[..., :half]], axis=-1)


def _in_proj_weights(w_in, d_model):
    splits = (A_HEADS * A_HEAD_DIM, A_HEAD_DIM, A_HEAD_DIM, IDX_HEADS * IDX_DIM, IDX_DIM, IDX_HEADS,
              B_HEADS * B_KEY_DIM, B_HEADS * B_KEY_DIM, B_HEADS * B_VAL_DIM, B_HEADS * B_VAL_DIM,
              C_Q_RANK, C_KV_RANK, C_ROPE, N_BRANCH * d_model)
    st = [int(v) for v in np.concatenate([[0], np.cumsum(splits)])]
    depth, k, n_cols = w_in.shape
    w_pad = jnp.pad(w_in.astype(BF16), ((0, 0), (0, 0), (0, -n_cols % LANES)))
    n_pad = w_pad.shape[2]
    reps = LANES // C_ROPE
    rows = 128

    def regroup_kernel(w_ref, a_ref, b_ref, c_ref, d_ref, g_ref):
        col = lambda a, b: w_ref[:, st[a]:st[b]]
        ik, pe = col(4, 5), col(12, 13)
        iw_pad = jnp.zeros((rows, LANES - IDX_HEADS), w_ref.dtype)
        a_ref[...] = col(0, 4).astype(BF16)
        b_ref[...] = jnp.concatenate([ik, ik, col(5, 6), iw_pad], axis=1).astype(BF16)
        c_ref[...] = col(6, 12).astype(BF16)
        d_ref[...] = jnp.concatenate([pe] * reps + [_swap_halves(pe)] * reps, axis=1).astype(BF16)
        g_ref[...] = col(13, 14).astype(BF16)

    out_w = (st[4], 2 * LANES, st[12] - st[6], 2 * LANES, st[14] - st[13])
    outs = pl.pallas_call(
        regroup_kernel,
        grid=(depth, k // rows),
        in_specs=[pl.BlockSpec((None, rows, n_pad), lambda l, i: (l, i, 0))],
        out_specs=[pl.BlockSpec((None, rows, wd), lambda l, i: (l, i, 0)) for wd in out_w],
        out_shape=[jax.ShapeDtypeStruct((depth, k, wd), BF16) for wd in out_w],
        compiler_params=_params("parallel", "parallel"),
        name="regroup_w_in",
    )(w_pad)
    ws = list(outs)
    widths = [splits[:4], (LANES, LANES), splits[6:12], (2 * LANES,)]
    dtypes = [BF16] * 5 + [F32] * 8
    return ws[:4], widths, dtypes, ws[4]


def kernel(x, positions, w_in, w_up_a, w_up_b, w_up_c, w_out, mla_q_norm, mla_w_qb, mla_kv_norm,
           mla_w_kvb, hgrn_lb_logits, hgrn_out_norm, rel_bias, attn_norm, ffn_norm, w_ffn_gate,
           w_ffn_up, w_ffn_down, final_norm):
    bsz, s_len, d_model = x.shape
    depth = w_in.shape[0]
    n = bsz * s_len
    t = LANES

    w_proj, widths, dtypes, w_gate = _in_proj_weights(w_in, d_model)
    wq = mla_w_qb.reshape(depth, C_Q_RANK, C_HEADS, C_NOPE + C_ROPE)
    q_pe = wq[..., C_NOPE:]
    w_qb = jnp.concatenate([wq[..., :C_NOPE].reshape(depth, C_Q_RANK, -1),
                            q_pe.reshape(depth, C_Q_RANK, -1),
                            _swap_halves(q_pe).reshape(depth, C_Q_RANK, -1)], axis=2).astype(BF16)
    wkv = mla_w_kvb.reshape(depth, C_KV_RANK, C_HEADS, C_NOPE + C_V)
    w_kvb = jnp.concatenate([wkv[..., :C_NOPE].reshape(depth, C_KV_RANK, -1),
                             wkv[..., C_NOPE:].reshape(depth, C_KV_RANK, -1)], axis=2).astype(BF16)
    bf = lambda w: w.astype(BF16)
    w_up_a, w_up_b, w_up_c, w_out = bf(w_up_a), bf(w_up_b), bf(w_up_c), bf(w_out)
    w_ffn_gate, w_ffn_up, w_ffn_down = bf(w_ffn_gate), bf(w_ffn_up), bf(w_ffn_down)

    p_lb = jax.nn.softmax(hgrn_lb_logits.astype(F32), axis=0)
    lower_bounds = jnp.cumsum(p_lb, axis=0) - p_lb[0:1]
    inv_freq = ROPE_THETA ** (-jnp.arange(0, C_ROPE, 2, dtype=F32) / C_ROPE)
    ang = positions.astype(F32)[..., None] * inv_freq
    cos, sin = jnp.cos(ang), jnp.sin(ang)
    reps = LANES // C_ROPE
    cos_t = jnp.tile(jnp.concatenate([cos, cos], axis=-1), (1, 1, reps)).reshape(n, LANES)
    sin_t = jnp.tile(jnp.concatenate([-sin, sin], axis=-1), (1, 1, reps)).reshape(n, LANES)
    q_idx = jnp.arange(t, dtype=I32)[None, :]
    k_idx = jnp.arange(t, dtype=I32)[:, None]
    def tab(dist):
        onehot = jax.nn.one_hot(_t5_bucket(dist), REL_BUCKETS, dtype=F32)
        return jnp.einsum("kqb,bh->khq", onehot, rel_bias.astype(F32),
                          precision=lax.Precision.HIGHEST).reshape(t, A_HEADS * t)
    far = jnp.broadcast_to(rel_bias[REL_BUCKETS - 1].astype(F32)[None, :, None],
                           (t, A_HEADS, t)).reshape(t, A_HEADS * t)
    bias_tab = (jnp.stack([tab(q_idx + t - k_idx), tab(q_idx - k_idx), far]) - far) * math.log2(math.e)

    tm = min(512, n)
    t_mla = min(256, s_len)
    x2 = x.reshape(n, d_model)
    r3 = lambda a: a.reshape(bsz, s_len, a.shape[-1])
    per_seq = lambda a: a.reshape(bsz, a.shape[0] // bsz, a.shape[1], a.shape[2])
    key_tiles = [0, 0, t] + [0] * 10
    for l in range(depth):
        mla = (cos_t, sin_t, mla_q_norm[l], mla_kv_norm[l], w_qb, w_kvb, t_mla)
        (qa, ka, vta, iq, ik2, iw, bq, bfr, bi, bg, qn, qpe, kn, vtc, kpe) = _norm_matmul(
            x2, attn_norm[l], w_proj, l, widths, dtypes, key_tiles, mla, tm)
        o_a = _dsa(r3(qa), r3(iq), r3(iw), r3(ka), per_seq(vta), r3(ik2), bias_tab)
        o_b = _hgrn(r3(bq), r3(bfr), r3(bi), r3(bg), lower_bounds[l], hgrn_out_norm[l],
                    sb=min(512, s_len))
        o_c = _mla(r3(qn), r3(qpe), r3(kn), r3(kpe), per_seq(vtc), t_mla)
        x2 = _merge(x2, attn_norm[l], w_gate, o_a.reshape(n, -1), o_b.reshape(n, -1),
                    o_c.reshape(n, -1), w_up_a, w_up_b, w_up_c, w_out, l, tm)
        x2 = _ffn(x2, ffn_norm[l], w_ffn_gate, w_ffn_up, w_ffn_down, final_norm,
                  l, tm, final_norm=(l == depth - 1))
    return x2.reshape(bsz, s_len, d_model)
```

```python
import functools
import math

import jax
import jax.numpy as jnp
import numpy as np
from jax import lax
from jax.experimental import pallas as pl
from jax.experimental.pallas import tpu as pltpu

F32 = jnp.float32
BF16 = jnp.bfloat16
I32 = jnp.int32

A_HEADS = 4
A_HEAD_DIM = 128
IDX_HEADS = 8
IDX_DIM = 64
TOPK_MAX = 256
B_HEADS = 4
B_KEY_DIM = 128
B_VAL_DIM = 128
B_CHUNK = 64
C_HEADS = 4
C_Q_RANK = 384
C_KV_RANK = 256
C_NOPE = 128
C_ROPE = 64
C_V = 128
ROPE_THETA = 10000.0
REL_BUCKETS = 32
REL_MAX_DIST = 128
N_BRANCH = 3
EPS = 1e-6
NEG_BIG = -1e30
LB_FLOOR = 1e-30

LANES = 128
SUBLANES = 8
VMEM_LIMIT = 56 * 1024 * 1024
INT_MIN = np.int32(-2 ** 31)
SORT_ROWS = LANES // SUBLANES
RANK_NEVER = 3e38
CHUNK_SHIFT = B_CHUNK.bit_length() - 1
assert 1 << CHUNK_SHIFT == B_CHUNK


def _oddeven_merge_sort(n):
    pairs = []
    p = 1
    while p < n:
        k = p
        while k >= 1:
            for j in range(k % p, n - k, 2 * k):
                for i in range(min(k, n - j - k)):
                    if (i + j) // (2 * p) == (i + j + k) // (2 * p):
                        pairs.append((i + j, i + j + k))
            k //= 2
        p *= 2
    return tuple(pairs)


_SORT16 = _oddeven_merge_sort(SORT_ROWS)

_NT = (((1,), (1,)), ((), ()))


def _params(*sem):
    return pltpu.CompilerParams(dimension_semantics=sem, vmem_limit_bytes=VMEM_LIMIT)


def _rms(x, g):
    return x * lax.rsqrt(jnp.mean(x * x, axis=-1, keepdims=True) + EPS) * g


def _const_spec(shape):
    nd = len(shape)
    return pl.BlockSpec(shape, lambda *_: (0,) * nd)


def _layer_spec(w, layer):
    return pl.BlockSpec((None,) + w.shape[1:], lambda *_: (layer, 0, 0))


def _fold8(x, op):
    r, c = x.shape
    return op(x.reshape(r // SUBLANES, SUBLANES, c), axis=0)


def _store_key_major(o_ref, rows, tile):
    for u in range(rows.shape[0] // tile):
        o_ref[u] = rows[u * tile:(u + 1) * tile, :].T.astype(o_ref.dtype)


def _mla_up_project(cq, ckv, kpe_raw, cos, sin, gq, gkv, wq_ref, wkv_ref,
                    qn_ref, qpe_ref, kn_ref, vt_ref, kpe_ref, key_tile):
    nw = C_HEADS * C_NOPE
    pw = C_HEADS * C_ROPE
    cos2 = jnp.concatenate([cos] * (pw // LANES), axis=1)
    sin2 = jnp.concatenate([sin] * (pw // LANES), axis=1)
    q = jnp.dot(_rms(cq, gq).astype(BF16), wq_ref[...], preferred_element_type=F32)
    qn_ref[...] = q[:, :nw].astype(qn_ref.dtype)
    qpe_ref[...] = (q[:, nw:nw + pw] * cos2 + q[:, nw + pw:] * sin2).astype(qpe_ref.dtype)
    kv = jnp.dot(_rms(ckv, gkv).astype(BF16), wkv_ref[...], preferred_element_type=F32)
    kn_ref[...] = kv[:, :nw].astype(kn_ref.dtype)
    _store_key_major(vt_ref, kv[:, nw:], key_tile)
    kpe_ref[...] = (kpe_raw[:, :LANES] * cos + kpe_raw[:, LANES:] * sin).astype(kpe_ref.dtype)


N_LATENT = 3
N_MLA_IN = 6
N_MLA_OUT = 5


def _norm_matmul_kernel(x_ref, g_ref, *refs, widths, key_tiles, mla_key_tile, col_chunk):
    n_w = len(widths)
    w_refs, mla_in = refs[:n_w], refs[n_w:n_w + N_MLA_IN]
    out_refs, mla_out = refs[n_w + N_MLA_IN:-N_MLA_OUT], refs[-N_MLA_OUT:]
    h = _rms(x_ref[...], g_ref[...]).astype(BF16)
    k = 0
    latents = []
    for w_ref, w_widths in zip(w_refs, widths):
        c0 = 0
        for w in w_widths:
            if k >= len(out_refs):
                latents.append(jnp.dot(h, w_ref[:, c0:c0 + w], preferred_element_type=F32))
            elif key_tiles[k]:
                _store_key_major(out_refs[k], jnp.dot(h, w_ref[:, c0:c0 + w], preferred_element_type=F32),
                                 key_tiles[k])
            else:
                for s in range(0, w, col_chunk):
                    e = min(s + col_chunk, w)
                    out_refs[k][:, s:e] = jnp.dot(h, w_ref[:, c0 + s:c0 + e],
                                                  preferred_element_type=F32).astype(out_refs[k].dtype)
            c0 += w
            k += 1
    cos_ref, sin_ref, gq_ref, gkv_ref, wq_ref, wkv_ref = mla_in
    _mla_up_project(*latents, cos_ref[...], sin_ref[...], gq_ref[...], gkv_ref[...], wq_ref, wkv_ref,
                    *mla_out, mla_key_tile)


def _norm_matmul(x, g, ws, layer, widths, dtypes, key_tiles, mla, tm):
    cos_t, sin_t, gq, gkv, wq, wkv, mla_tile = mla
    n, k = x.shape
    flat = [wd for w_widths in widths for wd in w_widths]
    assert n % tm == 0 and all(w.shape[1:] == (k, sum(ww)) for w, ww in zip(ws, widths))
    kern = functools.partial(_norm_matmul_kernel, widths=tuple(tuple(ww) for ww in widths),
                             key_tiles=tuple(key_tiles), mla_key_tile=mla_tile, col_chunk=512)
    row = lambda w: pl.BlockSpec((tm, w), lambda i: (i, 0))
    tiles = lambda w, kt: pl.BlockSpec((tm // kt, w, kt), lambda i: (i, 0, 0))
    specs, shapes = [], []
    for wd, dt, kt in list(zip(flat, dtypes, key_tiles))[:-N_LATENT]:
        specs.append(tiles(wd, kt) if kt else row(wd))
        shapes.append(jax.ShapeDtypeStruct((n // kt, wd, kt) if kt else (n, wd), dt))
    nw, vw = C_HEADS * C_NOPE, C_HEADS * C_V
    for wd in (nw, C_HEADS * C_ROPE, nw, None, LANES):
        specs.append(tiles(vw, mla_tile) if wd is None else row(wd))
        shapes.append(jax.ShapeDtypeStruct((n // mla_tile, vw, mla_tile) if wd is None else (n, wd), BF16))
    return pl.pallas_call(
        kern,
        grid=(n // tm,),
        in_specs=[row(k), _const_spec((1, k))] + [_layer_spec(w, layer) for w in ws]
                 + [row(LANES), row(LANES), _const_spec((1, C_Q_RANK)), _const_spec((1, C_KV_RANK)),
                    _layer_spec(wq, layer), _layer_spec(wkv, layer)],
        out_specs=specs,
        out_shape=shapes,
        compiler_params=_params("parallel"),
        name="norm_proj",
    )(x, g.reshape(1, k), *ws, cos_t, sin_t, gq.reshape(1, -1), gkv.reshape(1, -1), wq, wkv)


def _dsa_kernel(qa_ref, iq_ref, iw_ref, ka_ref, vt_ref, ik_ref, bias_ref, o_ref,
                keys_ref, sorted_ref, s_ref, qm_ref, qs_ref, wt_ref, thr_ref, ties_ref,
                *, topk, n_blocks, group, scale, idx_scale):
    t = LANES
    gt = group * t
    i = pl.program_id(1)
    n_groups = (i + group) // group
    max_groups = n_blocks // group
    krow = lax.broadcasted_iota(I32, (t, t), 0)
    qcol = lax.broadcasted_iota(I32, (t, t), 1)
    lane_lo = qcol < IDX_DIM

    for h in range(IDX_HEADS):
        pair = iq_ref[0, :, (h // 2) * t:(h // 2 + 1) * t]
        keep = lane_lo if h % 2 == 0 else jnp.logical_not(lane_lo)
        qm_ref[h * t:(h + 1) * t, :] = jnp.where(keep, pair, jnp.zeros_like(pair))
    for h in range(A_HEADS):
        qs_ref[h * t:(h + 1) * t, :] = qa_ref[0, :, h * t:(h + 1) * t]
    wt_ref[...] = iw_ref[0].T

    def index_logits(g):
        return lax.dot_general(ik_ref[0, g * gt:(g + 1) * gt, :], qm_ref[...], _NT,
                               preferred_element_type=F32)

    def score_group(g, lg, ng):
        for u in range(group):
            j = g * group + u
            acc = jnp.zeros((t, t), F32)
            for h in range(IDX_HEADS):
                acc = acc + jnp.maximum(lg[u * t:(u + 1) * t, h * t:(h + 1) * t], 0.0) * wt_ref[h:h + 1, :]
            score = acc * idx_scale
            if j >= (ng - 1) * group:
                score = jnp.where(krow + j * t <= qcol + i * t, score, NEG_BIG)
            score = jnp.where(score == 0.0, 0.0, score)
            bits = pltpu.bitcast(score, I32)
            key = bits ^ ((bits >> 31) & np.int32(0x7FFFFFFF))
            keys_ref[j] = key
            rows = [key[r * SUBLANES:(r + 1) * SUBLANES, :] for r in range(SORT_ROWS)]
            for a, b in _SORT16:
                rows[a], rows[b] = jnp.maximum(rows[a], rows[b]), jnp.minimum(rows[a], rows[b])
            for r in range(SORT_ROWS):
                sorted_ref[j, r] = rows[r]

    def score_all(ng):
        lg = index_logits(0)
        for g in range(ng):
            nxt = index_logits(g + 1) if g + 1 < ng else None
            score_group(g, lg, ng)
            lg = nxt

    for v in range(1, max_groups + 1):
        pl.when(n_groups == v)(functools.partial(score_all, v))

    thr_ref[...] = jnp.full(thr_ref.shape, INT_MIN, I32)
    ties_ref[...] = jnp.zeros(ties_ref.shape, F32)

    def count(n, pred):
        acc = [jnp.zeros((SUBLANES, t), F32) for _ in range(5)]
        for j in range(n):
            v = [sorted_ref[j, r] for r in range(SORT_ROWS)]
            m1 = pred(v[7])
            m2 = pred(jnp.where(m1, v[11], v[3]))
            m3 = pred(jnp.where(m1, jnp.where(m2, v[13], v[9]), jnp.where(m2, v[5], v[1])))
            m4 = pred(jnp.where(m1, jnp.where(m2, jnp.where(m3, v[14], v[12]), jnp.where(m3, v[10], v[8])),
                                jnp.where(m2, jnp.where(m3, v[6], v[4]), jnp.where(m3, v[2], v[0]))))
            m5 = pred(v[15])
            acc = [a + jnp.where(m, 1.0, 0.0) for a, m in zip(acc, (m1, m2, m3, m4, m5))]
        cnt = 8.0 * acc[0] + 4.0 * acc[1] + 2.0 * acc[2] + acc[3] + acc[4]
        return jnp.sum(cnt, axis=0, keepdims=True)

    def search(n):
        def search_pass(b, thr):
            cand = thr + lax.shift_left(np.int32(1), 31 - b)
            return jnp.where(count(n, lambda key: key >= cand) >= topk, cand, thr)

        thr = lax.fori_loop(0, 32, search_pass, jnp.full((1, t), INT_MIN, I32))
        thr_ref[...] = jnp.broadcast_to(thr, thr_ref.shape)
        ties_ref[...] = jnp.broadcast_to(topk - count(n, lambda key: key > thr), ties_ref.shape)

    for c in range(n_blocks):
        if (c + 1) * t > topk:
            pl.when(i == c)(functools.partial(search, c + 1))

    thr = thr_ref[0:1, :]
    n_ties = ties_ref[0:1, :]

    r2 = lax.broadcasted_iota(I32, (2 * t, t), 0)
    c2 = lax.broadcasted_iota(I32, (2 * t, t), 1)
    tie_lhs = jnp.where(jnp.logical_or(r2 >= t, c2 < r2), 1.0, 0.0).astype(BF16)

    def sweep_a_mxu(g):
        keys = [keys_ref[g * group + u] for u in range(group)]
        eqs = [key == thr for key in keys]
        eq_all = jnp.concatenate([jnp.where(eq, 1.0, 0.0).astype(BF16) for eq in eqs], axis=1)
        pref = jnp.dot(tie_lhs, eq_all, preferred_element_type=F32)
        s_grp = lax.dot_general(ka_ref[0, g * gt:(g + 1) * gt, :], qs_ref[...], _NT,
                                preferred_element_type=F32)
        return keys, eqs, pref, s_grp

    def sweep_a_vpu(g, operands, seen, mx, ng):
        keys, eqs, pref, s_grp = operands
        for u in range(group):
            j = g * group + u
            us = slice(u * t, (u + 1) * t)
            rank = jnp.where(keys[u] > thr, -1.0, jnp.where(eqs[u], seen + pref[:t, us], RANK_NEVER))
            if j >= (ng - 1) * group:
                rank = jnp.where(krow + j * t <= qcol + i * t, rank, RANK_NEVER)
            valid = rank < n_ties
            which = jnp.where(j == i, 1, jnp.where(j == i - 1, 0, 2))
            s_all = s_grp[us, :] * (scale * math.log2(math.e))
            if j > (ng - 1) * group - 2:
                s_all = s_all + bias_ref[which]
            for h in range(A_HEADS):
                s_h = jnp.where(valid, s_all[:, h * t:(h + 1) * t], NEG_BIG)
                s_ref[j, :, h * t:(h + 1) * t] = s_h
                mx[h] = jnp.maximum(mx[h], _fold8(s_h, jnp.max))
            seen = seen + pref[t:t + 1, us]
        return seen, mx

    def sweep_b(g, m_all, l8):
        ps = []
        for u in range(group):
            p = jnp.exp2(s_ref[g * group + u] - m_all)
            l8 = l8 + _fold8(p, jnp.sum)
            ps.append(p.astype(BF16))
        vt_grp = jnp.concatenate([vt_ref[0, g * group + u] for u in range(group)], axis=1)
        return l8, jnp.dot(vt_grp, jnp.concatenate(ps, axis=0), preferred_element_type=F32)

    def attend(ng):
        seen = jnp.zeros((1, t), F32)
        mx = [jnp.full((SUBLANES, t), NEG_BIG, F32) for _ in range(A_HEADS)]
        operands = sweep_a_mxu(0)
        for g in range(ng):
            nxt = sweep_a_mxu(g + 1) if g + 1 < ng else None
            seen, mx = sweep_a_vpu(g, operands, seen, mx, ng)
            operands = nxt
        m_all = jnp.concatenate([jnp.max(m, axis=0, keepdims=True) for m in mx], axis=1)
        l8 = jnp.zeros((SUBLANES, A_HEADS * t), F32)
        acc = None
        for g in range(ng):
            l8, pv = sweep_b(g, m_all, l8)
            acc = pv if acc is None else acc + pv
        out = acc * (1.0 / jnp.sum(l8, axis=0, keepdims=True))
        for h in range(A_HEADS):
            o_ref[0, :, h * t:(h + 1) * t] = out[:, h * t:(h + 1) * t].T.astype(o_ref.dtype)

    for v in range(1, max_groups + 1):
        pl.when(n_groups == v)(functools.partial(attend, v))


def _dsa(qa, iq, iw, ka, vt, ik2, bias_tab):
    b, s, _ = qa.shape
    t = LANES
    nb = s // t
    group = math.gcd(nb, 2)
    topk = min(TOPK_MAX, s // 4)
    kern = functools.partial(_dsa_kernel, topk=float(topk), n_blocks=nb, group=group,
                             scale=A_HEAD_DIM ** -0.5,
                             idx_scale=(IDX_DIM ** -0.5) * (IDX_HEADS ** -0.5))
    qspec = lambda w: pl.BlockSpec((1, t, w), lambda bi, i: (bi, i, 0))
    kspec = pl.BlockSpec((1, s, t), lambda bi, i: (bi, 0, 0))
    return pl.pallas_call(
        kern,
        grid=(b, nb),
        in_specs=[qspec(A_HEADS * t), qspec(IDX_HEADS * IDX_DIM), qspec(t), kspec,
                  pl.BlockSpec((1, nb, t, t), lambda bi, i: (bi, 0, 0, 0)), kspec,
                  _const_spec(bias_tab.shape)],
        out_specs=qspec(A_HEADS * t),
        out_shape=jax.ShapeDtypeStruct((b, s, A_HEADS * t), BF16),
        scratch_shapes=[pltpu.VMEM((nb, t, t), I32),
                        pltpu.VMEM((nb, SORT_ROWS, SUBLANES, t), I32),
                        pltpu.VMEM((nb, t, A_HEADS * t), F32),
                        pltpu.VMEM((IDX_HEADS * t, t), BF16),
                        pltpu.VMEM((A_HEADS * t, t), BF16),
                        pltpu.VMEM((t, t), F32),
                        pltpu.VMEM((SUBLANES, t), I32),
                        pltpu.VMEM((SUBLANES, t), F32)],
        compiler_params=_params("parallel", "parallel"),
        name="dsa_attention",
    )(qa, iq, iw, ka, vt, ik2, bias_tab)


def _hgrn_kernel(q_ref, f_ref, i_ref, g_ref, lb_ref, gain_ref, o_ref, state_ref, *, n_chunks, n_seq):
    c = B_CHUNK
    rc = n_seq * c
    kd, vd = B_KEY_DIM, B_VAL_DIM
    w = B_HEADS * kd

    @pl.when(pl.program_id(1) == 0)
    def _():
        state_ref[...] = jnp.zeros_like(state_ref)

    row = lax.broadcasted_iota(I32, (rc, rc), 0)
    col = lax.broadcasted_iota(I32, (rc, rc), 1)
    seq_of = lambda r: lax.shift_right_logical(r, CHUNK_SHIFT)
    causal = (row >= col) if n_seq == 1 else jnp.logical_and(row >= col, seq_of(row) == seq_of(col))
    tril = jnp.where(causal, 1.0, 0.0).astype(BF16)
    seqs = [slice(b * c, (b + 1) * c) for b in range(n_seq)]
    seq_of_row = seq_of(lax.broadcasted_iota(I32, (rc, w), 0))

    def per_seq_row(x, r):
        return jnp.concatenate([jnp.broadcast_to(x[b * c + r:b * c + r + 1, :], (c, x.shape[1]))
                                for b in range(n_seq)], axis=0)

    def stacked(ref, rows):
        return jnp.concatenate([ref[b, rows, :] for b in range(n_seq)], axis=0)
    q_scale = kd ** -0.5
    lb = lb_ref[...]
    lb_floor = jnp.maximum(lb, LB_FLOOR)
    one_m_lb = 1.0 - lb
    heads = [slice(h * kd, (h + 1) * kd) for h in range(B_HEADS)]

    def stage_decay(ci):
        rows = slice(ci * c, (ci + 1) * c)
        fr = stacked(f_ref, rows)
        z = jnp.exp(-jnp.abs(fr))
        r = 1.0 / (1.0 + z)
        sig_pos = jnp.where(fr >= 0, r, z * r)
        sig_neg = jnp.where(fr >= 0, z * r, r)
        log_f = jnp.log(lb_floor + one_m_lb * sig_pos)
        k_in = one_m_lb * sig_neg
        hi = log_f.astype(BF16)
        rest = log_f - hi.astype(F32)
        mid = rest.astype(BF16)
        lo = (rest - mid.astype(F32)).astype(BF16)
        cs = jnp.dot(tril, jnp.concatenate([hi, mid, lo], axis=1), preferred_element_type=F32)
        return k_in, cs

    def apply_update(pending):
        if pending is not None:
            d_last, st, upd = pending
            for b in range(n_seq):
                for h, hs in enumerate(heads):
                    state_ref[b, h] = d_last[b * c:b * c + 1, hs] * st[b][h] + upd[b][h]

    def stage_scores(ci, decay, pending):
        rows = slice(ci * c, (ci + 1) * c)
        k_in, cs = decay
        bsum = cs[:, :w] + cs[:, w:2 * w] + cs[:, 2 * w:]
        b_mid = per_seq_row(bsum, c // 2 - 1)
        b_last = per_seq_row(bsum, c - 1)
        qs = stacked(q_ref, rows) * q_scale
        v = stacked(i_ref, rows)
        v_b = v.astype(BF16)
        v_t = v.T.astype(BF16)
        q_mid = (qs * jnp.exp(bsum - b_mid)).astype(BF16)
        k_mid = (k_in * jnp.exp(b_mid - bsum)).astype(BF16)
        q_dec = (qs * jnp.exp(bsum)).astype(BF16)
        k_end = (k_in * jnp.exp(b_last - bsum)).astype(BF16)
        d_last = jnp.exp(b_last)
        apply_update(pending)
        st = [[state_ref[b, h] for h in range(B_HEADS)] for b in range(n_seq)]
        attn = [lax.dot_general(q_mid[:, hs], k_mid[:, hs], _NT, preferred_element_type=F32)
                for hs in heads]
        inter = [jnp.concatenate(
            [lax.dot_general(q_dec[sq, hs], st[b][h].astype(BF16), _NT, preferred_element_type=F32)
             for b, sq in enumerate(seqs)], axis=0) for h, hs in enumerate(heads)]
        k_seq = [k_end if n_seq == 1 else jnp.where(seq_of_row == b, k_end, jnp.zeros_like(k_end))
                 for b in range(n_seq)]
        upd = [[jnp.dot(v_t[hs, :], k_seq[b][:, hs], preferred_element_type=F32) for hs in heads]
               for b in range(n_seq)]
        return (attn, inter, v_b), (d_last, st, upd)

    def stage_output(ci, scores):
        rows = slice(ci * c, (ci + 1) * c)
        attn, inter, v_b = scores
        attn = [jnp.where(causal, a, 0.0).astype(BF16) for a in attn]
        outs = [jnp.dot(attn[h], v_b[:, hs], preferred_element_type=F32) + inter[h]
                for h, hs in enumerate(heads)]
        o = jnp.concatenate([_rms(o_h, gain_ref[...]) for o_h in outs], axis=1)
        g = stacked(g_ref, rows)
        o = (o * (g * (1.0 / (1.0 + jnp.exp(-g))))).astype(o_ref.dtype)
        for b, sq in enumerate(seqs):
            o_ref[b, rows, :] = o[sq, :]

    decay, scores, pending = {}, {}, None
    for step in range(n_chunks + 2):
        if step < n_chunks:
            decay[step] = stage_decay(step)
        if 0 <= step - 1 < n_chunks:
            scores[step - 1], pending = stage_scores(step - 1, decay.pop(step - 1), pending)
        if 0 <= step - 2 < n_chunks:
            stage_output(step - 2, scores.pop(step - 2))
    apply_update(pending)


def _hgrn(bq, bf, bi, bg, lb, gain, sb):
    b, s, w = bq.shape
    assert s % sb == 0 and sb % B_CHUNK == 0
    n_seq = 2 if b % 2 == 0 else 1
    kern = functools.partial(_hgrn_kernel, n_chunks=sb // B_CHUNK, n_seq=n_seq)
    spec = pl.BlockSpec((n_seq, sb, w), lambda bi_, si: (bi_, si, 0))
    return pl.pallas_call(
        kern,
        grid=(b // n_seq, s // sb),
        in_specs=[spec, spec, spec, spec, _const_spec((1, w)), _const_spec((1, B_VAL_DIM))],
        out_specs=spec,
        out_shape=jax.ShapeDtypeStruct((b, s, w), BF16),
        scratch_shapes=[pltpu.VMEM((n_seq, B_HEADS, B_VAL_DIM, B_KEY_DIM), F32)],
        compiler_params=_params("parallel", "arbitrary"),
        name="hgrn2",
    )(bq, bf, bi, bg, lb.reshape(1, w), gain.reshape(1, B_VAL_DIM))


def _mla_kernel(qn_ref, qpe_ref, kn_ref, kpe_ref, vt_ref, o_ref, q_ref, acc_ref, *, t, n_tiles, scale):
    i = pl.program_id(1)
    lane_lo = lax.broadcasted_iota(I32, (t, LANES), 1) < C_ROPE
    for h in range(C_HEADS):
        pair = qpe_ref[0, :, (h // 2) * LANES:(h // 2 + 1) * LANES]
        keep = lane_lo if h % 2 == 0 else jnp.logical_not(lane_lo)
        q_ref[h, :, :C_NOPE] = qn_ref[0, :, h * C_NOPE:(h + 1) * C_NOPE]
        q_ref[h, :, C_NOPE:] = jnp.where(keep, pair, jnp.zeros_like(pair))
    acc_ref[...] = jnp.zeros_like(acc_ref)

    def logits_of(step):
        first, count = step
        rows = slice(first * t, (first + count) * t)
        kpe_t = kpe_ref[0, rows, :]
        return [lax.dot_general(
            jnp.concatenate([kn_ref[0, rows, h * C_NOPE:(h + 1) * C_NOPE], kpe_t], axis=1),
            q_ref[h], _NT, preferred_element_type=F32) for h in range(C_HEADS)]

    def softmax_pv(step, block, logits, ms, ls, masked):
        first, count = step
        new_m, new_l = [], []
        for h in range(C_HEADS):
            s = logits[h] * (scale * math.log2(math.e))
            if masked:
                key_pos = lax.broadcasted_iota(I32, (count * t, t), 0) + first * t
                query_pos = lax.broadcasted_iota(I32, (count * t, t), 1) + block * t
                s = jnp.where(key_pos <= query_pos, s, NEG_BIG)
            m_new = jnp.maximum(ms[h], jnp.max(s, axis=0, keepdims=True))
            alpha = jnp.exp2(ms[h] - m_new)
            p = jnp.exp2(s - m_new)
            new_l.append(alpha * ls[h] + jnp.sum(p, axis=0, keepdims=True))
            new_m.append(m_new)
            hs = slice(h * C_V, (h + 1) * C_V)
            tiles = [vt_ref[0, first + u, hs, :] for u in range(count)]
            v_t = tiles[0] if count == 1 else jnp.concatenate(tiles, axis=1)
            acc_ref[h] = alpha * acc_ref[h] + jnp.dot(v_t, p.astype(BF16), preferred_element_type=F32)
        return new_m, new_l

    def run(block):
        steps = [(2 * g, 2) for g in range(block // 2)] + [(block - block % 2, 1 + block % 2)]
        ms = [jnp.full((1, t), NEG_BIG, F32) for _ in range(C_HEADS)]
        ls = [jnp.zeros((1, t), F32) for _ in range(C_HEADS)]
        logits = logits_of(steps[0])
        for k, step in enumerate(steps):
            nxt = logits_of(steps[k + 1]) if k + 1 < len(steps) else None
            ms, ls = softmax_pv(step, block, logits, ms, ls, masked=(k + 1 == len(steps)))
            logits = nxt
        for h in range(C_HEADS):
            o_ref[0, :, h * C_V:(h + 1) * C_V] = (acc_ref[h] * (1.0 / ls[h])).T.astype(o_ref.dtype)

    for v in range(n_tiles):
        pl.when(i == v)(functools.partial(run, v))


def _mla(qn, qpe, kn, kpe, vt, t):
    b, s, _ = qn.shape
    nt = s // t
    kern = functools.partial(_mla_kernel, t=t, n_tiles=nt, scale=(C_NOPE + C_ROPE) ** -0.5)
    qspec = lambda w: pl.BlockSpec((1, t, w), lambda bi, i: (bi, i, 0))
    kspec = lambda w: pl.BlockSpec((1, s, w), lambda bi, i: (bi, 0, 0))
    return pl.pallas_call(
        kern,
        grid=(b, nt),
        in_specs=[qspec(qn.shape[2]), qspec(qpe.shape[2]), kspec(kn.shape[2]), kspec(kpe.shape[2]),
                  pl.BlockSpec((1, nt, C_HEADS * C_V, t), lambda bi, i: (bi, 0, 0, 0))],
        out_specs=qspec(C_HEADS * C_V),
        out_shape=jax.ShapeDtypeStruct((b, s, C_HEADS * C_V), BF16),
        scratch_shapes=[pltpu.VMEM((C_HEADS, t, C_NOPE + LANES), BF16),
                        pltpu.VMEM((C_HEADS, C_V, t), F32)],
        compiler_params=_params("parallel", "parallel"),
        name="mla_attention",
    )(qn, qpe, kn, kpe, vt)


def _merge_kernel(x_ref, g_ref, wg_ref, oa_ref, ob_ref, oc_ref, wa_ref, wb_ref, wc_ref, wo_ref,
                  out_ref, mixed_ref, *, col_chunk):
    x = x_ref[...]
    d = x.shape[1]
    h = _rms(x, g_ref[...]).astype(BF16)
    branches = ((oa_ref, wa_ref), (ob_ref, wb_ref), (oc_ref, wc_ref))
    for s in range(0, d, col_chunk):
        cs = slice(s, s + col_chunk)
        mixed = None
        for bidx, (o_ref, w_ref) in enumerate(branches):
            logits = jnp.dot(h, wg_ref[:, bidx * d + s:bidx * d + s + col_chunk],
                             preferred_element_type=F32)
            gate = 1.0 / (1.0 + jnp.exp(-logits))
            term = gate * jnp.dot(o_ref[...], w_ref[:, cs], preferred_element_type=F32)
            mixed = term if mixed is None else mixed + term
        mixed_ref[:, cs] = mixed.astype(BF16)
    out_ref[...] = x + jnp.dot(mixed_ref[...], wo_ref[...], preferred_element_type=F32)


def _merge(x, g, w_gate, oa, ob, oc, wa, wb, wc, wo, layer, tm):
    n, d = x.shape
    row = lambda w: pl.BlockSpec((tm, w), lambda i: (i, 0))
    kern = functools.partial(_merge_kernel, col_chunk=256)
    return pl.pallas_call(
        kern,
        grid=(n // tm,),
        in_specs=[row(d), _const_spec((1, d)), _layer_spec(w_gate, layer),
                  row(oa.shape[1]), row(ob.shape[1]), row(oc.shape[1]),
                  _layer_spec(wa, layer), _layer_spec(wb, layer), _layer_spec(wc, layer),
                  _layer_spec(wo, layer)],
        out_specs=row(d),
        out_shape=jax.ShapeDtypeStruct((n, d), F32),
        scratch_shapes=[pltpu.VMEM((tm, d), BF16)],
        compiler_params=_params("parallel"),
        name="merge_out_proj",
    )(x, g.reshape(1, d), w_gate, oa, ob, oc, wa, wb, wc, wo)


def _ffn_kernel(x_ref, g_ref, wg_ref, wu_ref, wd_ref, gf_ref, out_ref, act_ref, *, col_chunk,
                final_norm):
    x = x_ref[...]
    h = _rms(x, g_ref[...]).astype(BF16)
    dff = wg_ref.shape[1]
    for s in range(0, dff, col_chunk):
        cs = slice(s, s + col_chunk)
        gate = jnp.dot(h, wg_ref[:, cs], preferred_element_type=F32)
        up = jnp.dot(h, wu_ref[:, cs], preferred_element_type=F32)
        act_ref[:, cs] = (gate * (1.0 / (1.0 + jnp.exp(-gate))) * up).astype(BF16)
    y = x + jnp.dot(act_ref[...], wd_ref[...], preferred_element_type=F32)
    if final_norm:
        y = _rms(y, gf_ref[...])
    out_ref[...] = y


def _ffn(x, g, wg, wu, wd, gf, layer, tm, final_norm):
    n, d = x.shape
    dff = wg.shape[2]
    row = pl.BlockSpec((tm, d), lambda i: (i, 0))
    kern = functools.partial(_ffn_kernel, col_chunk=256, final_norm=final_norm)
    return pl.pallas_call(
        kern,
        grid=(n // tm,),
        in_specs=[row, _const_spec((1, d)), _layer_spec(wg, layer), _layer_spec(wu, layer),
                  _layer_spec(wd, layer), _const_spec((1, d))],
        out_specs=row,
        out_shape=jax.ShapeDtypeStruct((n, d), F32),
        scratch_shapes=[pltpu.VMEM((tm, dff), BF16)],
        compiler_params=_params("parallel"),
        name="swiglu_ffn",
    )(x, g.reshape(1, d), wg, wu, wd, gf.reshape(1, d))


def _t5_bucket(dist):
    max_exact = REL_BUCKETS // 2
    d = jnp.maximum(dist, 0)
    dl = jnp.maximum(d, max_exact).astype(F32)
    large = max_exact + (jnp.log(dl / max_exact) / math.log(REL_MAX_DIST / max_exact)
                         * (REL_BUCKETS - max_exact)).astype(I32)
    large = jnp.minimum(large, REL_BUCKETS - 1)
    return jnp.where(d < max_exact, d, large)


def _swap_halves(w):
    half = w.shape[-1] // 2
    return jnp.concatenate([w[..., half:], w[..., :half]], axis=-1)


def _in_proj_weights(w_in, d_model):
    splits = (A_HEADS * A_HEAD_DIM, A_HEAD_DIM, A_HEAD_DIM, IDX_HEADS * IDX_DIM, IDX_DIM, IDX_HEADS,
              B_HEADS * B_KEY_DIM, B_HEADS * B_KEY_DIM, B_HEADS * B_VAL_DIM, B_HEADS * B_VAL_DIM,
              C_Q_RANK, C_KV_RANK, C_ROPE, N_BRANCH * d_model)
    st = [int(v) for v in np.concatenate([[0], np.cumsum(splits)])]
    depth, k, n_cols = w_in.shape
    w_pad = jnp.pad(w_in.astype(BF16), ((0, 0), (0, 0), (0, -n_cols % LANES)))
    n_pad = w_pad.shape[2]
    reps = LANES // C_ROPE
    rows = 128

    def regroup_kernel(w_ref, a_ref, b_ref, c_ref, d_ref, g_ref):
        col = lambda a, b: w_ref[:, st[a]:st[b]]
        ik, pe = col(4, 5), col(12, 13)
        iw_pad = jnp.zeros((rows, LANES - IDX_HEADS), w_ref.dtype)
        a_ref[...] = col(0, 4).astype(BF16)
        b_ref[...] = jnp.concatenate([ik, ik, col(5, 6), iw_pad], axis=1).astype(BF16)
        c_ref[...] = col(6, 12).astype(BF16)
        d_ref[...] = jnp.concatenate([pe] * reps + [_swap_halves(pe)] * reps, axis=1).astype(BF16)
        g_ref[...] = col(13, 14).astype(BF16)

    out_w = (st[4], 2 * LANES, st[12] - st[6], 2 * LANES, st[14] - st[13])
    outs = pl.pallas_call(
        regroup_kernel,
        grid=(depth, k // rows),
        in_specs=[pl.BlockSpec((None, rows, n_pad), lambda l, i: (l, i, 0))],
        out_specs=[pl.BlockSpec((None, rows, wd), lambda l, i: (l, i, 0)) for wd in out_w],
        out_shape=[jax.ShapeDtypeStruct((depth, k, wd), BF16) for wd in out_w],
        compiler_params=_params("parallel", "parallel"),
        name="regroup_w_in",
    )(w_pad)
    ws = list(outs)
    widths = [splits[:4], (LANES, LANES), splits[6:12], (2 * LANES,)]
    dtypes = [BF16] * 5 + [F32] * 8
    return ws[:4], widths, dtypes, ws[4]


def kernel(x, positions, w_in, w_up_a, w_up_b, w_up_c, w_out, mla_q_norm, mla_w_qb, mla_kv_norm,
           mla_w_kvb, hgrn_lb_logits, hgrn_out_norm, rel_bias, attn_norm, ffn_norm, w_ffn_gate,
           w_ffn_up, w_ffn_down, final_norm):
    bsz, s_len, d_model = x.shape
    depth = w_in.shape[0]
    n = bsz * s_len
    t = LANES

    w_proj, widths, dtypes, w_gate = _in_proj_weights(w_in, d_model)
    wq = mla_w_qb.reshape(depth, C_Q_RANK, C_HEADS, C_NOPE + C_ROPE)
    q_pe = wq[..., C_NOPE:]
    w_qb = jnp.concatenate([wq[..., :C_NOPE].reshape(depth, C_Q_RANK, -1),
                            q_pe.reshape(depth, C_Q_RANK, -1),
                            _swap_halves(q_pe).reshape(depth, C_Q_RANK, -1)], axis=2).astype(BF16)
    wkv = mla_w_kvb.reshape(depth, C_KV_RANK, C_HEADS, C_NOPE + C_V)
    w_kvb = jnp.concatenate([wkv[..., :C_NOPE].reshape(depth, C_KV_RANK, -1),
                             wkv[..., C_NOPE:].reshape(depth, C_KV_RANK, -1)], axis=2).astype(BF16)
    bf = lambda w: w.astype(BF16)
    w_up_a, w_up_b, w_up_c, w_out = bf(w_up_a), bf(w_up_b), bf(w_up_c), bf(w_out)
    w_ffn_gate, w_ffn_up, w_ffn_down = bf(w_ffn_gate), bf(w_ffn_up), bf(w_ffn_down)

    p_lb = jax.nn.softmax(hgrn_lb_logits.astype(F32), axis=0)
    lower_bounds = jnp.cumsum(p_lb, axis=0) - p_lb[0:1]
    inv_freq = ROPE_THETA ** (-jnp.arange(0, C_ROPE, 2, dtype=F32) / C_ROPE)
    ang = positions.astype(F32)[..., None] * inv_freq
    cos, sin = jnp.cos(ang), jnp.sin(ang)
    reps = LANES // C_ROPE
    cos_t = jnp.tile(jnp.concatenate([cos, cos], axis=-1), (1, 1, reps)).reshape(n, LANES)
    sin_t = jnp.tile(jnp.concatenate([-sin, sin], axis=-1), (1, 1, reps)).reshape(n, LANES)
    q_idx = jnp.arange(t, dtype=I32)[None, :]
    k_idx = jnp.arange(t, dtype=I32)[:, None]
    def tab(dist):
        onehot = jax.nn.one_hot(_t5_bucket(dist), REL_BUCKETS, dtype=F32)
        return jnp.einsum("kqb,bh->khq", onehot, rel_bias.astype(F32),
                          precision=lax.Precision.HIGHEST).reshape(t, A_HEADS * t)
    far = jnp.broadcast_to(rel_bias[REL_BUCKETS - 1].astype(F32)[None, :, None],
                           (t, A_HEADS, t)).reshape(t, A_HEADS * t)
    bias_tab = (jnp.stack([tab(q_idx + t - k_idx), tab(q_idx - k_idx), far]) - far) * math.log2(math.e)

    tm = min(512, n)
    t_mla = min(512, s_len)
    x2 = x.reshape(n, d_model)
    r3 = lambda a: a.reshape(bsz, s_len, a.shape[-1])
    per_seq = lambda a: a.reshape(bsz, a.shape[0] // bsz, a.shape[1], a.shape[2])
    key_tiles = [0, 0, t] + [0] * 10
    for l in range(depth):
        mla = (cos_t, sin_t, mla_q_norm[l], mla_kv_norm[l], w_qb, w_kvb, t_mla)
        (qa, ka, vta, iq, ik2, iw, bq, bfr, bi, bg, qn, qpe, kn, vtc, kpe) = _norm_matmul(
            x2, attn_norm[l], w_proj, l, widths, dtypes, key_tiles, mla, tm)
        o_a = _dsa(r3(qa), r3(iq), r3(iw), r3(ka), per_seq(vta), r3(ik2), bias_tab)
        o_b = _hgrn(r3(bq), r3(bfr), r3(bi), r3(bg), lower_bounds[l], hgrn_out_norm[l],
                    sb=min(512, s_len))
        o_c = _mla(r3(qn), r3(qpe), r3(kn), r3(kpe), per_seq(vtc), t_mla)
        x2 = _merge(x2, attn_norm[l], w_gate, o_a.reshape(n, -1), o_b.reshape(n, -1),
                    o_c.reshape(n, -1), w_up_a, w_up_b, w_up_c, w_out, l, tm)
        x2 = _ffn(x2, ffn_norm[l], w_ffn_gate, w_ffn_up, w_ffn_down, final_norm,
                  l, tm, final_norm=(l == depth - 1))
    return x2.reshape(bsz, s_len, d_model)
```

```python
import functools
import math

import jax
import jax.numpy as jnp
import numpy as np
from jax import lax
from jax.experimental import pallas as pl
from jax.experimental.pallas import tpu as pltpu

F32 = jnp.float32
BF16 = jnp.bfloat16
I32 = jnp.int32

A_HEADS = 4
A_HEAD_DIM = 128
IDX_HEADS = 8
IDX_DIM = 64
TOPK_MAX = 256
B_HEADS = 4
B_KEY_DIM = 128
B_VAL_DIM = 128
B_CHUNK = 64
C_HEADS = 4
C_Q_RANK = 384
C_KV_RANK = 256
C_NOPE = 128
C_ROPE = 64
C_V = 128
ROPE_THETA = 10000.0
REL_BUCKETS = 32
REL_MAX_DIST = 128
N_BRANCH = 3
EPS = 1e-6
NEG_BIG = -1e30
LB_FLOOR = 1e-30

LANES = 128
SUBLANES = 8
VMEM_LIMIT = 56 * 1024 * 1024
INT_MIN = np.int32(-2 ** 31)
SORT_ROWS = LANES // SUBLANES
RANK_NEVER = 3e38
CHUNK_SHIFT = B_CHUNK.bit_length() - 1
assert 1 << CHUNK_SHIFT == B_CHUNK


def _oddeven_merge_sort(n):
    pairs = []
    p = 1
    while p < n:
        k = p
        while k >= 1:
            for j in range(k % p, n - k, 2 * k):
                for i in range(min(k, n - j - k)):
                    if (i + j) // (2 * p) == (i + j + k) // (2 * p):
                        pairs.append((i + j, i + j + k))
            k //= 2
        p *= 2
    return tuple(pairs)


_SORT16 = _oddeven_merge_sort(SORT_ROWS)

_NT = (((1,), (1,)), ((), ()))


def _params(*sem):
    return pltpu.CompilerParams(dimension_semantics=sem, vmem_limit_bytes=VMEM_LIMIT)


def _rms(x, g):
    return x * lax.rsqrt(jnp.mean(x * x, axis=-1, keepdims=True) + EPS) * g


def _const_spec(shape):
    nd = len(shape)
    return pl.BlockSpec(shape, lambda *_: (0,) * nd)


def _layer_spec(w, layer):
    return pl.BlockSpec((None,) + w.shape[1:], lambda *_: (layer, 0, 0))


def _fold8(x, op):
    r, c = x.shape
    return op(x.reshape(r // SUBLANES, SUBLANES, c), axis=0)


def _store_key_major(o_ref, rows, tile):
    for u in range(rows.shape[0] // tile):
        o_ref[u] = rows[u * tile:(u + 1) * tile, :].T.astype(o_ref.dtype)


def _mla_up_project(cq, ckv, kpe_raw, cos, sin, gq, gkv, wq_ref, wkv_ref,
                    qn_ref, qpe_ref, kn_ref, vt_ref, kpe_ref, key_tile):
    nw = C_HEADS * C_NOPE
    pw = C_HEADS * C_ROPE
    cos2 = jnp.concatenate([cos] * (pw // LANES), axis=1)
    sin2 = jnp.concatenate([sin] * (pw // LANES), axis=1)
    q = jnp.dot(_rms(cq, gq).astype(BF16), wq_ref[...], preferred_element_type=F32)
    qn_ref[...] = q[:, :nw].astype(qn_ref.dtype)
    qpe_ref[...] = (q[:, nw:nw + pw] * cos2 + q[:, nw + pw:] * sin2).astype(qpe_ref.dtype)
    kv = jnp.dot(_rms(ckv, gkv).astype(BF16), wkv_ref[...], preferred_element_type=F32)
    kn_ref[...] = kv[:, :nw].astype(kn_ref.dtype)
    _store_key_major(vt_ref, kv[:, nw:], key_tile)
    kpe_ref[...] = (kpe_raw[:, :LANES] * cos + kpe_raw[:, LANES:] * sin).astype(kpe_ref.dtype)


N_LATENT = 3
N_MLA_IN = 6
N_MLA_OUT = 5


def _norm_matmul_kernel(x_ref, g_ref, *refs, widths, key_tiles, mla_key_tile, col_chunk):
    n_w = len(widths)
    w_refs, mla_in = refs[:n_w], refs[n_w:n_w + N_MLA_IN]
    out_refs, mla_out = refs[n_w + N_MLA_IN:-N_MLA_OUT], refs[-N_MLA_OUT:]
    h = _rms(x_ref[...], g_ref[...]).astype(BF16)
    k = 0
    latents = []
    for w_ref, w_widths in zip(w_refs, widths):
        c0 = 0
        for w in w_widths:
            if k >= len(out_refs):
                latents.append(jnp.dot(h, w_ref[:, c0:c0 + w], preferred_element_type=F32))
            elif key_tiles[k]:
                _store_key_major(out_refs[k], jnp.dot(h, w_ref[:, c0:c0 + w], preferred_element_type=F32),
                                 key_tiles[k])
            else:
                for s in range(0, w, col_chunk):
                    e = min(s + col_chunk, w)
                    out_refs[k][:, s:e] = jnp.dot(h, w_ref[:, c0 + s:c0 + e],
                                                  preferred_element_type=F32).astype(out_refs[k].dtype)
            c0 += w
            k += 1
    cos_ref, sin_ref, gq_ref, gkv_ref, wq_ref, wkv_ref = mla_in
    _mla_up_project(*latents, cos_ref[...], sin_ref[...], gq_ref[...], gkv_ref[...], wq_ref, wkv_ref,
                    *mla_out, mla_key_tile)


def _norm_matmul(x, g, ws, layer, widths, dtypes, key_tiles, mla, tm):
    cos_t, sin_t, gq, gkv, wq, wkv, mla_tile = mla
    n, k = x.shape
    flat = [wd for w_widths in widths for wd in w_widths]
    assert n % tm == 0 and all(w.shape[1:] == (k, sum(ww)) for w, ww in zip(ws, widths))
    kern = functools.partial(_norm_matmul_kernel, widths=tuple(tuple(ww) for ww in widths),
                             key_tiles=tuple(key_tiles), mla_key_tile=mla_tile, col_chunk=512)
    row = lambda w: pl.BlockSpec((tm, w), lambda i: (i, 0))
    tiles = lambda w, kt: pl.BlockSpec((tm // kt, w, kt), lambda i: (i, 0, 0))
    specs, shapes = [], []
    for wd, dt, kt in list(zip(flat, dtypes, key_tiles))[:-N_LATENT]:
        specs.append(tiles(wd, kt) if kt else row(wd))
        shapes.append(jax.ShapeDtypeStruct((n // kt, wd, kt) if kt else (n, wd), dt))
    nw, vw = C_HEADS * C_NOPE, C_HEADS * C_V
    for wd in (nw, C_HEADS * C_ROPE, nw, None, LANES):
        specs.append(tiles(vw, mla_tile) if wd is None else row(wd))
        shapes.append(jax.ShapeDtypeStruct((n // mla_tile, vw, mla_tile) if wd is None else (n, wd), BF16))
    return pl.pallas_call(
        kern,
        grid=(n // tm,),
        in_specs=[row(k), _const_spec((1, k))] + [_layer_spec(w, layer) for w in ws]
                 + [row(LANES), row(LANES), _const_spec((1, C_Q_RANK)), _const_spec((1, C_KV_RANK)),
                    _layer_spec(wq, layer), _layer_spec(wkv, layer)],
        out_specs=specs,
        out_shape=shapes,
        compiler_params=_params("parallel"),
        name="norm_proj",
    )(x, g.reshape(1, k), *ws, cos_t, sin_t, gq.reshape(1, -1), gkv.reshape(1, -1), wq, wkv)


def _dsa_kernel(qa_ref, iq_ref, iw_ref, ka_ref, vt_ref, ik_ref, bias_ref, o_ref,
                keys_ref, sorted_ref, s_ref, qm_ref, qs_ref, wt_ref, thr_ref, ties_ref,
                *, topk, n_blocks, group, scale, idx_scale):
    t = LANES
    gt = group * t
    i = pl.program_id(1)
    n_groups = (i + group) // group
    max_groups = n_blocks // group
    krow = lax.broadcasted_iota(I32, (t, t), 0)
    qcol = lax.broadcasted_iota(I32, (t, t), 1)
    lane_lo = qcol < IDX_DIM

    for h in range(IDX_HEADS):
        pair = iq_ref[0, :, (h // 2) * t:(h // 2 + 1) * t]
        keep = lane_lo if h % 2 == 0 else jnp.logical_not(lane_lo)
        qm_ref[h * t:(h + 1) * t, :] = jnp.where(keep, pair, jnp.zeros_like(pair))
    for h in range(A_HEADS):
        qs_ref[h * t:(h + 1) * t, :] = qa_ref[0, :, h * t:(h + 1) * t]
    wt_ref[...] = iw_ref[0].T

    def index_logits(g):
        return lax.dot_general(ik_ref[0, g * gt:(g + 1) * gt, :], qm_ref[...], _NT,
                               preferred_element_type=F32)

    def score_group(g, lg, ng):
        for u in range(group):
            j = g * group + u
            acc = jnp.zeros((t, t), F32)
            for h in range(IDX_HEADS):
                acc = acc + jnp.maximum(lg[u * t:(u + 1) * t, h * t:(h + 1) * t], 0.0) * wt_ref[h:h + 1, :]
            score = acc * idx_scale
            if j >= (ng - 1) * group:
                score = jnp.where(krow + j * t <= qcol + i * t, score, NEG_BIG)
            bits = pltpu.bitcast(score, I32)
            sign = bits >> 31
            key = (bits ^ (sign & np.int32(0x7FFFFFFF))) - sign
            keys_ref[j] = key
            rows = [key[r * SUBLANES:(r + 1) * SUBLANES, :] for r in range(SORT_ROWS)]
            for a, b in _SORT16:
                rows[a], rows[b] = jnp.maximum(rows[a], rows[b]), jnp.minimum(rows[a], rows[b])
            for r in range(SORT_ROWS):
                sorted_ref[j, r] = rows[r]

    def score_all(ng):
        lg = index_logits(0)
        for g in range(ng):
            nxt = index_logits(g + 1) if g + 1 < ng else None
            score_group(g, lg, ng)
            lg = nxt

    for v in range(1, max_groups + 1):
        pl.when(n_groups == v)(functools.partial(score_all, v))

    thr_ref[...] = jnp.full(thr_ref.shape, INT_MIN, I32)
    ties_ref[...] = jnp.zeros(ties_ref.shape, F32)

    def count(n, pred):
        acc = [jnp.zeros((SUBLANES, t), F32) for _ in range(5)]
        for j in range(n):
            v = [sorted_ref[j, r] for r in range(SORT_ROWS)]
            m1 = pred(v[7])
            m2 = pred(jnp.where(m1, v[11], v[3]))
            m3 = pred(jnp.where(m1, jnp.where(m2, v[13], v[9]), jnp.where(m2, v[5], v[1])))
            m4 = pred(jnp.where(m1, jnp.where(m2, jnp.where(m3, v[14], v[12]), jnp.where(m3, v[10], v[8])),
                                jnp.where(m2, jnp.where(m3, v[6], v[4]), jnp.where(m3, v[2], v[0]))))
            m5 = pred(v[15])
            acc = [a + jnp.where(m, 1.0, 0.0) for a, m in zip(acc, (m1, m2, m3, m4, m5))]
        cnt = 8.0 * acc[0] + 4.0 * acc[1] + 2.0 * acc[2] + acc[3] + acc[4]
        return jnp.sum(cnt, axis=0, keepdims=True)

    def search(n):
        def search_pass(b, thr):
            cand = thr + lax.shift_left(np.int32(1), 31 - b)
            return jnp.where(count(n, lambda key: key >= cand) >= topk, cand, thr)

        thr = lax.fori_loop(0, 32, search_pass, jnp.full((1, t), INT_MIN, I32))
        thr_ref[...] = jnp.broadcast_to(thr, thr_ref.shape)
        ties_ref[...] = jnp.broadcast_to(topk - count(n, lambda key: key > thr), ties_ref.shape)

    for c in range(n_blocks):
        if (c + 1) * t > topk:
            pl.when(i == c)(functools.partial(search, c + 1))

    thr = thr_ref[0:1, :]
    n_ties = ties_ref[0:1, :]

    r2 = lax.broadcasted_iota(I32, (2 * t, t), 0)
    c2 = lax.broadcasted_iota(I32, (2 * t, t), 1)
    tie_lhs = jnp.where(jnp.logical_or(r2 >= t, c2 < r2), 1.0, 0.0).astype(BF16)

    def sweep_a_mxu(g):
        keys = [keys_ref[g * group + u] for u in range(group)]
        eqs = [key == thr for key in keys]
        eq_all = jnp.concatenate([jnp.where(eq, 1.0, 0.0).astype(BF16) for eq in eqs], axis=1)
        pref = jnp.dot(tie_lhs, eq_all, preferred_element_type=F32)
        s_grp = lax.dot_general(ka_ref[0, g * gt:(g + 1) * gt, :], qs_ref[...], _NT,
                                preferred_element_type=F32)
        return keys, eqs, pref, s_grp

    def sweep_a_vpu(g, operands, seen, mx, ng):
        keys, eqs, pref, s_grp = operands
        for u in range(group):
            j = g * group + u
            us = slice(u * t, (u + 1) * t)
            rank = jnp.where(keys[u] > thr, -1.0, jnp.where(eqs[u], seen + pref[:t, us], RANK_NEVER))
            if j >= (ng - 1) * group:
                rank = jnp.where(krow + j * t <= qcol + i * t, rank, RANK_NEVER)
            valid = rank < n_ties
            which = jnp.where(j == i, 1, jnp.where(j == i - 1, 0, 2))
            s_all = s_grp[us, :] * (scale * math.log2(math.e))
            if j > (ng - 1) * group - 2:
                s_all = s_all + bias_ref[which]
            for h in range(A_HEADS):
                s_h = jnp.where(valid, s_all[:, h * t:(h + 1) * t], NEG_BIG)
                s_ref[j, :, h * t:(h + 1) * t] = s_h
                mx[h] = jnp.maximum(mx[h], _fold8(s_h, jnp.max))
            seen = seen + pref[t:t + 1, us]
        return seen, mx

    def sweep_b(g, m_all, l8):
        ps = []
        for u in range(group):
            p = jnp.exp2(s_ref[g * group + u] - m_all)
            l8 = l8 + _fold8(p, jnp.sum)
            ps.append(p.astype(BF16))
        vt_grp = jnp.concatenate([vt_ref[0, g * group + u] for u in range(group)], axis=1)
        return l8, jnp.dot(vt_grp, jnp.concatenate(ps, axis=0), preferred_element_type=F32)

    def attend(ng):
        seen = jnp.zeros((1, t), F32)
        mx = [jnp.full((SUBLANES, t), NEG_BIG, F32) for _ in range(A_HEADS)]
        operands = sweep_a_mxu(0)
        for g in range(ng):
            nxt = sweep_a_mxu(g + 1) if g + 1 < ng else None
            seen, mx = sweep_a_vpu(g, operands, seen, mx, ng)
            operands = nxt
        m_all = jnp.concatenate([jnp.max(m, axis=0, keepdims=True) for m in mx], axis=1)
        l8 = jnp.zeros((SUBLANES, A_HEADS * t), F32)
        acc = None
        for g in range(ng):
            l8, pv = sweep_b(g, m_all, l8)
            acc = pv if acc is None else acc + pv
        out = acc * (1.0 / jnp.sum(l8, axis=0, keepdims=True))
        for h in range(A_HEADS):
            o_ref[0, :, h * t:(h + 1) * t] = out[:, h * t:(h + 1) * t].T.astype(o_ref.dtype)

    for v in range(1, max_groups + 1):
        pl.when(n_groups == v)(functools.partial(attend, v))


def _dsa(qa, iq, iw, ka, vt, ik2, bias_tab):
    b, s, _ = qa.shape
    t = LANES
    nb = s // t
    group = math.gcd(nb, 2)
    topk = min(TOPK_MAX, s // 4)
    kern = functools.partial(_dsa_kernel, topk=float(topk), n_blocks=nb, group=group,
                             scale=A_HEAD_DIM ** -0.5,
                             idx_scale=(IDX_DIM ** -0.5) * (IDX_HEADS ** -0.5))
    qspec = lambda w: pl.BlockSpec((1, t, w), lambda bi, i: (bi, i, 0))
    kspec = pl.BlockSpec((1, s, t), lambda bi, i: (bi, 0, 0))
    return pl.pallas_call(
        kern,
        grid=(b, nb),
        in_specs=[qspec(A_HEADS * t), qspec(IDX_HEADS * IDX_DIM), qspec(t), kspec,
                  pl.BlockSpec((1, nb, t, t), lambda bi, i: (bi, 0, 0, 0)), kspec,
                  _const_spec(bias_tab.shape)],
        out_specs=qspec(A_HEADS * t),
        out_shape=jax.ShapeDtypeStruct((b, s, A_HEADS * t), BF16),
        scratch_shapes=[pltpu.VMEM((nb, t, t), I32),
                        pltpu.VMEM((nb, SORT_ROWS, SUBLANES, t), I32),
                        pltpu.VMEM((nb, t, A_HEADS * t), F32),
                        pltpu.VMEM((IDX_HEADS * t, t), BF16),
                        pltpu.VMEM((A_HEADS * t, t), BF16),
                        pltpu.VMEM((t, t), F32),
                        pltpu.VMEM((SUBLANES, t), I32),
                        pltpu.VMEM((SUBLANES, t), F32)],
        compiler_params=_params("parallel", "parallel"),
        name="dsa_attention",
    )(qa, iq, iw, ka, vt, ik2, bias_tab)


def _hgrn_kernel(q_ref, f_ref, i_ref, g_ref, lb_ref, gain_ref, o_ref, state_ref, *, n_chunks, n_seq):
    c = B_CHUNK
    rc = n_seq * c
    kd, vd = B_KEY_DIM, B_VAL_DIM
    w = B_HEADS * kd

    @pl.when(pl.program_id(1) == 0)
    def _():
        state_ref[...] = jnp.zeros_like(state_ref)

    row = lax.broadcasted_iota(I32, (rc, rc), 0)
    col = lax.broadcasted_iota(I32, (rc, rc), 1)
    seq_of = lambda r: lax.shift_right_logical(r, CHUNK_SHIFT)
    causal = (row >= col) if n_seq == 1 else jnp.logical_and(row >= col, seq_of(row) == seq_of(col))
    tril = jnp.where(causal, 1.0, 0.0).astype(BF16)
    seqs = [slice(b * c, (b + 1) * c) for b in range(n_seq)]
    seq_of_row = seq_of(lax.broadcasted_iota(I32, (rc, w), 0))

    def per_seq_row(x, r):
        return jnp.concatenate([jnp.broadcast_to(x[b * c + r:b * c + r + 1, :], (c, x.shape[1]))
                                for b in range(n_seq)], axis=0)

    def stacked(ref, rows):
        return jnp.concatenate([ref[b, rows, :] for b in range(n_seq)], axis=0)
    q_scale = kd ** -0.5
    lb = lb_ref[...]
    lb_floor = jnp.maximum(lb, LB_FLOOR)
    one_m_lb = 1.0 - lb
    heads = [slice(h * kd, (h + 1) * kd) for h in range(B_HEADS)]

    def stage_decay(ci):
        rows = slice(ci * c, (ci + 1) * c)
        fr = stacked(f_ref, rows)
        z = jnp.exp(-jnp.abs(fr))
        r = 1.0 / (1.0 + z)
        sig_pos = jnp.where(fr >= 0, r, z * r)
        sig_neg = jnp.where(fr >= 0, z * r, r)
        log_f = jnp.log(lb_floor + one_m_lb * sig_pos)
        k_in = one_m_lb * sig_neg
        hi = log_f.astype(BF16)
        rest = log_f - hi.astype(F32)
        mid = rest.astype(BF16)
        lo = (rest - mid.astype(F32)).astype(BF16)
        cs = jnp.dot(tril, jnp.concatenate([hi, mid, lo], axis=1), preferred_element_type=F32)
        return k_in, cs

    def apply_update(pending):
        if pending is not None:
            d_last, st, upd = pending
            for b in range(n_seq):
                for h, hs in enumerate(heads):
                    state_ref[b, h] = d_last[b * c:b * c + 1, hs] * st[b][h] + upd[b][h]

    def stage_scores(ci, decay, pending):
        rows = slice(ci * c, (ci + 1) * c)
        k_in, cs = decay
        bsum = cs[:, :w] + cs[:, w:2 * w] + cs[:, 2 * w:]
        b_mid = per_seq_row(bsum, c // 2 - 1)
        b_last = per_seq_row(bsum, c - 1)
        qs = stacked(q_ref, rows) * q_scale
        v = stacked(i_ref, rows)
        v_b = v.astype(BF16)
        v_t = v.T.astype(BF16)
        q_mid = (qs * jnp.exp(bsum - b_mid)).astype(BF16)
        k_mid = (k_in * jnp.exp(b_mid - bsum)).astype(BF16)
        q_dec = (qs * jnp.exp(bsum)).astype(BF16)
        k_end = (k_in * jnp.exp(b_last - bsum)).astype(BF16)
        d_last = jnp.exp(b_last)
        apply_update(pending)
        st = [[state_ref[b, h] for h in range(B_HEADS)] for b in range(n_seq)]
        attn = [lax.dot_general(q_mid[:, hs], k_mid[:, hs], _NT, preferred_element_type=F32)
                for hs in heads]
        inter = [jnp.concatenate(
            [lax.dot_general(q_dec[sq, hs], st[b][h].astype(BF16), _NT, preferred_element_type=F32)
             for b, sq in enumerate(seqs)], axis=0) for h, hs in enumerate(heads)]
        k_seq = [k_end if n_seq == 1 else jnp.where(seq_of_row == b, k_end, jnp.zeros_like(k_end))
                 for b in range(n_seq)]
        upd = [[jnp.dot(v_t[hs, :], k_seq[b][:, hs], preferred_element_type=F32) for hs in heads]
               for b in range(n_seq)]
        return (attn, inter, v_b), (d_last, st, upd)

    def stage_output(ci, scores):
        rows = slice(ci * c, (ci + 1) * c)
        attn, inter, v_b = scores
        attn = [jnp.where(causal, a, 0.0).astype(BF16) for a in attn]
        outs = [jnp.dot(attn[h], v_b[:, hs], preferred_element_type=F32) + inter[h]
                for h, hs in enumerate(heads)]
        o = jnp.concatenate([_rms(o_h, gain_ref[...]) for o_h in outs], axis=1)
        g = stacked(g_ref, rows)
        o = (o * (g * (1.0 / (1.0 + jnp.exp(-g))))).astype(o_ref.dtype)
        for b, sq in enumerate(seqs):
            o_ref[b, rows, :] = o[sq, :]

    decay, scores, pending = {}, {}, None
    for step in range(n_chunks + 2):
        if step < n_chunks:
            decay[step] = stage_decay(step)
        if 0 <= step - 1 < n_chunks:
            scores[step - 1], pending = stage_scores(step - 1, decay.pop(step - 1), pending)
        if 0 <= step - 2 < n_chunks:
            stage_output(step - 2, scores.pop(step - 2))
    apply_update(pending)


def _hgrn(bq, bf, bi, bg, lb, gain, sb):
    b, s, w = bq.shape
    assert s % sb == 0 and sb % B_CHUNK == 0
    n_seq = 2 if b % 2 == 0 else 1
    kern = functools.partial(_hgrn_kernel, n_chunks=sb // B_CHUNK, n_seq=n_seq)
    spec = pl.BlockSpec((n_seq, sb, w), lambda bi_, si: (bi_, si, 0))
    return pl.pallas_call(
        kern,
        grid=(b // n_seq, s // sb),
        in_specs=[spec, spec, spec, spec, _const_spec((1, w)), _const_spec((1, B_VAL_DIM))],
        out_specs=spec,
        out_shape=jax.ShapeDtypeStruct((b, s, w), BF16),
        scratch_shapes=[pltpu.VMEM((n_seq, B_HEADS, B_VAL_DIM, B_KEY_DIM), F32)],
        compiler_params=_params("parallel", "arbitrary"),
        name="hgrn2",
    )(bq, bf, bi, bg, lb.reshape(1, w), gain.reshape(1, B_VAL_DIM))


def _mla_kernel(qn_ref, qpe_ref, kn_ref, kpe_ref, vt_ref, o_ref, q_ref, acc_ref, *, t, n_tiles, scale):
    i = pl.program_id(1)
    lane_lo = lax.broadcasted_iota(I32, (t, LANES), 1) < C_ROPE
    for h in range(C_HEADS):
        pair = qpe_ref[0, :, (h // 2) * LANES:(h // 2 + 1) * LANES]
        keep = lane_lo if h % 2 == 0 else jnp.logical_not(lane_lo)
        q_ref[h, :, :C_NOPE] = qn_ref[0, :, h * C_NOPE:(h + 1) * C_NOPE]
        q_ref[h, :, C_NOPE:] = jnp.where(keep, pair, jnp.zeros_like(pair))
    acc_ref[...] = jnp.zeros_like(acc_ref)

    def logits_of(step):
        first, count = step
        rows = slice(first * t, (first + count) * t)
        kpe_t = kpe_ref[0, rows, :]
        return [lax.dot_general(
            jnp.concatenate([kn_ref[0, rows, h * C_NOPE:(h + 1) * C_NOPE], kpe_t], axis=1),
            q_ref[h], _NT, preferred_element_type=F32) for h in range(C_HEADS)]

    def softmax_pv(step, block, logits, ms, ls, masked):
        first, count = step
        new_m, new_l = [], []
        for h in range(C_HEADS):
            s = logits[h] * (scale * math.log2(math.e))
            if masked:
                key_pos = lax.broadcasted_iota(I32, (count * t, t), 0) + first * t
                query_pos = lax.broadcasted_iota(I32, (count * t, t), 1) + block * t
                s = jnp.where(key_pos <= query_pos, s, NEG_BIG)
            m_new = jnp.maximum(ms[h], jnp.max(s, axis=0, keepdims=True))
            alpha = jnp.exp2(ms[h] - m_new)
            p = jnp.exp2(s - m_new)
            new_l.append(alpha * ls[h] + jnp.sum(p, axis=0, keepdims=True))
            new_m.append(m_new)
            hs = slice(h * C_V, (h + 1) * C_V)
            tiles = [vt_ref[0, first + u, hs, :] for u in range(count)]
            v_t = tiles[0] if count == 1 else jnp.concatenate(tiles, axis=1)
            acc_ref[h] = alpha * acc_ref[h] + jnp.dot(v_t, p.astype(BF16), preferred_element_type=F32)
        return new_m, new_l

    def run(block):
        steps = [(2 * g, 2) for g in range(block // 2)] + [(block - block % 2, 1 + block % 2)]
        ms = [jnp.full((1, t), NEG_BIG, F32) for _ in range(C_HEADS)]
        ls = [jnp.zeros((1, t), F32) for _ in range(C_HEADS)]
        logits = logits_of(steps[0])
        for k, step in enumerate(steps):
            nxt = logits_of(steps[k + 1]) if k + 1 < len(steps) else None
            ms, ls = softmax_pv(step, block, logits, ms, ls, masked=(k + 1 == len(steps)))
            logits = nxt
        for h in range(C_HEADS):
            o_ref[0, :, h * C_V:(h + 1) * C_V] = (acc_ref[h] * (1.0 / ls[h])).T.astype(o_ref.dtype)

    for v in range(n_tiles):
        pl.when(i == v)(functools.partial(run, v))


def _mla(qn, qpe, kn, kpe, vt, t):
    b, s, _ = qn.shape
    nt = s // t
    kern = functools.partial(_mla_kernel, t=t, n_tiles=nt, scale=(C_NOPE + C_ROPE) ** -0.5)
    qspec = lambda w: pl.BlockSpec((1, t, w), lambda bi, i: (bi, i, 0))
    kspec = lambda w: pl.BlockSpec((1, s, w), lambda bi, i: (bi, 0, 0))
    return pl.pallas_call(
        kern,
        grid=(b, nt),
        in_specs=[qspec(qn.shape[2]), qspec(qpe.shape[2]), kspec(kn.shape[2]), kspec(kpe.shape[2]),
                  pl.BlockSpec((1, nt, C_HEADS * C_V, t), lambda bi, i: (bi, 0, 0, 0))],
        out_specs=qspec(C_HEADS * C_V),
        out_shape=jax.ShapeDtypeStruct((b, s, C_HEADS * C_V), BF16),
        scratch_shapes=[pltpu.VMEM((C_HEADS, t, C_NOPE + LANES), BF16),
                        pltpu.VMEM((C_HEADS, C_V, t), F32)],
        compiler_params=_params("parallel", "parallel"),
        name="mla_attention",
    )(qn, qpe, kn, kpe, vt)


def _merge_kernel(x_ref, g_ref, wg_ref, oa_ref, ob_ref, oc_ref, wa_ref, wb_ref, wc_ref, wo_ref,
                  out_ref, mixed_ref, *, col_chunk):
    x = x_ref[...]
    d = x.shape[1]
    h = _rms(x, g_ref[...]).astype(BF16)
    branches = ((oa_ref, wa_ref), (ob_ref, wb_ref), (oc_ref, wc_ref))
    for s in range(0, d, col_chunk):
        cs = slice(s, s + col_chunk)
        mixed = None
        for bidx, (o_ref, w_ref) in enumerate(branches):
            logits = jnp.dot(h, wg_ref[:, bidx * d + s:bidx * d + s + col_chunk],
                             preferred_element_type=F32)
            gate = 1.0 / (1.0 + jnp.exp(-logits))
            term = gate * jnp.dot(o_ref[...], w_ref[:, cs], preferred_element_type=F32)
            mixed = term if mixed is None else mixed + term
        mixed_ref[:, cs] = mixed.astype(BF16)
    out_ref[...] = x + jnp.dot(mixed_ref[...], wo_ref[...], preferred_element_type=F32)


def _merge(x, g, w_gate, oa, ob, oc, wa, wb, wc, wo, layer, tm):
    n, d = x.shape
    row = lambda w: pl.BlockSpec((tm, w), lambda i: (i, 0))
    kern = functools.partial(_merge_kernel, col_chunk=256)
    return pl.pallas_call(
        kern,
        grid=(n // tm,),
        in_specs=[row(d), _const_spec((1, d)), _layer_spec(w_gate, layer),
                  row(oa.shape[1]), row(ob.shape[1]), row(oc.shape[1]),
                  _layer_spec(wa, layer), _layer_spec(wb, layer), _layer_spec(wc, layer),
                  _layer_spec(wo, layer)],
        out_specs=row(d),
        out_shape=jax.ShapeDtypeStruct((n, d), F32),
        scratch_shapes=[pltpu.VMEM((tm, d), BF16)],
        compiler_params=_params("parallel"),
        name="merge_out_proj",
    )(x, g.reshape(1, d), w_gate, oa, ob, oc, wa, wb, wc, wo)


def _ffn_kernel(x_ref, g_ref, wg_ref, wu_ref, wd_ref, gf_ref, out_ref, act_ref, *, col_chunk,
                final_norm):
    x = x_ref[...]
    h = _rms(x, g_ref[...]).astype(BF16)
    dff = wg_ref.shape[1]
    for s in range(0, dff, col_chunk):
        cs = slice(s, s + col_chunk)
        gate = jnp.dot(h, wg_ref[:, cs], preferred_element_type=F32)
        up = jnp.dot(h, wu_ref[:, cs], preferred_element_type=F32)
        act_ref[:, cs] = (gate * (1.0 / (1.0 + jnp.exp(-gate))) * up).astype(BF16)
    y = x + jnp.dot(act_ref[...], wd_ref[...], preferred_element_type=F32)
    if final_norm:
        y = _rms(y, gf_ref[...])
    out_ref[...] = y


def _ffn(x, g, wg, wu, wd, gf, layer, tm, final_norm):
    n, d = x.shape
    dff = wg.shape[2]
    row = pl.BlockSpec((tm, d), lambda i: (i, 0))
    kern = functools.partial(_ffn_kernel, col_chunk=256, final_norm=final_norm)
    return pl.pallas_call(
        kern,
        grid=(n // tm,),
        in_specs=[row, _const_spec((1, d)), _layer_spec(wg, layer), _layer_spec(wu, layer),
                  _layer_spec(wd, layer), _const_spec((1, d))],
        out_specs=row,
        out_shape=jax.ShapeDtypeStruct((n, d), F32),
        scratch_shapes=[pltpu.VMEM((tm, dff), BF16)],
        compiler_params=_params("parallel"),
        name="swiglu_ffn",
    )(x, g.reshape(1, d), wg, wu, wd, gf.reshape(1, d))


def _t5_bucket(dist):
    max_exact = REL_BUCKETS // 2
    d = jnp.maximum(dist, 0)
    dl = jnp.maximum(d, max_exact).astype(F32)
    large = max_exact + (jnp.log(dl / max_exact) / math.log(REL_MAX_DIST / max_exact)
                         * (REL_BUCKETS - max_exact)).astype(I32)
    large = jnp.minimum(large, REL_BUCKETS - 1)
    return jnp.where(d < max_exact, d, large)


def _swap_halves(w):
    half = w.shape[-1] // 2
    return jnp.concatenate([w[..., half:], w[..., :half]], axis=-1)


def _in_proj_weights(w_in, d_model):
    splits = (A_HEADS * A_HEAD_DIM, A_HEAD_DIM, A_HEAD_DIM, IDX_HEADS * IDX_DIM, IDX_DIM, IDX_HEADS,
              B_HEADS * B_KEY_DIM, B_HEADS * B_KEY_DIM, B_HEADS * B_VAL_DIM, B_HEADS * B_VAL_DIM,
              C_Q_RANK, C_KV_RANK, C_ROPE, N_BRANCH * d_model)
    st = [int(v) for v in np.concatenate([[0], np.cumsum(splits)])]
    depth, k, n_cols = w_in.shape
    w_pad = jnp.pad(w_in.astype(BF16), ((0, 0), (0, 0), (0, -n_cols % LANES)))
    n_pad = w_pad.shape[2]
    reps = LANES // C_ROPE
    rows = 128

    def regroup_kernel(w_ref, a_ref, b_ref, c_ref, d_ref, g_ref):
        col = lambda a, b: w_ref[:, st[a]:st[b]]
        ik, pe = col(4, 5), col(12, 13)
        iw_pad = jnp.zeros((rows, LANES - IDX_HEADS), w_ref.dtype)
        a_ref[...] = col(0, 4).astype(BF16)
        b_ref[...] = jnp.concatenate([ik, ik, col(5, 6), iw_pad], axis=1).astype(BF16)
        c_ref[...] = col(6, 12).astype(BF16)
        d_ref[...] = jnp.concatenate([pe] * reps + [_swap_halves(pe)] * reps, axis=1).astype(BF16)
        g_ref[...] = col(13, 14).astype(BF16)

    out_w = (st[4], 2 * LANES, st[12] - st[6], 2 * LANES, st[14] - st[13])
    outs = pl.pallas_call(
        regroup_kernel,
        grid=(depth, k // rows),
        in_specs=[pl.BlockSpec((None, rows, n_pad), lambda l, i: (l, i, 0))],
        out_specs=[pl.BlockSpec((None, rows, wd), lambda l, i: (l, i, 0)) for wd in out_w],
        out_shape=[jax.ShapeDtypeStruct((depth, k, wd), BF16) for wd in out_w],
        compiler_params=_params("parallel", "parallel"),
        name="regroup_w_in",
    )(w_pad)
    ws = list(outs)
    widths = [splits[:4], (LANES, LANES), splits[6:12], (2 * LANES,)]
    dtypes = [BF16] * 5 + [F32] * 8
    return ws[:4], widths, dtypes, ws[4]


def kernel(x, positions, w_in, w_up_a, w_up_b, w_up_c, w_out, mla_q_norm, mla_w_qb, mla_kv_norm,
           mla_w_kvb, hgrn_lb_logits, hgrn_out_norm, rel_bias, attn_norm, ffn_norm, w_ffn_gate,
           w_ffn_up, w_ffn_down, final_norm):
    bsz, s_len, d_model = x.shape
    depth = w_in.shape[0]
    n = bsz * s_len
    t = LANES

    w_proj, widths, dtypes, w_gate = _in_proj_weights(w_in, d_model)
    wq = mla_w_qb.reshape(depth, C_Q_RANK, C_HEADS, C_NOPE + C_ROPE)
    q_pe = wq[..., C_NOPE:]
    w_qb = jnp.concatenate([wq[..., :C_NOPE].reshape(depth, C_Q_RANK, -1),
                            q_pe.reshape(depth, C_Q_RANK, -1),
                            _swap_halves(q_pe).reshape(depth, C_Q_RANK, -1)], axis=2).astype(BF16)
    wkv = mla_w_kvb.reshape(depth, C_KV_RANK, C_HEADS, C_NOPE + C_V)
    w_kvb = jnp.concatenate([wkv[..., :C_NOPE].reshape(depth, C_KV_RANK, -1),
                             wkv[..., C_NOPE:].reshape(depth, C_KV_RANK, -1)], axis=2).astype(BF16)
    bf = lambda w: w.astype(BF16)
    w_up_a, w_up_b, w_up_c, w_out = bf(w_up_a), bf(w_up_b), bf(w_up_c), bf(w_out)
    w_ffn_gate, w_ffn_up, w_ffn_down = bf(w_ffn_gate), bf(w_ffn_up), bf(w_ffn_down)

    p_lb = jax.nn.softmax(hgrn_lb_logits.astype(F32), axis=0)
    lower_bounds = jnp.cumsum(p_lb, axis=0) - p_lb[0:1]
    inv_freq = ROPE_THETA ** (-jnp.arange(0, C_ROPE, 2, dtype=F32) / C_ROPE)
    ang = positions.astype(F32)[..., None] * inv_freq
    cos, sin = jnp.cos(ang), jnp.sin(ang)
    reps = LANES // C_ROPE
    cos_t = jnp.tile(jnp.concatenate([cos, cos], axis=-1), (1, 1, reps)).reshape(n, LANES)
    sin_t = jnp.tile(jnp.concatenate([-sin, sin], axis=-1), (1, 1, reps)).reshape(n, LANES)
    q_idx = jnp.arange(t, dtype=I32)[None, :]
    k_idx = jnp.arange(t, dtype=I32)[:, None]
    def tab(dist):
        onehot = jax.nn.one_hot(_t5_bucket(dist), REL_BUCKETS, dtype=F32)
        return jnp.einsum("kqb,bh->khq", onehot, rel_bias.astype(F32),
                          precision=lax.Precision.HIGHEST).reshape(t, A_HEADS * t)
    far = jnp.broadcast_to(rel_bias[REL_BUCKETS - 1].astype(F32)[None, :, None],
                           (t, A_HEADS, t)).reshape(t, A_HEADS * t)
    bias_tab = (jnp.stack([tab(q_idx + t - k_idx), tab(q_idx - k_idx), far]) - far) * math.log2(math.e)

    tm = min(512, n)
    t_mla = min(512, s_len)
    x2 = x.reshape(n, d_model)
    r3 = lambda a: a.reshape(bsz, s_len, a.shape[-1])
    per_seq = lambda a: a.reshape(bsz, a.shape[0] // bsz, a.shape[1], a.shape[2])
    key_tiles = [0, 0, t] + [0] * 10
    for l in range(depth):
        mla = (cos_t, sin_t, mla_q_norm[l], mla_kv_norm[l], w_qb, w_kvb, t_mla)
        (qa, ka, vta, iq, ik2, iw, bq, bfr, bi, bg, qn, qpe, kn, vtc, kpe) = _norm_matmul(
            x2, attn_norm[l], w_proj, l, widths, dtypes, key_tiles, mla, tm)
        o_a = _dsa(r3(qa), r3(iq), r3(iw), r3(ka), per_seq(vta), r3(ik2), bias_tab)
        o_b = _hgrn(r3(bq), r3(bfr), r3(bi), r3(bg), lower_bounds[l], hgrn_out_norm[l],
                    sb=min(512, s_len))
        o_c = _mla(r3(qn), r3(qpe), r3(kn), r3(kpe), per_seq(vtc), t_mla)
        x2 = _merge(x2, attn_norm[l], w_gate, o_a.reshape(n, -1), o_b.reshape(n, -1),
                    o_c.reshape(n, -1), w_up_a, w_up_b, w_up_c, w_out, l, tm)
        x2 = _ffn(x2, ffn_norm[l], w_ffn_gate, w_ffn_up, w_ffn_down, final_norm,
                  l, tm, final_norm=(l == depth - 1))
    return x2.reshape(bsz, s_len, d_model)
```

```python
import functools
import math

import jax
import jax.numpy as jnp
import numpy as np
from jax import lax
from jax.experimental import pallas as pl
from jax.experimental.pallas import tpu as pltpu

F32 = jnp.float32
BF16 = jnp.bfloat16
I32 = jnp.int32

A_HEADS = 4
A_HEAD_DIM = 128
IDX_HEADS = 8
IDX_DIM = 64
TOPK_MAX = 256
B_HEADS = 4
B_KEY_DIM = 128
B_VAL_DIM = 128
B_CHUNK = 64
C_HEADS = 4
C_Q_RANK = 384
C_KV_RANK = 256
C_NOPE = 128
C_ROPE = 64
C_V = 128
ROPE_THETA = 10000.0
REL_BUCKETS = 32
REL_MAX_DIST = 128
N_BRANCH = 3
EPS = 1e-6
NEG_BIG = -1e30
LB_FLOOR = 1e-30

LANES = 128
SUBLANES = 8
VMEM_LIMIT = 56 * 1024 * 1024
INT_MIN = np.int32(-2 ** 31)
SORT_ROWS = LANES // SUBLANES
RANK_NEVER = 3e38
CHUNK_SHIFT = B_CHUNK.bit_length() - 1
assert 1 << CHUNK_SHIFT == B_CHUNK


def _oddeven_merge_sort(n):
    pairs = []
    p = 1
    while p < n:
        k = p
        while k >= 1:
            for j in range(k % p, n - k, 2 * k):
                for i in range(min(k, n - j - k)):
                    if (i + j) // (2 * p) == (i + j + k) // (2 * p):
                        pairs.append((i + j, i + j + k))
            k //= 2
        p *= 2
    return tuple(pairs)


_SORT16 = _oddeven_merge_sort(SORT_ROWS)

_NT = (((1,), (1,)), ((), ()))


def _params(*sem):
    return pltpu.CompilerParams(dimension_semantics=sem, vmem_limit_bytes=VMEM_LIMIT)


def _rms(x, g):
    return x * lax.rsqrt(jnp.mean(x * x, axis=-1, keepdims=True) + EPS) * g


def _const_spec(shape):
    nd = len(shape)
    return pl.BlockSpec(shape, lambda *_: (0,) * nd)


def _layer_spec(w, layer):
    return pl.BlockSpec((None,) + w.shape[1:], lambda *_: (layer, 0, 0))


def _fold8(x, op):
    r, c = x.shape
    return op(x.reshape(r // SUBLANES, SUBLANES, c), axis=0)


def _store_key_major(o_ref, rows, tile):
    for u in range(rows.shape[0] // tile):
        o_ref[u] = rows[u * tile:(u + 1) * tile, :].T.astype(o_ref.dtype)


def _mla_up_project(cq, ckv, kpe_raw, cos, sin, gq, gkv, wq_ref, wkv_ref,
                    qn_ref, qpe_ref, kn_ref, vt_ref, kpe_ref, key_tile):
    nw = C_HEADS * C_NOPE
    pw = C_HEADS * C_ROPE
    cos2 = jnp.concatenate([cos] * (pw // LANES), axis=1)
    sin2 = jnp.concatenate([sin] * (pw // LANES), axis=1)
    q = jnp.dot(_rms(cq, gq).astype(BF16), wq_ref[...], preferred_element_type=F32)
    qn_ref[...] = q[:, :nw].astype(qn_ref.dtype)
    qpe_ref[...] = (q[:, nw:nw + pw] * cos2 + q[:, nw + pw:] * sin2).astype(qpe_ref.dtype)
    kv = jnp.dot(_rms(ckv, gkv).astype(BF16), wkv_ref[...], preferred_element_type=F32)
    kn_ref[...] = kv[:, :nw].astype(kn_ref.dtype)
    _store_key_major(vt_ref, kv[:, nw:], key_tile)
    kpe_ref[...] = (kpe_raw[:, :LANES] * cos + kpe_raw[:, LANES:] * sin).astype(kpe_ref.dtype)


N_LATENT = 3
N_MLA_IN = 6
N_MLA_OUT = 5


def _norm_matmul_kernel(x_ref, g_ref, *refs, widths, key_tiles, mla_key_tile, col_chunk):
    n_w = len(widths)
    w_refs, mla_in = refs[:n_w], refs[n_w:n_w + N_MLA_IN]
    out_refs, mla_out = refs[n_w + N_MLA_IN:-N_MLA_OUT], refs[-N_MLA_OUT:]
    h = _rms(x_ref[...], g_ref[...]).astype(BF16)
    k = 0
    latents = []
    for w_ref, w_widths in zip(w_refs, widths):
        c0 = 0
        for w in w_widths:
            if k >= len(out_refs):
                latents.append(jnp.dot(h, w_ref[:, c0:c0 + w], preferred_element_type=F32))
            elif key_tiles[k]:
                _store_key_major(out_refs[k], jnp.dot(h, w_ref[:, c0:c0 + w], preferred_element_type=F32),
                                 key_tiles[k])
            else:
                for s in range(0, w, col_chunk):
                    e = min(s + col_chunk, w)
                    out_refs[k][:, s:e] = jnp.dot(h, w_ref[:, c0 + s:c0 + e],
                                                  preferred_element_type=F32).astype(out_refs[k].dtype)
            c0 += w
            k += 1
    cos_ref, sin_ref, gq_ref, gkv_ref, wq_ref, wkv_ref = mla_in
    _mla_up_project(*latents, cos_ref[...], sin_ref[...], gq_ref[...], gkv_ref[...], wq_ref, wkv_ref,
                    *mla_out, mla_key_tile)


def _norm_matmul(x, g, ws, layer, widths, dtypes, key_tiles, mla, tm):
    cos_t, sin_t, gq, gkv, wq, wkv, mla_tile = mla
    n, k = x.shape
    flat = [wd for w_widths in widths for wd in w_widths]
    assert n % tm == 0 and all(w.shape[1:] == (k, sum(ww)) for w, ww in zip(ws, widths))
    kern = functools.partial(_norm_matmul_kernel, widths=tuple(tuple(ww) for ww in widths),
                             key_tiles=tuple(key_tiles), mla_key_tile=mla_tile, col_chunk=512)
    row = lambda w: pl.BlockSpec((tm, w), lambda i: (i, 0))
    tiles = lambda w, kt: pl.BlockSpec((tm // kt, w, kt), lambda i: (i, 0, 0))
    specs, shapes = [], []
    for wd, dt, kt in list(zip(flat, dtypes, key_tiles))[:-N_LATENT]:
        specs.append(tiles(wd, kt) if kt else row(wd))
        shapes.append(jax.ShapeDtypeStruct((n // kt, wd, kt) if kt else (n, wd), dt))
    nw, vw = C_HEADS * C_NOPE, C_HEADS * C_V
    for wd in (nw, C_HEADS * C_ROPE, nw, None, LANES):
        specs.append(tiles(vw, mla_tile) if wd is None else row(wd))
        shapes.append(jax.ShapeDtypeStruct((n // mla_tile, vw, mla_tile) if wd is None else (n, wd), BF16))
    return pl.pallas_call(
        kern,
        grid=(n // tm,),
        in_specs=[row(k), _const_spec((1, k))] + [_layer_spec(w, layer) for w in ws]
                 + [row(LANES), row(LANES), _const_spec((1, C_Q_RANK)), _const_spec((1, C_KV_RANK)),
                    _layer_spec(wq, layer), _layer_spec(wkv, layer)],
        out_specs=specs,
        out_shape=shapes,
        compiler_params=_params("parallel"),
        name="norm_proj",
    )(x, g.reshape(1, k), *ws, cos_t, sin_t, gq.reshape(1, -1), gkv.reshape(1, -1), wq, wkv)


def _dsa_kernel(qa_ref, iq_ref, iw_ref, ka_ref, vt_ref, ik_ref, bias_ref, o_ref,
                keys_ref, sorted_ref, s_ref, qm_ref, qs_ref, wt_ref, thr_ref, ties_ref,
                *, topk, n_blocks, group, scale, idx_scale):
    t = LANES
    gt = group * t
    i = pl.program_id(1)
    n_groups = (i + group) // group
    max_groups = n_blocks // group
    krow = lax.broadcasted_iota(I32, (t, t), 0)
    qcol = lax.broadcasted_iota(I32, (t, t), 1)
    lane_lo = qcol < IDX_DIM

    for h in range(IDX_HEADS):
        pair = iq_ref[0, :, (h // 2) * t:(h // 2 + 1) * t]
        keep = lane_lo if h % 2 == 0 else jnp.logical_not(lane_lo)
        qm_ref[h * t:(h + 1) * t, :] = jnp.where(keep, pair, jnp.zeros_like(pair))
    for h in range(A_HEADS):
        qs_ref[h * t:(h + 1) * t, :] = qa_ref[0, :, h * t:(h + 1) * t]
    wt_ref[...] = iw_ref[0].T

    def index_logits(g):
        return lax.dot_general(ik_ref[0, g * gt:(g + 1) * gt, :], qm_ref[...], _NT,
                               preferred_element_type=F32)

    def score_group(g, lg, ng):
        for u in range(group):
            j = g * group + u
            acc = jnp.zeros((t, t), F32)
            for h in range(IDX_HEADS):
                acc = acc + jnp.maximum(lg[u * t:(u + 1) * t, h * t:(h + 1) * t], 0.0) * wt_ref[h:h + 1, :]
            score = acc * idx_scale
            if j >= (ng - 1) * group:
                score = jnp.where(krow + j * t <= qcol + i * t, score, NEG_BIG)
            score = jnp.where(score == 0.0, 0.0, score)
            bits = pltpu.bitcast(score, I32)
            key = bits ^ ((bits >> 31) & np.int32(0x7FFFFFFF))
            keys_ref[j] = key
            rows = [key[r * SUBLANES:(r + 1) * SUBLANES, :] for r in range(SORT_ROWS)]
            for a, b in _SORT16:
                rows[a], rows[b] = jnp.maximum(rows[a], rows[b]), jnp.minimum(rows[a], rows[b])
            for r in range(SORT_ROWS):
                sorted_ref[j, r] = rows[r]

    def score_all(ng):
        lg = index_logits(0)
        for g in range(ng):
            nxt = index_logits(g + 1) if g + 1 < ng else None
            score_group(g, lg, ng)
            lg = nxt

    for v in range(1, max_groups + 1):
        pl.when(n_groups == v)(functools.partial(score_all, v))

    thr_ref[...] = jnp.full(thr_ref.shape, INT_MIN, I32)
    ties_ref[...] = jnp.zeros(ties_ref.shape, F32)

    def count(n, pred):
        acc = [jnp.zeros((SUBLANES, t), F32) for _ in range(5)]
        for j in range(n):
            v = [sorted_ref[j, r] for r in range(SORT_ROWS)]
            m1 = pred(v[7])
            m2 = pred(jnp.where(m1, v[11], v[3]))
            m3 = pred(jnp.where(m1, jnp.where(m2, v[13], v[9]), jnp.where(m2, v[5], v[1])))
            m4 = pred(jnp.where(m1, jnp.where(m2, jnp.where(m3, v[14], v[12]), jnp.where(m3, v[10], v[8])),
                                jnp.where(m2, jnp.where(m3, v[6], v[4]), jnp.where(m3, v[2], v[0]))))
            m5 = pred(v[15])
            acc = [a + jnp.where(m, 1.0, 0.0) for a, m in zip(acc, (m1, m2, m3, m4, m5))]
        cnt = 8.0 * acc[0] + 4.0 * acc[1] + 2.0 * acc[2] + acc[3] + acc[4]
        return jnp.sum(cnt, axis=0, keepdims=True)

    def search(n):
        def search_pass(b, thr):
            cand = thr + lax.shift_left(np.int32(1), 31 - b)
            return jnp.where(count(n, lambda key: key >= cand) >= topk, cand, thr)

        thr = lax.fori_loop(0, 32, search_pass, jnp.full((1, t), INT_MIN, I32))
        thr_ref[...] = jnp.broadcast_to(thr, thr_ref.shape)
        ties_ref[...] = jnp.broadcast_to(topk - count(n, lambda key: key > thr), ties_ref.shape)

    for c in range(n_blocks):
        if (c + 1) * t > topk:
            pl.when(i == c)(functools.partial(search, c + 1))

    thr = thr_ref[0:1, :]
    n_ties = ties_ref[0:1, :]

    r2 = lax.broadcasted_iota(I32, (2 * t, t), 0)
    c2 = lax.broadcasted_iota(I32, (2 * t, t), 1)
    tie_lhs = jnp.where(jnp.logical_or(r2 >= t, c2 < r2), 1.0, 0.0).astype(BF16)

    def sweep_a_mxu(g):
        keys = [keys_ref[g * group + u] for u in range(group)]
        eqs = [key == thr for key in keys]
        eq_all = jnp.concatenate([jnp.where(eq, 1.0, 0.0).astype(BF16) for eq in eqs], axis=1)
        pref = jnp.dot(tie_lhs, eq_all, preferred_element_type=F32)
        s_grp = lax.dot_general(ka_ref[0, g * gt:(g + 1) * gt, :], qs_ref[...], _NT,
                                preferred_element_type=F32)
        return keys, eqs, pref, s_grp

    def sweep_a_vpu(g, operands, seen, mx, ng):
        keys, eqs, pref, s_grp = operands
        for u in range(group):
            j = g * group + u
            us = slice(u * t, (u + 1) * t)
            rank = jnp.where(keys[u] > thr, -1.0, jnp.where(eqs[u], seen + pref[:t, us], RANK_NEVER))
            if j >= (ng - 1) * group:
                rank = jnp.where(krow + j * t <= qcol + i * t, rank, RANK_NEVER)
            valid = rank < n_ties
            which = jnp.where(j == i, 1, jnp.where(j == i - 1, 0, 2))
            s_all = s_grp[us, :] * (scale * math.log2(math.e))
            if j > (ng - 1) * group - 2:
                s_all = s_all + bias_ref[which]
            for h in range(A_HEADS):
                s_h = jnp.where(valid, s_all[:, h * t:(h + 1) * t], NEG_BIG)
                s_ref[j, :, h * t:(h + 1) * t] = s_h
                mx[h] = jnp.maximum(mx[h], _fold8(s_h, jnp.max))
            seen = seen + pref[t:t + 1, us]
        return seen, mx

    def sweep_b(g, m_all, l8):
        ps = []
        for u in range(group):
            p = jnp.exp2(s_ref[g * group + u] - m_all)
            l8 = l8 + _fold8(p, jnp.sum)
            ps.append(p.astype(BF16))
        vt_grp = jnp.concatenate([vt_ref[0, g * group + u] for u in range(group)], axis=1)
        return l8, jnp.dot(vt_grp, jnp.concatenate(ps, axis=0), preferred_element_type=F32)

    def attend(ng):
        seen = jnp.zeros((1, t), F32)
        mx = [jnp.full((SUBLANES, t), NEG_BIG, F32) for _ in range(A_HEADS)]
        operands = sweep_a_mxu(0)
        for g in range(ng):
            nxt = sweep_a_mxu(g + 1) if g + 1 < ng else None
            seen, mx = sweep_a_vpu(g, operands, seen, mx, ng)
            operands = nxt
        m_all = jnp.concatenate([jnp.max(m, axis=0, keepdims=True) for m in mx], axis=1)
        l8 = jnp.zeros((SUBLANES, A_HEADS * t), F32)
        acc = None
        for g in range(ng):
            l8, pv = sweep_b(g, m_all, l8)
            acc = pv if acc is None else acc + pv
        out = acc * (1.0 / jnp.sum(l8, axis=0, keepdims=True))
        for h in range(A_HEADS):
            o_ref[0, :, h * t:(h + 1) * t] = out[:, h * t:(h + 1) * t].T.astype(o_ref.dtype)

    for v in range(1, max_groups + 1):
        pl.when(n_groups == v)(functools.partial(attend, v))


def _dsa(qa, iq, iw, ka, vt, ik2, bias_tab):
    b, s, _ = qa.shape
    t = LANES
    nb = s // t
    group = 1
    topk = min(TOPK_MAX, s // 4)
    kern = functools.partial(_dsa_kernel, topk=float(topk), n_blocks=nb, group=group,
                             scale=A_HEAD_DIM ** -0.5,
                             idx_scale=(IDX_DIM ** -0.5) * (IDX_HEADS ** -0.5))
    qspec = lambda w: pl.BlockSpec((1, t, w), lambda bi, i: (bi, i, 0))
    kspec = pl.BlockSpec((1, s, t), lambda bi, i: (bi, 0, 0))
    return pl.pallas_call(
        kern,
        grid=(b, nb),
        in_specs=[qspec(A_HEADS * t), qspec(IDX_HEADS * IDX_DIM), qspec(t), kspec,
                  pl.BlockSpec((1, nb, t, t), lambda bi, i: (bi, 0, 0, 0)), kspec,
                  _const_spec(bias_tab.shape)],
        out_specs=qspec(A_HEADS * t),
        out_shape=jax.ShapeDtypeStruct((b, s, A_HEADS * t), BF16),
        scratch_shapes=[pltpu.VMEM((nb, t, t), I32),
                        pltpu.VMEM((nb, SORT_ROWS, SUBLANES, t), I32),
                        pltpu.VMEM((nb, t, A_HEADS * t), F32),
                        pltpu.VMEM((IDX_HEADS * t, t), BF16),
                        pltpu.VMEM((A_HEADS * t, t), BF16),
                        pltpu.VMEM((t, t), F32),
                        pltpu.VMEM((SUBLANES, t), I32),
                        pltpu.VMEM((SUBLANES, t), F32)],
        compiler_params=_params("parallel", "parallel"),
        name="dsa_attention",
    )(qa, iq, iw, ka, vt, ik2, bias_tab)


def _hgrn_kernel(q_ref, f_ref, i_ref, g_ref, lb_ref, gain_ref, o_ref, state_ref, *, n_chunks, n_seq):
    c = B_CHUNK
    rc = n_seq * c
    kd, vd = B_KEY_DIM, B_VAL_DIM
    w = B_HEADS * kd

    @pl.when(pl.program_id(1) == 0)
    def _():
        state_ref[...] = jnp.zeros_like(state_ref)

    row = lax.broadcasted_iota(I32, (rc, rc), 0)
    col = lax.broadcasted_iota(I32, (rc, rc), 1)
    seq_of = lambda r: lax.shift_right_logical(r, CHUNK_SHIFT)
    causal = (row >= col) if n_seq == 1 else jnp.logical_and(row >= col, seq_of(row) == seq_of(col))
    tril = jnp.where(causal, 1.0, 0.0).astype(BF16)
    seqs = [slice(b * c, (b + 1) * c) for b in range(n_seq)]
    seq_of_row = seq_of(lax.broadcasted_iota(I32, (rc, w), 0))

    def per_seq_row(x, r):
        return jnp.concatenate([jnp.broadcast_to(x[b * c + r:b * c + r + 1, :], (c, x.shape[1]))
                                for b in range(n_seq)], axis=0)

    def stacked(ref, rows):
        return jnp.concatenate([ref[b, rows, :] for b in range(n_seq)], axis=0)
    q_scale = kd ** -0.5
    lb = lb_ref[...]
    lb_floor = jnp.maximum(lb, LB_FLOOR)
    one_m_lb = 1.0 - lb
    heads = [slice(h * kd, (h + 1) * kd) for h in range(B_HEADS)]

    def stage_decay(ci):
        rows = slice(ci * c, (ci + 1) * c)
        fr = stacked(f_ref, rows)
        z = jnp.exp(-jnp.abs(fr))
        r = 1.0 / (1.0 + z)
        sig_pos = jnp.where(fr >= 0, r, z * r)
        sig_neg = jnp.where(fr >= 0, z * r, r)
        log_f = jnp.log(lb_floor + one_m_lb * sig_pos)
        k_in = one_m_lb * sig_neg
        hi = log_f.astype(BF16)
        rest = log_f - hi.astype(F32)
        mid = rest.astype(BF16)
        lo = (rest - mid.astype(F32)).astype(BF16)
        cs = jnp.dot(tril, jnp.concatenate([hi, mid, lo], axis=1), preferred_element_type=F32)
        return k_in, cs

    def apply_update(pending):
        if pending is not None:
            d_last, st, upd = pending
            for b in range(n_seq):
                for h, hs in enumerate(heads):
                    state_ref[b, h] = d_last[b * c:b * c + 1, hs] * st[b][h] + upd[b][h]

    def stage_scores(ci, decay, pending):
        rows = slice(ci * c, (ci + 1) * c)
        k_in, cs = decay
        bsum = cs[:, :w] + cs[:, w:2 * w] + cs[:, 2 * w:]
        b_mid = per_seq_row(bsum, c // 2 - 1)
        b_last = per_seq_row(bsum, c - 1)
        qs = stacked(q_ref, rows) * q_scale
        v = stacked(i_ref, rows)
        v_b = v.astype(BF16)
        v_t = v.T.astype(BF16)
        q_mid = (qs * jnp.exp(bsum - b_mid)).astype(BF16)
        k_mid = (k_in * jnp.exp(b_mid - bsum)).astype(BF16)
        q_dec = (qs * jnp.exp(bsum)).astype(BF16)
        k_end = (k_in * jnp.exp(b_last - bsum)).astype(BF16)
        d_last = jnp.exp(b_last)
        apply_update(pending)
        st = [[state_ref[b, h] for h in range(B_HEADS)] for b in range(n_seq)]
        attn = [lax.dot_general(q_mid[:, hs], k_mid[:, hs], _NT, preferred_element_type=F32)
                for hs in heads]
        inter = [jnp.concatenate(
            [lax.dot_general(q_dec[sq, hs], st[b][h].astype(BF16), _NT, preferred_element_type=F32)
             for b, sq in enumerate(seqs)], axis=0) for h, hs in enumerate(heads)]
        k_seq = [k_end if n_seq == 1 else jnp.where(seq_of_row == b, k_end, jnp.zeros_like(k_end))
                 for b in range(n_seq)]
        upd = [[jnp.dot(v_t[hs, :], k_seq[b][:, hs], preferred_element_type=F32) for hs in heads]
               for b in range(n_seq)]
        return (attn, inter, v_b), (d_last, st, upd)

    def stage_output(ci, scores):
        rows = slice(ci * c, (ci + 1) * c)
        attn, inter, v_b = scores
        attn = [jnp.where(causal, a, 0.0).astype(BF16) for a in attn]
        outs = [jnp.dot(attn[h], v_b[:, hs], preferred_element_type=F32) + inter[h]
                for h, hs in enumerate(heads)]
        o = jnp.concatenate([_rms(o_h, gain_ref[...]) for o_h in outs], axis=1)
        g = stacked(g_ref, rows)
        o = (o * (g * (1.0 / (1.0 + jnp.exp(-g))))).astype(o_ref.dtype)
        for b, sq in enumerate(seqs):
            o_ref[b, rows, :] = o[sq, :]

    decay, scores, pending = {}, {}, None
    for step in range(n_chunks + 2):
        if step < n_chunks:
            decay[step] = stage_decay(step)
        if 0 <= step - 1 < n_chunks:
            scores[step - 1], pending = stage_scores(step - 1, decay.pop(step - 1), pending)
        if 0 <= step - 2 < n_chunks:
            stage_output(step - 2, scores.pop(step - 2))
    apply_update(pending)


def _hgrn(bq, bf, bi, bg, lb, gain, sb):
    b, s, w = bq.shape
    assert s % sb == 0 and sb % B_CHUNK == 0
    n_seq = 2 if b % 2 == 0 else 1
    kern = functools.partial(_hgrn_kernel, n_chunks=sb // B_CHUNK, n_seq=n_seq)
    spec = pl.BlockSpec((n_seq, sb, w), lambda bi_, si: (bi_, si, 0))
    return pl.pallas_call(
        kern,
        grid=(b // n_seq, s // sb),
        in_specs=[spec, spec, spec, spec, _const_spec((1, w)), _const_spec((1, B_VAL_DIM))],
        out_specs=spec,
        out_shape=jax.ShapeDtypeStruct((b, s, w), BF16),
        scratch_shapes=[pltpu.VMEM((n_seq, B_HEADS, B_VAL_DIM, B_KEY_DIM), F32)],
        compiler_params=_params("parallel", "arbitrary"),
        name="hgrn2",
    )(bq, bf, bi, bg, lb.reshape(1, w), gain.reshape(1, B_VAL_DIM))


def _mla_kernel(qn_ref, qpe_ref, kn_ref, kpe_ref, vt_ref, o_ref, q_ref, acc_ref, *, t, n_tiles, scale):
    i = pl.program_id(1)
    lane_lo = lax.broadcasted_iota(I32, (t, LANES), 1) < C_ROPE
    for h in range(C_HEADS):
        pair = qpe_ref[0, :, (h // 2) * LANES:(h // 2 + 1) * LANES]
        keep = lane_lo if h % 2 == 0 else jnp.logical_not(lane_lo)
        q_ref[h, :, :C_NOPE] = qn_ref[0, :, h * C_NOPE:(h + 1) * C_NOPE]
        q_ref[h, :, C_NOPE:] = jnp.where(keep, pair, jnp.zeros_like(pair))
    acc_ref[...] = jnp.zeros_like(acc_ref)

    def logits_of(step):
        first, count = step
        rows = slice(first * t, (first + count) * t)
        kpe_t = kpe_ref[0, rows, :]
        return [lax.dot_general(
            jnp.concatenate([kn_ref[0, rows, h * C_NOPE:(h + 1) * C_NOPE], kpe_t], axis=1),
            q_ref[h], _NT, preferred_element_type=F32) for h in range(C_HEADS)]

    def softmax_pv(step, block, logits, ms, ls, masked):
        first, count = step
        new_m, new_l = [], []
        for h in range(C_HEADS):
            s = logits[h] * (scale * math.log2(math.e))
            if masked:
                key_pos = lax.broadcasted_iota(I32, (count * t, t), 0) + first * t
                query_pos = lax.broadcasted_iota(I32, (count * t, t), 1) + block * t
                s = jnp.where(key_pos <= query_pos, s, NEG_BIG)
            m_new = jnp.maximum(ms[h], jnp.max(s, axis=0, keepdims=True))
            alpha = jnp.exp2(ms[h] - m_new)
            p = jnp.exp2(s - m_new)
            new_l.append(alpha * ls[h] + jnp.sum(p, axis=0, keepdims=True))
            new_m.append(m_new)
            hs = slice(h * C_V, (h + 1) * C_V)
            tiles = [vt_ref[0, first + u, hs, :] for u in range(count)]
            v_t = tiles[0] if count == 1 else jnp.concatenate(tiles, axis=1)
            acc_ref[h] = alpha * acc_ref[h] + jnp.dot(v_t, p.astype(BF16), preferred_element_type=F32)
        return new_m, new_l

    def run(block):
        steps = [(2 * g, 2) for g in range(block // 2)] + [(block - block % 2, 1 + block % 2)]
        ms = [jnp.full((1, t), NEG_BIG, F32) for _ in range(C_HEADS)]
        ls = [jnp.zeros((1, t), F32) for _ in range(C_HEADS)]
        logits = logits_of(steps[0])
        for k, step in enumerate(steps):
            nxt = logits_of(steps[k + 1]) if k + 1 < len(steps) else None
            ms, ls = softmax_pv(step, block, logits, ms, ls, masked=(k + 1 == len(steps)))
            logits = nxt
        for h in range(C_HEADS):
            o_ref[0, :, h * C_V:(h + 1) * C_V] = (acc_ref[h] * (1.0 / ls[h])).T.astype(o_ref.dtype)

    for v in range(n_tiles):
        pl.when(i == v)(functools.partial(run, v))


def _mla(qn, qpe, kn, kpe, vt, t):
    b, s, _ = qn.shape
    nt = s // t
    kern = functools.partial(_mla_kernel, t=t, n_tiles=nt, scale=(C_NOPE + C_ROPE) ** -0.5)
    qspec = lambda w: pl.BlockSpec((1, t, w), lambda bi, i: (bi, i, 0))
    kspec = lambda w: pl.BlockSpec((1, s, w), lambda bi, i: (bi, 0, 0))
    return pl.pallas_call(
        kern,
        grid=(b, nt),
        in_specs=[qspec(qn.shape[2]), qspec(qpe.shape[2]), kspec(kn.shape[2]), kspec(kpe.shape[2]),
                  pl.BlockSpec((1, nt, C_HEADS * C_V, t), lambda bi, i: (bi, 0, 0, 0))],
        out_specs=qspec(C_HEADS * C_V),
        out_shape=jax.ShapeDtypeStruct((b, s, C_HEADS * C_V), BF16),
        scratch_shapes=[pltpu.VMEM((C_HEADS, t, C_NOPE + LANES), BF16),
                        pltpu.VMEM((C_HEADS, C_V, t), F32)],
        compiler_params=_params("parallel", "parallel"),
        name="mla_attention",
    )(qn, qpe, kn, kpe, vt)


def _merge_kernel(x_ref, g_ref, wg_ref, oa_ref, ob_ref, oc_ref, wa_ref, wb_ref, wc_ref, wo_ref,
                  out_ref, mixed_ref, *, col_chunk):
    x = x_ref[...]
    d = x.shape[1]
    h = _rms(x, g_ref[...]).astype(BF16)
    branches = ((oa_ref, wa_ref), (ob_ref, wb_ref), (oc_ref, wc_ref))
    for s in range(0, d, col_chunk):
        cs = slice(s, s + col_chunk)
        mixed = None
        for bidx, (o_ref, w_ref) in enumerate(branches):
            logits = jnp.dot(h, wg_ref[:, bidx * d + s:bidx * d + s + col_chunk],
                             preferred_element_type=F32)
            gate = 1.0 / (1.0 + jnp.exp(-logits))
            term = gate * jnp.dot(o_ref[...], w_ref[:, cs], preferred_element_type=F32)
            mixed = term if mixed is None else mixed + term
        mixed_ref[:, cs] = mixed.astype(BF16)
    out_ref[...] = x + jnp.dot(mixed_ref[...], wo_ref[...], preferred_element_type=F32)


def _merge(x, g, w_gate, oa, ob, oc, wa, wb, wc, wo, layer, tm):
    n, d = x.shape
    row = lambda w: pl.BlockSpec((tm, w), lambda i: (i, 0))
    kern = functools.partial(_merge_kernel, col_chunk=256)
    return pl.pallas_call(
        kern,
        grid=(n // tm,),
        in_specs=[row(d), _const_spec((1, d)), _layer_spec(w_gate, layer),
                  row(oa.shape[1]), row(ob.shape[1]), row(oc.shape[1]),
                  _layer_spec(wa, layer), _layer_spec(wb, layer), _layer_spec(wc, layer),
                  _layer_spec(wo, layer)],
        out_specs=row(d),
        out_shape=jax.ShapeDtypeStruct((n, d), F32),
        scratch_shapes=[pltpu.VMEM((tm, d), BF16)],
        compiler_params=_params("parallel"),
        name="merge_out_proj",
    )(x, g.reshape(1, d), w_gate, oa, ob, oc, wa, wb, wc, wo)


def _ffn_kernel(x_ref, g_ref, wg_ref, wu_ref, wd_ref, gf_ref, out_ref, act_ref, *, col_chunk,
                final_norm):
    x = x_ref[...]
    h = _rms(x, g_ref[...]).astype(BF16)
    dff = wg_ref.shape[1]
    for s in range(0, dff, col_chunk):
        cs = slice(s, s + col_chunk)
        gate = jnp.dot(h, wg_ref[:, cs], preferred_element_type=F32)
        up = jnp.dot(h, wu_ref[:, cs], preferred_element_type=F32)
        act_ref[:, cs] = (gate * (1.0 / (1.0 + jnp.exp(-gate))) * up).astype(BF16)
    y = x + jnp.dot(act_ref[...], wd_ref[...], preferred_element_type=F32)
    if final_norm:
        y = _rms(y, gf_ref[...])
    out_ref[...] = y


def _ffn(x, g, wg, wu, wd, gf, layer, tm, final_norm):
    n, d = x.shape
    dff = wg.shape[2]
    row = pl.BlockSpec((tm, d), lambda i: (i, 0))
    kern = functools.partial(_ffn_kernel, col_chunk=256, final_norm=final_norm)
    return pl.pallas_call(
        kern,
        grid=(n // tm,),
        in_specs=[row, _const_spec((1, d)), _layer_spec(wg, layer), _layer_spec(wu, layer),
                  _layer_spec(wd, layer), _const_spec((1, d))],
        out_specs=row,
        out_shape=jax.ShapeDtypeStruct((n, d), F32),
        scratch_shapes=[pltpu.VMEM((tm, dff), BF16)],
        compiler_params=_params("parallel"),
        name="swiglu_ffn",
    )(x, g.reshape(1, d), wg, wu, wd, gf.reshape(1, d))


def _t5_bucket(dist):
    max_exact = REL_BUCKETS // 2
    d = jnp.maximum(dist, 0)
    dl = jnp.maximum(d, max_exact).astype(F32)
    large = max_exact + (jnp.log(dl / max_exact) / math.log(REL_MAX_DIST / max_exact)
                         * (REL_BUCKETS - max_exact)).astype(I32)
    large = jnp.minimum(large, REL_BUCKETS - 1)
    return jnp.where(d < max_exact, d, large)


def _swap_halves(w):
    half = w.shape[-1] // 2
    return jnp.concatenate([w[..., half:], w[..., :half]], axis=-1)


def _in_proj_weights(w_in, d_model):
    splits = (A_HEADS * A_HEAD_DIM, A_HEAD_DIM, A_HEAD_DIM, IDX_HEADS * IDX_DIM, IDX_DIM, IDX_HEADS,
              B_HEADS * B_KEY_DIM, B_HEADS * B_KEY_DIM, B_HEADS * B_VAL_DIM, B_HEADS * B_VAL_DIM,
              C_Q_RANK, C_KV_RANK, C_ROPE, N_BRANCH * d_model)
    st = [int(v) for v in np.concatenate([[0], np.cumsum(splits)])]
    depth, k, n_cols = w_in.shape
    w_pad = jnp.pad(w_in.astype(BF16), ((0, 0), (0, 0), (0, -n_cols % LANES)))
    n_pad = w_pad.shape[2]
    reps = LANES // C_ROPE
    rows = 128

    def regroup_kernel(w_ref, a_ref, b_ref, c_ref, d_ref, g_ref):
        col = lambda a, b: w_ref[:, st[a]:st[b]]
        ik, pe = col(4, 5), col(12, 13)
        iw_pad = jnp.zeros((rows, LANES - IDX_HEADS), w_ref.dtype)
        a_ref[...] = col(0, 4).astype(BF16)
        b_ref[...] = jnp.concatenate([ik, ik, col(5, 6), iw_pad], axis=1).astype(BF16)
        c_ref[...] = col(6, 12).astype(BF16)
        d_ref[...] = jnp.concatenate([pe] * reps + [_swap_halves(pe)] * reps, axis=1).astype(BF16)
        g_ref[...] = col(13, 14).astype(BF16)

    out_w = (st[4], 2 * LANES, st[12] - st[6], 2 * LANES, st[14] - st[13])
    outs = pl.pallas_call(
        regroup_kernel,
        grid=(depth, k // rows),
        in_specs=[pl.BlockSpec((None, rows, n_pad), lambda l, i: (l, i, 0))],
        out_specs=[pl.BlockSpec((None, rows, wd), lambda l, i: (l, i, 0)) for wd in out_w],
        out_shape=[jax.ShapeDtypeStruct((depth, k, wd), BF16) for wd in out_w],
        compiler_params=_params("parallel", "parallel"),
        name="regroup_w_in",
    )(w_pad)
    ws = list(outs)
    widths = [splits[:4], (LANES, LANES), splits[6:12], (2 * LANES,)]
    dtypes = [BF16] * 5 + [F32] * 8
    return ws[:4], widths, dtypes, ws[4]


def kernel(x, positions, w_in, w_up_a, w_up_b, w_up_c, w_out, mla_q_norm, mla_w_qb, mla_kv_norm,
           mla_w_kvb, hgrn_lb_logits, hgrn_out_norm, rel_bias, attn_norm, ffn_norm, w_ffn_gate,
           w_ffn_up, w_ffn_down, final_norm):
    bsz, s_len, d_model = x.shape
    depth = w_in.shape[0]
    n = bsz * s_len
    t = LANES

    w_proj, widths, dtypes, w_gate = _in_proj_weights(w_in, d_model)
    wq = mla_w_qb.reshape(depth, C_Q_RANK, C_HEADS, C_NOPE + C_ROPE)
    q_pe = wq[..., C_NOPE:]
    w_qb = jnp.concatenate([wq[..., :C_NOPE].reshape(depth, C_Q_RANK, -1),
                            q_pe.reshape(depth, C_Q_RANK, -1),
                            _swap_halves(q_pe).reshape(depth, C_Q_RANK, -1)], axis=2).astype(BF16)
    wkv = mla_w_kvb.reshape(depth, C_KV_RANK, C_HEADS, C_NOPE + C_V)
    w_kvb = jnp.concatenate([wkv[..., :C_NOPE].reshape(depth, C_KV_RANK, -1),
                             wkv[..., C_NOPE:].reshape(depth, C_KV_RANK, -1)], axis=2).astype(BF16)
    bf = lambda w: w.astype(BF16)
    w_up_a, w_up_b, w_up_c, w_out = bf(w_up_a), bf(w_up_b), bf(w_up_c), bf(w_out)
    w_ffn_gate, w_ffn_up, w_ffn_down = bf(w_ffn_gate), bf(w_ffn_up), bf(w_ffn_down)

    p_lb = jax.nn.softmax(hgrn_lb_logits.astype(F32), axis=0)
    lower_bounds = jnp.cumsum(p_lb, axis=0) - p_lb[0:1]
    inv_freq = ROPE_THETA ** (-jnp.arange(0, C_ROPE, 2, dtype=F32) / C_ROPE)
    ang = positions.astype(F32)[..., None] * inv_freq
    cos, sin = jnp.cos(ang), jnp.sin(ang)
    reps = LANES // C_ROPE
    cos_t = jnp.tile(jnp.concatenate([cos, cos], axis=-1), (1, 1, reps)).reshape(n, LANES)
    sin_t = jnp.tile(jnp.concatenate([-sin, sin], axis=-1), (1, 1, reps)).reshape(n, LANES)
    q_idx = jnp.arange(t, dtype=I32)[None, :]
    k_idx = jnp.arange(t, dtype=I32)[:, None]
    def tab(dist):
        onehot = jax.nn.one_hot(_t5_bucket(dist), REL_BUCKETS, dtype=F32)
        return jnp.einsum("kqb,bh->khq", onehot, rel_bias.astype(F32),
                          precision=lax.Precision.HIGHEST).reshape(t, A_HEADS * t)
    far = jnp.broadcast_to(rel_bias[REL_BUCKETS - 1].astype(F32)[None, :, None],
                           (t, A_HEADS, t)).reshape(t, A_HEADS * t)
    bias_tab = (jnp.stack([tab(q_idx + t - k_idx), tab(q_idx - k_idx), far]) - far) * math.log2(math.e)

    tm = min(512, n)
    t_mla = min(512, s_len)
    x2 = x.reshape(n, d_model)
    r3 = lambda a: a.reshape(bsz, s_len, a.shape[-1])
    per_seq = lambda a: a.reshape(bsz, a.shape[0] // bsz, a.shape[1], a.shape[2])
    key_tiles = [0, 0, t] + [0] * 10
    for l in range(depth):
        mla = (cos_t, sin_t, mla_q_norm[l], mla_kv_norm[l], w_qb, w_kvb, t_mla)
        (qa, ka, vta, iq, ik2, iw, bq, bfr, bi, bg, qn, qpe, kn, vtc, kpe) = _norm_matmul(
            x2, attn_norm[l], w_proj, l, widths, dtypes, key_tiles, mla, tm)
        o_a = _dsa(r3(qa), r3(iq), r3(iw), r3(ka), per_seq(vta), r3(ik2), bias_tab)
        o_b = _hgrn(r3(bq), r3(bfr), r3(bi), r3(bg), lower_bounds[l], hgrn_out_norm[l],
                    sb=min(512, s_len))
        o_c = _mla(r3(qn), r3(qpe), r3(kn), r3(kpe), per_seq(vtc), t_mla)
        x2 = _merge(x2, attn_norm[l], w_gate, o_a.reshape(n, -1), o_b.reshape(n, -1),
                    o_c.reshape(n, -1), w_up_a, w_up_b, w_up_c, w_out, l, tm)
        x2 = _ffn(x2, ffn_norm[l], w_ffn_gate, w_ffn_up, w_ffn_down, final_norm,
                  l, tm, final_norm=(l == depth - 1))
    return x2.reshape(bsz, s_len, d_model)
```
